```python
import jax, jax.numpy as jnp
from jax import lax
import numpy as np

D_MODEL = 1024
BATCH = 8
SEQ = 2048
DEPTH = 2
DEC_BATCH = 128
DEC_SEQ = 8
PAST_LEN = 16384
PAGE_SIZE = 128

N_AB_LAYERS = (DEPTH + 1) // 2
N_CD_LAYERS = DEPTH // 2
DEEPNORM_ALPHA = (2 * DEPTH) ** 0.25
DEEPNORM_BETA = (8 * DEPTH) ** -0.25
HG_HEADS = 4
HG_DK = 128
HG_DV = 128
HG_WIDTH = HG_HEADS * HG_DK
HG_CHUNK = 16
POOL_WINDOWS = (2, 4, 8, 16)
POOL_GROUPS = 4
POOL_GC = 128
POOL_WIDTH = POOL_GROUPS * POOL_GC
POOL_BUF = max(POOL_WINDOWS) - 1
AB_IN = 4 * HG_WIDTH + POOL_WIDTH
SC_WIDTH = 512
SC_K = 3
GD_HEADS = 4
GD_DK = 128
GD_DV = 128
GD_CONV = 4
GD_CHUNK = 64
GD_QKV = GD_HEADS * (2 * GD_DK + GD_DV)
CD_IN = 3 * SC_WIDTH + GD_QKV + GD_HEADS * GD_DV + 2 * GD_HEADS
MIX_OUT = HG_HEADS * HG_DV + POOL_WIDTH
N_EXPERTS = 32
TOP_K = 4
D_FF = D_MODEL
SWIGLU_LIMIT = 7.0
SWIGLU_ALPHA = 1.702
MOE_ROWS = 128
LN_EPS = 1e-5
RMS_EPS = 1e-6

kernel_name = 'hgrn2_pool_shortconv_gdn_moe_step'


def layer_norm(x, g, b):
    xf = x.astype(jnp.float32)
    mu = jnp.mean(xf, axis=-1, keepdims=True)
    var = jnp.mean(jnp.square(xf - mu), axis=-1, keepdims=True)
    return ((xf - mu) * lax.rsqrt(var + LN_EPS) * g.astype(jnp.float32) + b.astype(jnp.float32)).astype(x.dtype)


def rms_norm(x, w):
    return x * lax.rsqrt(jnp.mean(jnp.square(x), axis=-1, keepdims=True) + RMS_EPS) * w.astype(jnp.float32)


def l2_norm(x):
    return x * lax.rsqrt(jnp.sum(jnp.square(x), axis=-1, keepdims=True) + 1e-6)


def to_chunks(a, chunk):
    bsz, length = a.shape[:2]
    n = -(-length // chunk)
    a = jnp.pad(a, ((0, 0), (0, n * chunk - length), (0, 0), (0, 0)))
    return a.reshape(bsz, n, chunk, a.shape[2], a.shape[3]).transpose(1, 0, 3, 2, 4)


def from_chunks(o, length):
    n, bsz, h, c, d = o.shape
    return o.transpose(1, 0, 3, 2, 4).reshape(bsz, n * c, h, d)[:, :length]


def causal_dwconv(u, buf, w):
    width, length = w.shape[0], u.shape[1]
    ext = jnp.concatenate([buf.astype(u.dtype), u], axis=1)
    y = sum(ext[:, j:j + length] * w[j] for j in range(width))
    return y, ext[:, ext.shape[1] - (width - 1):]


def hgrn2_recurrence(q, k, v, log_f, s0):
    length = q.shape[1]
    qc, kc, vc, gc = (to_chunks(a, HG_CHUNK) for a in (q, k, v, log_f))
    b = jnp.cumsum(gc, axis=-2)
    causal = jnp.tril(jnp.ones((HG_CHUNK, HG_CHUNK), bool))
    q_dec = qc * jnp.exp(b)
    att = jnp.where(causal, jnp.einsum('nbhtk,nbhsk->nbhts', q_dec, kc * jnp.exp(-b)), 0.0)
    o_intra = jnp.einsum('nbhts,nbhsv->nbhtv', att, vc)
    b_last = b[..., -1:, :]
    k_to_end = kc * jnp.exp(b_last - b)
    decay_end = jnp.exp(b_last[..., 0, :])

    def step(s, xs):
        q_d, o_in, k_e, v_i, d_e = xs
        o = o_in + jnp.einsum('bhtk,bhkv->bhtv', q_d, s)
        s = d_e[..., None] * s + jnp.einsum('bhsk,bhsv->bhkv', k_e, v_i)
        return s, o

    s, o = lax.scan(step, s0.astype(jnp.float32), (q_dec, o_intra, k_to_end, vc, decay_end))
    return from_chunks(o, length), s


def gated_delta_recurrence(q, k, v, g, beta, s0):
    length = q.shape[1]
    qc, kc, vc = (to_chunks(a, GD_CHUNK) for a in (q, k, v))
    gc = to_chunks(g[..., None], GD_CHUNK)[..., 0]
    bc = to_chunks(beta[..., None], GD_CHUNK)[..., 0]
    gcum = jnp.cumsum(gc, axis=-1)
    incl = jnp.tril(jnp.ones((GD_CHUNK, GD_CHUNK), bool))
    strict = jnp.tril(jnp.ones((GD_CHUNK, GD_CHUNK), bool), k=-1)
    diff = gcum[..., :, None] - gcum[..., None, :]
    decay = jnp.where(incl, jnp.exp(jnp.where(incl, diff, 0.0)), 0.0)
    k_beta = kc * bc[..., None]
    lower = jnp.where(strict, jnp.einsum('nbhtk,nbhsk->nbhts', k_beta, kc) * decay, 0.0)
    unit_lower = lower + jnp.eye(GD_CHUNK, dtype=jnp.float32)

    def solve(rhs):
        return lax.linalg.triangular_solve(unit_lower, rhs, left_side=True, lower=True, unit_diagonal=True)

    u = solve(vc * bc[..., None])
    w = solve(k_beta * jnp.exp(gcum)[..., None])
    att = jnp.where(incl, jnp.einsum('nbhtk,nbhsk->nbhts', qc, kc) * decay, 0.0)
    q_dec = qc * jnp.exp(gcum)[..., None]
    k_to_end = kc * jnp.exp(gcum[..., -1:] - gcum)[..., None]
    decay_end = jnp.exp(gcum[..., -1])

    def step(s, xs):
        q_d, a_i, u_i, w_i, k_e, d_e = xs
        v_new = u_i - jnp.einsum('bhtk,bhkv->bhtv', w_i, s)
        o = jnp.einsum('bhtk,bhkv->bhtv', q_d, s) + jnp.einsum('bhts,bhsv->bhtv', a_i, v_new)
        s = d_e[..., None, None] * s + jnp.einsum('bhsk,bhsv->bhkv', k_e, v_new)
        return s, o

    s, o = lax.scan(step, s0.astype(jnp.float32), (q_dec, att, u, w, k_to_end, decay_end))
    return from_chunks(o, length), s


def multiscale_pool(u, buf, pos0, pool_w, pool_scale):
    bsz, length, _ = u.shape
    ext = jnp.concatenate([buf.astype(u.dtype), u], axis=1)
    cs = jnp.cumsum(jnp.pad(ext.astype(jnp.float32), ((0, 0), (1, 0), (0, 0))), axis=1)
    pos = pos0 + jnp.arange(length)
    means = []
    for gi, win in enumerate(POOL_WINDOWS):
        sl = slice(gi * POOL_GC, (gi + 1) * POOL_GC)
        wsum = cs[:, POOL_BUF + 1:POOL_BUF + 1 + length, sl] - cs[:, POOL_BUF + 1 - win:POOL_BUF + 1 - win + length, sl]
        cnt = jnp.minimum(win, pos + 1).astype(jnp.float32)
        means.append(wsum / cnt[None, :, None])
    d = (jnp.concatenate(means, axis=-1) - u.astype(jnp.float32)).reshape(bsz, length, POOL_GROUPS, POOL_GC)
    y = jnp.einsum('blgc,gcd->blgd', d, pool_w.astype(jnp.float32)).reshape(bsz, length, POOL_WIDTH)
    return y * pool_scale.astype(jnp.float32), ext[:, ext.shape[1] - POOL_BUF:]


def mixer_ab(x, pos0, s_hgrn, s_pool, w_in, lb, norm_w, pool_w, pool_scale, w_out):
    bsz, length, _ = x.shape
    proj = jnp.dot(x, w_in)
    q, f, i_in, g, p_in = jnp.split(proj, [HG_WIDTH, 2 * HG_WIDTH, 3 * HG_WIDTH, 4 * HG_WIDTH], axis=-1)
    heads = lambda a: a.reshape(bsz, length, HG_HEADS, -1)
    f = lb + (1.0 - lb) * jax.nn.sigmoid(f.astype(jnp.float32))
    o_a, s_h = hgrn2_recurrence(heads(jax.nn.silu(q.astype(jnp.float32))), heads(1.0 - f),
                                heads(jax.nn.silu(i_in.astype(jnp.float32))), heads(jnp.log(f)), s_hgrn)
    o_a = (rms_norm(o_a, norm_w) * jax.nn.silu(heads(g.astype(jnp.float32)))).reshape(bsz, length, HG_HEADS * HG_DV)
    o_b, s_p = multiscale_pool(p_in, s_pool, pos0, pool_w, pool_scale)
    y = jnp.dot(jnp.concatenate([o_a, o_b], axis=-1).astype(x.dtype), w_out)
    return y, s_h.astype(s_hgrn.dtype), s_p.astype(s_pool.dtype)


def mixer_cd(x, s_sconv, s_gconv, s_gdn, w_in, sconv_w, gconv_w, a_log, dt_bias, norm_w, w_out):
    bsz, length, _ = x.shape
    proj = jnp.dot(x, w_in)
    cuts = [SC_WIDTH, 2 * SC_WIDTH, 3 * SC_WIDTH, 3 * SC_WIDTH + GD_QKV,
            3 * SC_WIDTH + GD_QKV + GD_HEADS * GD_DV, 3 * SC_WIDTH + GD_QKV + GD_HEADS * GD_DV + GD_HEADS]
    b_gate, c_gate, h, qkv, z, a_in, b_in = jnp.split(proj, cuts, axis=-1)
    conv_out, s_sc = causal_dwconv(c_gate * h, s_sconv, sconv_w)
    o_c = b_gate * conv_out
    qkv_c, s_gc = causal_dwconv(qkv, s_gconv, gconv_w)
    qkv_c = jax.nn.silu(qkv_c.astype(jnp.float32))
    q, k, v = jnp.split(qkv_c, [GD_HEADS * GD_DK, 2 * GD_HEADS * GD_DK], axis=-1)
    heads = lambda a: a.reshape(bsz, length, GD_HEADS, -1)
    q = l2_norm(heads(q)) * (GD_DK ** -0.5)
    k = l2_norm(heads(k))
    g_dec = -jnp.exp(a_log.astype(jnp.float32)) * jax.nn.softplus(a_in.astype(jnp.float32) + dt_bias.astype(jnp.float32))
    beta = jax.nn.sigmoid(b_in.astype(jnp.float32))
    o_d, s_g = gated_delta_recurrence(q, k, heads(v), g_dec, beta, s_gdn)
    o_d = (rms_norm(o_d, norm_w) * jax.nn.silu(heads(z.astype(jnp.float32)))).reshape(bsz, length, GD_HEADS * GD_DV)
    y = jnp.dot(jnp.concatenate([o_c, o_d.astype(x.dtype)], axis=-1), w_out)
    return y, s_sc.astype(s_sconv.dtype), s_gc.astype(s_gconv.dtype), s_g.astype(s_gdn.dtype)


def moe_ffn(x, router_w, router_b, w_gu, b_gu, w_dn, b_dn):
    lead = x.shape[:-1]
    xt = x.reshape(-1, D_MODEL)
    n_tok = xt.shape[0]
    logits = jnp.dot(xt, router_w).astype(jnp.float32) + router_b.astype(jnp.float32)
    top_logit, top_e = lax.top_k(logits, TOP_K)
    gate = jax.nn.softmax(top_logit, axis=-1)
    n_assign = n_tok * TOP_K
    n_blocks = -(-n_assign // MOE_ROWS) + N_EXPERTS
    flat_e = top_e.reshape(-1)
    order = jnp.argsort(flat_e)
    sorted_e = flat_e[order]
    counts = jnp.bincount(flat_e, length=N_EXPERTS)
    first = jnp.cumsum(counts) - counts
    padded = (counts + MOE_ROWS - 1) // MOE_ROWS * MOE_ROWS
    pad_end = jnp.cumsum(padded)
    dest = (pad_end - padded)[sorted_e] + jnp.arange(n_assign) - first[sorted_e]
    row_tok = jnp.full((n_blocks * MOE_ROWS,), n_tok, jnp.int32).at[dest].set((order // TOP_K).astype(jnp.int32))
    row_gate = jnp.zeros((n_blocks * MOE_ROWS,), jnp.float32).at[dest].set(gate.reshape(-1)[order])
    block_e = jnp.minimum(jnp.searchsorted(pad_end, jnp.arange(n_blocks) * MOE_ROWS, side='right'), N_EXPERTS - 1)
    x_pad = jnp.concatenate([xt, jnp.zeros((1, D_MODEL), xt.dtype)], axis=0)

    def run_block(args):
        tok, g, e = args
        hgu = jnp.dot(x_pad[tok], w_gu[e]) + b_gu[e]
        glu = jnp.minimum(hgu[:, :D_FF], SWIGLU_LIMIT)
        lin = jnp.clip(hgu[:, D_FF:], -SWIGLU_LIMIT, SWIGLU_LIMIT)
        act = (lin + 1.0) * glu * jax.nn.sigmoid(SWIGLU_ALPHA * glu)
        return (jnp.dot(act, w_dn[e]) + b_dn[e]) * g[:, None].astype(xt.dtype)

    y_rows = lax.map(run_block, (row_tok.reshape(n_blocks, MOE_ROWS), row_gate.reshape(n_blocks, MOE_ROWS), block_e))
    y = jnp.zeros((n_tok + 1, D_MODEL), xt.dtype).at[row_tok].add(y_rows.reshape(-1, D_MODEL).astype(xt.dtype))
    return y[:n_tok].reshape(*lead, D_MODEL)


def run_trunk(x, pos0, st_hgrn, st_pool, st_sconv, st_gconv, st_gdn, p):
    new_hgrn, new_pool, new_sconv, new_gconv, new_gdn = [], [], [], [], []
    lower_bounds = jnp.cumsum(jax.nn.softmax(p['hgrn_lower_bounds'].astype(jnp.float32), axis=0), axis=0)
    for l in range(DEPTH):
        i = l // 2
        if l % 2 == 0:
            mix, s_h, s_p = mixer_ab(x, pos0, st_hgrn[i], st_pool[i], p['w_in_ab'][i], lower_bounds[l],
                                     p['hgrn_norm_w'][i], p['pool_w'][i], p['pool_scale'][i], p['w_out_ab'][i])
            new_hgrn.append(s_h)
            new_pool.append(s_p)
        else:
            mix, s_sc, s_gc, s_g = mixer_cd(x, st_sconv[i], st_gconv[i], st_gdn[i], p['w_in_cd'][i], p['sconv_w'][i],
                                            p['gdn_conv_w'][i], p['gdn_a_log'][i], p['gdn_dt_bias'][i],
                                            p['gdn_norm_w'][i], p['w_out_cd'][i])
            new_sconv.append(s_sc)
            new_gconv.append(s_gc)
            new_gdn.append(s_g)
        x = layer_norm(DEEPNORM_ALPHA * x + mix, p['ln1_g'][l], p['ln1_b'][l])
        ffn = moe_ffn(x, p['router_w'][l], p['router_b'][l], p['w_gu'][l], p['b_gu'][l], p['w_dn'][l], p['b_dn'][l])
        x = layer_norm(DEEPNORM_ALPHA * x + ffn, p['ln2_g'][l], p['ln2_b'][l])
    return (x, jnp.stack(new_hgrn), jnp.stack(new_pool), jnp.stack(new_sconv), jnp.stack(new_gconv), jnp.stack(new_gdn))


def setup_inputs(seed: int = 0) -> dict:
    key = jax.random.key(seed)
    ks = iter(jax.random.split(key, 48))

    def nrm(shape, scale):
        return scale * jax.random.normal(next(ks), shape, jnp.float32)

    dt = jnp.exp(jax.random.uniform(next(ks), (N_CD_LAYERS, GD_HEADS), jnp.float32, np.log(1e-3), np.log(1e-1)))
    return {
        'x_prompt': nrm((BATCH, SEQ, D_MODEL), 1.0),
        'x_sample': nrm((DEC_BATCH, DEC_SEQ, D_MODEL), 1.0),
        'state_hgrn': nrm((N_AB_LAYERS, DEC_BATCH, HG_HEADS, HG_DK, HG_DV), 0.5),
        'state_pool': nrm((N_AB_LAYERS, DEC_BATCH, POOL_BUF, POOL_WIDTH), 1.0),
        'state_sconv': nrm((N_CD_LAYERS, DEC_BATCH, SC_K - 1, SC_WIDTH), 1.0),
        'state_gdn_conv': nrm((N_CD_LAYERS, DEC_BATCH, GD_CONV - 1, GD_QKV), 1.0),
        'state_gdn': nrm((N_CD_LAYERS, DEC_BATCH, GD_HEADS, GD_DK, GD_DV), 0.1),
        'w_in_ab': nrm((N_AB_LAYERS, D_MODEL, AB_IN), D_MODEL ** -0.5),
        'hgrn_lower_bounds': nrm((DEPTH + 1, HG_WIDTH), 0.1),
        'hgrn_norm_w': 1.0 + nrm((N_AB_LAYERS, HG_DV), 0.02),
        'pool_w': nrm((N_AB_LAYERS, POOL_GROUPS, POOL_GC, POOL_GC), POOL_GC ** -0.5),
        'pool_scale': 1.0 + nrm((N_AB_LAYERS, POOL_WIDTH), 0.02),
        'w_out_ab': nrm((N_AB_LAYERS, MIX_OUT, D_MODEL), DEEPNORM_BETA * MIX_OUT ** -0.5),
        'w_in_cd': nrm((N_CD_LAYERS, D_MODEL, CD_IN), D_MODEL ** -0.5),
        'sconv_w': nrm((N_CD_LAYERS, SC_K, SC_WIDTH), SC_K ** -0.5),
        'gdn_conv_w': nrm((N_CD_LAYERS, GD_CONV, GD_QKV), GD_CONV ** -0.5),
        'gdn_a_log': jnp.log(jax.random.uniform(next(ks), (N_CD_LAYERS, GD_HEADS), jnp.float32, 1.0, 16.0)),
        'gdn_dt_bias': dt + jnp.log(-jnp.expm1(-dt)),
        'gdn_norm_w': 1.0 + nrm((N_CD_LAYERS, GD_DV), 0.02),
        'w_out_cd': nrm((N_CD_LAYERS, MIX_OUT, D_MODEL), DEEPNORM_BETA * MIX_OUT ** -0.5),
        'ln1_g': 1.0 + nrm((DEPTH, D_MODEL), 0.02),
        'ln1_b': nrm((DEPTH, D_MODEL), 0.02),
        'ln2_g': 1.0 + nrm((DEPTH, D_MODEL), 0.02),
        'ln2_b': nrm((DEPTH, D_MODEL), 0.02),
        'router_w': nrm((DEPTH, D_MODEL, N_EXPERTS), D_MODEL ** -0.5),
        'router_b': nrm((DEPTH, N_EXPERTS), 0.01),
        'w_gu': nrm((DEPTH, N_EXPERTS, D_MODEL, 2 * D_FF), D_MODEL ** -0.5),
        'b_gu': nrm((DEPTH, N_EXPERTS, 2 * D_FF), 0.01),
        'w_dn': nrm((DEPTH, N_EXPERTS, D_FF, D_MODEL), DEEPNORM_BETA * D_FF ** -0.5),
        'b_dn': nrm((DEPTH, N_EXPERTS, D_MODEL), 0.01),
    }


def reference(x_prompt, x_sample, state_hgrn, state_pool, state_sconv, state_gdn_conv, state_gdn,
              w_in_ab, hgrn_lower_bounds, hgrn_norm_w, pool_w, pool_scale, w_out_ab,
              w_in_cd, sconv_w, gdn_conv_w, gdn_a_log, gdn_dt_bias, gdn_norm_w, w_out_cd,
              ln1_g, ln1_b, ln2_g, ln2_b, router_w, router_b, w_gu, b_gu, w_dn, b_dn):
    p = dict(w_in_ab=w_in_ab, hgrn_lower_bounds=hgrn_lower_bounds, hgrn_norm_w=hgrn_norm_w, pool_w=pool_w,
             pool_scale=pool_scale, w_out_ab=w_out_ab, w_in_cd=w_in_cd, sconv_w=sconv_w, gdn_conv_w=gdn_conv_w,
             gdn_a_log=gdn_a_log, gdn_dt_bias=gdn_dt_bias, gdn_norm_w=gdn_norm_w, w_out_cd=w_out_cd,
             ln1_g=ln1_g, ln1_b=ln1_b, ln2_g=ln2_g, ln2_b=ln2_b, router_w=router_w, router_b=router_b,
             w_gu=w_gu, b_gu=b_gu, w_dn=w_dn, b_dn=b_dn)
    bp = x_prompt.shape[0]
    dt = x_prompt.dtype
    z_hgrn = jnp.zeros((N_AB_LAYERS, bp, HG_HEADS, HG_DK, HG_DV), dt)
    z_pool = jnp.zeros((N_AB_LAYERS, bp, POOL_BUF, POOL_WIDTH), dt)
    z_sconv = jnp.zeros((N_CD_LAYERS, bp, SC_K - 1, SC_WIDTH), dt)
    z_gconv = jnp.zeros((N_CD_LAYERS, bp, GD_CONV - 1, GD_QKV), dt)
    z_gdn = jnp.zeros((N_CD_LAYERS, bp, GD_HEADS, GD_DK, GD_DV), dt)
    y_prompt, hgrn_p, pool_p, sconv_p, gconv_p, gdn_p = run_trunk(x_prompt, 0, z_hgrn, z_pool, z_sconv, z_gconv, z_gdn, p)
    y_sample, hgrn_s, pool_s, sconv_s, gconv_s, gdn_s = run_trunk(x_sample, PAST_LEN, state_hgrn, state_pool, state_sconv,
                                                                  state_gdn_conv, state_gdn, p)
    return (y_prompt, y_sample, hgrn_p, hgrn_s, pool_p, pool_s, sconv_p, sconv_s, gconv_p, gconv_s, gdn_p, gdn_s)
```

```python
import functools

import jax
import jax.numpy as jnp
from jax import lax
from jax.experimental import pallas as pl
from jax.experimental.pallas import tpu as pltpu

D_MODEL = 1024
DEPTH = 2
N_EXPERTS = 32
TOP_K = 4
D_FF = 1024
SWIGLU_LIMIT = 7.0
SWIGLU_ALPHA = 1.702
LN_EPS = 1e-5
DEEPNORM_ALPHA = (2 * DEPTH) ** 0.25

HALF = D_MODEL // 2
TOKEN_TILE = 256
EXPERT_ROWS = 256
N_GROUPS = 2
COMBINE_CHUNK = 16
VMEM_LIMIT = 56 * 1024 * 1024
MIX_OUT = 1024
BATCH_ROW_TILE = 128
PAST_LEN = 16384


def _layer_norm_rows(v, g, b):
    mu = jnp.mean(v, axis=-1, keepdims=True)
    d = v - mu
    var = jnp.mean(d * d, axis=-1, keepdims=True)
    return d * lax.rsqrt(var + LN_EPS) * g + b


def _pack_bf16_pairs(v):
    lo = pltpu.bitcast(v[:, :HALF].astype(jnp.bfloat16).astype(jnp.float32), jnp.uint32)
    hi = pltpu.bitcast(v[:, HALF:].astype(jnp.bfloat16).astype(jnp.float32), jnp.uint32)
    return (lo >> 16) | (hi & jnp.uint32(0xFFFF0000))


def _unpack_bf16_pairs(p):
    lo = pltpu.bitcast(p << 16, jnp.float32).astype(jnp.bfloat16)
    hi = pltpu.bitcast(p & jnp.uint32(0xFFFF0000), jnp.float32).astype(jnp.bfloat16)
    return lo, hi


ROW_TILE = 256
HG_HEADS, HG_D = 4, 128
HG_WIDTH = HG_HEADS * HG_D
HG_CHUNK = 16
POOL_WINDOWS = (2, 4, 8, 16)
POOL_GC = 128
POOL_WIDTH = 512
RMS_EPS = 1e-6
_HI = lax.Precision.HIGHEST


def _silu(v):
    return v * jax.nn.sigmoid(v)


def _bf(v):
    return v.astype(jnp.bfloat16)


def _dot(a, b, precision=None):
    return jnp.dot(a, b, preferred_element_type=jnp.float32, precision=precision)


def _dot_nt(a, b, precision=None):
    return lax.dot_general(a, b, (((1,), (1,)), ((), ())), preferred_element_type=jnp.float32, precision=precision)


def _dot_tn(a, b, precision=None):
    return lax.dot_general(a, b, (((0,), (0,)), ((), ())), preferred_element_type=jnp.float32, precision=precision)


def _chunk_masks(rows, chunk):
    shift = chunk.bit_length() - 1
    t = lax.broadcasted_iota(jnp.int32, (rows, rows), 0)
    s = lax.broadcasted_iota(jnp.int32, (rows, rows), 1)
    same = (t >> shift) == (s >> shift)
    return same, jnp.logical_and(same, s <= t), jnp.logical_and(same, s < t)


def _shift_rows_chain(cur, prev, j):
    if j == 0:
        return cur
    p = prev.shape[0]
    rc = pltpu.roll(cur, j, 0)
    rp = prev if j == p else pltpu.roll(prev, j, 0)
    row = lax.broadcasted_iota(jnp.int32, (p, cur.shape[1]), 0)
    head = jnp.where(row < j, rp, rc[:p])
    return jnp.concatenate([head, rc[p:]], axis=0) if cur.shape[0] > p else head


def _shift_rows_batch(cur, prev, j):
    if j == 0:
        return cur
    if j == 8:
        return prev
    rows = cur.shape[0]
    row = lax.broadcasted_iota(jnp.int32, cur.shape, 0)
    return jnp.where((row & 7) < j, pltpu.roll(prev, rows + j - 8, 0), pltpu.roll(cur, j, 0))


def _rms_gate(o, norm_w, gate):
    ms = jnp.mean(o * o, axis=-1, keepdims=True)
    return o * lax.rsqrt(ms + RMS_EPS) * norm_w * _silu(gate)


def _mixer_ab_kernel(x_ref, w_ref, lb_ref, nw_ref, pw_ref, ps_ref, s0_ref, pa_ref, pb_ref,
                     mix_ref, s_out_ref, pa_out_ref, pb_out_ref, state_ref, prev_ref, *, chain, pos0):
    rows = x_ref.shape[0]
    chunk = HG_CHUNK if chain else 8
    n_chunks = rows // chunk
    j = pl.program_id(1) if chain else 0

    proj = _dot(_bf(x_ref[...]), w_ref[...])
    q = _silu(proj[:, 0:HG_WIDTH])
    f = lb_ref[...] + (1.0 - lb_ref[...]) * jax.nn.sigmoid(proj[:, HG_WIDTH:2 * HG_WIDTH])
    v = _silu(proj[:, 2 * HG_WIDTH:3 * HG_WIDTH])
    gate = proj[:, 3 * HG_WIDTH:4 * HG_WIDTH]
    u = proj[:, 4 * HG_WIDTH:]
    log_f = jnp.log(f)
    k = 1.0 - f

    same, incl, _ = _chunk_masks(rows, chunk)
    cum = _dot(incl.astype(jnp.float32), log_f, _HI)
    total = _dot(same.astype(jnp.float32), log_f, _HI)
    q_dec = q * jnp.exp(cum)
    k_dec = k * jnp.exp(-cum)
    k_end = k * jnp.exp(total - cum)
    chunk_of_row = lax.broadcasted_iota(jnp.int32, (rows, 128), 0) >> (chunk.bit_length() - 1)
    onehot = (chunk_of_row == lax.broadcasted_iota(jnp.int32, (rows, 128), 1)).astype(jnp.float32)
    decay_cols = jnp.exp(_dot_tn(log_f, onehot, _HI))

    if chain:
        @pl.when(j == 0)
        def _():
            state_ref[...] = s0_ref[0]
            prev_ref[...] = jnp.concatenate([pa_ref[...], pb_ref[...]], axis=0)

    outs = []
    for h in range(HG_HEADS):
        sl = slice(h * HG_D, (h + 1) * HG_D)
        att = jnp.where(incl, _dot_nt(_bf(q_dec[:, sl]), _bf(k_dec[:, sl])), 0.0)
        o = _dot(_bf(att), _bf(v[:, sl]))
        inter = []
        s = state_ref[h] if chain else None
        for c in range(n_chunks):
            rs = slice(c * chunk, (c + 1) * chunk)
            if not chain:
                s = s0_ref[c, h]
            inter.append(_dot(_bf(q_dec[rs, sl]), _bf(s)))
            s = decay_cols[sl, c:c + 1] * s + _dot_tn(_bf(k_end[rs, sl]), _bf(v[rs, sl]))
            if not chain:
                s_out_ref[c, h] = s
        if chain:
            state_ref[h] = s
        o = o + jnp.concatenate(inter, axis=0)
        outs.append(_rms_gate(o, nw_ref[...], gate[:, sl]))

    if chain:
        prev = prev_ref[...]
        t = j * rows + lax.broadcasted_iota(jnp.int32, (rows, POOL_GC), 0)
    else:
        prev_a, prev_b = pa_ref[...], pb_ref[...]
        t = lax.broadcasted_iota(jnp.int32, (rows, POOL_GC), 0) & 7
    for gi, win in enumerate(POOL_WINDOWS):
        gs = slice(gi * POOL_GC, (gi + 1) * POOL_GC)
        ug = u[:, gs]
        wsum = ug
        for d in range(1, win):
            if chain:
                wsum = wsum + _shift_rows_chain(ug, prev[:, gs], d)
            elif d <= 8:
                wsum = wsum + _shift_rows_batch(ug, prev_b[:, gs], d)
            else:
                wsum = wsum + _shift_rows_batch(prev_b[:, gs], prev_a[:, gs], d - 8)
        cnt = jnp.minimum(win, pos0 + t + 1).astype(jnp.float32)
        diff = wsum / cnt - ug
        outs.append(_dot(_bf(diff), pw_ref[gi]) * ps_ref[:, gs])
    mix_ref[...] = jnp.concatenate(outs, axis=-1).astype(mix_ref.dtype)

    if chain:
        prev_ref[...] = u[rows - 16:]

        @pl.when(j == pl.num_programs(1) - 1)
        def _():
            s_out_ref[0] = state_ref[...]
            pa_out_ref[...] = u[rows - 16:rows - 8]
            pb_out_ref[...] = u[rows - 8:]
    else:
        pa_out_ref[...] = prev_b
        pb_out_ref[...] = u


def _mixer_ab(x, n_seq, pos0, s0, pool_state, w_in, lb, norm_w, pool_w, pool_scale):
    n_rows = x.shape[0]
    length = n_rows // n_seq
    chain = length >= ROW_TILE
    rows = ROW_TILE if chain else BATCH_ROW_TILE
    pool16 = jnp.pad(pool_state, ((0, 0), (1, 0), (0, 0)))
    pa = pool16[:, :8].reshape(n_seq * 8, POOL_WIDTH)
    pb = pool16[:, 8:].reshape(n_seq * 8, POOL_WIDTH)
    if chain:
        assert length % rows == 0
        tiles = length // rows
        grid = (n_seq, tiles)
        row_map = lambda b, j: (b * tiles + j, 0)
        seq_map4 = lambda b, j: (b, 0, 0, 0)
        seq_map2 = lambda b, j: (b, 0)
        const2 = lambda b, j: (0, 0)
        const3 = lambda b, j: (0, 0, 0)
        seq_block = 1
    else:
        assert length == 8 and n_rows % rows == 0
        seq_block = rows // 8
        grid = (n_rows // rows,)
        row_map = lambda i: (i, 0)
        seq_map4 = lambda i: (i, 0, 0, 0)
        seq_map2 = lambda i: (i, 0)
        const2 = lambda i: (0, 0)
        const3 = lambda i: (0, 0, 0)
    d_in = w_in.shape[1]
    mix, s_new, pa_new, pb_new = pl.pallas_call(
        functools.partial(_mixer_ab_kernel, chain=chain, pos0=pos0),
        grid=grid,
        in_specs=[
            pl.BlockSpec((rows, D_MODEL), row_map),
            pl.BlockSpec((D_MODEL, d_in), const2),
            pl.BlockSpec((1, HG_WIDTH), const2),
            pl.BlockSpec((1, HG_D), const2),
            pl.BlockSpec((len(POOL_WINDOWS), POOL_GC, POOL_GC), const3),
            pl.BlockSpec((1, POOL_WIDTH), const2),
            pl.BlockSpec((seq_block, HG_HEADS, HG_D, HG_D), seq_map4),
            pl.BlockSpec((seq_block * 8, POOL_WIDTH), seq_map2),
            pl.BlockSpec((seq_block * 8, POOL_WIDTH), seq_map2),
        ],
        out_specs=[
            pl.BlockSpec((rows, MIX_OUT), row_map),
            pl.BlockSpec((seq_block, HG_HEADS, HG_D, HG_D), seq_map4),
            pl.BlockSpec((seq_block * 8, POOL_WIDTH), seq_map2),
            pl.BlockSpec((seq_block * 8, POOL_WIDTH), seq_map2),
        ],
        out_shape=[
            jax.ShapeDtypeStruct((n_rows, MIX_OUT), jnp.bfloat16),
            jax.ShapeDtypeStruct(s0.shape, jnp.float32),
            jax.ShapeDtypeStruct(pa.shape, jnp.float32),
            jax.ShapeDtypeStruct(pb.shape, jnp.float32),
        ],
        scratch_shapes=[
            pltpu.VMEM((HG_HEADS, HG_D, HG_D), jnp.float32),
            pltpu.VMEM((16, POOL_WIDTH), jnp.float32),
        ],
        compiler_params=pltpu.CompilerParams(
            dimension_semantics=("arbitrary",) * len(grid), vmem_limit_bytes=VMEM_LIMIT),
        name="mixer_ab_chain" if chain else "mixer_ab_batch",
    )(x, w_in, lb.reshape(1, -1), norm_w.reshape(1, -1), pool_w, pool_scale.reshape(1, -1), s0, pa, pb)
    pool_new = jnp.concatenate([pa_new.reshape(n_seq, 8, POOL_WIDTH), pb_new.reshape(n_seq, 8, POOL_WIDTH)], axis=1)
    return mix, s_new, pool_new[:, 1:]


SC_WIDTH, SC_K = 512, 3
GD_HEADS, GD_D = 4, 128
GD_CONV = 4
GD_CHUNK = 64
GD_QKV = GD_HEADS * 3 * GD_D
CD_MAIN = 3 * SC_WIDTH + GD_QKV + GD_HEADS * GD_D


def _causal_conv(cur, prev, w_ref, shift_fn):
    width = w_ref.shape[0]
    acc = cur * w_ref[width - 1:width, :]
    for j in range(width - 1):
        acc = acc + shift_fn(cur, prev, width - 1 - j) * w_ref[j:j + 1, :]
    return acc


def _softplus(v):
    return jnp.maximum(v, 0.0) + jnp.log1p(jnp.exp(-jnp.abs(v)))


def _unit_lower_inverse(lower, chunk):
    rows = lower.shape[0]
    eye = (lax.broadcasted_iota(jnp.int32, (rows, rows), 0)
           == lax.broadcasted_iota(jnp.int32, (rows, rows), 1)).astype(jnp.float32)
    power = -lower
    inv = eye + power
    for _ in range(chunk.bit_length() - 2):
        power = _dot(power, power, _HI)
        inv = inv + _dot(inv, power, _HI)
    return inv


def _mixer_cd_kernel(x_ref, w_ref, wg_ref, scw_ref, gcw_ref, alog_ref, dtb_ref, nw_ref, s0_ref, scp_ref, gcp_ref,
                     mix_ref, s_out_ref, scp_out_ref, gcp_out_ref, state_ref, sc_prev_ref, gc_prev_ref, *, chain):
    rows = x_ref.shape[0]
    chunk = GD_CHUNK if chain else 8
    n_chunks = rows // chunk
    j = pl.program_id(1) if chain else 0
    shift_fn = _shift_rows_chain if chain else _shift_rows_batch

    xb = _bf(x_ref[...])
    proj = _dot(xb, w_ref[...])
    gates = _dot(xb, wg_ref[...])
    b_gate = proj[:, 0:SC_WIDTH]
    conv_in = proj[:, SC_WIDTH:2 * SC_WIDTH] * proj[:, 2 * SC_WIDTH:3 * SC_WIDTH]
    qkv = proj[:, 3 * SC_WIDTH:3 * SC_WIDTH + GD_QKV]
    z = proj[:, 3 * SC_WIDTH + GD_QKV:]

    if chain:
        @pl.when(j == 0)
        def _():
            state_ref[...] = s0_ref[0]
            sc_prev_ref[...] = scp_ref[...]
            gc_prev_ref[...] = gcp_ref[...]
        sc_prev, gc_prev = sc_prev_ref[...], gc_prev_ref[...]
    else:
        sc_prev, gc_prev = scp_ref[...], gcp_ref[...]

    outs = [b_gate * _causal_conv(conv_in, sc_prev, scw_ref, shift_fn)]
    qkv_c = _silu(_causal_conv(qkv, gc_prev, gcw_ref, shift_fn))

    lane8 = lax.broadcasted_iota(jnp.int32, gates.shape, 1)
    g_dec = -jnp.exp(alog_ref[...]) * _softplus(gates + dtb_ref[...])
    cols = jnp.where(lane8 < GD_HEADS, g_dec, jax.nn.sigmoid(gates))
    same, incl, strict = _chunk_masks(rows, chunk)
    gcum = _dot(incl.astype(jnp.float32), cols, _HI)
    gtot = _dot(same.astype(jnp.float32), cols, _HI)
    eye8 = (lax.broadcasted_iota(jnp.int32, (8, 8), 0) == lax.broadcasted_iota(jnp.int32, (8, 8), 1)).astype(jnp.float32)
    gcum_rows = _dot_nt(eye8, gcum, _HI)

    for h in range(GD_HEADS):
        sl = lambda part: slice(part * GD_HEADS * GD_D + h * GD_D, part * GD_HEADS * GD_D + (h + 1) * GD_D)
        q, k, v = qkv_c[:, sl(0)], qkv_c[:, sl(1)], qkv_c[:, sl(2)]
        q = q * lax.rsqrt(jnp.sum(q * q, axis=-1, keepdims=True) + 1e-6) * (GD_D ** -0.5)
        k = k * lax.rsqrt(jnp.sum(k * k, axis=-1, keepdims=True) + 1e-6)
        beta = cols[:, GD_HEADS + h:GD_HEADS + h + 1]
        gc = gcum[:, h:h + 1]
        gt = gtot[:, h:h + 1]
        decay = jnp.where(incl, jnp.exp(jnp.where(incl, gc - gcum_rows[h:h + 1, :], 0.0)), 0.0)
        k_beta = k * beta
        kb = _bf(k)
        lower = jnp.where(strict, _dot_nt(_bf(k_beta), kb) * decay, 0.0)
        att = jnp.where(incl, _dot_nt(_bf(q), kb) * decay, 0.0)
        inv = _unit_lower_inverse(lower, chunk)
        egc = jnp.exp(gc)
        u = _dot(inv, v * beta, _HI)
        w = _dot(inv, k_beta * egc, _HI)
        q_dec = q * egc
        k_end = k * jnp.exp(gt - gc)
        decay_end = jnp.exp(gt)

        inter, v_news = [], []
        s = state_ref[h] if chain else None
        for c in range(n_chunks):
            rs = slice(c * chunk, (c + 1) * chunk)
            if not chain:
                s = s0_ref[c, h]
            sb = _bf(s)
            v_new = u[rs] - _dot(_bf(w[rs]), sb)
            inter.append(_dot(_bf(q_dec[rs]), sb))
            s = decay_end[c * chunk:c * chunk + 1, :] * s + _dot_tn(_bf(k_end[rs]), _bf(v_new))
            v_news.append(v_new)
            if not chain:
                s_out_ref[c, h] = s
        if chain:
            state_ref[h] = s
        o = jnp.concatenate(inter, axis=0) + _dot(_bf(att), _bf(jnp.concatenate(v_news, axis=0)))
        outs.append(_rms_gate(o, nw_ref[...], z[:, h * GD_D:(h + 1) * GD_D]))
    mix_ref[...] = jnp.concatenate(outs, axis=-1).astype(mix_ref.dtype)

    if chain:
        sc_prev_ref[...] = conv_in[rows - 8:]
        gc_prev_ref[...] = qkv[rows - 8:]

        @pl.when(j == pl.num_programs(1) - 1)
        def _():
            s_out_ref[0] = state_ref[...]
            scp_out_ref[...] = conv_in[rows - 8:]
            gcp_out_ref[...] = qkv[rows - 8:]
    else:
        scp_out_ref[...] = conv_in
        gcp_out_ref[...] = qkv


def _mixer_cd(x, n_seq, s0, sconv_state, gconv_state, w_main, w_gates, sconv_w, gconv_w, a_log, dt_bias, norm_w):
    n_rows = x.shape[0]
    length = n_rows // n_seq
    chain = length >= ROW_TILE
    rows = ROW_TILE if chain else BATCH_ROW_TILE
    pad8 = lambda st: jnp.pad(st, ((0, 0), (8 - st.shape[1], 0), (0, 0))).reshape(n_seq * 8, st.shape[2])
    scp, gcp = pad8(sconv_state), pad8(gconv_state)
    if chain:
        assert length % rows == 0
        tiles = length // rows
        grid = (n_seq, tiles)
        row_map = lambda b, j: (b * tiles + j, 0)
        seq_map4 = lambda b, j: (b, 0, 0, 0)
        seq_map2 = lambda b, j: (b, 0)
        const2 = lambda b, j: (0, 0)
        seq_block = 1
    else:
        assert length == 8 and n_rows % rows == 0
        seq_block = rows // 8
        grid = (n_rows // rows,)
        row_map = lambda i: (i, 0)
        seq_map4 = lambda i: (i, 0, 0, 0)
        seq_map2 = lambda i: (i, 0)
        const2 = lambda i: (0, 0)
    zeros4 = jnp.zeros((GD_HEADS,), jnp.float32)
    alog8 = jnp.concatenate([a_log.astype(jnp.float32), zeros4]).reshape(1, 8)
    dtb8 = jnp.concatenate([dt_bias.astype(jnp.float32), zeros4]).reshape(1, 8)
    mix, s_new, scp_new, gcp_new = pl.pallas_call(
        functools.partial(_mixer_cd_kernel, chain=chain),
        grid=grid,
        in_specs=[
            pl.BlockSpec((rows, D_MODEL), row_map),
            pl.BlockSpec((D_MODEL, CD_MAIN), const2),
            pl.BlockSpec((D_MODEL, 8), const2),
            pl.BlockSpec((SC_K, SC_WIDTH), const2),
            pl.BlockSpec((GD_CONV, GD_QKV), const2),
            pl.BlockSpec((1, 8), const2),
            pl.BlockSpec((1, 8), const2),
            pl.BlockSpec((1, GD_D), const2),
            pl.BlockSpec((seq_block, GD_HEADS, GD_D, GD_D), seq_map4),
            pl.BlockSpec((seq_block * 8, SC_WIDTH), seq_map2),
            pl.BlockSpec((seq_block * 8, GD_QKV), seq_map2),
        ],
        out_specs=[
            pl.BlockSpec((rows, MIX_OUT), row_map),
            pl.BlockSpec((seq_block, GD_HEADS, GD_D, GD_D), seq_map4),
            pl.BlockSpec((seq_block * 8, SC_WIDTH), seq_map2),
            pl.BlockSpec((seq_block * 8, GD_QKV), seq_map2),
        ],
        out_shape=[
            jax.ShapeDtypeStruct((n_rows, MIX_OUT), jnp.bfloat16),
            jax.ShapeDtypeStruct(s0.shape, jnp.float32),
            jax.ShapeDtypeStruct(scp.shape, jnp.float32),
            jax.ShapeDtypeStruct(gcp.shape, jnp.float32),
        ],
        scratch_shapes=[
            pltpu.VMEM((GD_HEADS, GD_D, GD_D), jnp.float32),
            pltpu.VMEM((8, SC_WIDTH), jnp.float32),
            pltpu.VMEM((8, GD_QKV), jnp.float32),
        ],
        compiler_params=pltpu.CompilerParams(
            dimension_semantics=("arbitrary",) * len(grid), vmem_limit_bytes=VMEM_LIMIT),
        name="mixer_cd_chain" if chain else "mixer_cd_batch",
    )(x, w_main, w_gates, sconv_w, gconv_w, alog8, dtb8, norm_w.reshape(1, -1), s0, scp, gcp)
    tail = lambda st, keep: st.reshape(n_seq, 8, -1)[:, 8 - keep:]
    return mix, s_new, tail(scp_new, SC_K - 1), tail(gcp_new, GD_CONV - 1)


def _outproj_ln_router_kernel(mix_ref, x_ref, w_ref, g_ref, b_ref, rw_ref, rb_ref,
                              x1_ref, xp_ref, tope_ref, gate_ref):
    mix = jnp.dot(mix_ref[...], w_ref[...], preferred_element_type=jnp.float32)
    x1 = _layer_norm_rows(DEEPNORM_ALPHA * x_ref[...] + mix, g_ref[...], b_ref[...])
    x1_ref[...] = x1
    xp_ref[...] = _pack_bf16_pairs(x1)
    logits = jnp.dot(x1, rw_ref[...], preferred_element_type=jnp.float32,
                     precision=lax.Precision.HIGHEST) + rb_ref[...]
    lane = lax.broadcasted_iota(jnp.int32, logits.shape, 1)
    vals, idxs = [], []
    for _ in range(TOP_K):
        m = jnp.max(logits, axis=-1, keepdims=True)
        idx = jnp.min(jnp.where(logits == m, lane, N_EXPERTS), axis=-1, keepdims=True)
        vals.append(m)
        idxs.append(idx)
        logits = jnp.where(lane == idx, -jnp.inf, logits)
    ex = [jnp.exp(v - vals[0]) for v in vals]
    den = ex[0] + ex[1] + ex[2] + ex[3]
    col = lax.broadcasted_iota(jnp.int32, tope_ref.shape, 1)
    tope = jnp.zeros(tope_ref.shape, jnp.int32)
    gate = jnp.zeros(gate_ref.shape, jnp.float32)
    for k in range(TOP_K):
        tope = jnp.where(col == k, idxs[k], tope)
        gate = jnp.where(col == k, ex[k] / den, gate)
    tope_ref[...] = tope
    gate_ref[...] = gate


def _outproj_ln_router(mix, x, w_out, ln_g, ln_b, router_w, router_b):
    n = x.shape[0]
    tm = TOKEN_TILE
    row = lambda i: (i, 0)
    full = lambda i: (0, 0)
    return pl.pallas_call(
        _outproj_ln_router_kernel,
        grid=(n // tm,),
        in_specs=[
            pl.BlockSpec((tm, D_MODEL), row),
            pl.BlockSpec((tm, D_MODEL), row),
            pl.BlockSpec((D_MODEL, D_MODEL), full),
            pl.BlockSpec((1, D_MODEL), full),
            pl.BlockSpec((1, D_MODEL), full),
            pl.BlockSpec((D_MODEL, N_EXPERTS), full),
            pl.BlockSpec((1, N_EXPERTS), full),
        ],
        out_specs=[
            pl.BlockSpec((tm, D_MODEL), row),
            pl.BlockSpec((tm, HALF), row),
            pl.BlockSpec((tm, TOP_K), row),
            pl.BlockSpec((tm, TOP_K), row),
        ],
        out_shape=[
            jax.ShapeDtypeStruct((n, D_MODEL), jnp.float32),
            jax.ShapeDtypeStruct((n, HALF), jnp.uint32),
            jax.ShapeDtypeStruct((n, TOP_K), jnp.int32),
            jax.ShapeDtypeStruct((n, TOP_K), jnp.float32),
        ],
        compiler_params=pltpu.CompilerParams(dimension_semantics=("arbitrary",), vmem_limit_bytes=VMEM_LIMIT),
        name="outproj_ln_router",
    )(mix, x, w_out, ln_g.reshape(1, -1), ln_b.reshape(1, -1), router_w, router_b.reshape(1, -1))


def _route_plan(top_e):
    n = top_e.shape[0]
    a = n * TOP_K
    br, tm, ch = EXPERT_ROWS, TOKEN_TILE, COMBINE_CHUNK
    n_virtual = N_EXPERTS * N_GROUPS
    group_tokens = n // N_GROUPS
    n_blocks = a // br + n_virtual + 1
    n_tiles = n // tm
    max_chunks = (tm * TOP_K) // ch + N_EXPERTS + 1

    flat_e = top_e.reshape(-1)
    tok = jnp.arange(a, dtype=jnp.int32) // TOP_K
    grp = tok // group_tokens
    ve = grp * N_EXPERTS + flat_e
    onehot = (ve[:, None] == jnp.arange(n_virtual, dtype=jnp.int32)[None, :]).astype(jnp.int32)
    csum = jnp.cumsum(onehot, axis=0)
    rank = jnp.take_along_axis(csum, ve[:, None], axis=1)[:, 0] - 1
    counts = csum[-1]
    nblk = (counts + br - 1) // br
    blk_end = jnp.cumsum(nblk)
    row_off = (blk_end - nblk) * br
    dest = row_off[ve] + rank
    row_tok = jnp.zeros((n_blocks * br,), jnp.int32).at[dest].set(tok - grp * group_tokens)
    blocks = jnp.arange(n_blocks, dtype=jnp.int32)
    block_ve = jnp.minimum(jnp.searchsorted(blk_end, blocks, side="right"), n_virtual - 1).astype(jnp.int32)
    block_info = jnp.stack([block_ve % N_EXPERTS, block_ve // N_EXPERTS,
                            (blocks < blk_end[-1]).astype(jnp.int32)])

    tile_first = jnp.arange(n_tiles, dtype=jnp.int32) * (tm * TOP_K)
    csum_excl = jnp.concatenate([jnp.zeros((1, n_virtual), jnp.int32), csum], axis=0)
    before = csum_excl[tile_first]
    after = csum_excl[tile_first + tm * TOP_K]
    tile_grp = (tile_first // TOP_K) // group_tokens
    cols = tile_grp[:, None] * N_EXPERTS + jnp.arange(N_EXPERTS, dtype=jnp.int32)[None, :]
    seg_len = jnp.take_along_axis(after - before, cols, axis=1)
    seg_start = row_off[cols] + jnp.take_along_axis(before, cols, axis=1)
    aligned = seg_start // 8 * 8
    lead = seg_start - aligned
    nch = jnp.where(seg_len > 0, (lead + seg_len + ch - 1) // ch, 0)
    ch_end = jnp.cumsum(nch, axis=1)
    ch_first = ch_end - nch
    n_chunks = ch_end[:, -1]
    cidx = jnp.arange(max_chunks, dtype=jnp.int32)
    owner = jax.vmap(lambda ce: jnp.searchsorted(ce, cidx, side="right"))(ch_end)
    owner = jnp.minimum(owner, N_EXPERTS - 1)
    chunk_src = (jnp.take_along_axis(aligned, owner, axis=1)
                 + (cidx[None, :] - jnp.take_along_axis(ch_first, owner, axis=1)) * ch)
    chunk_src = jnp.where(cidx[None, :] < n_chunks[:, None], chunk_src, 0).astype(jnp.int32)
    tile_of = tok // tm
    stage_base = (ch_first * ch + lead - seg_start)
    pos = stage_base[tile_of, flat_e] + dest
    return dict(row_tok=row_tok.reshape(n_blocks, 1, br), block_info=block_info,
                chunk_src=chunk_src, n_chunks=n_chunks.astype(jnp.int32),
                pos=pos.reshape(n_tiles, 1, tm * TOP_K).astype(jnp.int32))


def _expert_kernel(info_ref, tok_ref, xp_hbm, wgu_ref, bgu_ref, wdn_ref, bdn_ref, y_ref,
                   xp_vmem, stage_ref, load_sem):
    b = pl.program_id(0)
    grp = info_ref[1, b]
    prev_grp = info_ref[1, jnp.maximum(b - 1, 0)]
    group_tokens = xp_vmem.shape[0]

    @pl.when(jnp.logical_or(b == 0, grp != prev_grp))
    def _():
        cp = pltpu.make_async_copy(xp_hbm.at[pl.ds(grp * group_tokens, group_tokens)], xp_vmem, load_sem)
        cp.start()
        cp.wait()

    @pl.when(info_ref[2, b] == 1)
    def _():
        unroll = 8

        def gather(i, carry):
            for u in range(unroll):
                r = i * unroll + u
                t = tok_ref[0, 0, r]
                stage_ref[pl.ds(r, 1), :] = xp_vmem[pl.ds(t, 1), :]
            return carry

        lax.fori_loop(0, EXPERT_ROWS // unroll, gather, 0)
        lo, hi = _unpack_bf16_pairs(stage_ref[...])
        hgu = (jnp.dot(lo, wgu_ref[0, :HALF, :], preferred_element_type=jnp.float32)
               + jnp.dot(hi, wgu_ref[0, HALF:, :], preferred_element_type=jnp.float32) + bgu_ref[0])
        glu = jnp.minimum(hgu[:, :D_FF], SWIGLU_LIMIT)
        lin = jnp.clip(hgu[:, D_FF:], -SWIGLU_LIMIT, SWIGLU_LIMIT)
        act = (lin + 1.0) * glu * jax.nn.sigmoid(SWIGLU_ALPHA * glu)
        y_ref[...] = jnp.dot(act.astype(jnp.bfloat16), wdn_ref[0], preferred_element_type=jnp.float32) + bdn_ref[0]

    @pl.when(info_ref[2, b] == 0)
    def _():
        y_ref[...] = jnp.zeros_like(y_ref)


def _expert_mlp(plan, xp, w_gu, b_gu, w_dn, b_dn):
    n = xp.shape[0]
    br = EXPERT_ROWS
    n_blocks = plan["row_tok"].shape[0]
    assert br >= COMBINE_CHUNK + 8
    pad_rows = n_blocks * br
    expert = lambda b, info: (info[0, b], 0, 0)
    grid_spec = pltpu.PrefetchScalarGridSpec(
        num_scalar_prefetch=1,
        grid=(n_blocks,),
        in_specs=[
            pl.BlockSpec((1, 1, br), lambda b, info: (b, 0, 0), memory_space=pltpu.SMEM),
            pl.BlockSpec(memory_space=pl.ANY),
            pl.BlockSpec((1, D_MODEL, 2 * D_FF), expert),
            pl.BlockSpec((1, 1, 2 * D_FF), expert),
            pl.BlockSpec((1, D_FF, D_MODEL), expert),
            pl.BlockSpec((1, 1, D_MODEL), expert),
        ],
        out_specs=pl.BlockSpec((br, D_MODEL), lambda b, info: (b, 0)),
        scratch_shapes=[
            pltpu.VMEM((n // N_GROUPS, HALF), jnp.uint32),
            pltpu.VMEM((br, HALF), jnp.uint32),
            pltpu.SemaphoreType.DMA(()),
        ],
    )
    return pl.pallas_call(
        _expert_kernel,
        grid_spec=grid_spec,
        out_shape=jax.ShapeDtypeStruct((pad_rows, D_MODEL), jnp.float32),
        compiler_params=pltpu.CompilerParams(dimension_semantics=("arbitrary",), vmem_limit_bytes=VMEM_LIMIT),
        name="expert_mlp",
    )(plan["block_info"], plan["row_tok"], xp, w_gu, b_gu.reshape(N_EXPERTS, 1, -1), w_dn,
      b_dn.reshape(N_EXPERTS, 1, -1))


def _combine_ln_kernel(src_ref, nch_ref, pos_ref, gate_ref, x1_ref, y_hbm, g_ref, b_ref, out_ref,
                       stage_ref, ffn_ref, sems):
    i = pl.program_id(0)
    n_tiles = pl.num_programs(0)
    ch = COMBINE_CHUNK

    def chunk_copy(tile, slot, c):
        src = pl.multiple_of(src_ref[tile, c], 8)
        return pltpu.make_async_copy(y_hbm.at[pl.ds(src, ch)], stage_ref.at[slot, pl.ds(c * ch, ch)], sems.at[slot])

    def issue(tile, slot):
        lax.fori_loop(0, nch_ref[tile], lambda c, carry: (chunk_copy(tile, slot, c).start(), carry)[1], 0)

    @pl.when(i == 0)
    def _():
        issue(0, 0)

    @pl.when(i + 1 < n_tiles)
    def _():
        issue(i + 1, (i + 1) % 2)

    slot = i % 2
    lax.fori_loop(0, nch_ref[i], lambda c, carry: (chunk_copy(i, slot, c).wait(), carry)[1], 0)

    tm = x1_ref.shape[0]
    unroll = 4

    def combine(j, carry):
        for u in range(unroll):
            t = j * unroll + u
            acc = None
            for k in range(TOP_K):
                row = stage_ref[slot, pl.ds(pos_ref[0, 0, t * TOP_K + k], 1), :] * gate_ref[0, 0, t * TOP_K + k]
                acc = row if acc is None else acc + row
            ffn_ref[pl.ds(t, 1), :] = acc
        return carry

    lax.fori_loop(0, tm // unroll, combine, 0)
    out_ref[...] = _layer_norm_rows(DEEPNORM_ALPHA * x1_ref[...] + ffn_ref[...], g_ref[...], b_ref[...])


def _combine_ln(plan, gate, x1, y_sorted, ln_g, ln_b):
    n = x1.shape[0]
    tm = TOKEN_TILE
    n_tiles = n // tm
    max_chunks = plan["chunk_src"].shape[1]
    row = lambda i, src, nch: (i, 0)
    full = lambda i, src, nch: (0, 0)
    per_tile = lambda i, src, nch: (i, 0, 0)
    grid_spec = pltpu.PrefetchScalarGridSpec(
        num_scalar_prefetch=2,
        grid=(n_tiles,),
        in_specs=[
            pl.BlockSpec((1, 1, tm * TOP_K), per_tile, memory_space=pltpu.SMEM),
            pl.BlockSpec((1, 1, tm * TOP_K), per_tile, memory_space=pltpu.SMEM),
            pl.BlockSpec((tm, D_MODEL), row),
            pl.BlockSpec(memory_space=pl.ANY),
            pl.BlockSpec((1, D_MODEL), full),
            pl.BlockSpec((1, D_MODEL), full),
        ],
        out_specs=pl.BlockSpec((tm, D_MODEL), row),
        scratch_shapes=[
            pltpu.VMEM((2, max_chunks * COMBINE_CHUNK, D_MODEL), jnp.float32),
            pltpu.VMEM((tm, D_MODEL), jnp.float32),
            pltpu.SemaphoreType.DMA((2,)),
        ],
    )
    return pl.pallas_call(
        _combine_ln_kernel,
        grid_spec=grid_spec,
        out_shape=jax.ShapeDtypeStruct((n, D_MODEL), jnp.float32),
        compiler_params=pltpu.CompilerParams(dimension_semantics=("arbitrary",), vmem_limit_bytes=VMEM_LIMIT),
        name="combine_ln",
    )(plan["chunk_src"], plan["n_chunks"], plan["pos"], gate.reshape(n_tiles, 1, tm * TOP_K), x1, y_sorted,
      ln_g.reshape(1, -1), ln_b.reshape(1, -1))


def _post_mixer(mix, x, w_out, ln1_g, ln1_b, router_w, router_b, w_gu, b_gu, w_dn, b_dn, ln2_g, ln2_b):
    x1, xp, top_e, gate = _outproj_ln_router(mix, x, w_out, ln1_g, ln1_b, router_w, router_b)
    plan = _route_plan(top_e)
    y_sorted = _expert_mlp(plan, xp, w_gu, b_gu, w_dn, b_dn)
    return _combine_ln(plan, gate, x1, y_sorted, ln2_g, ln2_b)


def kernel(x_prompt, x_sample, state_hgrn, state_pool, state_sconv, state_gdn_conv, state_gdn, w_in_ab, hgrn_lower_bounds, hgrn_norm_w, pool_w, pool_scale, w_out_ab, w_in_cd, sconv_w, gdn_conv_w, gdn_a_log, gdn_dt_bias, gdn_norm_w, w_out_cd, ln1_g, ln1_b, ln2_g, ln2_b, router_w, router_b, w_gu, b_gu, w_dn, b_dn):
    bp, sp, _ = x_prompt.shape
    bs, ss, _ = x_sample.shape
    n_p, n_s = bp * sp, bs * ss
    bf = jnp.bfloat16
    f32 = jnp.float32
    lower_bounds = jnp.cumsum(jax.nn.softmax(hgrn_lower_bounds.astype(f32), axis=0), axis=0)
    x = jnp.concatenate([x_prompt.reshape(n_p, D_MODEL), x_sample.reshape(n_s, D_MODEL)], axis=0)
    zeros_like_prompt = lambda st: jnp.zeros((bp,) + st.shape[2:], f32)

    states = {}
    for l in range(DEPTH):
        xq, xs = x[:n_p], x[n_p:]
        if l % 2 == 0:
            args = (w_in_ab[0].astype(bf), lower_bounds[l], hgrn_norm_w[0], pool_w[0].astype(bf), pool_scale[0])
            mp, hp, pp = _mixer_ab(xq, bp, 0, zeros_like_prompt(state_hgrn), zeros_like_prompt(state_pool), *args)
            ms, hs, ps = _mixer_ab(xs, bs, PAST_LEN, state_hgrn[0], state_pool[0], *args)
            states.update(hp=hp[None], hs=hs[None], pp=pp[None], ps=ps[None])
            w_out = w_out_ab[0]
        else:
            args = (w_in_cd[0][:, :CD_MAIN].astype(bf), w_in_cd[0][:, CD_MAIN:].astype(bf), sconv_w[0], gdn_conv_w[0],
                    gdn_a_log[0], gdn_dt_bias[0], gdn_norm_w[0])
            mp, gp, scp, gcp = _mixer_cd(xq, bp, zeros_like_prompt(state_gdn), zeros_like_prompt(state_sconv),
                                         zeros_like_prompt(state_gdn_conv), *args)
            ms, gs, scs, gcs = _mixer_cd(xs, bs, state_gdn[0], state_sconv[0], state_gdn_conv[0], *args)
            states.update(scp=scp[None], scs=scs[None], gcp=gcp[None], gcs=gcs[None], gp=gp[None], gs=gs[None])
            w_out = w_out_cd[0]
        mix = jnp.concatenate([mp, ms], axis=0)
        x = _post_mixer(mix, x, w_out.astype(bf), ln1_g[l], ln1_b[l], router_w[l], router_b[l],
                        w_gu[l].astype(bf), b_gu[l], w_dn[l].astype(bf), b_dn[l], ln2_g[l], ln2_b[l])
    return (x[:n_p].reshape(bp, sp, D_MODEL), x[n_p:].reshape(bs, ss, D_MODEL),
            states["hp"], states["hs"], states["pp"], states["ps"], states["scp"], states["scs"],
            states["gcp"], states["gcs"], states["gp"], states["gs"])
```

```python
import functools

import jax
import jax.numpy as jnp
from jax import lax
from jax.experimental import pallas as pl
from jax.experimental.pallas import tpu as pltpu

D_MODEL = 1024
DEPTH = 2
N_EXPERTS = 32
TOP_K = 4
D_FF = 1024
SWIGLU_LIMIT = 7.0
SWIGLU_ALPHA = 1.702
LN_EPS = 1e-5
DEEPNORM_ALPHA = (2 * DEPTH) ** 0.25

HALF = D_MODEL // 2
TOKEN_TILE = 256
EXPERT_ROWS = 256
N_GROUPS = 2
COMBINE_CHUNK = 16
VMEM_LIMIT = 56 * 1024 * 1024
MIX_OUT = 1024
BATCH_ROW_TILE = 128
PAST_LEN = 16384


def _layer_norm_rows(v, g, b):
    mu = jnp.mean(v, axis=-1, keepdims=True)
    d = v - mu
    var = jnp.mean(d * d, axis=-1, keepdims=True)
    return d * lax.rsqrt(var + LN_EPS) * g + b


def _pack_bf16_pairs(v):
    lo = pltpu.bitcast(v[:, :HALF].astype(jnp.bfloat16).astype(jnp.float32), jnp.uint32)
    hi = pltpu.bitcast(v[:, HALF:].astype(jnp.bfloat16).astype(jnp.float32), jnp.uint32)
    return (lo >> 16) | (hi & jnp.uint32(0xFFFF0000))


def _unpack_bf16_pairs(p):
    lo = pltpu.bitcast(p << 16, jnp.float32).astype(jnp.bfloat16)
    hi = pltpu.bitcast(p & jnp.uint32(0xFFFF0000), jnp.float32).astype(jnp.bfloat16)
    return lo, hi


ROW_TILE = 256
HG_HEADS, HG_D = 4, 128
HG_WIDTH = HG_HEADS * HG_D
HG_CHUNK = 16
POOL_WINDOWS = (2, 4, 8, 16)
POOL_GC = 128
POOL_WIDTH = 512
RMS_EPS = 1e-6
_HI = lax.Precision.HIGHEST


def _silu(v):
    return v * jax.nn.sigmoid(v)


def _bf(v):
    return v.astype(jnp.bfloat16)


def _dot(a, b, precision=None):
    return jnp.dot(a, b, preferred_element_type=jnp.float32, precision=precision)


def _dot_nt(a, b, precision=None):
    return lax.dot_general(a, b, (((1,), (1,)), ((), ())), preferred_element_type=jnp.float32, precision=precision)


def _dot_tn(a, b, precision=None):
    return lax.dot_general(a, b, (((0,), (0,)), ((), ())), preferred_element_type=jnp.float32, precision=precision)


def _chunk_masks(rows, chunk):
    shift = chunk.bit_length() - 1
    t = lax.broadcasted_iota(jnp.int32, (rows, rows), 0)
    s = lax.broadcasted_iota(jnp.int32, (rows, rows), 1)
    same = (t >> shift) == (s >> shift)
    return same, jnp.logical_and(same, s <= t), jnp.logical_and(same, s < t)


def _shift_rows_chain(cur, prev, j):
    if j == 0:
        return cur
    p = prev.shape[0]
    rc = pltpu.roll(cur, j, 0)
    rp = prev if j == p else pltpu.roll(prev, j, 0)
    row = lax.broadcasted_iota(jnp.int32, (p, cur.shape[1]), 0)
    head = jnp.where(row < j, rp, rc[:p])
    return jnp.concatenate([head, rc[p:]], axis=0) if cur.shape[0] > p else head


def _shift_rows_batch(cur, prev, j):
    if j == 0:
        return cur
    if j == 8:
        return prev
    rows = cur.shape[0]
    row = lax.broadcasted_iota(jnp.int32, cur.shape, 0)
    return jnp.where((row & 7) < j, pltpu.roll(prev, rows + j - 8, 0), pltpu.roll(cur, j, 0))


def _rms_gate(o, norm_w, gate):
    ms = jnp.mean(o * o, axis=-1, keepdims=True)
    return o * lax.rsqrt(ms + RMS_EPS) * norm_w * _silu(gate)


def _mixer_ab_kernel(x_ref, w_ref, lb_ref, nw_ref, pw_ref, ps_ref, s0_ref, pa_ref, pb_ref,
                     mix_ref, s_out_ref, pa_out_ref, pb_out_ref, state_ref, prev_ref, *, chain, pos0):
    rows = x_ref.shape[0]
    chunk = HG_CHUNK if chain else 8
    n_chunks = rows // chunk
    j = pl.program_id(1) if chain else 0

    proj = _dot(_bf(x_ref[...]), w_ref[...])
    q = _silu(proj[:, 0:HG_WIDTH])
    f = lb_ref[...] + (1.0 - lb_ref[...]) * jax.nn.sigmoid(proj[:, HG_WIDTH:2 * HG_WIDTH])
    v = _silu(proj[:, 2 * HG_WIDTH:3 * HG_WIDTH])
    gate = proj[:, 3 * HG_WIDTH:4 * HG_WIDTH]
    u = proj[:, 4 * HG_WIDTH:]
    log_f = jnp.log(f)
    k = 1.0 - f

    same, incl, _ = _chunk_masks(rows, chunk)
    cum = _dot(incl.astype(jnp.float32), log_f, _HI)
    total = _dot(same.astype(jnp.float32), log_f, _HI)
    q_dec = q * jnp.exp(cum)
    k_dec = k * jnp.exp(-cum)
    k_end = k * jnp.exp(total - cum)
    chunk_of_row = lax.broadcasted_iota(jnp.int32, (rows, 128), 0) >> (chunk.bit_length() - 1)
    onehot = (chunk_of_row == lax.broadcasted_iota(jnp.int32, (rows, 128), 1)).astype(jnp.float32)
    decay_cols = jnp.exp(_dot_tn(log_f, onehot, _HI))

    if chain:
        @pl.when(j == 0)
        def _():
            state_ref[...] = s0_ref[0]
            prev_ref[...] = jnp.concatenate([pa_ref[...], pb_ref[...]], axis=0)

    outs = []
    for h in range(HG_HEADS):
        sl = slice(h * HG_D, (h + 1) * HG_D)
        att = jnp.where(incl, _dot_nt(_bf(q_dec[:, sl]), _bf(k_dec[:, sl])), 0.0)
        o = _dot(_bf(att), _bf(v[:, sl]))
        inter = []
        s = state_ref[h] if chain else None
        for c in range(n_chunks):
            rs = slice(c * chunk, (c + 1) * chunk)
            if not chain:
                s = s0_ref[c, h]
            inter.append(_dot(_bf(q_dec[rs, sl]), _bf(s)))
            s = decay_cols[sl, c:c + 1] * s + _dot_tn(_bf(k_end[rs, sl]), _bf(v[rs, sl]))
            if not chain:
                s_out_ref[c, h] = s
        if chain:
            state_ref[h] = s
        o = o + jnp.concatenate(inter, axis=0)
        outs.append(_rms_gate(o, nw_ref[...], gate[:, sl]))

    if chain:
        prev = prev_ref[...]
        t = j * rows + lax.broadcasted_iota(jnp.int32, (rows, POOL_GC), 0)
    else:
        prev_a, prev_b = pa_ref[...], pb_ref[...]
        t = lax.broadcasted_iota(jnp.int32, (rows, POOL_GC), 0) & 7
    for gi, win in enumerate(POOL_WINDOWS):
        gs = slice(gi * POOL_GC, (gi + 1) * POOL_GC)
        ug = u[:, gs]
        wsum = ug
        for d in range(1, win):
            if chain:
                wsum = wsum + _shift_rows_chain(ug, prev[:, gs], d)
            elif d <= 8:
                wsum = wsum + _shift_rows_batch(ug, prev_b[:, gs], d)
            else:
                wsum = wsum + _shift_rows_batch(prev_b[:, gs], prev_a[:, gs], d - 8)
        cnt = jnp.minimum(win, pos0 + t + 1).astype(jnp.float32)
        diff = wsum / cnt - ug
        outs.append(_dot(_bf(diff), pw_ref[gi]) * ps_ref[:, gs])
    mix_ref[...] = jnp.concatenate(outs, axis=-1).astype(mix_ref.dtype)

    if chain:
        prev_ref[...] = u[rows - 16:]

        @pl.when(j == pl.num_programs(1) - 1)
        def _():
            s_out_ref[0] = state_ref[...]
            pa_out_ref[...] = u[rows - 16:rows - 8]
            pb_out_ref[...] = u[rows - 8:]
    else:
        pa_out_ref[...] = prev_b
        pb_out_ref[...] = u


def _mixer_ab(x, row0, n_seq, length, pos0, s0, pool_state, w_in, lb, norm_w, pool_w, pool_scale):
    n_rows = n_seq * length
    chain = length >= ROW_TILE
    rows = ROW_TILE if chain else BATCH_ROW_TILE
    assert row0 % rows == 0
    first = row0 // rows
    pool16 = jnp.pad(pool_state, ((0, 0), (1, 0), (0, 0)))
    pa = pool16[:, :8].reshape(n_seq * 8, POOL_WIDTH)
    pb = pool16[:, 8:].reshape(n_seq * 8, POOL_WIDTH)
    if chain:
        assert length % rows == 0
        tiles = length // rows
        grid = (n_seq, tiles)
        row_map = lambda b, j: (b * tiles + j, 0)
        x_map = lambda b, j: (first + b * tiles + j, 0)
        seq_map4 = lambda b, j: (b, 0, 0, 0)
        seq_map2 = lambda b, j: (b, 0)
        const2 = lambda b, j: (0, 0)
        const3 = lambda b, j: (0, 0, 0)
        seq_block = 1
    else:
        assert length == 8 and n_rows % rows == 0
        seq_block = rows // 8
        grid = (n_rows // rows,)
        row_map = lambda i: (i, 0)
        x_map = lambda i: (first + i, 0)
        seq_map4 = lambda i: (i, 0, 0, 0)
        seq_map2 = lambda i: (i, 0)
        const2 = lambda i: (0, 0)
        const3 = lambda i: (0, 0, 0)
    d_in = w_in.shape[1]
    mix, s_new, pa_new, pb_new = pl.pallas_call(
        functools.partial(_mixer_ab_kernel, chain=chain, pos0=pos0),
        grid=grid,
        in_specs=[
            pl.BlockSpec((rows, D_MODEL), x_map),
            pl.BlockSpec((D_MODEL, d_in), const2),
            pl.BlockSpec((1, HG_WIDTH), const2),
            pl.BlockSpec((1, HG_D), const2),
            pl.BlockSpec((len(POOL_WINDOWS), POOL_GC, POOL_GC), const3),
            pl.BlockSpec((1, POOL_WIDTH), const2),
            pl.BlockSpec((seq_block, HG_HEADS, HG_D, HG_D), seq_map4),
            pl.BlockSpec((seq_block * 8, POOL_WIDTH), seq_map2),
            pl.BlockSpec((seq_block * 8, POOL_WIDTH), seq_map2),
        ],
        out_specs=[
            pl.BlockSpec((rows, MIX_OUT), row_map),
            pl.BlockSpec((seq_block, HG_HEADS, HG_D, HG_D), seq_map4),
            pl.BlockSpec((seq_block * 8, POOL_WIDTH), seq_map2),
            pl.BlockSpec((seq_block * 8, POOL_WIDTH), seq_map2),
        ],
        out_shape=[
            jax.ShapeDtypeStruct((n_rows, MIX_OUT), jnp.bfloat16),
            jax.ShapeDtypeStruct(s0.shape, jnp.float32),
            jax.ShapeDtypeStruct(pa.shape, jnp.float32),
            jax.ShapeDtypeStruct(pb.shape, jnp.float32),
        ],
        scratch_shapes=[
            pltpu.VMEM((HG_HEADS, HG_D, HG_D), jnp.float32),
            pltpu.VMEM((16, POOL_WIDTH), jnp.float32),
        ],
        compiler_params=pltpu.CompilerParams(
            dimension_semantics=("arbitrary",) * len(grid), vmem_limit_bytes=VMEM_LIMIT),
        name="mixer_ab_chain" if chain else "mixer_ab_batch",
    )(x, w_in, lb.reshape(1, -1), norm_w.reshape(1, -1), pool_w, pool_scale.reshape(1, -1), s0, pa, pb)
    pool_new = jnp.concatenate([pa_new.reshape(n_seq, 8, POOL_WIDTH), pb_new.reshape(n_seq, 8, POOL_WIDTH)], axis=1)
    return mix, s_new, pool_new[:, 1:]


SC_WIDTH, SC_K = 512, 3
GD_HEADS, GD_D = 4, 128
GD_CONV = 4
GD_CHUNK = 64
GD_QKV = GD_HEADS * 3 * GD_D
CD_MAIN = 3 * SC_WIDTH + GD_QKV + GD_HEADS * GD_D


def _causal_conv(cur, prev, w_ref, shift_fn):
    width = w_ref.shape[0]
    acc = cur * w_ref[width - 1:width, :]
    for j in range(width - 1):
        acc = acc + shift_fn(cur, prev, width - 1 - j) * w_ref[j:j + 1, :]
    return acc


def _softplus(v):
    return jnp.maximum(v, 0.0) + jnp.log1p(jnp.exp(-jnp.abs(v)))


def _unit_lower_inverse(lower, chunk):
    rows = lower.shape[0]
    eye = (lax.broadcasted_iota(jnp.int32, (rows, rows), 0)
           == lax.broadcasted_iota(jnp.int32, (rows, rows), 1)).astype(jnp.float32)
    power = -lower
    inv = eye + power
    for _ in range(chunk.bit_length() - 2):
        power = _dot(power, power, _HI)
        inv = inv + _dot(inv, power, _HI)
    return inv


def _mixer_cd_kernel(x_ref, w_ref, wg_ref, scw_ref, gcw_ref, alog_ref, dtb_ref, nw_ref, s0_ref, scp_ref, gcp_ref,
                     mix_ref, s_out_ref, scp_out_ref, gcp_out_ref, state_ref, sc_prev_ref, gc_prev_ref, *, chain):
    rows = x_ref.shape[0]
    chunk = GD_CHUNK if chain else 8
    n_chunks = rows // chunk
    j = pl.program_id(1) if chain else 0
    shift_fn = _shift_rows_chain if chain else _shift_rows_batch

    xb = _bf(x_ref[...])
    proj = _dot(xb, w_ref[...])
    gates = _dot(xb, wg_ref[...])
    b_gate = proj[:, 0:SC_WIDTH]
    conv_in = proj[:, SC_WIDTH:2 * SC_WIDTH] * proj[:, 2 * SC_WIDTH:3 * SC_WIDTH]
    qkv = proj[:, 3 * SC_WIDTH:3 * SC_WIDTH + GD_QKV]
    z = proj[:, 3 * SC_WIDTH + GD_QKV:]

    if chain:
        @pl.when(j == 0)
        def _():
            state_ref[...] = s0_ref[0]
            sc_prev_ref[...] = scp_ref[...]
            gc_prev_ref[...] = gcp_ref[...]
        sc_prev, gc_prev = sc_prev_ref[...], gc_prev_ref[...]
    else:
        sc_prev, gc_prev = scp_ref[...], gcp_ref[...]

    outs = [b_gate * _causal_conv(conv_in, sc_prev, scw_ref, shift_fn)]
    qkv_c = _silu(_causal_conv(qkv, gc_prev, gcw_ref, shift_fn))

    lane8 = lax.broadcasted_iota(jnp.int32, gates.shape, 1)
    g_dec = -jnp.exp(alog_ref[...]) * _softplus(gates + dtb_ref[...])
    cols = jnp.where(lane8 < GD_HEADS, g_dec, jax.nn.sigmoid(gates))
    same, incl, strict = _chunk_masks(rows, chunk)
    gcum = _dot(incl.astype(jnp.float32), cols, _HI)
    gtot = _dot(same.astype(jnp.float32), cols, _HI)
    eye8 = (lax.broadcasted_iota(jnp.int32, (8, 8), 0) == lax.broadcasted_iota(jnp.int32, (8, 8), 1)).astype(jnp.float32)
    gcum_rows = _dot_nt(eye8, gcum, _HI)

    for h in range(GD_HEADS):
        sl = lambda part: slice(part * GD_HEADS * GD_D + h * GD_D, part * GD_HEADS * GD_D + (h + 1) * GD_D)
        q, k, v = qkv_c[:, sl(0)], qkv_c[:, sl(1)], qkv_c[:, sl(2)]
        q = q * lax.rsqrt(jnp.sum(q * q, axis=-1, keepdims=True) + 1e-6) * (GD_D ** -0.5)
        k = k * lax.rsqrt(jnp.sum(k * k, axis=-1, keepdims=True) + 1e-6)
        beta = cols[:, GD_HEADS + h:GD_HEADS + h + 1]
        gc = gcum[:, h:h + 1]
        gt = gtot[:, h:h + 1]
        decay = jnp.where(incl, jnp.exp(jnp.where(incl, gc - gcum_rows[h:h + 1, :], 0.0)), 0.0)
        k_beta = k * beta
        kb = _bf(k)
        lower = jnp.where(strict, _dot_nt(_bf(k_beta), kb) * decay, 0.0)
        att = jnp.where(incl, _dot_nt(_bf(q), kb) * decay, 0.0)
        inv = _unit_lower_inverse(lower, chunk)
        egc = jnp.exp(gc)
        u = _dot(inv, v * beta, _HI)
        w = _dot(inv, k_beta * egc, _HI)
        q_dec = q * egc
        k_end = k * jnp.exp(gt - gc)
        decay_end = jnp.exp(gt)

        inter, v_news = [], []
        s = state_ref[h] if chain else None
        for c in range(n_chunks):
            rs = slice(c * chunk, (c + 1) * chunk)
            if not chain:
                s = s0_ref[c, h]
            sb = _bf(s)
            v_new = u[rs] - _dot(_bf(w[rs]), sb)
            inter.append(_dot(_bf(q_dec[rs]), sb))
            s = decay_end[c * chunk:c * chunk + 1, :] * s + _dot_tn(_bf(k_end[rs]), _bf(v_new))
            v_news.append(v_new)
            if not chain:
                s_out_ref[c, h] = s
        if chain:
            state_ref[h] = s
        o = jnp.concatenate(inter, axis=0) + _dot(_bf(att), _bf(jnp.concatenate(v_news, axis=0)))
        outs.append(_rms_gate(o, nw_ref[...], z[:, h * GD_D:(h + 1) * GD_D]))
    mix_ref[...] = jnp.concatenate(outs, axis=-1).astype(mix_ref.dtype)

    if chain:
        sc_prev_ref[...] = conv_in[rows - 8:]
        gc_prev_ref[...] = qkv[rows - 8:]

        @pl.when(j == pl.num_programs(1) - 1)
        def _():
            s_out_ref[0] = state_ref[...]
            scp_out_ref[...] = conv_in[rows - 8:]
            gcp_out_ref[...] = qkv[rows - 8:]
    else:
        scp_out_ref[...] = conv_in
        gcp_out_ref[...] = qkv


def _mixer_cd(x, row0, n_seq, length, s0, sconv_state, gconv_state, w_main, w_gates, sconv_w, gconv_w, a_log, dt_bias,
              norm_w):
    n_rows = n_seq * length
    chain = length >= ROW_TILE
    rows = ROW_TILE if chain else BATCH_ROW_TILE
    assert row0 % rows == 0
    first = row0 // rows
    pad8 = lambda st: jnp.pad(st, ((0, 0), (8 - st.shape[1], 0), (0, 0))).reshape(n_seq * 8, st.shape[2])
    scp, gcp = pad8(sconv_state), pad8(gconv_state)
    if chain:
        assert length % rows == 0
        tiles = length // rows
        grid = (n_seq, tiles)
        row_map = lambda b, j: (b * tiles + j, 0)
        x_map = lambda b, j: (first + b * tiles + j, 0)
        seq_map4 = lambda b, j: (b, 0, 0, 0)
        seq_map2 = lambda b, j: (b, 0)
        const2 = lambda b, j: (0, 0)
        seq_block = 1
    else:
        assert length == 8 and n_rows % rows == 0
        seq_block = rows // 8
        grid = (n_rows // rows,)
        row_map = lambda i: (i, 0)
        x_map = lambda i: (first + i, 0)
        seq_map4 = lambda i: (i, 0, 0, 0)
        seq_map2 = lambda i: (i, 0)
        const2 = lambda i: (0, 0)
    zeros4 = jnp.zeros((GD_HEADS,), jnp.float32)
    alog8 = jnp.concatenate([a_log.astype(jnp.float32), zeros4]).reshape(1, 8)
    dtb8 = jnp.concatenate([dt_bias.astype(jnp.float32), zeros4]).reshape(1, 8)
    mix, s_new, scp_new, gcp_new = pl.pallas_call(
        functools.partial(_mixer_cd_kernel, chain=chain),
        grid=grid,
        in_specs=[
            pl.BlockSpec((rows, D_MODEL), x_map),
            pl.BlockSpec((D_MODEL, CD_MAIN), const2),
            pl.BlockSpec((D_MODEL, 8), const2),
            pl.BlockSpec((SC_K, SC_WIDTH), const2),
            pl.BlockSpec((GD_CONV, GD_QKV), const2),
            pl.BlockSpec((1, 8), const2),
            pl.BlockSpec((1, 8), const2),
            pl.BlockSpec((1, GD_D), const2),
            pl.BlockSpec((seq_block, GD_HEADS, GD_D, GD_D), seq_map4),
            pl.BlockSpec((seq_block * 8, SC_WIDTH), seq_map2),
            pl.BlockSpec((seq_block * 8, GD_QKV), seq_map2),
        ],
        out_specs=[
            pl.BlockSpec((rows, MIX_OUT), row_map),
            pl.BlockSpec((seq_block, GD_HEADS, GD_D, GD_D), seq_map4),
            pl.BlockSpec((seq_block * 8, SC_WIDTH), seq_map2),
            pl.BlockSpec((seq_block * 8, GD_QKV), seq_map2),
        ],
        out_shape=[
            jax.ShapeDtypeStruct((n_rows, MIX_OUT), jnp.bfloat16),
            jax.ShapeDtypeStruct(s0.shape, jnp.float32),
            jax.ShapeDtypeStruct(scp.shape, jnp.float32),
            jax.ShapeDtypeStruct(gcp.shape, jnp.float32),
        ],
        scratch_shapes=[
            pltpu.VMEM((GD_HEADS, GD_D, GD_D), jnp.float32),
            pltpu.VMEM((8, SC_WIDTH), jnp.float32),
            pltpu.VMEM((8, GD_QKV), jnp.float32),
        ],
        compiler_params=pltpu.CompilerParams(
            dimension_semantics=("arbitrary",) * len(grid), vmem_limit_bytes=VMEM_LIMIT),
        name="mixer_cd_chain" if chain else "mixer_cd_batch",
    )(x, w_main, w_gates, sconv_w, gconv_w, alog8, dtb8, norm_w.reshape(1, -1), s0, scp, gcp)
    tail = lambda st, keep: st.reshape(n_seq, 8, -1)[:, 8 - keep:]
    return mix, s_new, tail(scp_new, SC_K - 1), tail(gcp_new, GD_CONV - 1)


def _outproj_ln_router_kernel(mix_ref, x_ref, w_ref, g_ref, b_ref, rw_ref, rb_ref,
                              x1_ref, xp_ref, tope_ref, gate_ref, rank_ref, cnt_ref):
    mix = jnp.dot(mix_ref[...], w_ref[...], preferred_element_type=jnp.float32)
    x1 = _layer_norm_rows(DEEPNORM_ALPHA * x_ref[...] + mix, g_ref[...], b_ref[...])
    x1_ref[...] = x1
    xp_ref[...] = _pack_bf16_pairs(x1)
    logits = jnp.dot(x1, rw_ref[...], preferred_element_type=jnp.float32,
                     precision=lax.Precision.HIGHEST) + rb_ref[...]
    lane = lax.broadcasted_iota(jnp.int32, logits.shape, 1)
    vals, idxs = [], []
    for _ in range(TOP_K):
        m = jnp.max(logits, axis=-1, keepdims=True)
        idx = jnp.min(jnp.where(logits == m, lane, N_EXPERTS), axis=-1, keepdims=True)
        vals.append(m)
        idxs.append(idx)
        logits = jnp.where(lane == idx, -jnp.inf, logits)
    ex = [jnp.exp(v - vals[0]) for v in vals]
    den = ex[0] + ex[1] + ex[2] + ex[3]
    tm = logits.shape[0]
    hits = [lane == idx for idx in idxs]
    member = sum(h.astype(jnp.float32) for h in hits)
    earlier = (lax.broadcasted_iota(jnp.int32, (tm, tm), 1) < lax.broadcasted_iota(jnp.int32, (tm, tm), 0))
    before = _dot(earlier.astype(jnp.bfloat16), member.astype(jnp.bfloat16))
    cnt_ref[0] = jnp.sum(member, axis=0, keepdims=True).astype(jnp.int32)
    col = lax.broadcasted_iota(jnp.int32, tope_ref.shape, 1)
    tope = jnp.zeros(tope_ref.shape, jnp.int32)
    gate = jnp.zeros(gate_ref.shape, jnp.float32)
    rank = jnp.zeros(rank_ref.shape, jnp.int32)
    for k in range(TOP_K):
        tope = jnp.where(col == k, idxs[k], tope)
        gate = jnp.where(col == k, ex[k] / den, gate)
        rank_k = jnp.sum(jnp.where(hits[k], before, 0.0), axis=-1, keepdims=True).astype(jnp.int32)
        rank = jnp.where(col == k, rank_k, rank)
    tope_ref[...] = tope
    gate_ref[...] = gate
    rank_ref[...] = rank


def _outproj_ln_router(mix, x, w_out, ln_g, ln_b, router_w, router_b):
    n = x.shape[0]
    tm = TOKEN_TILE
    row = lambda i: (i, 0)
    full = lambda i: (0, 0)
    return pl.pallas_call(
        _outproj_ln_router_kernel,
        grid=(n // tm,),
        in_specs=[
            pl.BlockSpec((tm, D_MODEL), row),
            pl.BlockSpec((tm, D_MODEL), row),
            pl.BlockSpec((D_MODEL, D_MODEL), full),
            pl.BlockSpec((1, D_MODEL), full),
            pl.BlockSpec((1, D_MODEL), full),
            pl.BlockSpec((D_MODEL, N_EXPERTS), full),
            pl.BlockSpec((1, N_EXPERTS), full),
        ],
        out_specs=[
            pl.BlockSpec((tm, D_MODEL), row),
            pl.BlockSpec((tm, HALF), row),
            pl.BlockSpec((tm, TOP_K), row),
            pl.BlockSpec((tm, TOP_K), row),
            pl.BlockSpec((tm, TOP_K), row),
            pl.BlockSpec((1, 1, N_EXPERTS), lambda i: (i, 0, 0)),
        ],
        out_shape=[
            jax.ShapeDtypeStruct((n, D_MODEL), jnp.float32),
            jax.ShapeDtypeStruct((n, HALF), jnp.uint32),
            jax.ShapeDtypeStruct((n, TOP_K), jnp.int32),
            jax.ShapeDtypeStruct((n, TOP_K), jnp.float32),
            jax.ShapeDtypeStruct((n, TOP_K), jnp.int32),
            jax.ShapeDtypeStruct((n // tm, 1, N_EXPERTS), jnp.int32),
        ],
        compiler_params=pltpu.CompilerParams(dimension_semantics=("arbitrary",), vmem_limit_bytes=VMEM_LIMIT),
        name="outproj_ln_router",
    )(mix, x, w_out, ln_g.reshape(1, -1), ln_b.reshape(1, -1), router_w, router_b.reshape(1, -1))


def _route_plan(top_e, rank, tile_counts):
    n = top_e.shape[0]
    a = n * TOP_K
    br, tm, ch = EXPERT_ROWS, TOKEN_TILE, COMBINE_CHUNK
    n_virtual = N_EXPERTS * N_GROUPS
    group_tokens = n // N_GROUPS
    n_blocks = a // br + n_virtual + 1
    n_tiles = n // tm
    group_tiles = n_tiles // N_GROUPS
    max_chunks = (tm * TOP_K) // ch + N_EXPERTS + 1
    i32 = jnp.int32

    cnt = tile_counts.reshape(N_GROUPS, group_tiles, N_EXPERTS)
    nblk = (cnt.sum(axis=1) + br - 1) // br
    blk_end = jnp.cumsum(nblk.reshape(-1))
    row_off = ((blk_end - nblk.reshape(-1)) * br).reshape(N_GROUPS, 1, N_EXPERTS)
    seg_start = (row_off + jnp.cumsum(cnt, axis=1) - cnt).reshape(n_tiles, N_EXPERTS)
    seg_len = tile_counts
    blocks = jnp.arange(n_blocks, dtype=i32)
    block_ve = jnp.minimum(jnp.sum(blk_end[None, :] <= blocks[:, None], axis=1), n_virtual - 1).astype(i32)
    block_info = jnp.stack([block_ve % N_EXPERTS, block_ve // N_EXPERTS,
                            (blocks < blk_end[-1]).astype(i32)])

    aligned = seg_start // 8 * 8
    lead = seg_start - aligned
    nch = jnp.where(seg_len > 0, (lead + seg_len + ch - 1) // ch, 0)
    ch_end = jnp.cumsum(nch, axis=1)
    ch_first = ch_end - nch
    n_chunks = ch_end[:, -1]
    cidx = jnp.arange(max_chunks, dtype=i32)[None, :, None]
    owns = jnp.logical_and(ch_first[:, None, :] <= cidx, cidx < ch_end[:, None, :])
    chunk_src = jnp.sum(jnp.where(owns, aligned[:, None, :] + (cidx - ch_first[:, None, :]) * ch, 0), axis=2)

    hit = top_e.reshape(n_tiles, tm, TOP_K, 1) == jnp.arange(N_EXPERTS, dtype=i32)
    pick = lambda table: jnp.sum(jnp.where(hit, table[:, None, None, :], 0), axis=-1)
    rank = rank.reshape(n_tiles, tm, TOP_K)
    dest = pick(seg_start) + rank
    pos = pick(ch_first * ch + lead) + rank
    tok = (jnp.arange(n, dtype=i32) % group_tokens).reshape(n_tiles, tm, 1)
    row_tok = jnp.zeros((n_blocks * br,), i32).at[dest.reshape(-1)].set(jnp.broadcast_to(tok, dest.shape).reshape(-1))
    return dict(row_tok=row_tok.reshape(n_blocks, 1, br), block_info=block_info,
                chunk_src=chunk_src.astype(i32), n_chunks=n_chunks.astype(i32),
                pos=pos.reshape(n_tiles, 1, tm * TOP_K).astype(i32))


def _expert_kernel(info_ref, tok_ref, xp_hbm, wgu_ref, bgu_ref, wdn_ref, bdn_ref, y_ref,
                   xp_vmem, stage_ref, load_sem):
    b = pl.program_id(0)
    grp = info_ref[1, b]
    prev_grp = info_ref[1, jnp.maximum(b - 1, 0)]
    group_tokens = xp_vmem.shape[0]

    @pl.when(jnp.logical_or(b == 0, grp != prev_grp))
    def _():
        cp = pltpu.make_async_copy(xp_hbm.at[pl.ds(grp * group_tokens, group_tokens)], xp_vmem, load_sem)
        cp.start()
        cp.wait()

    @pl.when(info_ref[2, b] == 1)
    def _():
        unroll = 8

        def gather(i, carry):
            for u in range(unroll):
                r = i * unroll + u
                t = tok_ref[0, 0, r]
                stage_ref[pl.ds(r, 1), :] = xp_vmem[pl.ds(t, 1), :]
            return carry

        lax.fori_loop(0, EXPERT_ROWS // unroll, gather, 0)
        lo, hi = _unpack_bf16_pairs(stage_ref[...])
        hgu = (jnp.dot(lo, wgu_ref[0, :HALF, :], preferred_element_type=jnp.float32)
               + jnp.dot(hi, wgu_ref[0, HALF:, :], preferred_element_type=jnp.float32) + bgu_ref[0])
        glu = jnp.minimum(hgu[:, :D_FF], SWIGLU_LIMIT)
        lin = jnp.clip(hgu[:, D_FF:], -SWIGLU_LIMIT, SWIGLU_LIMIT)
        act = (lin + 1.0) * glu * jax.nn.sigmoid(SWIGLU_ALPHA * glu)
        y_ref[...] = jnp.dot(act.astype(jnp.bfloat16), wdn_ref[0], preferred_element_type=jnp.float32) + bdn_ref[0]

    @pl.when(info_ref[2, b] == 0)
    def _():
        y_ref[...] = jnp.zeros_like(y_ref)


def _expert_mlp(plan, xp, w_gu, b_gu, w_dn, b_dn):
    n = xp.shape[0]
    br = EXPERT_ROWS
    n_blocks = plan["row_tok"].shape[0]
    assert br >= COMBINE_CHUNK + 8
    pad_rows = n_blocks * br
    expert = lambda b, info: (info[0, b], 0, 0)
    grid_spec = pltpu.PrefetchScalarGridSpec(
        num_scalar_prefetch=1,
        grid=(n_blocks,),
        in_specs=[
            pl.BlockSpec((1, 1, br), lambda b, info: (b, 0, 0), memory_space=pltpu.SMEM),
            pl.BlockSpec(memory_space=pl.ANY),
            pl.BlockSpec((1, D_MODEL, 2 * D_FF), expert),
            pl.BlockSpec((1, 1, 2 * D_FF), expert),
            pl.BlockSpec((1, D_FF, D_MODEL), expert),
            pl.BlockSpec((1, 1, D_MODEL), expert),
        ],
        out_specs=pl.BlockSpec((br, D_MODEL), lambda b, info: (b, 0)),
        scratch_shapes=[
            pltpu.VMEM((n // N_GROUPS, HALF), jnp.uint32),
            pltpu.VMEM((br, HALF), jnp.uint32),
            pltpu.SemaphoreType.DMA(()),
        ],
    )
    return pl.pallas_call(
        _expert_kernel,
        grid_spec=grid_spec,
        out_shape=jax.ShapeDtypeStruct((pad_rows, D_MODEL), jnp.float32),
        compiler_params=pltpu.CompilerParams(dimension_semantics=("arbitrary",), vmem_limit_bytes=VMEM_LIMIT),
        name="expert_mlp",
    )(plan["block_info"], plan["row_tok"], xp, w_gu, b_gu.reshape(N_EXPERTS, 1, -1), w_dn,
      b_dn.reshape(N_EXPERTS, 1, -1))


def _combine_ln_kernel(src_ref, nch_ref, pos_ref, gate_ref, x1_ref, y_hbm, g_ref, b_ref, out_ref,
                       stage_ref, ffn_ref, sems):
    i = pl.program_id(0)
    n_tiles = pl.num_programs(0)
    ch = COMBINE_CHUNK

    def chunk_copy(tile, slot, c):
        src = pl.multiple_of(src_ref[tile, c], 8)
        return pltpu.make_async_copy(y_hbm.at[pl.ds(src, ch)], stage_ref.at[slot, pl.ds(c * ch, ch)], sems.at[slot])

    def issue(tile, slot):
        lax.fori_loop(0, nch_ref[tile], lambda c, carry: (chunk_copy(tile, slot, c).start(), carry)[1], 0)

    @pl.when(i == 0)
    def _():
        issue(0, 0)

    @pl.when(i + 1 < n_tiles)
    def _():
        issue(i + 1, (i + 1) % 2)

    slot = i % 2
    lax.fori_loop(0, nch_ref[i], lambda c, carry: (chunk_copy(i, slot, c).wait(), carry)[1], 0)

    tm = x1_ref.shape[0]
    unroll = 4

    def combine(j, carry):
        for u in range(unroll):
            t = j * unroll + u
            acc = None
            for k in range(TOP_K):
                row = stage_ref[slot, pl.ds(pos_ref[0, 0, t * TOP_K + k], 1), :] * gate_ref[0, 0, t * TOP_K + k]
                acc = row if acc is None else acc + row
            ffn_ref[pl.ds(t, 1), :] = acc
        return carry

    lax.fori_loop(0, tm // unroll, combine, 0)
    out_ref[...] = _layer_norm_rows(DEEPNORM_ALPHA * x1_ref[...] + ffn_ref[...], g_ref[...], b_ref[...])


def _combine_ln(plan, gate, x1, y_sorted, ln_g, ln_b):
    n = x1.shape[0]
    tm = TOKEN_TILE
    n_tiles = n // tm
    max_chunks = plan["chunk_src"].shape[1]
    row = lambda i, src, nch: (i, 0)
    full = lambda i, src, nch: (0, 0)
    per_tile = lambda i, src, nch: (i, 0, 0)
    grid_spec = pltpu.PrefetchScalarGridSpec(
        num_scalar_prefetch=2,
        grid=(n_tiles,),
        in_specs=[
            pl.BlockSpec((1, 1, tm * TOP_K), per_tile, memory_space=pltpu.SMEM),
            pl.BlockSpec((1, 1, tm * TOP_K), per_tile, memory_space=pltpu.SMEM),
            pl.BlockSpec((tm, D_MODEL), row),
            pl.BlockSpec(memory_space=pl.ANY),
            pl.BlockSpec((1, D_MODEL), full),
            pl.BlockSpec((1, D_MODEL), full),
        ],
        out_specs=pl.BlockSpec((tm, D_MODEL), row),
        scratch_shapes=[
            pltpu.VMEM((2, max_chunks * COMBINE_CHUNK, D_MODEL), jnp.float32),
            pltpu.VMEM((tm, D_MODEL), jnp.float32),
            pltpu.SemaphoreType.DMA((2,)),
        ],
    )
    return pl.pallas_call(
        _combine_ln_kernel,
        grid_spec=grid_spec,
        out_shape=jax.ShapeDtypeStruct((n, D_MODEL), jnp.float32),
        compiler_params=pltpu.CompilerParams(dimension_semantics=("arbitrary",), vmem_limit_bytes=VMEM_LIMIT),
        name="combine_ln",
    )(plan["chunk_src"], plan["n_chunks"], plan["pos"], gate.reshape(n_tiles, 1, tm * TOP_K), x1, y_sorted,
      ln_g.reshape(1, -1), ln_b.reshape(1, -1))


def _post_mixer(mix, x, w_out, ln1_g, ln1_b, router_w, router_b, w_gu, b_gu, w_dn, b_dn, ln2_g, ln2_b):
    x1, xp, top_e, gate, rank, tile_counts = _outproj_ln_router(mix, x, w_out, ln1_g, ln1_b, router_w, router_b)
    plan = _route_plan(top_e, rank, tile_counts.reshape(-1, N_EXPERTS))
    y_sorted = _expert_mlp(plan, xp, w_gu, b_gu, w_dn, b_dn)
    return _combine_ln(plan, gate, x1, y_sorted, ln2_g, ln2_b)


def kernel(x_prompt, x_sample, state_hgrn, state_pool, state_sconv, state_gdn_conv, state_gdn, w_in_ab, hgrn_lower_bounds, hgrn_norm_w, pool_w, pool_scale, w_out_ab, w_in_cd, sconv_w, gdn_conv_w, gdn_a_log, gdn_dt_bias, gdn_norm_w, w_out_cd, ln1_g, ln1_b, ln2_g, ln2_b, router_w, router_b, w_gu, b_gu, w_dn, b_dn):
    bp, sp, _ = x_prompt.shape
    bs, ss, _ = x_sample.shape
    n_p, n_s = bp * sp, bs * ss
    bf = jnp.bfloat16
    f32 = jnp.float32
    lower_bounds = jnp.cumsum(jax.nn.softmax(hgrn_lower_bounds.astype(f32), axis=0), axis=0)
    x = jnp.concatenate([x_prompt.reshape(n_p, D_MODEL), x_sample.reshape(n_s, D_MODEL)], axis=0)
    zeros_like_prompt = lambda st: jnp.zeros((bp,) + st.shape[2:], f32)

    states = {}
    for l in range(DEPTH):
        if l % 2 == 0:
            args = (w_in_ab[0].astype(bf), lower_bounds[l], hgrn_norm_w[0], pool_w[0].astype(bf), pool_scale[0])
            mp, hp, pp = _mixer_ab(x, 0, bp, sp, 0, zeros_like_prompt(state_hgrn), zeros_like_prompt(state_pool), *args)
            ms, hs, ps = _mixer_ab(x, n_p, bs, ss, PAST_LEN, state_hgrn[0], state_pool[0], *args)
            states.update(hp=hp[None], hs=hs[None], pp=pp[None], ps=ps[None])
            w_out = w_out_ab[0]
        else:
            args = (w_in_cd[0][:, :CD_MAIN].astype(bf), w_in_cd[0][:, CD_MAIN:].astype(bf), sconv_w[0], gdn_conv_w[0],
                    gdn_a_log[0], gdn_dt_bias[0], gdn_norm_w[0])
            mp, gp, scp, gcp = _mixer_cd(x, 0, bp, sp, zeros_like_prompt(state_gdn), zeros_like_prompt(state_sconv),
                                         zeros_like_prompt(state_gdn_conv), *args)
            ms, gs, scs, gcs = _mixer_cd(x, n_p, bs, ss, state_gdn[0], state_sconv[0], state_gdn_conv[0], *args)
            states.update(scp=scp[None], scs=scs[None], gcp=gcp[None], gcs=gcs[None], gp=gp[None], gs=gs[None])
            w_out = w_out_cd[0]
        mix = jnp.concatenate([mp, ms], axis=0)
        x = _post_mixer(mix, x, w_out.astype(bf), ln1_g[l], ln1_b[l], router_w[l], router_b[l],
                        w_gu[l].astype(bf), b_gu[l], w_dn[l].astype(bf), b_dn[l], ln2_g[l], ln2_b[l])
    return (x[:n_p].reshape(bp, sp, D_MODEL), x[n_p:].reshape(bs, ss, D_MODEL),
            states["hp"], states["hs"], states["pp"], states["ps"], states["scp"], states["scs"],
            states["gcp"], states["gcs"], states["gp"], states["gs"])
```

```python
import functools

import jax
import jax.numpy as jnp
from jax import lax
from jax.experimental import pallas as pl
from jax.experimental.pallas import tpu as pltpu

D_MODEL = 1024
DEPTH = 2
N_EXPERTS = 32
TOP_K = 4
D_FF = 1024
SWIGLU_LIMIT = 7.0
SWIGLU_ALPHA = 1.702
LN_EPS = 1e-5
DEEPNORM_ALPHA = (2 * DEPTH) ** 0.25

HALF = D_MODEL // 2
TOKEN_TILE = 256
EXPERT_ROWS = 256
N_GROUPS = 2
COMBINE_CHUNK = 16
VMEM_LIMIT = 56 * 1024 * 1024
MIX_OUT = 1024
BATCH_ROW_TILE = 128
PAST_LEN = 16384


def _layer_norm_rows(v, g, b):
    mu = jnp.mean(v, axis=-1, keepdims=True)
    d = v - mu
    var = jnp.mean(d * d, axis=-1, keepdims=True)
    return d * lax.rsqrt(var + LN_EPS) * g + b


def _pack_bf16_pairs(v):
    lo = pltpu.bitcast(v[:, :HALF].astype(jnp.bfloat16).astype(jnp.float32), jnp.uint32)
    hi = pltpu.bitcast(v[:, HALF:].astype(jnp.bfloat16).astype(jnp.float32), jnp.uint32)
    return (lo >> 16) | (hi & jnp.uint32(0xFFFF0000))


def _unpack_bf16_pairs(p):
    lo = pltpu.bitcast(p << 16, jnp.float32).astype(jnp.bfloat16)
    hi = pltpu.bitcast(p & jnp.uint32(0xFFFF0000), jnp.float32).astype(jnp.bfloat16)
    return lo, hi


ROW_TILE = 256
HG_HEADS, HG_D = 4, 128
HG_WIDTH = HG_HEADS * HG_D
HG_CHUNK = 16
POOL_WINDOWS = (2, 4, 8, 16)
POOL_GC = 128
POOL_WIDTH = 512
RMS_EPS = 1e-6
_HI = lax.Precision.HIGHEST


def _silu(v):
    return v * jax.nn.sigmoid(v)


def _bf(v):
    return v.astype(jnp.bfloat16)


def _dot(a, b, precision=None):
    return jnp.dot(a, b, preferred_element_type=jnp.float32, precision=precision)


def _dot_nt(a, b, precision=None):
    return lax.dot_general(a, b, (((1,), (1,)), ((), ())), preferred_element_type=jnp.float32, precision=precision)


def _dot_tn(a, b, precision=None):
    return lax.dot_general(a, b, (((0,), (0,)), ((), ())), preferred_element_type=jnp.float32, precision=precision)


def _split3(v):
    hi = _bf(v)
    rest = v - hi.astype(jnp.float32)
    mid = _bf(rest)
    return hi, mid, _bf(rest - mid.astype(jnp.float32))


def _mask_bf16(mask):
    return _bf(mask.astype(jnp.float32))


def _exact_mask_dot(dot_fn, m, terms):
    return dot_fn(m, terms[0]) + (dot_fn(m, terms[1]) + dot_fn(m, terms[2]))


def _split2(v):
    hi = _bf(v)
    return hi, _bf(v - hi.astype(jnp.float32))


def _dot3(a2, b2):
    return _dot(a2[0], b2[0]) + (_dot(a2[0], b2[1]) + _dot(a2[1], b2[0]))


def _chunk_masks(rows, chunk):
    shift = chunk.bit_length() - 1
    t = lax.broadcasted_iota(jnp.int32, (rows, rows), 0)
    s = lax.broadcasted_iota(jnp.int32, (rows, rows), 1)
    same = (t >> shift) == (s >> shift)
    return same, jnp.logical_and(same, s <= t), jnp.logical_and(same, s < t)


def _shift_rows_chain(cur, prev, j):
    if j == 0:
        return cur
    p = prev.shape[0]
    rc = pltpu.roll(cur, j, 0)
    rp = prev if j == p else pltpu.roll(prev, j, 0)
    row = lax.broadcasted_iota(jnp.int32, (p, cur.shape[1]), 0)
    head = jnp.where(row < j, rp, rc[:p])
    return jnp.concatenate([head, rc[p:]], axis=0) if cur.shape[0] > p else head


def _shift_rows_batch(cur, prev, j):
    if j == 0:
        return cur
    if j == 8:
        return prev
    rows = cur.shape[0]
    row = lax.broadcasted_iota(jnp.int32, cur.shape, 0)
    return jnp.where((row & 7) < j, pltpu.roll(prev, rows + j - 8, 0), pltpu.roll(cur, j, 0))


def _rms_gate(o, norm_w, gate):
    ms = jnp.mean(o * o, axis=-1, keepdims=True)
    return o * lax.rsqrt(ms + RMS_EPS) * norm_w * _silu(gate)


def _mixer_ab_kernel(x_ref, w_ref, lb_ref, nw_ref, pw_ref, ps_ref, s0_ref, pa_ref, pb_ref,
                     mix_ref, s_out_ref, pa_out_ref, pb_out_ref, state_ref, prev_ref, *, chain, pos0):
    rows = x_ref.shape[0]
    chunk = HG_CHUNK if chain else 8
    n_chunks = rows // chunk
    j = pl.program_id(1) if chain else 0

    proj = _dot(_bf(x_ref[...]), w_ref[...])
    q = _silu(proj[:, 0:HG_WIDTH])
    f = lb_ref[...] + (1.0 - lb_ref[...]) * jax.nn.sigmoid(proj[:, HG_WIDTH:2 * HG_WIDTH])
    v = _silu(proj[:, 2 * HG_WIDTH:3 * HG_WIDTH])
    gate = proj[:, 3 * HG_WIDTH:4 * HG_WIDTH]
    u = proj[:, 4 * HG_WIDTH:]
    log_f = jnp.log(f)
    k = 1.0 - f

    same, incl, _ = _chunk_masks(rows, chunk)
    log_f3 = _split3(log_f)
    sums = _exact_mask_dot(_dot, jnp.concatenate([_mask_bf16(incl), _mask_bf16(same)], axis=0), log_f3)
    cum = sums[:rows]
    total = sums[rows:]
    q_dec = q * jnp.exp(cum)
    k_dec = k * jnp.exp(-cum)
    k_end = k * jnp.exp(total - cum)
    chunk_of_row = lax.broadcasted_iota(jnp.int32, (rows, 128), 0) >> (chunk.bit_length() - 1)
    onehot = _mask_bf16(chunk_of_row == lax.broadcasted_iota(jnp.int32, (rows, 128), 1))
    decay_cols = jnp.exp(_exact_mask_dot(lambda m, t: _dot_tn(t, m), onehot, log_f3))

    if chain:
        @pl.when(j == 0)
        def _():
            state_ref[...] = s0_ref[0]
            prev_ref[...] = jnp.concatenate([pa_ref[...], pb_ref[...]], axis=0)

    head_sl = [slice(h * HG_D, (h + 1) * HG_D) for h in range(HG_HEADS)]
    intra, inter, state = [], [[] for _ in head_sl], [None] * HG_HEADS
    for sl in head_sl:
        att = jnp.where(incl, _dot_nt(_bf(q_dec[:, sl]), _bf(k_dec[:, sl])), 0.0)
        intra.append(_dot(_bf(att), _bf(v[:, sl])))
    for c in range(n_chunks):
        rs = slice(c * chunk, (c + 1) * chunk)
        for h, sl in enumerate(head_sl):
            if chain:
                s = state_ref[h] if c == 0 else state[h]
            else:
                s = s0_ref[c, h]
            inter[h].append(_dot(_bf(q_dec[rs, sl]), _bf(s)))
            state[h] = decay_cols[sl, c:c + 1] * s + _dot_tn(_bf(k_end[rs, sl]), _bf(v[rs, sl]))
            if not chain:
                s_out_ref[c, h] = state[h]
    outs = []
    for h, sl in enumerate(head_sl):
        if chain:
            state_ref[h] = state[h]
        outs.append(_rms_gate(intra[h] + jnp.concatenate(inter[h], axis=0), nw_ref[...], gate[:, sl]))

    if chain:
        prev = prev_ref[...]
        t = j * rows + lax.broadcasted_iota(jnp.int32, (rows, POOL_GC), 0)
    else:
        prev_a, prev_b = pa_ref[...], pb_ref[...]
        t = lax.broadcasted_iota(jnp.int32, (rows, POOL_GC), 0) & 7
    for gi, win in enumerate(POOL_WINDOWS):
        gs = slice(gi * POOL_GC, (gi + 1) * POOL_GC)
        ug = u[:, gs]
        wsum = ug
        for d in range(1, win):
            if chain:
                wsum = wsum + _shift_rows_chain(ug, prev[:, gs], d)
            elif d <= 8:
                wsum = wsum + _shift_rows_batch(ug, prev_b[:, gs], d)
            else:
                wsum = wsum + _shift_rows_batch(prev_b[:, gs], prev_a[:, gs], d - 8)
        cnt = jnp.minimum(win, pos0 + t + 1).astype(jnp.float32)
        diff = wsum / cnt - ug
        outs.append(_dot(_bf(diff), pw_ref[gi]) * ps_ref[:, gs])
    mix_ref[...] = jnp.concatenate(outs, axis=-1).astype(mix_ref.dtype)

    if chain:
        prev_ref[...] = u[rows - 16:]

        @pl.when(j == pl.num_programs(1) - 1)
        def _():
            s_out_ref[0] = state_ref[...]
            pa_out_ref[...] = u[rows - 16:rows - 8]
            pb_out_ref[...] = u[rows - 8:]
    else:
        pa_out_ref[...] = prev_b
        pb_out_ref[...] = u


def _mixer_ab(x, row0, n_seq, length, pos0, s0, pool_state, w_in, lb, norm_w, pool_w, pool_scale):
    n_rows = n_seq * length
    chain = length >= ROW_TILE
    rows = ROW_TILE if chain else BATCH_ROW_TILE
    assert row0 % rows == 0
    first = row0 // rows
    pool16 = jnp.pad(pool_state, ((0, 0), (1, 0), (0, 0)))
    pa = pool16[:, :8].reshape(n_seq * 8, POOL_WIDTH)
    pb = pool16[:, 8:].reshape(n_seq * 8, POOL_WIDTH)
    if chain:
        assert length % rows == 0
        tiles = length // rows
        grid = (n_seq, tiles)
        row_map = lambda b, j: (b * tiles + j, 0)
        x_map = lambda b, j: (first + b * tiles + j, 0)
        seq_map4 = lambda b, j: (b, 0, 0, 0)
        seq_map2 = lambda b, j: (b, 0)
        const2 = lambda b, j: (0, 0)
        const3 = lambda b, j: (0, 0, 0)
        seq_block = 1
    else:
        assert length == 8 and n_rows % rows == 0
        seq_block = rows // 8
        grid = (n_rows // rows,)
        row_map = lambda i: (i, 0)
        x_map = lambda i: (first + i, 0)
        seq_map4 = lambda i: (i, 0, 0, 0)
        seq_map2 = lambda i: (i, 0)
        const2 = lambda i: (0, 0)
        const3 = lambda i: (0, 0, 0)
    d_in = w_in.shape[1]
    mix, s_new, pa_new, pb_new = pl.pallas_call(
        functools.partial(_mixer_ab_kernel, chain=chain, pos0=pos0),
        grid=grid,
        in_specs=[
            pl.BlockSpec((rows, D_MODEL), x_map),
            pl.BlockSpec((D_MODEL, d_in), const2),
            pl.BlockSpec((1, HG_WIDTH), const2),
            pl.BlockSpec((1, HG_D), const2),
            pl.BlockSpec((len(POOL_WINDOWS), POOL_GC, POOL_GC), const3),
            pl.BlockSpec((1, POOL_WIDTH), const2),
            pl.BlockSpec((seq_block, HG_HEADS, HG_D, HG_D), seq_map4),
            pl.BlockSpec((seq_block * 8, POOL_WIDTH), seq_map2),
            pl.BlockSpec((seq_block * 8, POOL_WIDTH), seq_map2),
        ],
        out_specs=[
            pl.BlockSpec((rows, MIX_OUT), row_map),
            pl.BlockSpec((seq_block, HG_HEADS, HG_D, HG_D), seq_map4),
            pl.BlockSpec((seq_block * 8, POOL_WIDTH), seq_map2),
            pl.BlockSpec((seq_block * 8, POOL_WIDTH), seq_map2),
        ],
        out_shape=[
            jax.ShapeDtypeStruct((n_rows, MIX_OUT), jnp.bfloat16),
            jax.ShapeDtypeStruct(s0.shape, jnp.float32),
            jax.ShapeDtypeStruct(pa.shape, jnp.float32),
            jax.ShapeDtypeStruct(pb.shape, jnp.float32),
        ],
        scratch_shapes=[
            pltpu.VMEM((HG_HEADS, HG_D, HG_D), jnp.float32),
            pltpu.VMEM((16, POOL_WIDTH), jnp.float32),
        ],
        compiler_params=pltpu.CompilerParams(
            dimension_semantics=("arbitrary",) * len(grid), vmem_limit_bytes=VMEM_LIMIT),
        name="mixer_ab_chain" if chain else "mixer_ab_batch",
    )(x, w_in, lb.reshape(1, -1), norm_w.reshape(1, -1), pool_w, pool_scale.reshape(1, -1), s0, pa, pb)
    pool_new = jnp.concatenate([pa_new.reshape(n_seq, 8, POOL_WIDTH), pb_new.reshape(n_seq, 8, POOL_WIDTH)], axis=1)
    return mix, s_new, pool_new[:, 1:]


SC_WIDTH, SC_K = 512, 3
GD_HEADS, GD_D = 4, 128
GD_CONV = 4
GD_CHUNK = 64
GD_QKV = GD_HEADS * 3 * GD_D
CD_MAIN = 3 * SC_WIDTH + GD_QKV + GD_HEADS * GD_D


def _causal_conv(cur, prev, w_ref, shift_fn):
    width = w_ref.shape[0]
    acc = cur * w_ref[width - 1:width, :]
    for j in range(width - 1):
        acc = acc + shift_fn(cur, prev, width - 1 - j) * w_ref[j:j + 1, :]
    return acc


def _softplus(v):
    return jnp.maximum(v, 0.0) + jnp.log1p(jnp.exp(-jnp.abs(v)))


def _mixer_cd_kernel(x_ref, w_ref, wg_ref, scw_ref, gcw_ref, alog_ref, dtb_ref, nw_ref, s0_ref, scp_ref, gcp_ref,
                     mix_ref, s_out_ref, scp_out_ref, gcp_out_ref, state_ref, sc_prev_ref, gc_prev_ref, *, chain):
    rows = x_ref.shape[0]
    chunk = GD_CHUNK if chain else 8
    n_chunks = rows // chunk
    j = pl.program_id(1) if chain else 0
    shift_fn = _shift_rows_chain if chain else _shift_rows_batch

    xb = _bf(x_ref[...])
    proj = _dot(xb, w_ref[...])
    gates = _dot(xb, wg_ref[...])
    b_gate = proj[:, 0:SC_WIDTH]
    conv_in = proj[:, SC_WIDTH:2 * SC_WIDTH] * proj[:, 2 * SC_WIDTH:3 * SC_WIDTH]
    qkv = proj[:, 3 * SC_WIDTH:3 * SC_WIDTH + GD_QKV]
    z = proj[:, 3 * SC_WIDTH + GD_QKV:]

    if chain:
        @pl.when(j == 0)
        def _():
            state_ref[...] = s0_ref[0]
            sc_prev_ref[...] = scp_ref[...]
            gc_prev_ref[...] = gcp_ref[...]
        sc_prev, gc_prev = sc_prev_ref[...], gc_prev_ref[...]
    else:
        sc_prev, gc_prev = scp_ref[...], gcp_ref[...]

    outs = [b_gate * _causal_conv(conv_in, sc_prev, scw_ref, shift_fn)]
    qkv_c = _silu(_causal_conv(qkv, gc_prev, gcw_ref, shift_fn))

    lane8 = lax.broadcasted_iota(jnp.int32, gates.shape, 1)
    g_dec = -jnp.exp(alog_ref[...]) * _softplus(gates + dtb_ref[...])
    cols = jnp.where(lane8 < GD_HEADS, g_dec, jax.nn.sigmoid(gates))
    same, incl, strict = _chunk_masks(rows, chunk)
    cols3 = _split3(cols)
    gcum = _exact_mask_dot(_dot, _mask_bf16(incl), cols3)
    gtot = _exact_mask_dot(_dot, _mask_bf16(same), cols3)
    eye8 = _mask_bf16(lax.broadcasted_iota(jnp.int32, (8, 8), 0) == lax.broadcasted_iota(jnp.int32, (8, 8), 1))
    gcum_rows = _exact_mask_dot(_dot_nt, eye8, _split3(gcum))

    eye = (lax.broadcasted_iota(jnp.int32, (rows, rows), 0)
           == lax.broadcasted_iota(jnp.int32, (rows, rows), 1)).astype(jnp.float32)
    hd = []
    for h in range(GD_HEADS):
        sl = lambda part: slice(part * GD_HEADS * GD_D + h * GD_D, part * GD_HEADS * GD_D + (h + 1) * GD_D)
        q, k, v = qkv_c[:, sl(0)], qkv_c[:, sl(1)], qkv_c[:, sl(2)]
        q = q * lax.rsqrt(jnp.sum(q * q, axis=-1, keepdims=True) + 1e-6) * (GD_D ** -0.5)
        k = k * lax.rsqrt(jnp.sum(k * k, axis=-1, keepdims=True) + 1e-6)
        beta = cols[:, GD_HEADS + h:GD_HEADS + h + 1]
        gc = gcum[:, h:h + 1]
        gt = gtot[:, h:h + 1]
        decay = jnp.where(incl, jnp.exp(jnp.where(incl, gc - gcum_rows[h:h + 1, :], 0.0)), 0.0)
        k_beta = k * beta
        kb = _bf(k)
        egc = jnp.exp(gc)
        power = -jnp.where(strict, _dot_nt(_bf(k_beta), kb) * decay, 0.0)
        hd.append(dict(
            inv=eye + power, power2=_split2(power),
            att=_bf(jnp.where(incl, _dot_nt(_bf(q), kb) * decay, 0.0)),
            rhs=_split2(jnp.concatenate([v * beta, k_beta * egc], axis=1)),
            q_dec=q * egc, k_end=k * jnp.exp(gt - gc),
            decay_end=jnp.exp(gt), inter=[], v_new=[], s=None))

    for _ in range(chunk.bit_length() - 2):
        for d in hd:
            d["power2"] = _split2(_dot3(d["power2"], d["power2"]))
        for d in hd:
            d["inv"] = d["inv"] + _dot3(_split2(d["inv"]), d["power2"])
    for d in hd:
        uw = _dot3(_split2(d["inv"]), d["rhs"])
        d["u"], d["w"] = uw[:, :GD_D], uw[:, GD_D:]

    for c in range(n_chunks):
        rs = slice(c * chunk, (c + 1) * chunk)
        for h, d in enumerate(hd):
            if chain:
                s = state_ref[h] if c == 0 else d["s"]
            else:
                s = s0_ref[c, h]
            sb = _bf(s)
            v_new = d["u"][rs] - _dot(_bf(d["w"][rs]), sb)
            d["inter"].append(_dot(_bf(d["q_dec"][rs]), sb))
            d["s"] = d["decay_end"][c * chunk:c * chunk + 1, :] * s + _dot_tn(_bf(d["k_end"][rs]), _bf(v_new))
            d["v_new"].append(v_new)
            if not chain:
                s_out_ref[c, h] = d["s"]
    for h, d in enumerate(hd):
        if chain:
            state_ref[h] = d["s"]
        o = jnp.concatenate(d["inter"], axis=0) + _dot(d["att"], _bf(jnp.concatenate(d["v_new"], axis=0)))
        outs.append(_rms_gate(o, nw_ref[...], z[:, h * GD_D:(h + 1) * GD_D]))
    mix_ref[...] = jnp.concatenate(outs, axis=-1).astype(mix_ref.dtype)

    if chain:
        sc_prev_ref[...] = conv_in[rows - 8:]
        gc_prev_ref[...] = qkv[rows - 8:]

        @pl.when(j == pl.num_programs(1) - 1)
        def _():
            s_out_ref[0] = state_ref[...]
            scp_out_ref[...] = conv_in[rows - 8:]
            gcp_out_ref[...] = qkv[rows - 8:]
    else:
        scp_out_ref[...] = conv_in
        gcp_out_ref[...] = qkv


def _mixer_cd(x, row0, n_seq, length, s0, sconv_state, gconv_state, w_main, w_gates, sconv_w, gconv_w, a_log, dt_bias,
              norm_w):
    n_rows = n_seq * length
    chain = length >= ROW_TILE
    rows = ROW_TILE if chain else BATCH_ROW_TILE
    assert row0 % rows == 0
    first = row0 // rows
    pad8 = lambda st: jnp.pad(st, ((0, 0), (8 - st.shape[1], 0), (0, 0))).reshape(n_seq * 8, st.shape[2])
    scp, gcp = pad8(sconv_state), pad8(gconv_state)
    if chain:
        assert length % rows == 0
        tiles = length // rows
        grid = (n_seq, tiles)
        row_map = lambda b, j: (b * tiles + j, 0)
        x_map = lambda b, j: (first + b * tiles + j, 0)
        seq_map4 = lambda b, j: (b, 0, 0, 0)
        seq_map2 = lambda b, j: (b, 0)
        const2 = lambda b, j: (0, 0)
        seq_block = 1
    else:
        assert length == 8 and n_rows % rows == 0
        seq_block = rows // 8
        grid = (n_rows // rows,)
        row_map = lambda i: (i, 0)
        x_map = lambda i: (first + i, 0)
        seq_map4 = lambda i: (i, 0, 0, 0)
        seq_map2 = lambda i: (i, 0)
        const2 = lambda i: (0, 0)
    zeros4 = jnp.zeros((GD_HEADS,), jnp.float32)
    alog8 = jnp.concatenate([a_log.astype(jnp.float32), zeros4]).reshape(1, 8)
    dtb8 = jnp.concatenate([dt_bias.astype(jnp.float32), zeros4]).reshape(1, 8)
    mix, s_new, scp_new, gcp_new = pl.pallas_call(
        functools.partial(_mixer_cd_kernel, chain=chain),
        grid=grid,
        in_specs=[
            pl.BlockSpec((rows, D_MODEL), x_map),
            pl.BlockSpec((D_MODEL, CD_MAIN), const2),
            pl.BlockSpec((D_MODEL, 8), const2),
            pl.BlockSpec((SC_K, SC_WIDTH), const2),
            pl.BlockSpec((GD_CONV, GD_QKV), const2),
            pl.BlockSpec((1, 8), const2),
            pl.BlockSpec((1, 8), const2),
            pl.BlockSpec((1, GD_D), const2),
            pl.BlockSpec((seq_block, GD_HEADS, GD_D, GD_D), seq_map4),
            pl.BlockSpec((seq_block * 8, SC_WIDTH), seq_map2),
            pl.BlockSpec((seq_block * 8, GD_QKV), seq_map2),
        ],
        out_specs=[
            pl.BlockSpec((rows, MIX_OUT), row_map),
            pl.BlockSpec((seq_block, GD_HEADS, GD_D, GD_D), seq_map4),
            pl.BlockSpec((seq_block * 8, SC_WIDTH), seq_map2),
            pl.BlockSpec((seq_block * 8, GD_QKV), seq_map2),
        ],
        out_shape=[
            jax.ShapeDtypeStruct((n_rows, MIX_OUT), jnp.bfloat16),
            jax.ShapeDtypeStruct(s0.shape, jnp.float32),
            jax.ShapeDtypeStruct(scp.shape, jnp.float32),
            jax.ShapeDtypeStruct(gcp.shape, jnp.float32),
        ],
        scratch_shapes=[
            pltpu.VMEM((GD_HEADS, GD_D, GD_D), jnp.float32),
            pltpu.VMEM((8, SC_WIDTH), jnp.float32),
            pltpu.VMEM((8, GD_QKV), jnp.float32),
        ],
        compiler_params=pltpu.CompilerParams(
            dimension_semantics=("arbitrary",) * len(grid), vmem_limit_bytes=VMEM_LIMIT),
        name="mixer_cd_chain" if chain else "mixer_cd_batch",
    )(x, w_main, w_gates, sconv_w, gconv_w, alog8, dtb8, norm_w.reshape(1, -1), s0, scp, gcp)
    tail = lambda st, keep: st.reshape(n_seq, 8, -1)[:, 8 - keep:]
    return mix, s_new, tail(scp_new, SC_K - 1), tail(gcp_new, GD_CONV - 1)


def _outproj_ln_router_kernel(mix_ref, x_ref, w_ref, g_ref, b_ref, rw_ref, rb_ref,
                              x1_ref, xp_ref, tope_ref, gate_ref, rank_ref, cnt_ref):
    mix = jnp.dot(mix_ref[...], w_ref[...], preferred_element_type=jnp.float32)
    x1 = _layer_norm_rows(DEEPNORM_ALPHA * x_ref[...] + mix, g_ref[...], b_ref[...])
    x1_ref[...] = x1
    xp_ref[...] = _pack_bf16_pairs(x1)
    logits = jnp.dot(x1, rw_ref[...], preferred_element_type=jnp.float32,
                     precision=lax.Precision.HIGHEST) + rb_ref[...]
    lane = lax.broadcasted_iota(jnp.int32, logits.shape, 1)
    vals, idxs = [], []
    for _ in range(TOP_K):
        m = jnp.max(logits, axis=-1, keepdims=True)
        idx = jnp.min(jnp.where(logits == m, lane, N_EXPERTS), axis=-1, keepdims=True)
        vals.append(m)
        idxs.append(idx)
        logits = jnp.where(lane == idx, -jnp.inf, logits)
    ex = [jnp.exp(v - vals[0]) for v in vals]
    den = ex[0] + ex[1] + ex[2] + ex[3]
    tm = logits.shape[0]
    hits = [lane == idx for idx in idxs]
    member = sum(h.astype(jnp.float32) for h in hits)
    earlier = (lax.broadcasted_iota(jnp.int32, (tm, tm), 1) < lax.broadcasted_iota(jnp.int32, (tm, tm), 0))
    before = _dot(earlier.astype(jnp.bfloat16), member.astype(jnp.bfloat16))
    cnt_ref[0] = jnp.sum(member, axis=0, keepdims=True).astype(jnp.int32)
    col = lax.broadcasted_iota(jnp.int32, tope_ref.shape, 1)
    tope = jnp.zeros(tope_ref.shape, jnp.int32)
    gate = jnp.zeros(gate_ref.shape, jnp.float32)
    rank = jnp.zeros(rank_ref.shape, jnp.int32)
    for k in range(TOP_K):
        tope = jnp.where(col == k, idxs[k], tope)
        gate = jnp.where(col == k, ex[k] / den, gate)
        rank_k = jnp.sum(jnp.where(hits[k], before, 0.0), axis=-1, keepdims=True).astype(jnp.int32)
        rank = jnp.where(col == k, rank_k, rank)
    tope_ref[...] = tope
    gate_ref[...] = gate
    rank_ref[...] = rank


def _outproj_ln_router(mix, x, w_out, ln_g, ln_b, router_w, router_b):
    n = x.shape[0]
    tm = TOKEN_TILE
    row = lambda i: (i, 0)
    full = lambda i: (0, 0)
    return pl.pallas_call(
        _outproj_ln_router_kernel,
        grid=(n // tm,),
        in_specs=[
            pl.BlockSpec((tm, D_MODEL), row),
            pl.BlockSpec((tm, D_MODEL), row),
            pl.BlockSpec((D_MODEL, D_MODEL), full),
            pl.BlockSpec((1, D_MODEL), full),
            pl.BlockSpec((1, D_MODEL), full),
            pl.BlockSpec((D_MODEL, N_EXPERTS), full),
            pl.BlockSpec((1, N_EXPERTS), full),
        ],
        out_specs=[
            pl.BlockSpec((tm, D_MODEL), row),
            pl.BlockSpec((tm, HALF), row),
            pl.BlockSpec((tm, TOP_K), row),
            pl.BlockSpec((tm, TOP_K), row),
            pl.BlockSpec((tm, TOP_K), row),
            pl.BlockSpec((1, 1, N_EXPERTS), lambda i: (i, 0, 0)),
        ],
        out_shape=[
            jax.ShapeDtypeStruct((n, D_MODEL), jnp.float32),
            jax.ShapeDtypeStruct((n, HALF), jnp.uint32),
            jax.ShapeDtypeStruct((n, TOP_K), jnp.int32),
            jax.ShapeDtypeStruct((n, TOP_K), jnp.float32),
            jax.ShapeDtypeStruct((n, TOP_K), jnp.int32),
            jax.ShapeDtypeStruct((n // tm, 1, N_EXPERTS), jnp.int32),
        ],
        compiler_params=pltpu.CompilerParams(dimension_semantics=("arbitrary",), vmem_limit_bytes=VMEM_LIMIT),
        name="outproj_ln_router",
    )(mix, x, w_out, ln_g.reshape(1, -1), ln_b.reshape(1, -1), router_w, router_b.reshape(1, -1))


def _route_plan(top_e, rank, tile_counts):
    n = top_e.shape[0]
    a = n * TOP_K
    br, tm, ch = EXPERT_ROWS, TOKEN_TILE, COMBINE_CHUNK
    n_virtual = N_EXPERTS * N_GROUPS
    group_tokens = n // N_GROUPS
    n_blocks = a // br + n_virtual + 1
    n_tiles = n // tm
    group_tiles = n_tiles // N_GROUPS
    max_chunks = (tm * TOP_K) // ch + N_EXPERTS + 1
    i32 = jnp.int32

    cnt = tile_counts.reshape(N_GROUPS, group_tiles, N_EXPERTS)
    nblk = (cnt.sum(axis=1) + br - 1) // br
    blk_end = jnp.cumsum(nblk.reshape(-1))
    row_off = ((blk_end - nblk.reshape(-1)) * br).reshape(N_GROUPS, 1, N_EXPERTS)
    seg_start = (row_off + jnp.cumsum(cnt, axis=1) - cnt).reshape(n_tiles, N_EXPERTS)
    seg_len = tile_counts
    blocks = jnp.arange(n_blocks, dtype=i32)
    block_ve = jnp.minimum(jnp.sum(blk_end[None, :] <= blocks[:, None], axis=1), n_virtual - 1).astype(i32)
    block_info = jnp.stack([block_ve % N_EXPERTS, block_ve // N_EXPERTS,
                            (blocks < blk_end[-1]).astype(i32)])

    aligned = seg_start // 8 * 8
    lead = seg_start - aligned
    nch = jnp.where(seg_len > 0, (lead + seg_len + ch - 1) // ch, 0)
    ch_end = jnp.cumsum(nch, axis=1)
    ch_first = ch_end - nch
    n_chunks = ch_end[:, -1]
    cidx = jnp.arange(max_chunks, dtype=i32)[None, :, None]
    owns = jnp.logical_and(ch_first[:, None, :] <= cidx, cidx < ch_end[:, None, :])
    chunk_src = jnp.sum(jnp.where(owns, aligned[:, None, :] + (cidx - ch_first[:, None, :]) * ch, 0), axis=2)

    hit = top_e.reshape(n_tiles, tm, TOP_K, 1) == jnp.arange(N_EXPERTS, dtype=i32)
    pick = lambda table: jnp.sum(jnp.where(hit, table[:, None, None, :], 0), axis=-1)
    rank = rank.reshape(n_tiles, tm, TOP_K)
    dest = pick(seg_start) + rank
    pos = pick(ch_first * ch + lead) + rank
    tok = (jnp.arange(n, dtype=i32) % group_tokens).reshape(n_tiles, tm, 1)
    row_tok = jnp.zeros((n_blocks * br,), i32).at[dest.reshape(-1)].set(jnp.broadcast_to(tok, dest.shape).reshape(-1))
    return dict(row_tok=row_tok.reshape(n_blocks, 1, br), block_info=block_info,
                chunk_src=chunk_src.astype(i32), n_chunks=n_chunks.astype(i32),
                pos=pos.reshape(n_tiles, 1, tm * TOP_K).astype(i32))


def _expert_kernel(info_ref, tok_ref, xp_hbm, wgu_ref, bgu_ref, wdn_ref, bdn_ref, y_ref,
                   xp_vmem, stage_ref, load_sem):
    b = pl.program_id(0)
    grp = info_ref[1, b]
    prev_grp = info_ref[1, jnp.maximum(b - 1, 0)]
    group_tokens = xp_vmem.shape[0]

    @pl.when(jnp.logical_or(b == 0, grp != prev_grp))
    def _():
        cp = pltpu.make_async_copy(xp_hbm.at[pl.ds(grp * group_tokens, group_tokens)], xp_vmem, load_sem)
        cp.start()
        cp.wait()

    @pl.when(info_ref[2, b] == 1)
    def _():
        unroll = 8

        def gather(i, carry):
            for u in range(unroll):
                r = i * unroll + u
                t = tok_ref[0, 0, r]
                stage_ref[pl.ds(r, 1), :] = xp_vmem[pl.ds(t, 1), :]
            return carry

        lax.fori_loop(0, EXPERT_ROWS // unroll, gather, 0)
        lo, hi = _unpack_bf16_pairs(stage_ref[...])
        hgu = (jnp.dot(lo, wgu_ref[0, :HALF, :], preferred_element_type=jnp.float32)
               + jnp.dot(hi, wgu_ref[0, HALF:, :], preferred_element_type=jnp.float32) + bgu_ref[0])
        glu = jnp.minimum(hgu[:, :D_FF], SWIGLU_LIMIT)
        lin = jnp.clip(hgu[:, D_FF:], -SWIGLU_LIMIT, SWIGLU_LIMIT)
        act = (lin + 1.0) * glu * jax.nn.sigmoid(SWIGLU_ALPHA * glu)
        y_ref[...] = jnp.dot(act.astype(jnp.bfloat16), wdn_ref[0], preferred_element_type=jnp.float32) + bdn_ref[0]

    @pl.when(info_ref[2, b] == 0)
    def _():
        y_ref[...] = jnp.zeros_like(y_ref)


def _expert_mlp(plan, xp, w_gu, b_gu, w_dn, b_dn):
    n = xp.shape[0]
    br = EXPERT_ROWS
    n_blocks = plan["row_tok"].shape[0]
    assert br >= COMBINE_CHUNK + 8
    pad_rows = n_blocks * br
    expert = lambda b, info: (info[0, b], 0, 0)
    grid_spec = pltpu.PrefetchScalarGridSpec(
        num_scalar_prefetch=1,
        grid=(n_blocks,),
        in_specs=[
            pl.BlockSpec((1, 1, br), lambda b, info: (b, 0, 0), memory_space=pltpu.SMEM),
            pl.BlockSpec(memory_space=pl.ANY),
            pl.BlockSpec((1, D_MODEL, 2 * D_FF), expert),
            pl.BlockSpec((1, 1, 2 * D_FF), expert),
            pl.BlockSpec((1, D_FF, D_MODEL), expert),
            pl.BlockSpec((1, 1, D_MODEL), expert),
        ],
        out_specs=pl.BlockSpec((br, D_MODEL), lambda b, info: (b, 0)),
        scratch_shapes=[
            pltpu.VMEM((n // N_GROUPS, HALF), jnp.uint32),
            pltpu.VMEM((br, HALF), jnp.uint32),
            pltpu.SemaphoreType.DMA(()),
        ],
    )
    return pl.pallas_call(
        _expert_kernel,
        grid_spec=grid_spec,
        out_shape=jax.ShapeDtypeStruct((pad_rows, D_MODEL), jnp.float32),
        compiler_params=pltpu.CompilerParams(dimension_semantics=("arbitrary",), vmem_limit_bytes=VMEM_LIMIT),
        name="expert_mlp",
    )(plan["block_info"], plan["row_tok"], xp, w_gu, b_gu.reshape(N_EXPERTS, 1, -1), w_dn,
      b_dn.reshape(N_EXPERTS, 1, -1))


def _combine_ln_kernel(src_ref, nch_ref, pos_ref, gate_ref, x1_ref, y_hbm, g_ref, b_ref, out_ref,
                       stage_ref, ffn_ref, sems):
    i = pl.program_id(0)
    n_tiles = pl.num_programs(0)
    ch = COMBINE_CHUNK

    def chunk_copy(tile, slot, c):
        src = pl.multiple_of(src_ref[tile, c], 8)
        return pltpu.make_async_copy(y_hbm.at[pl.ds(src, ch)], stage_ref.at[slot, pl.ds(c * ch, ch)], sems.at[slot])

    def issue(tile, slot):
        lax.fori_loop(0, nch_ref[tile], lambda c, carry: (chunk_copy(tile, slot, c).start(), carry)[1], 0)

    @pl.when(i == 0)
    def _():
        issue(0, 0)

    @pl.when(i + 1 < n_tiles)
    def _():
        issue(i + 1, (i + 1) % 2)

    slot = i % 2
    lax.fori_loop(0, nch_ref[i], lambda c, carry: (chunk_copy(i, slot, c).wait(), carry)[1], 0)

    tm = x1_ref.shape[0]
    unroll = 4

    def combine(j, carry):
        for u in range(unroll):
            t = j * unroll + u
            acc = None
            for k in range(TOP_K):
                row = stage_ref[slot, pl.ds(pos_ref[0, 0, t * TOP_K + k], 1), :] * gate_ref[0, 0, t * TOP_K + k]
                acc = row if acc is None else acc + row
            ffn_ref[pl.ds(t, 1), :] = acc
        return carry

    lax.fori_loop(0, tm // unroll, combine, 0)
    out_ref[...] = _layer_norm_rows(DEEPNORM_ALPHA * x1_ref[...] + ffn_ref[...], g_ref[...], b_ref[...])


def _combine_ln(plan, gate, x1, y_sorted, ln_g, ln_b):
    n = x1.shape[0]
    tm = TOKEN_TILE
    n_tiles = n // tm
    max_chunks = plan["chunk_src"].shape[1]
    row = lambda i, src, nch: (i, 0)
    full = lambda i, src, nch: (0, 0)
    per_tile = lambda i, src, nch: (i, 0, 0)
    grid_spec = pltpu.PrefetchScalarGridSpec(
        num_scalar_prefetch=2,
        grid=(n_tiles,),
        in_specs=[
            pl.BlockSpec((1, 1, tm * TOP_K), per_tile, memory_space=pltpu.SMEM),
            pl.BlockSpec((1, 1, tm * TOP_K), per_tile, memory_space=pltpu.SMEM),
            pl.BlockSpec((tm, D_MODEL), row),
            pl.BlockSpec(memory_space=pl.ANY),
            pl.BlockSpec((1, D_MODEL), full),
            pl.BlockSpec((1, D_MODEL), full),
        ],
        out_specs=pl.BlockSpec((tm, D_MODEL), row),
        scratch_shapes=[
            pltpu.VMEM((2, max_chunks * COMBINE_CHUNK, D_MODEL), jnp.float32),
            pltpu.VMEM((tm, D_MODEL), jnp.float32),
            pltpu.SemaphoreType.DMA((2,)),
        ],
    )
    return pl.pallas_call(
        _combine_ln_kernel,
        grid_spec=grid_spec,
        out_shape=jax.ShapeDtypeStruct((n, D_MODEL), jnp.float32),
        compiler_params=pltpu.CompilerParams(dimension_semantics=("arbitrary",), vmem_limit_bytes=VMEM_LIMIT),
        name="combine_ln",
    )(plan["chunk_src"], plan["n_chunks"], plan["pos"], gate.reshape(n_tiles, 1, tm * TOP_K), x1, y_sorted,
      ln_g.reshape(1, -1), ln_b.reshape(1, -1))


def _post_mixer(mix, x, w_out, ln1_g, ln1_b, router_w, router_b, w_gu, b_gu, w_dn, b_dn, ln2_g, ln2_b):
    x1, xp, top_e, gate, rank, tile_counts = _outproj_ln_router(mix, x, w_out, ln1_g, ln1_b, router_w, router_b)
    plan = _route_plan(top_e, rank, tile_counts.reshape(-1, N_EXPERTS))
    y_sorted = _expert_mlp(plan, xp, w_gu, b_gu, w_dn, b_dn)
    return _combine_ln(plan, gate, x1, y_sorted, ln2_g, ln2_b)


def kernel(x_prompt, x_sample, state_hgrn, state_pool, state_sconv, state_gdn_conv, state_gdn, w_in_ab, hgrn_lower_bounds, hgrn_norm_w, pool_w, pool_scale, w_out_ab, w_in_cd, sconv_w, gdn_conv_w, gdn_a_log, gdn_dt_bias, gdn_norm_w, w_out_cd, ln1_g, ln1_b, ln2_g, ln2_b, router_w, router_b, w_gu, b_gu, w_dn, b_dn):
    bp, sp, _ = x_prompt.shape
    bs, ss, _ = x_sample.shape
    n_p, n_s = bp * sp, bs * ss
    bf = jnp.bfloat16
    f32 = jnp.float32
    lower_bounds = jnp.cumsum(jax.nn.softmax(hgrn_lower_bounds.astype(f32), axis=0), axis=0)
    x = jnp.concatenate([x_prompt.reshape(n_p, D_MODEL), x_sample.reshape(n_s, D_MODEL)], axis=0)
    zeros_like_prompt = lambda st: jnp.zeros((bp,) + st.shape[2:], f32)

    states = {}
    for l in range(DEPTH):
        if l % 2 == 0:
            args = (w_in_ab[0].astype(bf), lower_bounds[l], hgrn_norm_w[0], pool_w[0].astype(bf), pool_scale[0])
            mp, hp, pp = _mixer_ab(x, 0, bp, sp, 0, zeros_like_prompt(state_hgrn), zeros_like_prompt(state_pool), *args)
            ms, hs, ps = _mixer_ab(x, n_p, bs, ss, PAST_LEN, state_hgrn[0], state_pool[0], *args)
            states.update(hp=hp[None], hs=hs[None], pp=pp[None], ps=ps[None])
            w_out = w_out_ab[0]
        else:
            args = (w_in_cd[0][:, :CD_MAIN].astype(bf), w_in_cd[0][:, CD_MAIN:].astype(bf), sconv_w[0], gdn_conv_w[0],
                    gdn_a_log[0], gdn_dt_bias[0], gdn_norm_w[0])
            mp, gp, scp, gcp = _mixer_cd(x, 0, bp, sp, zeros_like_prompt(state_gdn), zeros_like_prompt(state_sconv),
                                         zeros_like_prompt(state_gdn_conv), *args)
            ms, gs, scs, gcs = _mixer_cd(x, n_p, bs, ss, state_gdn[0], state_sconv[0], state_gdn_conv[0], *args)
            states.update(scp=scp[None], scs=scs[None], gcp=gcp[None], gcs=gcs[None], gp=gp[None], gs=gs[None])
            w_out = w_out_cd[0]
        mix = jnp.concatenate([mp, ms], axis=0)
        x = _post_mixer(mix, x, w_out.astype(bf), ln1_g[l], ln1_b[l], router_w[l], router_b[l],
                        w_gu[l].astype(bf), b_gu[l], w_dn[l].astype(bf), b_dn[l], ln2_g[l], ln2_b[l])
    return (x[:n_p].reshape(bp, sp, D_MODEL), x[n_p:].reshape(bs, ss, D_MODEL),
            states["hp"], states["hs"], states["pp"], states["ps"], states["scp"], states["scs"],
            states["gcp"], states["gcs"], states["gp"], states["gs"])
```

```python
import functools

import jax
import jax.numpy as jnp
from jax import lax
from jax.experimental import pallas as pl
from jax.experimental.pallas import tpu as pltpu

D_MODEL = 1024
DEPTH = 2
N_EXPERTS = 32
TOP_K = 4
D_FF = 1024
SWIGLU_LIMIT = 7.0
SWIGLU_ALPHA = 1.702
LN_EPS = 1e-5
DEEPNORM_ALPHA = (2 * DEPTH) ** 0.25

HALF = D_MODEL // 2
LANE_TILES = D_MODEL // 128
TOKEN_TILE = 256
EXPERT_ROWS = 256
N_GROUPS = 2
COMBINE_CHUNK = 16
VMEM_LIMIT = 56 * 1024 * 1024
MIX_OUT = 1024
BATCH_ROW_TILE = 128
PAST_LEN = 16384


def _layer_norm_rows(v, g, b):
    mu = jnp.mean(v, axis=-1, keepdims=True)
    d = v - mu
    var = jnp.mean(d * d, axis=-1, keepdims=True)
    return d * lax.rsqrt(var + LN_EPS) * g + b


def _pack_bf16_pairs(v):
    lo = pltpu.bitcast(v[:, :HALF].astype(jnp.bfloat16).astype(jnp.float32), jnp.uint32)
    hi = pltpu.bitcast(v[:, HALF:].astype(jnp.bfloat16).astype(jnp.float32), jnp.uint32)
    return (lo >> 16) | (hi & jnp.uint32(0xFFFF0000))


def _unpack_bf16_pairs(p):
    lo = pltpu.bitcast(p << 16, jnp.float32).astype(jnp.bfloat16)
    hi = pltpu.bitcast(p & jnp.uint32(0xFFFF0000), jnp.float32).astype(jnp.bfloat16)
    return lo, hi


ROW_TILE = 256
HG_HEADS, HG_D = 4, 128
HG_WIDTH = HG_HEADS * HG_D
HG_CHUNK = 16
POOL_WINDOWS = (2, 4, 8, 16)
POOL_GC = 128
POOL_WIDTH = 512
RMS_EPS = 1e-6
_HI = lax.Precision.HIGHEST


def _silu(v):
    return v * jax.nn.sigmoid(v)


def _bf(v):
    return v.astype(jnp.bfloat16)


def _dot(a, b, precision=None):
    return jnp.dot(a, b, preferred_element_type=jnp.float32, precision=precision)


def _dot_nt(a, b, precision=None):
    return lax.dot_general(a, b, (((1,), (1,)), ((), ())), preferred_element_type=jnp.float32, precision=precision)


def _dot_tn(a, b, precision=None):
    return lax.dot_general(a, b, (((0,), (0,)), ((), ())), preferred_element_type=jnp.float32, precision=precision)


def _split3(v):
    hi = _bf(v)
    rest = v - hi.astype(jnp.float32)
    mid = _bf(rest)
    return hi, mid, _bf(rest - mid.astype(jnp.float32))


def _mask_bf16(mask):
    return _bf(mask.astype(jnp.float32))


def _exact_mask_dot(dot_fn, m, terms):
    return dot_fn(m, terms[0]) + (dot_fn(m, terms[1]) + dot_fn(m, terms[2]))


def _split2(v):
    hi = _bf(v)
    return hi, _bf(v - hi.astype(jnp.float32))


def _dot3(a2, b2):
    return _dot(a2[0], b2[0]) + (_dot(a2[0], b2[1]) + _dot(a2[1], b2[0]))


def _chunk_masks(rows, chunk):
    shift = chunk.bit_length() - 1
    t = lax.broadcasted_iota(jnp.int32, (rows, rows), 0)
    s = lax.broadcasted_iota(jnp.int32, (rows, rows), 1)
    same = (t >> shift) == (s >> shift)
    return same, jnp.logical_and(same, s <= t), jnp.logical_and(same, s < t)


def _shift_rows_chain(cur, prev, j):
    if j == 0:
        return cur
    p = prev.shape[0]
    rc = pltpu.roll(cur, j, 0)
    rp = prev if j == p else pltpu.roll(prev, j, 0)
    row = lax.broadcasted_iota(jnp.int32, (p, cur.shape[1]), 0)
    head = jnp.where(row < j, rp, rc[:p])
    return jnp.concatenate([head, rc[p:]], axis=0) if cur.shape[0] > p else head


def _shift_rows_batch(cur, prev, j):
    if j == 0:
        return cur
    if j == 8:
        return prev
    rows = cur.shape[0]
    row = lax.broadcasted_iota(jnp.int32, cur.shape, 0)
    return jnp.where((row & 7) < j, pltpu.roll(prev, rows + j - 8, 0), pltpu.roll(cur, j, 0))


def _rms_gate(o, norm_w, gate):
    ms = jnp.mean(o * o, axis=-1, keepdims=True)
    return o * lax.rsqrt(ms + RMS_EPS) * norm_w * _silu(gate)


def _mixer_ab_kernel(x_ref, w_ref, lb_ref, nw_ref, pw_ref, ps_ref, s0_ref, pa_ref, pb_ref,
                     mix_ref, s_out_ref, pa_out_ref, pb_out_ref, state_ref, prev_ref, *, chain, pos0):
    rows = x_ref.shape[0]
    chunk = HG_CHUNK if chain else 8
    n_chunks = rows // chunk
    j = pl.program_id(1) if chain else 0

    proj = _dot(_bf(x_ref[...]), w_ref[...])
    q = _silu(proj[:, 0:HG_WIDTH])
    f = lb_ref[...] + (1.0 - lb_ref[...]) * jax.nn.sigmoid(proj[:, HG_WIDTH:2 * HG_WIDTH])
    v = _silu(proj[:, 2 * HG_WIDTH:3 * HG_WIDTH])
    gate = proj[:, 3 * HG_WIDTH:4 * HG_WIDTH]
    u = proj[:, 4 * HG_WIDTH:]
    log_f = jnp.log(f)
    k = 1.0 - f

    same, incl, _ = _chunk_masks(rows, chunk)
    log_f3 = _split3(log_f)
    sums = _exact_mask_dot(_dot, jnp.concatenate([_mask_bf16(incl), _mask_bf16(same)], axis=0), log_f3)
    cum = sums[:rows]
    total = sums[rows:]
    q_dec = q * jnp.exp(cum)
    k_dec = k * jnp.exp(-cum)
    k_end = k * jnp.exp(total - cum)
    chunk_of_row = lax.broadcasted_iota(jnp.int32, (rows, 128), 0) >> (chunk.bit_length() - 1)
    onehot = _mask_bf16(chunk_of_row == lax.broadcasted_iota(jnp.int32, (rows, 128), 1))
    decay_cols = jnp.exp(_exact_mask_dot(lambda m, t: _dot_tn(t, m), onehot, log_f3))

    if chain:
        @pl.when(j == 0)
        def _():
            state_ref[...] = s0_ref[0]
            prev_ref[...] = jnp.concatenate([pa_ref[...], pb_ref[...]], axis=0)

    head_sl = [slice(h * HG_D, (h + 1) * HG_D) for h in range(HG_HEADS)]
    intra, inter, state = [], [[] for _ in head_sl], [None] * HG_HEADS
    for sl in head_sl:
        att = jnp.where(incl, _dot_nt(_bf(q_dec[:, sl]), _bf(k_dec[:, sl])), 0.0)
        intra.append(_dot(_bf(att), _bf(v[:, sl])))
    for c in range(n_chunks):
        rs = slice(c * chunk, (c + 1) * chunk)
        for h, sl in enumerate(head_sl):
            if chain:
                s = state_ref[h] if c == 0 else state[h]
            else:
                s = s0_ref[c, h]
            inter[h].append(_dot(_bf(q_dec[rs, sl]), _bf(s)))
            state[h] = decay_cols[sl, c:c + 1] * s + _dot_tn(_bf(k_end[rs, sl]), _bf(v[rs, sl]))
            if not chain:
                s_out_ref[c, h] = state[h]
    outs = []
    for h, sl in enumerate(head_sl):
        if chain:
            state_ref[h] = state[h]
        outs.append(_rms_gate(intra[h] + jnp.concatenate(inter[h], axis=0), nw_ref[...], gate[:, sl]))

    if chain:
        prev = prev_ref[...]
        t = j * rows + lax.broadcasted_iota(jnp.int32, (rows, POOL_GC), 0)
    else:
        prev_a, prev_b = pa_ref[...], pb_ref[...]
        t = lax.broadcasted_iota(jnp.int32, (rows, POOL_GC), 0) & 7
    for gi, win in enumerate(POOL_WINDOWS):
        gs = slice(gi * POOL_GC, (gi + 1) * POOL_GC)
        ug = u[:, gs]
        wsum = ug
        for d in range(1, win):
            if chain:
                wsum = wsum + _shift_rows_chain(ug, prev[:, gs], d)
            elif d <= 8:
                wsum = wsum + _shift_rows_batch(ug, prev_b[:, gs], d)
            else:
                wsum = wsum + _shift_rows_batch(prev_b[:, gs], prev_a[:, gs], d - 8)
        cnt = jnp.minimum(win, pos0 + t + 1).astype(jnp.float32)
        diff = wsum / cnt - ug
        outs.append(_dot(_bf(diff), pw_ref[gi]) * ps_ref[:, gs])
    mix_ref[...] = jnp.concatenate(outs, axis=-1).astype(mix_ref.dtype)

    if chain:
        prev_ref[...] = u[rows - 16:]

        @pl.when(j == pl.num_programs(1) - 1)
        def _():
            s_out_ref[0] = state_ref[...]
            pa_out_ref[...] = u[rows - 16:rows - 8]
            pb_out_ref[...] = u[rows - 8:]
    else:
        pa_out_ref[...] = prev_b
        pb_out_ref[...] = u


def _mixer_ab(x, row0, n_seq, length, pos0, s0, pool_state, w_in, lb, norm_w, pool_w, pool_scale):
    n_rows = n_seq * length
    chain = length >= ROW_TILE
    rows = ROW_TILE if chain else BATCH_ROW_TILE
    assert row0 % rows == 0
    first = row0 // rows
    pool16 = jnp.pad(pool_state, ((0, 0), (1, 0), (0, 0)))
    pa = pool16[:, :8].reshape(n_seq * 8, POOL_WIDTH)
    pb = pool16[:, 8:].reshape(n_seq * 8, POOL_WIDTH)
    if chain:
        assert length % rows == 0
        tiles = length // rows
        grid = (n_seq, tiles)
        row_map = lambda b, j: (b * tiles + j, 0)
        x_map = lambda b, j: (first + b * tiles + j, 0)
        seq_map4 = lambda b, j: (b, 0, 0, 0)
        seq_map2 = lambda b, j: (b, 0)
        const2 = lambda b, j: (0, 0)
        const3 = lambda b, j: (0, 0, 0)
        seq_block = 1
    else:
        assert length == 8 and n_rows % rows == 0
        seq_block = rows // 8
        grid = (n_rows // rows,)
        row_map = lambda i: (i, 0)
        x_map = lambda i: (first + i, 0)
        seq_map4 = lambda i: (i, 0, 0, 0)
        seq_map2 = lambda i: (i, 0)
        const2 = lambda i: (0, 0)
        const3 = lambda i: (0, 0, 0)
    d_in = w_in.shape[1]
    mix, s_new, pa_new, pb_new = pl.pallas_call(
        functools.partial(_mixer_ab_kernel, chain=chain, pos0=pos0),
        grid=grid,
        in_specs=[
            pl.BlockSpec((rows, D_MODEL), x_map),
            pl.BlockSpec((D_MODEL, d_in), const2),
            pl.BlockSpec((1, HG_WIDTH), const2),
            pl.BlockSpec((1, HG_D), const2),
            pl.BlockSpec((len(POOL_WINDOWS), POOL_GC, POOL_GC), const3),
            pl.BlockSpec((1, POOL_WIDTH), const2),
            pl.BlockSpec((seq_block, HG_HEADS, HG_D, HG_D), seq_map4),
            pl.BlockSpec((seq_block * 8, POOL_WIDTH), seq_map2),
            pl.BlockSpec((seq_block * 8, POOL_WIDTH), seq_map2),
        ],
        out_specs=[
            pl.BlockSpec((rows, MIX_OUT), row_map),
            pl.BlockSpec((seq_block, HG_HEADS, HG_D, HG_D), seq_map4),
            pl.BlockSpec((seq_block * 8, POOL_WIDTH), seq_map2),
            pl.BlockSpec((seq_block * 8, POOL_WIDTH), seq_map2),
        ],
        out_shape=[
            jax.ShapeDtypeStruct((n_rows, MIX_OUT), jnp.bfloat16),
            jax.ShapeDtypeStruct(s0.shape, jnp.float32),
            jax.ShapeDtypeStruct(pa.shape, jnp.float32),
            jax.ShapeDtypeStruct(pb.shape, jnp.float32),
        ],
        scratch_shapes=[
            pltpu.VMEM((HG_HEADS, HG_D, HG_D), jnp.float32),
            pltpu.VMEM((16, POOL_WIDTH), jnp.float32),
        ],
        compiler_params=pltpu.CompilerParams(
            dimension_semantics=("arbitrary",) * len(grid), vmem_limit_bytes=VMEM_LIMIT),
        name="mixer_ab_chain" if chain else "mixer_ab_batch",
    )(x, w_in, lb.reshape(1, -1), norm_w.reshape(1, -1), pool_w, pool_scale.reshape(1, -1), s0, pa, pb)
    pool_new = jnp.concatenate([pa_new.reshape(n_seq, 8, POOL_WIDTH), pb_new.reshape(n_seq, 8, POOL_WIDTH)], axis=1)
    return mix, s_new, pool_new[:, 1:]


SC_WIDTH, SC_K = 512, 3
GD_HEADS, GD_D = 4, 128
GD_CONV = 4
GD_CHUNK = 64
GD_QKV = GD_HEADS * 3 * GD_D
CD_MAIN = 3 * SC_WIDTH + GD_QKV + GD_HEADS * GD_D


def _causal_conv(cur, prev, w_ref, shift_fn):
    width = w_ref.shape[0]
    acc = cur * w_ref[width - 1:width, :]
    for j in range(width - 1):
        acc = acc + shift_fn(cur, prev, width - 1 - j) * w_ref[j:j + 1, :]
    return acc


def _softplus(v):
    return jnp.maximum(v, 0.0) + jnp.log1p(jnp.exp(-jnp.abs(v)))


def _mixer_cd_kernel(x_ref, w_ref, wg_ref, scw_ref, gcw_ref, alog_ref, dtb_ref, nw_ref, s0_ref, scp_ref, gcp_ref,
                     mix_ref, s_out_ref, scp_out_ref, gcp_out_ref, state_ref, sc_prev_ref, gc_prev_ref, *, chain):
    rows = x_ref.shape[0]
    chunk = GD_CHUNK if chain else 8
    n_chunks = rows // chunk
    j = pl.program_id(1) if chain else 0
    shift_fn = _shift_rows_chain if chain else _shift_rows_batch

    xb = _bf(x_ref[...])
    proj = _dot(xb, w_ref[...])
    gates = _dot(xb, wg_ref[...])
    b_gate = proj[:, 0:SC_WIDTH]
    conv_in = proj[:, SC_WIDTH:2 * SC_WIDTH] * proj[:, 2 * SC_WIDTH:3 * SC_WIDTH]
    qkv = proj[:, 3 * SC_WIDTH:3 * SC_WIDTH + GD_QKV]
    z = proj[:, 3 * SC_WIDTH + GD_QKV:]

    if chain:
        @pl.when(j == 0)
        def _():
            state_ref[...] = s0_ref[0]
            sc_prev_ref[...] = scp_ref[...]
            gc_prev_ref[...] = gcp_ref[...]
        sc_prev, gc_prev = sc_prev_ref[...], gc_prev_ref[...]
    else:
        sc_prev, gc_prev = scp_ref[...], gcp_ref[...]

    outs = [b_gate * _causal_conv(conv_in, sc_prev, scw_ref, shift_fn)]
    qkv_c = _silu(_causal_conv(qkv, gc_prev, gcw_ref, shift_fn))

    lane8 = lax.broadcasted_iota(jnp.int32, gates.shape, 1)
    g_dec = -jnp.exp(alog_ref[...]) * _softplus(gates + dtb_ref[...])
    cols = jnp.where(lane8 < GD_HEADS, g_dec, jax.nn.sigmoid(gates))
    same, incl, strict = _chunk_masks(rows, chunk)
    cols3 = _split3(cols)
    gcum = _exact_mask_dot(_dot, _mask_bf16(incl), cols3)
    gtot = _exact_mask_dot(_dot, _mask_bf16(same), cols3)
    eye8 = _mask_bf16(lax.broadcasted_iota(jnp.int32, (8, 8), 0) == lax.broadcasted_iota(jnp.int32, (8, 8), 1))
    gcum_rows = _exact_mask_dot(_dot_nt, eye8, _split3(gcum))

    eye = (lax.broadcasted_iota(jnp.int32, (rows, rows), 0)
           == lax.broadcasted_iota(jnp.int32, (rows, rows), 1)).astype(jnp.float32)
    hd = []
    for h in range(GD_HEADS):
        sl = lambda part: slice(part * GD_HEADS * GD_D + h * GD_D, part * GD_HEADS * GD_D + (h + 1) * GD_D)
        q, k, v = qkv_c[:, sl(0)], qkv_c[:, sl(1)], qkv_c[:, sl(2)]
        q = q * lax.rsqrt(jnp.sum(q * q, axis=-1, keepdims=True) + 1e-6) * (GD_D ** -0.5)
        k = k * lax.rsqrt(jnp.sum(k * k, axis=-1, keepdims=True) + 1e-6)
        beta = cols[:, GD_HEADS + h:GD_HEADS + h + 1]
        gc = gcum[:, h:h + 1]
        gt = gtot[:, h:h + 1]
        decay = jnp.where(incl, jnp.exp(jnp.where(incl, gc - gcum_rows[h:h + 1, :], 0.0)), 0.0)
        k_beta = k * beta
        kb = _bf(k)
        egc = jnp.exp(gc)
        power = -jnp.where(strict, _dot_nt(_bf(k_beta), kb) * decay, 0.0)
        hd.append(dict(
            inv=eye + power, power2=_split2(power),
            att=_bf(jnp.where(incl, _dot_nt(_bf(q), kb) * decay, 0.0)),
            rhs=_split2(jnp.concatenate([v * beta, k_beta * egc], axis=1)),
            q_dec=q * egc, k_end=k * jnp.exp(gt - gc),
            decay_end=jnp.exp(gt), inter=[], v_new=[], s=None))

    for _ in range(chunk.bit_length() - 2):
        for d in hd:
            d["power2"] = _split2(_dot3(d["power2"], d["power2"]))
        for d in hd:
            d["inv"] = d["inv"] + _dot3(_split2(d["inv"]), d["power2"])
    for d in hd:
        uw = _dot3(_split2(d["inv"]), d["rhs"])
        d["u"], d["w"] = uw[:, :GD_D], uw[:, GD_D:]

    for c in range(n_chunks):
        rs = slice(c * chunk, (c + 1) * chunk)
        for h, d in enumerate(hd):
            if chain:
                s = state_ref[h] if c == 0 else d["s"]
            else:
                s = s0_ref[c, h]
            sb = _bf(s)
            v_new = d["u"][rs] - _dot(_bf(d["w"][rs]), sb)
            d["inter"].append(_dot(_bf(d["q_dec"][rs]), sb))
            d["s"] = d["decay_end"][c * chunk:c * chunk + 1, :] * s + _dot_tn(_bf(d["k_end"][rs]), _bf(v_new))
            d["v_new"].append(v_new)
            if not chain:
                s_out_ref[c, h] = d["s"]
    for h, d in enumerate(hd):
        if chain:
            state_ref[h] = d["s"]
        o = jnp.concatenate(d["inter"], axis=0) + _dot(d["att"], _bf(jnp.concatenate(d["v_new"], axis=0)))
        outs.append(_rms_gate(o, nw_ref[...], z[:, h * GD_D:(h + 1) * GD_D]))
    mix_ref[...] = jnp.concatenate(outs, axis=-1).astype(mix_ref.dtype)

    if chain:
        sc_prev_ref[...] = conv_in[rows - 8:]
        gc_prev_ref[...] = qkv[rows - 8:]

        @pl.when(j == pl.num_programs(1) - 1)
        def _():
            s_out_ref[0] = state_ref[...]
            scp_out_ref[...] = conv_in[rows - 8:]
            gcp_out_ref[...] = qkv[rows - 8:]
    else:
        scp_out_ref[...] = conv_in
        gcp_out_ref[...] = qkv


def _mixer_cd(x, row0, n_seq, length, s0, sconv_state, gconv_state, w_main, w_gates, sconv_w, gconv_w, a_log, dt_bias,
              norm_w):
    n_rows = n_seq * length
    chain = length >= ROW_TILE
    rows = ROW_TILE if chain else BATCH_ROW_TILE
    assert row0 % rows == 0
    first = row0 // rows
    pad8 = lambda st: jnp.pad(st, ((0, 0), (8 - st.shape[1], 0), (0, 0))).reshape(n_seq * 8, st.shape[2])
    scp, gcp = pad8(sconv_state), pad8(gconv_state)
    if chain:
        assert length % rows == 0
        tiles = length // rows
        grid = (n_seq, tiles)
        row_map = lambda b, j: (b * tiles + j, 0)
        x_map = lambda b, j: (first + b * tiles + j, 0)
        seq_map4 = lambda b, j: (b, 0, 0, 0)
        seq_map2 = lambda b, j: (b, 0)
        const2 = lambda b, j: (0, 0)
        seq_block = 1
    else:
        assert length == 8 and n_rows % rows == 0
        seq_block = rows // 8
        grid = (n_rows // rows,)
        row_map = lambda i: (i, 0)
        x_map = lambda i: (first + i, 0)
        seq_map4 = lambda i: (i, 0, 0, 0)
        seq_map2 = lambda i: (i, 0)
        const2 = lambda i: (0, 0)
    zeros4 = jnp.zeros((GD_HEADS,), jnp.float32)
    alog8 = jnp.concatenate([a_log.astype(jnp.float32), zeros4]).reshape(1, 8)
    dtb8 = jnp.concatenate([dt_bias.astype(jnp.float32), zeros4]).reshape(1, 8)
    mix, s_new, scp_new, gcp_new = pl.pallas_call(
        functools.partial(_mixer_cd_kernel, chain=chain),
        grid=grid,
        in_specs=[
            pl.BlockSpec((rows, D_MODEL), x_map),
            pl.BlockSpec((D_MODEL, CD_MAIN), const2),
            pl.BlockSpec((D_MODEL, 8), const2),
            pl.BlockSpec((SC_K, SC_WIDTH), const2),
            pl.BlockSpec((GD_CONV, GD_QKV), const2),
            pl.BlockSpec((1, 8), const2),
            pl.BlockSpec((1, 8), const2),
            pl.BlockSpec((1, GD_D), const2),
            pl.BlockSpec((seq_block, GD_HEADS, GD_D, GD_D), seq_map4),
            pl.BlockSpec((seq_block * 8, SC_WIDTH), seq_map2),
            pl.BlockSpec((seq_block * 8, GD_QKV), seq_map2),
        ],
        out_specs=[
            pl.BlockSpec((rows, MIX_OUT), row_map),
            pl.BlockSpec((seq_block, GD_HEADS, GD_D, GD_D), seq_map4),
            pl.BlockSpec((seq_block * 8, SC_WIDTH), seq_map2),
            pl.BlockSpec((seq_block * 8, GD_QKV), seq_map2),
        ],
        out_shape=[
            jax.ShapeDtypeStruct((n_rows, MIX_OUT), jnp.bfloat16),
            jax.ShapeDtypeStruct(s0.shape, jnp.float32),
            jax.ShapeDtypeStruct(scp.shape, jnp.float32),
            jax.ShapeDtypeStruct(gcp.shape, jnp.float32),
        ],
        scratch_shapes=[
            pltpu.VMEM((GD_HEADS, GD_D, GD_D), jnp.float32),
            pltpu.VMEM((8, SC_WIDTH), jnp.float32),
            pltpu.VMEM((8, GD_QKV), jnp.float32),
        ],
        compiler_params=pltpu.CompilerParams(
            dimension_semantics=("arbitrary",) * len(grid), vmem_limit_bytes=VMEM_LIMIT),
        name="mixer_cd_chain" if chain else "mixer_cd_batch",
    )(x, w_main, w_gates, sconv_w, gconv_w, alog8, dtb8, norm_w.reshape(1, -1), s0, scp, gcp)
    tail = lambda st, keep: st.reshape(n_seq, 8, -1)[:, 8 - keep:]
    return mix, s_new, tail(scp_new, SC_K - 1), tail(gcp_new, GD_CONV - 1)


def _outproj_ln_router_kernel(mix_ref, x_ref, w_ref, g_ref, b_ref, rw_ref, rb_ref,
                              x1_ref, xp_ref, tope_ref, gate_ref, rank_ref, cnt_ref):
    mix = jnp.dot(mix_ref[...], w_ref[...], preferred_element_type=jnp.float32)
    x1 = _layer_norm_rows(DEEPNORM_ALPHA * x_ref[...] + mix, g_ref[...], b_ref[...])
    x1_ref[...] = x1
    packed = _pack_bf16_pairs(x1)
    for j in range(HALF // 128):
        xp_ref[:, j * 8:(j + 1) * 8, :] = packed[:, j * 128:(j + 1) * 128].reshape(x1.shape[0] // 8, 8, 128)
    logits = jnp.dot(x1, rw_ref[...], preferred_element_type=jnp.float32,
                     precision=lax.Precision.HIGHEST) + rb_ref[...]
    lane = lax.broadcasted_iota(jnp.int32, logits.shape, 1)
    vals, idxs = [], []
    for _ in range(TOP_K):
        m = jnp.max(logits, axis=-1, keepdims=True)
        idx = jnp.min(jnp.where(logits == m, lane, N_EXPERTS), axis=-1, keepdims=True)
        vals.append(m)
        idxs.append(idx)
        logits = jnp.where(lane == idx, -jnp.inf, logits)
    ex = [jnp.exp(v - vals[0]) for v in vals]
    den = ex[0] + ex[1] + ex[2] + ex[3]
    tm = logits.shape[0]
    hits = [lane == idx for idx in idxs]
    member = sum(h.astype(jnp.float32) for h in hits)
    earlier = (lax.broadcasted_iota(jnp.int32, (tm, tm), 1) < lax.broadcasted_iota(jnp.int32, (tm, tm), 0))
    before = _dot(earlier.astype(jnp.bfloat16), member.astype(jnp.bfloat16))
    cnt_ref[0] = jnp.sum(member, axis=0, keepdims=True).astype(jnp.int32)
    col = lax.broadcasted_iota(jnp.int32, tope_ref.shape, 1)
    tope = jnp.zeros(tope_ref.shape, jnp.int32)
    gate = jnp.zeros(gate_ref.shape, jnp.float32)
    rank = jnp.zeros(rank_ref.shape, jnp.int32)
    for k in range(TOP_K):
        tope = jnp.where(col == k, idxs[k], tope)
        gate = jnp.where(col == k, ex[k] / den, gate)
        rank_k = jnp.sum(jnp.where(hits[k], before, 0.0), axis=-1, keepdims=True).astype(jnp.int32)
        rank = jnp.where(col == k, rank_k, rank)
    tope_ref[...] = tope
    gate_ref[...] = gate
    rank_ref[...] = rank


def _outproj_ln_router(mix, x, w_out, ln_g, ln_b, router_w, router_b):
    n = x.shape[0]
    tm = TOKEN_TILE
    row = lambda i: (i, 0)
    full = lambda i: (0, 0)
    return pl.pallas_call(
        _outproj_ln_router_kernel,
        grid=(n // tm,),
        in_specs=[
            pl.BlockSpec((tm, D_MODEL), row),
            pl.BlockSpec((tm, D_MODEL), row),
            pl.BlockSpec((D_MODEL, D_MODEL), full),
            pl.BlockSpec((1, D_MODEL), full),
            pl.BlockSpec((1, D_MODEL), full),
            pl.BlockSpec((D_MODEL, N_EXPERTS), full),
            pl.BlockSpec((1, N_EXPERTS), full),
        ],
        out_specs=[
            pl.BlockSpec((tm, D_MODEL), row),
            pl.BlockSpec((tm // 8, HALF // 16, 128), lambda i: (i, 0, 0)),
            pl.BlockSpec((tm, TOP_K), row),
            pl.BlockSpec((tm, TOP_K), row),
            pl.BlockSpec((tm, TOP_K), row),
            pl.BlockSpec((1, 1, N_EXPERTS), lambda i: (i, 0, 0)),
        ],
        out_shape=[
            jax.ShapeDtypeStruct((n, D_MODEL), jnp.float32),
            jax.ShapeDtypeStruct((n // 8, HALF // 16, 128), jnp.uint32),
            jax.ShapeDtypeStruct((n, TOP_K), jnp.int32),
            jax.ShapeDtypeStruct((n, TOP_K), jnp.float32),
            jax.ShapeDtypeStruct((n, TOP_K), jnp.int32),
            jax.ShapeDtypeStruct((n // tm, 1, N_EXPERTS), jnp.int32),
        ],
        compiler_params=pltpu.CompilerParams(dimension_semantics=("arbitrary",), vmem_limit_bytes=VMEM_LIMIT),
        name="outproj_ln_router",
    )(mix, x, w_out, ln_g.reshape(1, -1), ln_b.reshape(1, -1), router_w, router_b.reshape(1, -1))


def _route_plan(top_e, rank, tile_counts):
    n = top_e.shape[0]
    a = n * TOP_K
    br, tm, ch = EXPERT_ROWS, TOKEN_TILE, COMBINE_CHUNK
    n_virtual = N_EXPERTS * N_GROUPS
    group_tokens = n // N_GROUPS
    n_blocks = a // br + n_virtual + 1
    n_tiles = n // tm
    group_tiles = n_tiles // N_GROUPS
    max_chunks = (tm * TOP_K) // ch + N_EXPERTS + 1
    i32 = jnp.int32

    cnt = tile_counts.reshape(N_GROUPS, group_tiles, N_EXPERTS)
    nblk = (cnt.sum(axis=1) + br - 1) // br
    blk_end = jnp.cumsum(nblk.reshape(-1))
    row_off = ((blk_end - nblk.reshape(-1)) * br).reshape(N_GROUPS, 1, N_EXPERTS)
    seg_start = (row_off + jnp.cumsum(cnt, axis=1) - cnt).reshape(n_tiles, N_EXPERTS)
    seg_len = tile_counts
    blocks = jnp.arange(n_blocks, dtype=i32)
    block_ve = jnp.minimum(jnp.sum(blk_end[None, :] <= blocks[:, None], axis=1), n_virtual - 1).astype(i32)
    block_info = jnp.stack([block_ve % N_EXPERTS, block_ve // N_EXPERTS,
                            (blocks < blk_end[-1]).astype(i32)])

    aligned = seg_start // 8 * 8
    lead = seg_start - aligned
    nch = jnp.where(seg_len > 0, (lead + seg_len + ch - 1) // ch, 0)
    ch_end = jnp.cumsum(nch, axis=1)
    ch_first = ch_end - nch
    n_chunks = ch_end[:, -1]
    cidx = jnp.arange(max_chunks, dtype=i32)[None, :, None]
    owns = jnp.logical_and(ch_first[:, None, :] <= cidx, cidx < ch_end[:, None, :])
    chunk_src = jnp.sum(jnp.where(owns, aligned[:, None, :] + (cidx - ch_first[:, None, :]) * ch, 0), axis=2)

    hit = top_e.reshape(n_tiles, tm, TOP_K, 1) == jnp.arange(N_EXPERTS, dtype=i32)
    pick = lambda table: jnp.sum(jnp.where(hit, table[:, None, None, :], 0), axis=-1)
    rank = rank.reshape(n_tiles, tm, TOP_K)
    dest = pick(seg_start) + rank
    pos = pick(ch_first * ch + lead) + rank
    tok = jnp.arange(n, dtype=i32) % group_tokens
    addr = ((tok // 8) * (8 * (HALF // 128)) + tok % 8).reshape(n_tiles, tm, 1)
    row_addr = jnp.zeros((n_blocks * br,), i32).at[dest.reshape(-1)].set(jnp.broadcast_to(addr, dest.shape).reshape(-1))
    pos = (pos // 8) * (8 * LANE_TILES) + pos % 8
    return dict(row_addr=row_addr.reshape(n_blocks, 1, br), block_info=block_info,
                chunk_src=(chunk_src * LANE_TILES).astype(i32), n_chunks=n_chunks.astype(i32),
                pos=pos.reshape(n_tiles, 1, tm * TOP_K).astype(i32))


def _expert_kernel(info_ref, addr_ref, addr_next_ref, xq_hbm, wgu_ref, bgu_ref, wdn_ref, bdn_ref, y_ref,
                   xq_vmem, stage_ref, load_sem):
    b = pl.program_id(0)
    grp = info_ref[1, b]
    prev_grp = info_ref[1, jnp.maximum(b - 1, 0)]
    group_rows = xq_vmem.shape[0]
    slot = b % 2
    tiles = HALF // 128

    def gather_row(addr, dst_slot, group8, sub):
        stage_ref[dst_slot, group8, pl.ds(sub, tiles, stride=8), :] = xq_vmem[pl.ds(addr, tiles, stride=8), :]

    @pl.when(jnp.logical_or(b == 0, grp != prev_grp))
    def _():
        cp = pltpu.make_async_copy(xq_hbm.at[pl.ds(grp * group_rows, group_rows)], xq_vmem, load_sem)
        cp.start()
        cp.wait()

        def gather(i, carry):
            for u in range(8):
                gather_row(addr_ref[0, 0, i * 8 + u], slot, i, u)
            return carry

        lax.fori_loop(0, EXPERT_ROWS // 8, gather, 0)

    @pl.when(info_ref[2, b] == 1)
    def _():
        for r in range(EXPERT_ROWS):
            gather_row(addr_next_ref[0, 0, r], 1 - slot, r // 8, r % 8)
        pieces = [_unpack_bf16_pairs(stage_ref[slot, :, j * 8:(j + 1) * 8, :].reshape(EXPERT_ROWS, 128))
                  for j in range(tiles)]
        lo = jnp.concatenate([p[0] for p in pieces], axis=1)
        hi = jnp.concatenate([p[1] for p in pieces], axis=1)
        hgu = (jnp.dot(lo, wgu_ref[0, :HALF, :], preferred_element_type=jnp.float32)
               + jnp.dot(hi, wgu_ref[0, HALF:, :], preferred_element_type=jnp.float32) + bgu_ref[0])
        glu = jnp.minimum(hgu[:, :D_FF], SWIGLU_LIMIT)
        lin = jnp.clip(hgu[:, D_FF:], -SWIGLU_LIMIT, SWIGLU_LIMIT)
        act = (lin + 1.0) * glu * jax.nn.sigmoid(SWIGLU_ALPHA * glu)
        y = jnp.dot(act.astype(jnp.bfloat16), wdn_ref[0], preferred_element_type=jnp.float32) + bdn_ref[0]
        for c in range(LANE_TILES):
            y_ref[:, c * 8:(c + 1) * 8, :] = y[:, c * 128:(c + 1) * 128].reshape(EXPERT_ROWS // 8, 8, 128)

    @pl.when(info_ref[2, b] == 0)
    def _():
        y_ref[...] = jnp.zeros_like(y_ref)


def _expert_mlp(plan, xq, w_gu, b_gu, w_dn, b_dn):
    br = EXPERT_ROWS
    n_blocks = plan["row_addr"].shape[0]
    assert br >= COMBINE_CHUNK + 8
    pad_rows = n_blocks * br
    expert = lambda b, info: (info[0, b], 0, 0)
    grid_spec = pltpu.PrefetchScalarGridSpec(
        num_scalar_prefetch=1,
        grid=(n_blocks,),
        in_specs=[
            pl.BlockSpec((1, 1, br), lambda b, info: (b, 0, 0), memory_space=pltpu.SMEM),
            pl.BlockSpec((1, 1, br), lambda b, info: (jnp.minimum(b + 1, n_blocks - 1), 0, 0), memory_space=pltpu.SMEM),
            pl.BlockSpec(memory_space=pl.ANY),
            pl.BlockSpec((1, D_MODEL, 2 * D_FF), expert),
            pl.BlockSpec((1, 1, 2 * D_FF), expert),
            pl.BlockSpec((1, D_FF, D_MODEL), expert),
            pl.BlockSpec((1, 1, D_MODEL), expert),
        ],
        out_specs=pl.BlockSpec((br // 8, 8 * LANE_TILES, 128), lambda b, info: (b, 0, 0)),
        scratch_shapes=[
            pltpu.VMEM((xq.shape[0] // N_GROUPS, 128), jnp.uint32),
            pltpu.VMEM((2, br // 8, 8 * (HALF // 128), 128), jnp.uint32),
            pltpu.SemaphoreType.DMA(()),
        ],
    )
    return pl.pallas_call(
        _expert_kernel,
        grid_spec=grid_spec,
        out_shape=jax.ShapeDtypeStruct((pad_rows // 8, 8 * LANE_TILES, 128), jnp.float32),
        compiler_params=pltpu.CompilerParams(dimension_semantics=("arbitrary",), vmem_limit_bytes=VMEM_LIMIT),
        name="expert_mlp",
    )(plan["block_info"], plan["row_addr"], plan["row_addr"], xq, w_gu, b_gu.reshape(N_EXPERTS, 1, -1), w_dn,
      b_dn.reshape(N_EXPERTS, 1, -1))


def _combine_ln_kernel(src_ref, nch_ref, pos_ref, gate_ref, x1_ref, y_hbm, g_ref, b_ref, out_ref,
                       stage_ref, ffn_ref, sems):
    i = pl.program_id(0)
    n_tiles = pl.num_programs(0)
    chunk_rows = COMBINE_CHUNK * LANE_TILES

    def chunk_copy(tile, slot, c):
        src = pl.multiple_of(src_ref[tile, c], 8 * LANE_TILES)
        return pltpu.make_async_copy(y_hbm.at[pl.ds(src, chunk_rows)],
                                     stage_ref.at[slot, pl.ds(c * chunk_rows, chunk_rows)], sems.at[slot])

    def issue(tile, slot):
        lax.fori_loop(0, nch_ref[tile], lambda c, carry: (chunk_copy(tile, slot, c).start(), carry)[1], 0)

    @pl.when(i == 0)
    def _():
        issue(0, 0)

    @pl.when(i + 1 < n_tiles)
    def _():
        issue(i + 1, (i + 1) % 2)

    slot = i % 2
    lax.fori_loop(0, nch_ref[i], lambda c, carry: (chunk_copy(i, slot, c).wait(), carry)[1], 0)

    tm = x1_ref.shape[0]

    def combine(j, carry):
        for u in range(8):
            t = j * 8 + u
            acc = None
            for k in range(TOP_K):
                row = stage_ref[slot, pl.ds(pos_ref[0, 0, t * TOP_K + k], LANE_TILES, stride=8), :]
                row = row * gate_ref[0, 0, t * TOP_K + k]
                acc = row if acc is None else acc + row
            ffn_ref[j, pl.ds(u, LANE_TILES, stride=8), :] = acc
        return carry

    lax.fori_loop(0, tm // 8, combine, 0)
    ffn = jnp.concatenate([ffn_ref[:, c * 8:(c + 1) * 8, :].reshape(tm, 128) for c in range(LANE_TILES)], axis=1)
    out_ref[...] = _layer_norm_rows(DEEPNORM_ALPHA * x1_ref[...] + ffn, g_ref[...], b_ref[...])


def _combine_ln(plan, gate, x1, y_flat, ln_g, ln_b):
    n = x1.shape[0]
    tm = TOKEN_TILE
    n_tiles = n // tm
    max_chunks = plan["chunk_src"].shape[1]
    row = lambda i, src, nch: (i, 0)
    full = lambda i, src, nch: (0, 0)
    per_tile = lambda i, src, nch: (i, 0, 0)
    grid_spec = pltpu.PrefetchScalarGridSpec(
        num_scalar_prefetch=2,
        grid=(n_tiles,),
        in_specs=[
            pl.BlockSpec((1, 1, tm * TOP_K), per_tile, memory_space=pltpu.SMEM),
            pl.BlockSpec((1, 1, tm * TOP_K), per_tile, memory_space=pltpu.SMEM),
            pl.BlockSpec((tm, D_MODEL), row),
            pl.BlockSpec(memory_space=pl.ANY),
            pl.BlockSpec((1, D_MODEL), full),
            pl.BlockSpec((1, D_MODEL), full),
        ],
        out_specs=pl.BlockSpec((tm, D_MODEL), row),
        scratch_shapes=[
            pltpu.VMEM((2, max_chunks * COMBINE_CHUNK * LANE_TILES, 128), jnp.float32),
            pltpu.VMEM((tm // 8, 8 * LANE_TILES, 128), jnp.float32),
            pltpu.SemaphoreType.DMA((2,)),
        ],
    )
    return pl.pallas_call(
        _combine_ln_kernel,
        grid_spec=grid_spec,
        out_shape=jax.ShapeDtypeStruct((n, D_MODEL), jnp.float32),
        compiler_params=pltpu.CompilerParams(dimension_semantics=("arbitrary",), vmem_limit_bytes=VMEM_LIMIT),
        name="combine_ln",
    )(plan["chunk_src"], plan["n_chunks"], plan["pos"], gate.reshape(n_tiles, 1, tm * TOP_K), x1, y_flat,
      ln_g.reshape(1, -1), ln_b.reshape(1, -1))


def _post_mixer(mix, x, w_out, ln1_g, ln1_b, router_w, router_b, w_gu, b_gu, w_dn, b_dn, ln2_g, ln2_b):
    x1, xp, top_e, gate, rank, tile_counts = _outproj_ln_router(mix, x, w_out, ln1_g, ln1_b, router_w, router_b)
    plan = _route_plan(top_e, rank, tile_counts.reshape(-1, N_EXPERTS))
    y_sorted = _expert_mlp(plan, xp.reshape(-1, 128), w_gu, b_gu, w_dn, b_dn)
    return _combine_ln(plan, gate, x1, y_sorted.reshape(-1, 128), ln2_g, ln2_b)


def kernel(x_prompt, x_sample, state_hgrn, state_pool, state_sconv, state_gdn_conv, state_gdn, w_in_ab, hgrn_lower_bounds, hgrn_norm_w, pool_w, pool_scale, w_out_ab, w_in_cd, sconv_w, gdn_conv_w, gdn_a_log, gdn_dt_bias, gdn_norm_w, w_out_cd, ln1_g, ln1_b, ln2_g, ln2_b, router_w, router_b, w_gu, b_gu, w_dn, b_dn):
    bp, sp, _ = x_prompt.shape
    bs, ss, _ = x_sample.shape
    n_p, n_s = bp * sp, bs * ss
    bf = jnp.bfloat16
    f32 = jnp.float32
    lower_bounds = jnp.cumsum(jax.nn.softmax(hgrn_lower_bounds.astype(f32), axis=0), axis=0)
    x = jnp.concatenate([x_prompt.reshape(n_p, D_MODEL), x_sample.reshape(n_s, D_MODEL)], axis=0)
    zeros_like_prompt = lambda st: jnp.zeros((bp,) + st.shape[2:], f32)

    states = {}
    for l in range(DEPTH):
        if l % 2 == 0:
            args = (w_in_ab[0].astype(bf), lower_bounds[l], hgrn_norm_w[0], pool_w[0].astype(bf), pool_scale[0])
            mp, hp, pp = _mixer_ab(x, 0, bp, sp, 0, zeros_like_prompt(state_hgrn), zeros_like_prompt(state_pool), *args)
            ms, hs, ps = _mixer_ab(x, n_p, bs, ss, PAST_LEN, state_hgrn[0], state_pool[0], *args)
            states.update(hp=hp[None], hs=hs[None], pp=pp[None], ps=ps[None])
            w_out = w_out_ab[0]
        else:
            args = (w_in_cd[0][:, :CD_MAIN].astype(bf), w_in_cd[0][:, CD_MAIN:].astype(bf), sconv_w[0], gdn_conv_w[0],
                    gdn_a_log[0], gdn_dt_bias[0], gdn_norm_w[0])
            mp, gp, scp, gcp = _mixer_cd(x, 0, bp, sp, zeros_like_prompt(state_gdn), zeros_like_prompt(state_sconv),
                                         zeros_like_prompt(state_gdn_conv), *args)
            ms, gs, scs, gcs = _mixer_cd(x, n_p, bs, ss, state_gdn[0], state_sconv[0], state_gdn_conv[0], *args)
            states.update(scp=scp[None], scs=scs[None], gcp=gcp[None], gcs=gcs[None], gp=gp[None], gs=gs[None])
            w_out = w_out_cd[0]
        mix = jnp.concatenate([mp, ms], axis=0)
        x = _post_mixer(mix, x, w_out.astype(bf), ln1_g[l], ln1_b[l], router_w[l], router_b[l],
                        w_gu[l].astype(bf), b_gu[l], w_dn[l].astype(bf), b_dn[l], ln2_g[l], ln2_b[l])
    return (x[:n_p].reshape(bp, sp, D_MODEL), x[n_p:].reshape(bs, ss, D_MODEL),
            states["hp"], states["hs"], states["pp"], states["ps"], states["scp"], states["scs"],
            states["gcp"], states["gcs"], states["gp"], states["gs"])
```

```python
import functools

import jax
import jax.numpy as jnp
from jax import lax
from jax.experimental import pallas as pl
from jax.experimental.pallas import tpu as pltpu

D_MODEL = 1024
DEPTH = 2
N_EXPERTS = 32
TOP_K = 4
D_FF = 1024
SWIGLU_LIMIT = 7.0
SWIGLU_ALPHA = 1.702
LN_EPS = 1e-5
DEEPNORM_ALPHA = (2 * DEPTH) ** 0.25

HALF = D_MODEL // 2
LANE_TILES = D_MODEL // 128
TOKEN_TILE = 256
EXPERT_ROWS = 256
N_GROUPS = 2
COMBINE_CHUNK = 16
VMEM_LIMIT = 56 * 1024 * 1024
MIX_OUT = 1024
BATCH_ROW_TILE = 128
PAST_LEN = 16384


def _layer_norm_rows(v, g, b):
    mu = jnp.mean(v, axis=-1, keepdims=True)
    d = v - mu
    var = jnp.mean(d * d, axis=-1, keepdims=True)
    return d * lax.rsqrt(var + LN_EPS) * g + b


def _pack_bf16_pairs(v):
    lo = pltpu.bitcast(v[:, :HALF].astype(jnp.bfloat16).astype(jnp.float32), jnp.uint32)
    hi = pltpu.bitcast(v[:, HALF:].astype(jnp.bfloat16).astype(jnp.float32), jnp.uint32)
    return (lo >> 16) | (hi & jnp.uint32(0xFFFF0000))


def _unpack_bf16_pairs(p):
    lo = pltpu.bitcast(p << 16, jnp.float32).astype(jnp.bfloat16)
    hi = pltpu.bitcast(p & jnp.uint32(0xFFFF0000), jnp.float32).astype(jnp.bfloat16)
    return lo, hi


ROW_TILE = 256
HG_HEADS, HG_D = 4, 128
HG_WIDTH = HG_HEADS * HG_D
HG_CHUNK = 16
POOL_WINDOWS = (2, 4, 8, 16)
POOL_GC = 128
POOL_WIDTH = 512
RMS_EPS = 1e-6
_HI = lax.Precision.HIGHEST


def _silu(v):
    return v * jax.nn.sigmoid(v)


def _bf(v):
    return v.astype(jnp.bfloat16)


def _dot(a, b, precision=None):
    return jnp.dot(a, b, preferred_element_type=jnp.float32, precision=precision)


def _dot_nt(a, b, precision=None):
    return lax.dot_general(a, b, (((1,), (1,)), ((), ())), preferred_element_type=jnp.float32, precision=precision)


def _dot_tn(a, b, precision=None):
    return lax.dot_general(a, b, (((0,), (0,)), ((), ())), preferred_element_type=jnp.float32, precision=precision)


def _split3(v):
    hi = _bf(v)
    rest = v - hi.astype(jnp.float32)
    mid = _bf(rest)
    return hi, mid, _bf(rest - mid.astype(jnp.float32))


def _mask_bf16(mask):
    return _bf(mask.astype(jnp.float32))


def _exact_mask_dot(dot_fn, m, terms):
    return dot_fn(m, terms[0]) + (dot_fn(m, terms[1]) + dot_fn(m, terms[2]))


def _split2(v):
    hi = _bf(v)
    return hi, _bf(v - hi.astype(jnp.float32))


def _dot3(a2, b2):
    return _dot(a2[0], b2[0]) + (_dot(a2[0], b2[1]) + _dot(a2[1], b2[0]))


def _chunk_masks(rows, chunk):
    shift = chunk.bit_length() - 1
    t = lax.broadcasted_iota(jnp.int32, (rows, rows), 0)
    s = lax.broadcasted_iota(jnp.int32, (rows, rows), 1)
    same = (t >> shift) == (s >> shift)
    return same, jnp.logical_and(same, s <= t), jnp.logical_and(same, s < t)


def _shift_rows_chain(cur, prev, j):
    if j == 0:
        return cur
    p = prev.shape[0]
    rc = pltpu.roll(cur, j, 0)
    rp = prev if j == p else pltpu.roll(prev, j, 0)
    row = lax.broadcasted_iota(jnp.int32, (p, cur.shape[1]), 0)
    head = jnp.where(row < j, rp, rc[:p])
    return jnp.concatenate([head, rc[p:]], axis=0) if cur.shape[0] > p else head


def _shift_rows_batch(cur, prev, j):
    if j == 0:
        return cur
    if j == 8:
        return prev
    rows = cur.shape[0]
    row = lax.broadcasted_iota(jnp.int32, cur.shape, 0)
    return jnp.where((row & 7) < j, pltpu.roll(prev, rows + j - 8, 0), pltpu.roll(cur, j, 0))


def _rms_gate(o, norm_w, gate):
    ms = jnp.mean(o * o, axis=-1, keepdims=True)
    return o * lax.rsqrt(ms + RMS_EPS) * norm_w * _silu(gate)


def _mixer_ab_kernel(x_ref, w_ref, lb_ref, nw_ref, pw_ref, ps_ref, s0_ref, pa_ref, pb_ref,
                     mix_ref, s_out_ref, pa_out_ref, pb_out_ref, state_ref, prev_ref, *, chain, pos0):
    rows = x_ref.shape[0]
    chunk = HG_CHUNK if chain else 8
    n_chunks = rows // chunk
    j = pl.program_id(1) if chain else 0

    proj = _dot(_bf(x_ref[...]), w_ref[...])
    q = _silu(proj[:, 0:HG_WIDTH])
    f = lb_ref[...] + (1.0 - lb_ref[...]) * jax.nn.sigmoid(proj[:, HG_WIDTH:2 * HG_WIDTH])
    v = _silu(proj[:, 2 * HG_WIDTH:3 * HG_WIDTH])
    gate = proj[:, 3 * HG_WIDTH:4 * HG_WIDTH]
    u = proj[:, 4 * HG_WIDTH:]
    log_f = jnp.log(f)
    k = 1.0 - f

    same, incl, _ = _chunk_masks(rows, chunk)
    log_f3 = _split3(log_f)
    sums = _exact_mask_dot(_dot, jnp.concatenate([_mask_bf16(incl), _mask_bf16(same)], axis=0), log_f3)
    cum = sums[:rows]
    total = sums[rows:]
    q_dec = q * jnp.exp(cum)
    k_dec = k * jnp.exp(-cum)
    k_end = k * jnp.exp(total - cum)
    chunk_of_row = lax.broadcasted_iota(jnp.int32, (rows, 128), 0) >> (chunk.bit_length() - 1)
    onehot = _mask_bf16(chunk_of_row == lax.broadcasted_iota(jnp.int32, (rows, 128), 1))
    decay_cols = jnp.exp(_exact_mask_dot(lambda m, t: _dot_tn(t, m), onehot, log_f3))

    if chain:
        @pl.when(j == 0)
        def _():
            state_ref[...] = s0_ref[0]
            prev_ref[...] = jnp.concatenate([pa_ref[...], pb_ref[...]], axis=0)

    head_sl = [slice(h * HG_D, (h + 1) * HG_D) for h in range(HG_HEADS)]
    intra, inter, state = [], [[] for _ in head_sl], [None] * HG_HEADS
    for sl in head_sl:
        att = jnp.where(incl, _dot_nt(_bf(q_dec[:, sl]), _bf(k_dec[:, sl])), 0.0)
        intra.append(_dot(_bf(att), _bf(v[:, sl])))
    for c in range(n_chunks):
        rs = slice(c * chunk, (c + 1) * chunk)
        for h, sl in enumerate(head_sl):
            if chain:
                s = state_ref[h] if c == 0 else state[h]
            else:
                s = s0_ref[c, h]
            inter[h].append(_dot(_bf(q_dec[rs, sl]), _bf(s)))
            state[h] = decay_cols[sl, c:c + 1] * s + _dot_tn(_bf(k_end[rs, sl]), _bf(v[rs, sl]))
            if not chain:
                s_out_ref[c, h] = state[h]
    outs = []
    for h, sl in enumerate(head_sl):
        if chain:
            state_ref[h] = state[h]
        outs.append(_rms_gate(intra[h] + jnp.concatenate(inter[h], axis=0), nw_ref[...], gate[:, sl]))

    if chain:
        prev = prev_ref[...]
        t = j * rows + lax.broadcasted_iota(jnp.int32, (rows, POOL_GC), 0)
    else:
        prev_a, prev_b = pa_ref[...], pb_ref[...]
        t = lax.broadcasted_iota(jnp.int32, (rows, POOL_GC), 0) & 7
    for gi, win in enumerate(POOL_WINDOWS):
        gs = slice(gi * POOL_GC, (gi + 1) * POOL_GC)
        ug = u[:, gs]
        wsum = ug
        for d in range(1, win):
            if chain:
                wsum = wsum + _shift_rows_chain(ug, prev[:, gs], d)
            elif d <= 8:
                wsum = wsum + _shift_rows_batch(ug, prev_b[:, gs], d)
            else:
                wsum = wsum + _shift_rows_batch(prev_b[:, gs], prev_a[:, gs], d - 8)
        cnt = jnp.minimum(win, pos0 + t + 1).astype(jnp.float32)
        diff = wsum / cnt - ug
        outs.append(_dot(_bf(diff), pw_ref[gi]) * ps_ref[:, gs])
    mix_ref[...] = jnp.concatenate(outs, axis=-1).astype(mix_ref.dtype)

    if chain:
        prev_ref[...] = u[rows - 16:]

        @pl.when(j == pl.num_programs(1) - 1)
        def _():
            s_out_ref[0] = state_ref[...]
            pa_out_ref[...] = u[rows - 16:rows - 8]
            pb_out_ref[...] = u[rows - 8:]
    else:
        pa_out_ref[...] = prev_b
        pb_out_ref[...] = u


def _mixer_ab(x, row0, n_seq, length, pos0, s0, pool_state, w_in, lb, norm_w, pool_w, pool_scale):
    n_rows = n_seq * length
    chain = length >= ROW_TILE
    rows = ROW_TILE if chain else BATCH_ROW_TILE
    assert row0 % rows == 0
    first = row0 // rows
    pool16 = jnp.pad(pool_state, ((0, 0), (1, 0), (0, 0)))
    pa = pool16[:, :8].reshape(n_seq * 8, POOL_WIDTH)
    pb = pool16[:, 8:].reshape(n_seq * 8, POOL_WIDTH)
    if chain:
        assert length % rows == 0
        tiles = length // rows
        grid = (n_seq, tiles)
        row_map = lambda b, j: (b * tiles + j, 0)
        x_map = lambda b, j: (first + b * tiles + j, 0)
        seq_map4 = lambda b, j: (b, 0, 0, 0)
        seq_map2 = lambda b, j: (b, 0)
        const2 = lambda b, j: (0, 0)
        const3 = lambda b, j: (0, 0, 0)
        seq_block = 1
    else:
        assert length == 8 and n_rows % rows == 0
        seq_block = rows // 8
        grid = (n_rows // rows,)
        row_map = lambda i: (i, 0)
        x_map = lambda i: (first + i, 0)
        seq_map4 = lambda i: (i, 0, 0, 0)
        seq_map2 = lambda i: (i, 0)
        const2 = lambda i: (0, 0)
        const3 = lambda i: (0, 0, 0)
    d_in = w_in.shape[1]
    mix, s_new, pa_new, pb_new = pl.pallas_call(
        functools.partial(_mixer_ab_kernel, chain=chain, pos0=pos0),
        grid=grid,
        in_specs=[
            pl.BlockSpec((rows, D_MODEL), x_map),
            pl.BlockSpec((D_MODEL, d_in), const2),
            pl.BlockSpec((1, HG_WIDTH), const2),
            pl.BlockSpec((1, HG_D), const2),
            pl.BlockSpec((len(POOL_WINDOWS), POOL_GC, POOL_GC), const3),
            pl.BlockSpec((1, POOL_WIDTH), const2),
            pl.BlockSpec((seq_block, HG_HEADS, HG_D, HG_D), seq_map4),
            pl.BlockSpec((seq_block * 8, POOL_WIDTH), seq_map2),
            pl.BlockSpec((seq_block * 8, POOL_WIDTH), seq_map2),
        ],
        out_specs=[
            pl.BlockSpec((rows, MIX_OUT), row_map),
            pl.BlockSpec((seq_block, HG_HEADS, HG_D, HG_D), seq_map4),
            pl.BlockSpec((seq_block * 8, POOL_WIDTH), seq_map2),
            pl.BlockSpec((seq_block * 8, POOL_WIDTH), seq_map2),
        ],
        out_shape=[
            jax.ShapeDtypeStruct((n_rows, MIX_OUT), jnp.bfloat16),
            jax.ShapeDtypeStruct(s0.shape, jnp.float32),
            jax.ShapeDtypeStruct(pa.shape, jnp.float32),
            jax.ShapeDtypeStruct(pb.shape, jnp.float32),
        ],
        scratch_shapes=[
            pltpu.VMEM((HG_HEADS, HG_D, HG_D), jnp.float32),
            pltpu.VMEM((16, POOL_WIDTH), jnp.float32),
        ],
        compiler_params=pltpu.CompilerParams(
            dimension_semantics=("arbitrary",) * len(grid), vmem_limit_bytes=VMEM_LIMIT),
        name="mixer_ab_chain" if chain else "mixer_ab_batch",
    )(x, w_in, lb.reshape(1, -1), norm_w.reshape(1, -1), pool_w, pool_scale.reshape(1, -1), s0, pa, pb)
    pool_new = jnp.concatenate([pa_new.reshape(n_seq, 8, POOL_WIDTH), pb_new.reshape(n_seq, 8, POOL_WIDTH)], axis=1)
    return mix, s_new, pool_new[:, 1:]


SC_WIDTH, SC_K = 512, 3
GD_HEADS, GD_D = 4, 128
GD_CONV = 4
GD_CHUNK = 64
GD_QKV = GD_HEADS * 3 * GD_D
CD_MAIN = 3 * SC_WIDTH + GD_QKV + GD_HEADS * GD_D


def _causal_conv(cur, prev, w_ref, shift_fn):
    width = w_ref.shape[0]
    acc = cur * w_ref[width - 1:width, :]
    for j in range(width - 1):
        acc = acc + shift_fn(cur, prev, width - 1 - j) * w_ref[j:j + 1, :]
    return acc


def _softplus(v):
    return jnp.maximum(v, 0.0) + jnp.log1p(jnp.exp(-jnp.abs(v)))


def _mixer_cd_kernel(x_ref, w_ref, wg_ref, scw_ref, gcw_ref, alog_ref, dtb_ref, nw_ref, s0_ref, scp_ref, gcp_ref,
                     mix_ref, s_out_ref, scp_out_ref, gcp_out_ref, state_ref, sc_prev_ref, gc_prev_ref, *, chain):
    rows = x_ref.shape[0]
    chunk = GD_CHUNK if chain else 8
    n_chunks = rows // chunk
    j = pl.program_id(1) if chain else 0
    shift_fn = _shift_rows_chain if chain else _shift_rows_batch

    xb = _bf(x_ref[...])
    proj = _dot(xb, w_ref[...])
    gates = _dot(xb, wg_ref[...])
    b_gate = proj[:, 0:SC_WIDTH]
    conv_in = proj[:, SC_WIDTH:2 * SC_WIDTH] * proj[:, 2 * SC_WIDTH:3 * SC_WIDTH]
    qkv = proj[:, 3 * SC_WIDTH:3 * SC_WIDTH + GD_QKV]
    z = proj[:, 3 * SC_WIDTH + GD_QKV:]

    if chain:
        @pl.when(j == 0)
        def _():
            state_ref[...] = s0_ref[0]
            sc_prev_ref[...] = scp_ref[...]
            gc_prev_ref[...] = gcp_ref[...]
        sc_prev, gc_prev = sc_prev_ref[...], gc_prev_ref[...]
    else:
        sc_prev, gc_prev = scp_ref[...], gcp_ref[...]

    outs = [b_gate * _causal_conv(conv_in, sc_prev, scw_ref, shift_fn)]
    qkv_c = _silu(_causal_conv(qkv, gc_prev, gcw_ref, shift_fn))

    lane8 = lax.broadcasted_iota(jnp.int32, gates.shape, 1)
    g_dec = -jnp.exp(alog_ref[...]) * _softplus(gates + dtb_ref[...])
    cols = jnp.where(lane8 < GD_HEADS, g_dec, jax.nn.sigmoid(gates))
    same, incl, strict = _chunk_masks(rows, chunk)
    cols3 = _split3(cols)
    gcum = _exact_mask_dot(_dot, _mask_bf16(incl), cols3)
    gtot = _exact_mask_dot(_dot, _mask_bf16(same), cols3)
    eye8 = _mask_bf16(lax.broadcasted_iota(jnp.int32, (8, 8), 0) == lax.broadcasted_iota(jnp.int32, (8, 8), 1))
    gcum_rows = _exact_mask_dot(_dot_nt, eye8, _split3(gcum))

    eye = (lax.broadcasted_iota(jnp.int32, (rows, rows), 0)
           == lax.broadcasted_iota(jnp.int32, (rows, rows), 1)).astype(jnp.float32)
    hd = []
    for h in range(GD_HEADS):
        sl = lambda part: slice(part * GD_HEADS * GD_D + h * GD_D, part * GD_HEADS * GD_D + (h + 1) * GD_D)
        q, k, v = qkv_c[:, sl(0)], qkv_c[:, sl(1)], qkv_c[:, sl(2)]
        q = q * lax.rsqrt(jnp.sum(q * q, axis=-1, keepdims=True) + 1e-6) * (GD_D ** -0.5)
        k = k * lax.rsqrt(jnp.sum(k * k, axis=-1, keepdims=True) + 1e-6)
        beta = cols[:, GD_HEADS + h:GD_HEADS + h + 1]
        gc = gcum[:, h:h + 1]
        gt = gtot[:, h:h + 1]
        decay = jnp.where(incl, jnp.exp(jnp.where(incl, gc - gcum_rows[h:h + 1, :], 0.0)), 0.0)
        k_beta = k * beta
        kb = _bf(k)
        egc = jnp.exp(gc)
        power = -jnp.where(strict, _dot_nt(_bf(k_beta), kb) * decay, 0.0)
        hd.append(dict(
            inv=eye + power, power2=_split2(power),
            att=_bf(jnp.where(incl, _dot_nt(_bf(q), kb) * decay, 0.0)),
            rhs=_split2(jnp.concatenate([v * beta, k_beta * egc], axis=1)),
            q_dec=q * egc, k_end=k * jnp.exp(gt - gc),
            decay_end=jnp.exp(gt), inter=[], v_new=[], s=None))

    for _ in range(chunk.bit_length() - 2):
        for d in hd:
            d["power2"] = _split2(_dot3(d["power2"], d["power2"]))
        for d in hd:
            d["inv"] = d["inv"] + _dot3(_split2(d["inv"]), d["power2"])
    for d in hd:
        uw = _dot3(_split2(d["inv"]), d["rhs"])
        d["u"], d["w"] = uw[:, :GD_D], uw[:, GD_D:]

    for c in range(n_chunks):
        rs = slice(c * chunk, (c + 1) * chunk)
        for h, d in enumerate(hd):
            if chain:
                s = state_ref[h] if c == 0 else d["s"]
            else:
                s = s0_ref[c, h]
            sb = _bf(s)
            v_new = d["u"][rs] - _dot(_bf(d["w"][rs]), sb)
            d["inter"].append(_dot(_bf(d["q_dec"][rs]), sb))
            d["s"] = d["decay_end"][c * chunk:c * chunk + 1, :] * s + _dot_tn(_bf(d["k_end"][rs]), _bf(v_new))
            d["v_new"].append(v_new)
            if not chain:
                s_out_ref[c, h] = d["s"]
    for h, d in enumerate(hd):
        if chain:
            state_ref[h] = d["s"]
        o = jnp.concatenate(d["inter"], axis=0) + _dot(d["att"], _bf(jnp.concatenate(d["v_new"], axis=0)))
        outs.append(_rms_gate(o, nw_ref[...], z[:, h * GD_D:(h + 1) * GD_D]))
    mix_ref[...] = jnp.concatenate(outs, axis=-1).astype(mix_ref.dtype)

    if chain:
        sc_prev_ref[...] = conv_in[rows - 8:]
        gc_prev_ref[...] = qkv[rows - 8:]

        @pl.when(j == pl.num_programs(1) - 1)
        def _():
            s_out_ref[0] = state_ref[...]
            scp_out_ref[...] = conv_in[rows - 8:]
            gcp_out_ref[...] = qkv[rows - 8:]
    else:
        scp_out_ref[...] = conv_in
        gcp_out_ref[...] = qkv


def _mixer_cd(x, row0, n_seq, length, s0, sconv_state, gconv_state, w_main, w_gates, sconv_w, gconv_w, a_log, dt_bias,
              norm_w):
    n_rows = n_seq * length
    chain = length >= ROW_TILE
    rows = ROW_TILE if chain else BATCH_ROW_TILE
    assert row0 % rows == 0
    first = row0 // rows
    pad8 = lambda st: jnp.pad(st, ((0, 0), (8 - st.shape[1], 0), (0, 0))).reshape(n_seq * 8, st.shape[2])
    scp, gcp = pad8(sconv_state), pad8(gconv_state)
    if chain:
        assert length % rows == 0
        tiles = length // rows
        grid = (n_seq, tiles)
        row_map = lambda b, j: (b * tiles + j, 0)
        x_map = lambda b, j: (first + b * tiles + j, 0)
        seq_map4 = lambda b, j: (b, 0, 0, 0)
        seq_map2 = lambda b, j: (b, 0)
        const2 = lambda b, j: (0, 0)
        seq_block = 1
    else:
        assert length == 8 and n_rows % rows == 0
        seq_block = rows // 8
        grid = (n_rows // rows,)
        row_map = lambda i: (i, 0)
        x_map = lambda i: (first + i, 0)
        seq_map4 = lambda i: (i, 0, 0, 0)
        seq_map2 = lambda i: (i, 0)
        const2 = lambda i: (0, 0)
    zeros4 = jnp.zeros((GD_HEADS,), jnp.float32)
    alog8 = jnp.concatenate([a_log.astype(jnp.float32), zeros4]).reshape(1, 8)
    dtb8 = jnp.concatenate([dt_bias.astype(jnp.float32), zeros4]).reshape(1, 8)
    mix, s_new, scp_new, gcp_new = pl.pallas_call(
        functools.partial(_mixer_cd_kernel, chain=chain),
        grid=grid,
        in_specs=[
            pl.BlockSpec((rows, D_MODEL), x_map),
            pl.BlockSpec((D_MODEL, CD_MAIN), const2),
            pl.BlockSpec((D_MODEL, 8), const2),
            pl.BlockSpec((SC_K, SC_WIDTH), const2),
            pl.BlockSpec((GD_CONV, GD_QKV), const2),
            pl.BlockSpec((1, 8), const2),
            pl.BlockSpec((1, 8), const2),
            pl.BlockSpec((1, GD_D), const2),
            pl.BlockSpec((seq_block, GD_HEADS, GD_D, GD_D), seq_map4),
            pl.BlockSpec((seq_block * 8, SC_WIDTH), seq_map2),
            pl.BlockSpec((seq_block * 8, GD_QKV), seq_map2),
        ],
        out_specs=[
            pl.BlockSpec((rows, MIX_OUT), row_map),
            pl.BlockSpec((seq_block, GD_HEADS, GD_D, GD_D), seq_map4),
            pl.BlockSpec((seq_block * 8, SC_WIDTH), seq_map2),
            pl.BlockSpec((seq_block * 8, GD_QKV), seq_map2),
        ],
        out_shape=[
            jax.ShapeDtypeStruct((n_rows, MIX_OUT), jnp.bfloat16),
            jax.ShapeDtypeStruct(s0.shape, jnp.float32),
            jax.ShapeDtypeStruct(scp.shape, jnp.float32),
            jax.ShapeDtypeStruct(gcp.shape, jnp.float32),
        ],
        scratch_shapes=[
            pltpu.VMEM((GD_HEADS, GD_D, GD_D), jnp.float32),
            pltpu.VMEM((8, SC_WIDTH), jnp.float32),
            pltpu.VMEM((8, GD_QKV), jnp.float32),
        ],
        compiler_params=pltpu.CompilerParams(
            dimension_semantics=("arbitrary",) * len(grid), vmem_limit_bytes=VMEM_LIMIT),
        name="mixer_cd_chain" if chain else "mixer_cd_batch",
    )(x, w_main, w_gates, sconv_w, gconv_w, alog8, dtb8, norm_w.reshape(1, -1), s0, scp, gcp)
    tail = lambda st, keep: st.reshape(n_seq, 8, -1)[:, 8 - keep:]
    return mix, s_new, tail(scp_new, SC_K - 1), tail(gcp_new, GD_CONV - 1)


def _outproj_ln_router_kernel(mix_ref, x_ref, w_ref, g_ref, b_ref, rw_ref, rb_ref,
                              x1_ref, xp_ref, tope_ref, gate_ref, rank_ref, cnt_ref):
    mix = jnp.dot(mix_ref[...], w_ref[...], preferred_element_type=jnp.float32)
    x1 = _layer_norm_rows(DEEPNORM_ALPHA * x_ref[...] + mix, g_ref[...], b_ref[...])
    x1_ref[...] = x1
    packed = _pack_bf16_pairs(x1)
    for j in range(HALF // 128):
        xp_ref[:, j * 8:(j + 1) * 8, :] = packed[:, j * 128:(j + 1) * 128].reshape(x1.shape[0] // 8, 8, 128)
    logits = _dot(_bf(x1), _bf(rw_ref[...])) + rb_ref[...]
    lane = lax.broadcasted_iota(jnp.int32, logits.shape, 1)
    vals, idxs = [], []
    for _ in range(TOP_K):
        m = jnp.max(logits, axis=-1, keepdims=True)
        idx = jnp.min(jnp.where(logits == m, lane, N_EXPERTS), axis=-1, keepdims=True)
        vals.append(m)
        idxs.append(idx)
        logits = jnp.where(lane == idx, -jnp.inf, logits)
    ex = [jnp.exp(v - vals[0]) for v in vals]
    den = ex[0] + ex[1] + ex[2] + ex[3]
    tm = logits.shape[0]
    hits = [lane == idx for idx in idxs]
    member = sum(h.astype(jnp.float32) for h in hits)
    earlier = (lax.broadcasted_iota(jnp.int32, (tm, tm), 1) < lax.broadcasted_iota(jnp.int32, (tm, tm), 0))
    before = _dot(earlier.astype(jnp.bfloat16), member.astype(jnp.bfloat16))
    cnt_ref[0] = jnp.sum(member, axis=0, keepdims=True).astype(jnp.int32)
    col = lax.broadcasted_iota(jnp.int32, tope_ref.shape, 1)
    tope = jnp.zeros(tope_ref.shape, jnp.int32)
    gate = jnp.zeros(gate_ref.shape, jnp.float32)
    rank = jnp.zeros(rank_ref.shape, jnp.int32)
    for k in range(TOP_K):
        tope = jnp.where(col == k, idxs[k], tope)
        gate = jnp.where(col == k, ex[k] / den, gate)
        rank_k = jnp.sum(jnp.where(hits[k], before, 0.0), axis=-1, keepdims=True).astype(jnp.int32)
        rank = jnp.where(col == k, rank_k, rank)
    tope_ref[...] = tope
    gate_ref[...] = gate
    rank_ref[...] = rank


def _outproj_ln_router(mix, x, w_out, ln_g, ln_b, router_w, router_b):
    n = x.shape[0]
    tm = TOKEN_TILE
    row = lambda i: (i, 0)
    full = lambda i: (0, 0)
    return pl.pallas_call(
        _outproj_ln_router_kernel,
        grid=(n // tm,),
        in_specs=[
            pl.BlockSpec((tm, D_MODEL), row),
            pl.BlockSpec((tm, D_MODEL), row),
            pl.BlockSpec((D_MODEL, D_MODEL), full),
            pl.BlockSpec((1, D_MODEL), full),
            pl.BlockSpec((1, D_MODEL), full),
            pl.BlockSpec((D_MODEL, N_EXPERTS), full),
            pl.BlockSpec((1, N_EXPERTS), full),
        ],
        out_specs=[
            pl.BlockSpec((tm, D_MODEL), row),
            pl.BlockSpec((tm // 8, HALF // 16, 128), lambda i: (i, 0, 0)),
            pl.BlockSpec((tm, TOP_K), row),
            pl.BlockSpec((tm, TOP_K), row),
            pl.BlockSpec((tm, TOP_K), row),
            pl.BlockSpec((1, 1, N_EXPERTS), lambda i: (i, 0, 0)),
        ],
        out_shape=[
            jax.ShapeDtypeStruct((n, D_MODEL), jnp.float32),
            jax.ShapeDtypeStruct((n // 8, HALF // 16, 128), jnp.uint32),
            jax.ShapeDtypeStruct((n, TOP_K), jnp.int32),
            jax.ShapeDtypeStruct((n, TOP_K), jnp.float32),
            jax.ShapeDtypeStruct((n, TOP_K), jnp.int32),
            jax.ShapeDtypeStruct((n // tm, 1, N_EXPERTS), jnp.int32),
        ],
        compiler_params=pltpu.CompilerParams(dimension_semantics=("arbitrary",), vmem_limit_bytes=VMEM_LIMIT),
        name="outproj_ln_router",
    )(mix, x, w_out, ln_g.reshape(1, -1), ln_b.reshape(1, -1), router_w, router_b.reshape(1, -1))


def _scatter_rows_kernel(dest_ref, out_ref, *, group_tokens):
    i = pl.program_id(0)
    per_tile = dest_ref.shape[2]
    tm = per_tile // TOP_K

    @pl.when(i == 0)
    def _():
        def clear(j, carry):
            for u in range(8):
                out_ref[j * 8 + u] = 0
            return carry

        lax.fori_loop(0, out_ref.shape[0] // 8, clear, 0)

    first_tok = lax.rem(i, group_tokens // tm) * tm

    def scatter(j, carry):
        for u in range(8):
            a = j * 8 + u
            tok = first_tok + a // TOP_K
            out_ref[dest_ref[0, 0, a]] = (tok >> 3) * (8 * (HALF // 128)) + (tok & 7)
        return carry

    lax.fori_loop(0, per_tile // 8, scatter, 0)


def _scatter_row_addresses(dest, n_rows, group_tokens):
    n_tiles, _, per_tile = dest.shape
    assert n_rows % 8 == 0 and per_tile % 8 == 0
    return pl.pallas_call(
        functools.partial(_scatter_rows_kernel, group_tokens=group_tokens),
        grid=(n_tiles,),
        in_specs=[pl.BlockSpec((1, 1, per_tile), lambda i: (i, 0, 0), memory_space=pltpu.SMEM)],
        out_specs=pl.BlockSpec((n_rows,), lambda i: (0,), memory_space=pltpu.SMEM),
        out_shape=jax.ShapeDtypeStruct((n_rows,), jnp.int32),
        compiler_params=pltpu.CompilerParams(dimension_semantics=("arbitrary",)),
        name="scatter_row_addresses",
    )(dest)


def _route_plan(top_e, rank, tile_counts):
    n = top_e.shape[0]
    a = n * TOP_K
    br, tm, ch = EXPERT_ROWS, TOKEN_TILE, COMBINE_CHUNK
    n_virtual = N_EXPERTS * N_GROUPS
    group_tokens = n // N_GROUPS
    n_blocks = a // br + n_virtual + 1
    n_tiles = n // tm
    group_tiles = n_tiles // N_GROUPS
    max_chunks = (tm * TOP_K) // ch + N_EXPERTS + 1
    i32 = jnp.int32

    cnt = tile_counts.reshape(N_GROUPS, group_tiles, N_EXPERTS)
    nblk = (cnt.sum(axis=1) + br - 1) // br
    blk_end = jnp.cumsum(nblk.reshape(-1))
    row_off = ((blk_end - nblk.reshape(-1)) * br).reshape(N_GROUPS, 1, N_EXPERTS)
    seg_start = (row_off + jnp.cumsum(cnt, axis=1) - cnt).reshape(n_tiles, N_EXPERTS)
    seg_len = tile_counts
    blocks = jnp.arange(n_blocks, dtype=i32)
    block_ve = jnp.minimum(jnp.sum(blk_end[None, :] <= blocks[:, None], axis=1), n_virtual - 1).astype(i32)
    block_info = jnp.stack([block_ve % N_EXPERTS, block_ve // N_EXPERTS,
                            (blocks < blk_end[-1]).astype(i32)])

    aligned = seg_start // 8 * 8
    lead = seg_start - aligned
    nch = jnp.where(seg_len > 0, (lead + seg_len + ch - 1) // ch, 0)
    ch_end = jnp.cumsum(nch, axis=1)
    ch_first = ch_end - nch
    n_chunks = ch_end[:, -1]
    cidx = jnp.arange(max_chunks, dtype=i32)[None, :, None]
    owns = jnp.logical_and(ch_first[:, None, :] <= cidx, cidx < ch_end[:, None, :])
    chunk_src = jnp.sum(jnp.where(owns, aligned[:, None, :] + (cidx - ch_first[:, None, :]) * ch, 0), axis=2)

    hit = top_e.reshape(n_tiles, tm, TOP_K, 1) == jnp.arange(N_EXPERTS, dtype=i32)
    pick = lambda table: jnp.sum(jnp.where(hit, table[:, None, None, :], 0), axis=-1)
    rank = rank.reshape(n_tiles, tm, TOP_K)
    dest = pick(seg_start) + rank
    pos = pick(ch_first * ch + lead) + rank
    row_addr = _scatter_row_addresses(dest.reshape(n_tiles, 1, tm * TOP_K).astype(i32), n_blocks * br, group_tokens)
    pos = (pos // 8) * (8 * LANE_TILES) + pos % 8
    return dict(row_addr=row_addr.reshape(n_blocks, 1, br), block_info=block_info,
                chunk_src=(chunk_src * LANE_TILES).astype(i32), n_chunks=n_chunks.astype(i32),
                pos=pos.reshape(n_tiles, 1, tm * TOP_K).astype(i32))


def _expert_kernel(info_ref, addr_ref, addr_next_ref, xq_hbm, wgu_f32_ref, bgu_ref, wdn_ref, bdn_ref, y_ref,
                   xq_vmem, stage_ref, wgu_ref, load_sem):
    b = pl.program_id(0)
    grp = info_ref[1, b]
    prev_grp = info_ref[1, jnp.maximum(b - 1, 0)]

    @pl.when(jnp.logical_or(b == 0, info_ref[0, b] != info_ref[0, jnp.maximum(b - 1, 0)]))
    def _():
        rows = D_MODEL // 8

        def cast(i, carry):
            r0 = pl.multiple_of(i * rows, rows)
            wgu_ref[pl.ds(r0, rows), :] = wgu_f32_ref[0, 0, pl.ds(r0, rows), :].astype(jnp.bfloat16)
            return carry

        lax.fori_loop(0, 8, cast, 0)

    group_rows = xq_vmem.shape[0]
    slot = b % 2
    tiles = HALF // 128

    def gather_row(addr, dst_slot, group8, sub):
        stage_ref[dst_slot, group8, pl.ds(sub, tiles, stride=8), :] = xq_vmem[pl.ds(addr, tiles, stride=8), :]

    @pl.when(jnp.logical_or(b == 0, grp != prev_grp))
    def _():
        cp = pltpu.make_async_copy(xq_hbm.at[pl.ds(grp * group_rows, group_rows)], xq_vmem, load_sem)
        cp.start()
        cp.wait()

        def gather(i, carry):
            for u in range(8):
                gather_row(addr_ref[0, 0, i * 8 + u], slot, i, u)
            return carry

        lax.fori_loop(0, EXPERT_ROWS // 8, gather, 0)

    @pl.when(info_ref[2, b] == 1)
    def _():
        for r in range(EXPERT_ROWS):
            gather_row(addr_next_ref[0, 0, r], 1 - slot, r // 8, r % 8)
        pieces = [_unpack_bf16_pairs(stage_ref[slot, :, j * 8:(j + 1) * 8, :].reshape(EXPERT_ROWS, 128))
                  for j in range(tiles)]
        lo = jnp.concatenate([p[0] for p in pieces], axis=1)
        hi = jnp.concatenate([p[1] for p in pieces], axis=1)
        hgu = (jnp.dot(lo, wgu_ref[:HALF, :], preferred_element_type=jnp.float32)
               + jnp.dot(hi, wgu_ref[HALF:, :], preferred_element_type=jnp.float32) + bgu_ref[0, 0])
        glu = jnp.minimum(hgu[:, :D_FF], SWIGLU_LIMIT)
        lin = jnp.clip(hgu[:, D_FF:], -SWIGLU_LIMIT, SWIGLU_LIMIT)
        act = (lin + 1.0) * glu * jax.nn.sigmoid(SWIGLU_ALPHA * glu)
        y = jnp.dot(act.astype(jnp.bfloat16), wdn_ref[0, 0], preferred_element_type=jnp.float32) + bdn_ref[0, 0]
        for c in range(LANE_TILES):
            y_ref[:, c * 8:(c + 1) * 8, :] = y[:, c * 128:(c + 1) * 128].reshape(EXPERT_ROWS // 8, 8, 128)

    @pl.when(info_ref[2, b] == 0)
    def _():
        y_ref[...] = jnp.zeros_like(y_ref)


def _expert_mlp(plan, xq, layer, w_gu, b_gu, w_dn, b_dn):
    br = EXPERT_ROWS
    n_blocks = plan["row_addr"].shape[0]
    assert br >= COMBINE_CHUNK + 8
    pad_rows = n_blocks * br
    expert = lambda b, info: (layer, info[0, b], 0, 0)
    grid_spec = pltpu.PrefetchScalarGridSpec(
        num_scalar_prefetch=1,
        grid=(n_blocks,),
        in_specs=[
            pl.BlockSpec((1, 1, br), lambda b, info: (b, 0, 0), memory_space=pltpu.SMEM),
            pl.BlockSpec((1, 1, br), lambda b, info: (jnp.minimum(b + 1, n_blocks - 1), 0, 0), memory_space=pltpu.SMEM),
            pl.BlockSpec(memory_space=pl.ANY),
            pl.BlockSpec((1, 1, D_MODEL, 2 * D_FF), expert),
            pl.BlockSpec((1, 1, 1, 2 * D_FF), expert),
            pl.BlockSpec((1, 1, D_FF, D_MODEL), expert),
            pl.BlockSpec((1, 1, 1, D_MODEL), expert),
        ],
        out_specs=pl.BlockSpec((br // 8, 8 * LANE_TILES, 128), lambda b, info: (b, 0, 0)),
        scratch_shapes=[
            pltpu.VMEM((xq.shape[0] // N_GROUPS, 128), jnp.uint32),
            pltpu.VMEM((2, br // 8, 8 * (HALF // 128), 128), jnp.uint32),
            pltpu.VMEM((D_MODEL, 2 * D_FF), jnp.bfloat16),
            pltpu.SemaphoreType.DMA(()),
        ],
    )
    return pl.pallas_call(
        _expert_kernel,
        grid_spec=grid_spec,
        out_shape=jax.ShapeDtypeStruct((pad_rows // 8, 8 * LANE_TILES, 128), jnp.float32),
        compiler_params=pltpu.CompilerParams(dimension_semantics=("arbitrary",), vmem_limit_bytes=VMEM_LIMIT),
        name="expert_mlp",
    )(plan["block_info"], plan["row_addr"], plan["row_addr"], xq, w_gu, b_gu.reshape(DEPTH, N_EXPERTS, 1, -1), w_dn,
      b_dn.reshape(DEPTH, N_EXPERTS, 1, -1))


def _combine_ln_kernel(src_ref, nch_ref, pos_ref, gate_ref, x1_ref, y_hbm, g_ref, b_ref, out_ref,
                       stage_ref, ffn_ref, sems):
    i = pl.program_id(0)
    n_tiles = pl.num_programs(0)
    chunk_rows = COMBINE_CHUNK * LANE_TILES

    def chunk_copy(tile, slot, c):
        src = pl.multiple_of(src_ref[tile, c], 8 * LANE_TILES)
        return pltpu.make_async_copy(y_hbm.at[pl.ds(src, chunk_rows)],
                                     stage_ref.at[slot, pl.ds(c * chunk_rows, chunk_rows)], sems.at[slot])

    def issue(tile, slot):
        lax.fori_loop(0, nch_ref[tile], lambda c, carry: (chunk_copy(tile, slot, c).start(), carry)[1], 0)

    @pl.when(i == 0)
    def _():
        issue(0, 0)

    @pl.when(i + 1 < n_tiles)
    def _():
        issue(i + 1, (i + 1) % 2)

    slot = i % 2
    lax.fori_loop(0, nch_ref[i], lambda c, carry: (chunk_copy(i, slot, c).wait(), carry)[1], 0)

    tm = x1_ref.shape[0]

    def combine(j, carry):
        for u in range(8):
            t = j * 8 + u
            acc = None
            for k in range(TOP_K):
                row = stage_ref[slot, pl.ds(pos_ref[0, 0, t * TOP_K + k], LANE_TILES, stride=8), :]
                row = row * gate_ref[0, 0, t * TOP_K + k]
                acc = row if acc is None else acc + row
            ffn_ref[j, pl.ds(u, LANE_TILES, stride=8), :] = acc
        return carry

    lax.fori_loop(0, tm // 8, combine, 0)
    ffn = jnp.concatenate([ffn_ref[:, c * 8:(c + 1) * 8, :].reshape(tm, 128) for c in range(LANE_TILES)], axis=1)
    out_ref[...] = _layer_norm_rows(DEEPNORM_ALPHA * x1_ref[...] + ffn, g_ref[...], b_ref[...])


def _combine_ln(plan, gate, x1, y_flat, ln_g, ln_b):
    n = x1.shape[0]
    tm = TOKEN_TILE
    n_tiles = n // tm
    max_chunks = plan["chunk_src"].shape[1]
    row = lambda i, src, nch: (i, 0)
    full = lambda i, src, nch: (0, 0)
    per_tile = lambda i, src, nch: (i, 0, 0)
    grid_spec = pltpu.PrefetchScalarGridSpec(
        num_scalar_prefetch=2,
        grid=(n_tiles,),
        in_specs=[
            pl.BlockSpec((1, 1, tm * TOP_K), per_tile, memory_space=pltpu.SMEM),
            pl.BlockSpec((1, 1, tm * TOP_K), per_tile, memory_space=pltpu.SMEM),
            pl.BlockSpec((tm, D_MODEL), row),
            pl.BlockSpec(memory_space=pl.ANY),
            pl.BlockSpec((1, D_MODEL), full),
            pl.BlockSpec((1, D_MODEL), full),
        ],
        out_specs=pl.BlockSpec((tm, D_MODEL), row),
        scratch_shapes=[
            pltpu.VMEM((2, max_chunks * COMBINE_CHUNK * LANE_TILES, 128), jnp.float32),
            pltpu.VMEM((tm // 8, 8 * LANE_TILES, 128), jnp.float32),
            pltpu.SemaphoreType.DMA((2,)),
        ],
    )
    return pl.pallas_call(
        _combine_ln_kernel,
        grid_spec=grid_spec,
        out_shape=jax.ShapeDtypeStruct((n, D_MODEL), jnp.float32),
        compiler_params=pltpu.CompilerParams(dimension_semantics=("arbitrary",), vmem_limit_bytes=VMEM_LIMIT),
        name="combine_ln",
    )(plan["chunk_src"], plan["n_chunks"], plan["pos"], gate.reshape(n_tiles, 1, tm * TOP_K), x1, y_flat,
      ln_g.reshape(1, -1), ln_b.reshape(1, -1))


def _post_mixer(mix, x, layer, w_out, ln1_g, ln1_b, router_w, router_b, w_gu, b_gu, w_dn, b_dn, ln2_g, ln2_b):
    x1, xp, top_e, gate, rank, tile_counts = _outproj_ln_router(mix, x, w_out, ln1_g, ln1_b, router_w, router_b)
    plan = _route_plan(top_e, rank, tile_counts.reshape(-1, N_EXPERTS))
    y_sorted = _expert_mlp(plan, xp.reshape(-1, 128), layer, w_gu, b_gu, w_dn, b_dn)
    return _combine_ln(plan, gate, x1, y_sorted.reshape(-1, 128), ln2_g, ln2_b)


def kernel(x_prompt, x_sample, state_hgrn, state_pool, state_sconv, state_gdn_conv, state_gdn, w_in_ab, hgrn_lower_bounds, hgrn_norm_w, pool_w, pool_scale, w_out_ab, w_in_cd, sconv_w, gdn_conv_w, gdn_a_log, gdn_dt_bias, gdn_norm_w, w_out_cd, ln1_g, ln1_b, ln2_g, ln2_b, router_w, router_b, w_gu, b_gu, w_dn, b_dn):
    bp, sp, _ = x_prompt.shape
    bs, ss, _ = x_sample.shape
    n_p, n_s = bp * sp, bs * ss
    bf = jnp.bfloat16
    f32 = jnp.float32
    lower_bounds = jnp.cumsum(jax.nn.softmax(hgrn_lower_bounds.astype(f32), axis=0), axis=0)
    x = jnp.concatenate([x_prompt.reshape(n_p, D_MODEL), x_sample.reshape(n_s, D_MODEL)], axis=0)
    zeros_like_prompt = lambda st: jnp.zeros((bp,) + st.shape[2:], f32)
    w_dn_bf = w_dn.astype(bf)

    states = {}
    for l in range(DEPTH):
        if l % 2 == 0:
            args = (w_in_ab[0].astype(bf), lower_bounds[l], hgrn_norm_w[0], pool_w[0].astype(bf), pool_scale[0])
            mp, hp, pp = _mixer_ab(x, 0, bp, sp, 0, zeros_like_prompt(state_hgrn), zeros_like_prompt(state_pool), *args)
            ms, hs, ps = _mixer_ab(x, n_p, bs, ss, PAST_LEN, state_hgrn[0], state_pool[0], *args)
            states.update(hp=hp[None], hs=hs[None], pp=pp[None], ps=ps[None])
            w_out = w_out_ab[0]
        else:
            args = (w_in_cd[0][:, :CD_MAIN].astype(bf), w_in_cd[0][:, CD_MAIN:].astype(bf), sconv_w[0], gdn_conv_w[0],
                    gdn_a_log[0], gdn_dt_bias[0], gdn_norm_w[0])
            mp, gp, scp, gcp = _mixer_cd(x, 0, bp, sp, zeros_like_prompt(state_gdn), zeros_like_prompt(state_sconv),
                                         zeros_like_prompt(state_gdn_conv), *args)
            ms, gs, scs, gcs = _mixer_cd(x, n_p, bs, ss, state_gdn[0], state_sconv[0], state_gdn_conv[0], *args)
            states.update(scp=scp[None], scs=scs[None], gcp=gcp[None], gcs=gcs[None], gp=gp[None], gs=gs[None])
            w_out = w_out_cd[0]
        mix = jnp.concatenate([mp, ms], axis=0)
        x = _post_mixer(mix, x, l, w_out.astype(bf), ln1_g[l], ln1_b[l], router_w[l], router_b[l],
                        w_gu, b_gu, w_dn_bf, b_dn, ln2_g[l], ln2_b[l])
    return (x[:n_p].reshape(bp, sp, D_MODEL), x[n_p:].reshape(bs, ss, D_MODEL),
            states["hp"], states["hs"], states["pp"], states["ps"], states["scp"], states["scs"],
            states["gcp"], states["gcs"], states["gp"], states["gs"])
```

```python
import functools

import jax
import jax.numpy as jnp
from jax import lax
from jax.experimental import pallas as pl
from jax.experimental.pallas import tpu as pltpu

D_MODEL = 1024
DEPTH = 2
N_EXPERTS = 32
TOP_K = 4
D_FF = 1024
SWIGLU_LIMIT = 7.0
SWIGLU_ALPHA = 1.702
LN_EPS = 1e-5
DEEPNORM_ALPHA = (2 * DEPTH) ** 0.25

HALF = D_MODEL // 2
LANE_TILES = D_MODEL // 128
TOKEN_TILE = 256
EXPERT_ROWS = 256
N_GROUPS = 2
COMBINE_CHUNK = 16
VMEM_LIMIT = 56 * 1024 * 1024
MIX_OUT = 1024
BATCH_ROW_TILE = 128
PAST_LEN = 16384


def _layer_norm_rows(v, g, b):
    mu = jnp.mean(v, axis=-1, keepdims=True)
    d = v - mu
    var = jnp.mean(d * d, axis=-1, keepdims=True)
    return d * lax.rsqrt(var + LN_EPS) * g + b


def _pack_bf16_pairs(v):
    lo = pltpu.bitcast(v[:, :HALF].astype(jnp.bfloat16).astype(jnp.float32), jnp.uint32)
    hi = pltpu.bitcast(v[:, HALF:].astype(jnp.bfloat16).astype(jnp.float32), jnp.uint32)
    return (lo >> 16) | (hi & jnp.uint32(0xFFFF0000))


def _unpack_bf16_pairs(p):
    lo = pltpu.bitcast(p << 16, jnp.float32).astype(jnp.bfloat16)
    hi = pltpu.bitcast(p & jnp.uint32(0xFFFF0000), jnp.float32).astype(jnp.bfloat16)
    return lo, hi


ROW_TILE = 256
HG_HEADS, HG_D = 4, 128
HG_WIDTH = HG_HEADS * HG_D
HG_CHUNK = 16
POOL_WINDOWS = (2, 4, 8, 16)
POOL_GC = 128
POOL_WIDTH = 512
RMS_EPS = 1e-6
_HI = lax.Precision.HIGHEST


def _silu(v):
    return v * jax.nn.sigmoid(v)


def _bf(v):
    return v.astype(jnp.bfloat16)


def _dot(a, b, precision=None):
    return jnp.dot(a, b, preferred_element_type=jnp.float32, precision=precision)


def _dot_nt(a, b, precision=None):
    return lax.dot_general(a, b, (((1,), (1,)), ((), ())), preferred_element_type=jnp.float32, precision=precision)


def _dot_tn(a, b, precision=None):
    return lax.dot_general(a, b, (((0,), (0,)), ((), ())), preferred_element_type=jnp.float32, precision=precision)


def _split3(v):
    hi = _bf(v)
    rest = v - hi.astype(jnp.float32)
    mid = _bf(rest)
    return hi, mid, _bf(rest - mid.astype(jnp.float32))


def _mask_bf16(mask):
    return _bf(mask.astype(jnp.float32))


def _exact_mask_dot(dot_fn, m, terms):
    return dot_fn(m, terms[0]) + (dot_fn(m, terms[1]) + dot_fn(m, terms[2]))


def _split2(v):
    hi = _bf(v)
    return hi, _bf(v - hi.astype(jnp.float32))


def _dot3(a2, b2):
    return _dot(a2[0], b2[0]) + (_dot(a2[0], b2[1]) + _dot(a2[1], b2[0]))


def _chunk_masks(rows, chunk):
    shift = chunk.bit_length() - 1
    t = lax.broadcasted_iota(jnp.int32, (rows, rows), 0)
    s = lax.broadcasted_iota(jnp.int32, (rows, rows), 1)
    same = (t >> shift) == (s >> shift)
    return same, jnp.logical_and(same, s <= t), jnp.logical_and(same, s < t)


def _shift_rows_chain(cur, prev, j):
    if j == 0:
        return cur
    p = prev.shape[0]
    rc = pltpu.roll(cur, j, 0)
    rp = prev if j == p else pltpu.roll(prev, j, 0)
    row = lax.broadcasted_iota(jnp.int32, (p, cur.shape[1]), 0)
    head = jnp.where(row < j, rp, rc[:p])
    return jnp.concatenate([head, rc[p:]], axis=0) if cur.shape[0] > p else head


def _shift_rows_batch(cur, prev, j):
    if j == 0:
        return cur
    if j == 8:
        return prev
    rows = cur.shape[0]
    row = lax.broadcasted_iota(jnp.int32, cur.shape, 0)
    return jnp.where((row & 7) < j, pltpu.roll(prev, rows + j - 8, 0), pltpu.roll(cur, j, 0))


def _rms_gate(o, norm_w, gate):
    ms = jnp.mean(o * o, axis=-1, keepdims=True)
    return o * lax.rsqrt(ms + RMS_EPS) * norm_w * _silu(gate)


def _mixer_ab_kernel(x_ref, w_ref, lb_ref, nw_ref, pw_ref, ps_ref, s0_ref, pa_ref, pb_ref,
                     mix_ref, s_out_ref, pa_out_ref, pb_out_ref, state_ref, prev_ref, *, chain, pos0):
    rows = x_ref.shape[0]
    chunk = HG_CHUNK if chain else 8
    n_chunks = rows // chunk
    j = pl.program_id(1) if chain else 0

    proj = _dot(_bf(x_ref[...]), w_ref[...])
    q = _silu(proj[:, 0:HG_WIDTH])
    f = lb_ref[...] + (1.0 - lb_ref[...]) * jax.nn.sigmoid(proj[:, HG_WIDTH:2 * HG_WIDTH])
    v = _silu(proj[:, 2 * HG_WIDTH:3 * HG_WIDTH])
    gate = proj[:, 3 * HG_WIDTH:4 * HG_WIDTH]
    u = proj[:, 4 * HG_WIDTH:]
    log_f = jnp.log(f)
    k = 1.0 - f

    same, incl, _ = _chunk_masks(rows, chunk)
    log_f3 = _split3(log_f)
    sums = _exact_mask_dot(_dot, jnp.concatenate([_mask_bf16(incl), _mask_bf16(same)], axis=0), log_f3)
    cum = sums[:rows]
    total = sums[rows:]
    q_dec = q * jnp.exp(cum)
    k_dec = k * jnp.exp(-cum)
    k_end = k * jnp.exp(total - cum)
    chunk_of_row = lax.broadcasted_iota(jnp.int32, (rows, 128), 0) >> (chunk.bit_length() - 1)
    onehot = _mask_bf16(chunk_of_row == lax.broadcasted_iota(jnp.int32, (rows, 128), 1))
    decay_cols = jnp.exp(_exact_mask_dot(lambda m, t: _dot_tn(t, m), onehot, log_f3))

    if chain:
        @pl.when(j == 0)
        def _():
            state_ref[...] = s0_ref[0]
            prev_ref[...] = jnp.concatenate([pa_ref[...], pb_ref[...]], axis=0)

    head_sl = [slice(h * HG_D, (h + 1) * HG_D) for h in range(HG_HEADS)]
    intra, inter, state = [], [[] for _ in head_sl], [None] * HG_HEADS
    for sl in head_sl:
        att = jnp.where(incl, _dot_nt(_bf(q_dec[:, sl]), _bf(k_dec[:, sl])), 0.0)
        intra.append(_dot(_bf(att), _bf(v[:, sl])))
    for c in range(n_chunks):
        rs = slice(c * chunk, (c + 1) * chunk)
        for h, sl in enumerate(head_sl):
            if chain:
                s = state_ref[h] if c == 0 else state[h]
            else:
                s = s0_ref[c, h]
            inter[h].append(_dot(_bf(q_dec[rs, sl]), _bf(s)))
            state[h] = decay_cols[sl, c:c + 1] * s + _dot_tn(_bf(k_end[rs, sl]), _bf(v[rs, sl]))
            if not chain:
                s_out_ref[c, h] = state[h]
    outs = []
    for h, sl in enumerate(head_sl):
        if chain:
            state_ref[h] = state[h]
        outs.append(_rms_gate(intra[h] + jnp.concatenate(inter[h], axis=0), nw_ref[...], gate[:, sl]))

    if chain:
        prev = prev_ref[...]
        t = j * rows + lax.broadcasted_iota(jnp.int32, (rows, POOL_GC), 0)
    else:
        prev_a, prev_b = pa_ref[...], pb_ref[...]
        t = lax.broadcasted_iota(jnp.int32, (rows, POOL_GC), 0) & 7
    for gi, win in enumerate(POOL_WINDOWS):
        gs = slice(gi * POOL_GC, (gi + 1) * POOL_GC)
        ug = u[:, gs]
        wsum = ug
        for d in range(1, win):
            if chain:
                wsum = wsum + _shift_rows_chain(ug, prev[:, gs], d)
            elif d <= 8:
                wsum = wsum + _shift_rows_batch(ug, prev_b[:, gs], d)
            else:
                wsum = wsum + _shift_rows_batch(prev_b[:, gs], prev_a[:, gs], d - 8)
        cnt = jnp.minimum(win, pos0 + t + 1).astype(jnp.float32)
        diff = wsum / cnt - ug
        outs.append(_dot(_bf(diff), pw_ref[gi]) * ps_ref[:, gs])
    mix_ref[...] = jnp.concatenate(outs, axis=-1).astype(mix_ref.dtype)

    if chain:
        prev_ref[...] = u[rows - 16:]

        @pl.when(j == pl.num_programs(1) - 1)
        def _():
            s_out_ref[0] = state_ref[...]
            pa_out_ref[...] = u[rows - 16:rows - 8]
            pb_out_ref[...] = u[rows - 8:]
    else:
        pa_out_ref[...] = prev_b
        pb_out_ref[...] = u


def _mixer_ab(x, row0, n_seq, length, pos0, s0, pool_state, w_in, lb, norm_w, pool_w, pool_scale):
    n_rows = n_seq * length
    chain = length >= ROW_TILE
    rows = ROW_TILE if chain else BATCH_ROW_TILE
    assert row0 % rows == 0
    first = row0 // rows
    pool16 = jnp.pad(pool_state, ((0, 0), (1, 0), (0, 0)))
    pa = pool16[:, :8].reshape(n_seq * 8, POOL_WIDTH)
    pb = pool16[:, 8:].reshape(n_seq * 8, POOL_WIDTH)
    if chain:
        assert length % rows == 0
        tiles = length // rows
        grid = (n_seq, tiles)
        row_map = lambda b, j: (b * tiles + j, 0)
        x_map = lambda b, j: (first + b * tiles + j, 0)
        seq_map4 = lambda b, j: (b, 0, 0, 0)
        seq_map2 = lambda b, j: (b, 0)
        const2 = lambda b, j: (0, 0)
        const3 = lambda b, j: (0, 0, 0)
        seq_block = 1
    else:
        assert length == 8 and n_rows % rows == 0
        seq_block = rows // 8
        grid = (n_rows // rows,)
        row_map = lambda i: (i, 0)
        x_map = lambda i: (first + i, 0)
        seq_map4 = lambda i: (i, 0, 0, 0)
        seq_map2 = lambda i: (i, 0)
        const2 = lambda i: (0, 0)
        const3 = lambda i: (0, 0, 0)
    d_in = w_in.shape[1]
    mix, s_new, pa_new, pb_new = pl.pallas_call(
        functools.partial(_mixer_ab_kernel, chain=chain, pos0=pos0),
        grid=grid,
        in_specs=[
            pl.BlockSpec((rows, D_MODEL), x_map),
            pl.BlockSpec((D_MODEL, d_in), const2),
            pl.BlockSpec((1, HG_WIDTH), const2),
            pl.BlockSpec((1, HG_D), const2),
            pl.BlockSpec((len(POOL_WINDOWS), POOL_GC, POOL_GC), const3),
            pl.BlockSpec((1, POOL_WIDTH), const2),
            pl.BlockSpec((seq_block, HG_HEADS, HG_D, HG_D), seq_map4),
            pl.BlockSpec((seq_block * 8, POOL_WIDTH), seq_map2),
            pl.BlockSpec((seq_block * 8, POOL_WIDTH), seq_map2),
        ],
        out_specs=[
            pl.BlockSpec((rows, MIX_OUT), row_map),
            pl.BlockSpec((seq_block, HG_HEADS, HG_D, HG_D), seq_map4),
            pl.BlockSpec((seq_block * 8, POOL_WIDTH), seq_map2),
            pl.BlockSpec((seq_block * 8, POOL_WIDTH), seq_map2),
        ],
        out_shape=[
            jax.ShapeDtypeStruct((n_rows, MIX_OUT), jnp.bfloat16),
            jax.ShapeDtypeStruct(s0.shape, jnp.float32),
            jax.ShapeDtypeStruct(pa.shape, jnp.float32),
            jax.ShapeDtypeStruct(pb.shape, jnp.float32),
        ],
        scratch_shapes=[
            pltpu.VMEM((HG_HEADS, HG_D, HG_D), jnp.float32),
            pltpu.VMEM((16, POOL_WIDTH), jnp.float32),
        ],
        compiler_params=pltpu.CompilerParams(
            dimension_semantics=("arbitrary",) * len(grid), vmem_limit_bytes=VMEM_LIMIT),
        name="mixer_ab_chain" if chain else "mixer_ab_batch",
    )(x, w_in, lb.reshape(1, -1), norm_w.reshape(1, -1), pool_w, pool_scale.reshape(1, -1), s0, pa, pb)
    pool_new = jnp.concatenate([pa_new.reshape(n_seq, 8, POOL_WIDTH), pb_new.reshape(n_seq, 8, POOL_WIDTH)], axis=1)
    return mix, s_new, pool_new[:, 1:]


SC_WIDTH, SC_K = 512, 3
GD_HEADS, GD_D = 4, 128
GD_CONV = 4
GD_CHUNK = 64
GD_QKV = GD_HEADS * 3 * GD_D
CD_MAIN = 3 * SC_WIDTH + GD_QKV + GD_HEADS * GD_D


def _causal_conv(cur, prev, w_ref, shift_fn):
    width = w_ref.shape[0]
    acc = cur * w_ref[width - 1:width, :]
    for j in range(width - 1):
        acc = acc + shift_fn(cur, prev, width - 1 - j) * w_ref[j:j + 1, :]
    return acc


def _softplus(v):
    return jnp.maximum(v, 0.0) + jnp.log1p(jnp.exp(-jnp.abs(v)))


def _mixer_cd_kernel(x_ref, w_ref, wg_ref, scw_ref, gcw_ref, alog_ref, dtb_ref, nw_ref, s0_ref, scp_ref, gcp_ref,
                     mix_ref, s_out_ref, scp_out_ref, gcp_out_ref, state_ref, sc_prev_ref, gc_prev_ref, *, chain):
    rows = x_ref.shape[0]
    chunk = GD_CHUNK if chain else 8
    n_chunks = rows // chunk
    j = pl.program_id(1) if chain else 0
    shift_fn = _shift_rows_chain if chain else _shift_rows_batch

    xb = _bf(x_ref[...])
    proj = _dot(xb, w_ref[...])
    gates = _dot(xb, wg_ref[...])
    b_gate = proj[:, 0:SC_WIDTH]
    conv_in = proj[:, SC_WIDTH:2 * SC_WIDTH] * proj[:, 2 * SC_WIDTH:3 * SC_WIDTH]
    qkv = proj[:, 3 * SC_WIDTH:3 * SC_WIDTH + GD_QKV]
    z = proj[:, 3 * SC_WIDTH + GD_QKV:]

    if chain:
        @pl.when(j == 0)
        def _():
            state_ref[...] = s0_ref[0]
            sc_prev_ref[...] = scp_ref[...]
            gc_prev_ref[...] = gcp_ref[...]
        sc_prev, gc_prev = sc_prev_ref[...], gc_prev_ref[...]
    else:
        sc_prev, gc_prev = scp_ref[...], gcp_ref[...]

    outs = [b_gate * _causal_conv(conv_in, sc_prev, scw_ref, shift_fn)]
    qkv_c = _silu(_causal_conv(qkv, gc_prev, gcw_ref, shift_fn))

    lane8 = lax.broadcasted_iota(jnp.int32, gates.shape, 1)
    g_dec = -jnp.exp(alog_ref[...]) * _softplus(gates + dtb_ref[...])
    cols = jnp.where(lane8 < GD_HEADS, g_dec, jax.nn.sigmoid(gates))
    same, incl, strict = _chunk_masks(rows, chunk)
    cols3 = _split3(cols)
    gcum = _exact_mask_dot(_dot, _mask_bf16(incl), cols3)
    gtot = _exact_mask_dot(_dot, _mask_bf16(same), cols3)
    eye8 = _mask_bf16(lax.broadcasted_iota(jnp.int32, (8, 8), 0) == lax.broadcasted_iota(jnp.int32, (8, 8), 1))
    gcum_rows = _exact_mask_dot(_dot_nt, eye8, _split3(gcum))

    eye = (lax.broadcasted_iota(jnp.int32, (rows, rows), 0)
           == lax.broadcasted_iota(jnp.int32, (rows, rows), 1)).astype(jnp.float32)
    hd = []
    for h in range(GD_HEADS):
        sl = lambda part: slice(part * GD_HEADS * GD_D + h * GD_D, part * GD_HEADS * GD_D + (h + 1) * GD_D)
        q, k, v = qkv_c[:, sl(0)], qkv_c[:, sl(1)], qkv_c[:, sl(2)]
        q = q * lax.rsqrt(jnp.sum(q * q, axis=-1, keepdims=True) + 1e-6) * (GD_D ** -0.5)
        k = k * lax.rsqrt(jnp.sum(k * k, axis=-1, keepdims=True) + 1e-6)
        beta = cols[:, GD_HEADS + h:GD_HEADS + h + 1]
        gc = gcum[:, h:h + 1]
        gt = gtot[:, h:h + 1]
        decay = jnp.where(incl, jnp.exp(jnp.where(incl, gc - gcum_rows[h:h + 1, :], 0.0)), 0.0)
        k_beta = k * beta
        kb = _bf(k)
        egc = jnp.exp(gc)
        power = -jnp.where(strict, _dot_nt(_bf(k_beta), kb) * decay, 0.0)
        hd.append(dict(
            inv=eye + power, power2=_split2(power),
            att=_bf(jnp.where(incl, _dot_nt(_bf(q), kb) * decay, 0.0)),
            rhs=_split2(jnp.concatenate([v * beta, k_beta * egc], axis=1)),
            q_dec=q * egc, k_end=k * jnp.exp(gt - gc),
            decay_end=jnp.exp(gt), inter=[], v_new=[], s=None))

    for _ in range(chunk.bit_length() - 2):
        for d in hd:
            d["power2"] = _split2(_dot3(d["power2"], d["power2"]))
        for d in hd:
            d["inv"] = d["inv"] + _dot3(_split2(d["inv"]), d["power2"])
    for d in hd:
        uw = _dot3(_split2(d["inv"]), d["rhs"])
        d["u"], d["w"] = uw[:, :GD_D], uw[:, GD_D:]

    for c in range(n_chunks):
        rs = slice(c * chunk, (c + 1) * chunk)
        for h, d in enumerate(hd):
            if chain:
                s = state_ref[h] if c == 0 else d["s"]
            else:
                s = s0_ref[c, h]
            sb = _bf(s)
            v_new = d["u"][rs] - _dot(_bf(d["w"][rs]), sb)
            d["inter"].append(_dot(_bf(d["q_dec"][rs]), sb))
            d["s"] = d["decay_end"][c * chunk:c * chunk + 1, :] * s + _dot_tn(_bf(d["k_end"][rs]), _bf(v_new))
            d["v_new"].append(v_new)
            if not chain:
                s_out_ref[c, h] = d["s"]
    for h, d in enumerate(hd):
        if chain:
            state_ref[h] = d["s"]
        o = jnp.concatenate(d["inter"], axis=0) + _dot(d["att"], _bf(jnp.concatenate(d["v_new"], axis=0)))
        outs.append(_rms_gate(o, nw_ref[...], z[:, h * GD_D:(h + 1) * GD_D]))
    mix_ref[...] = jnp.concatenate(outs, axis=-1).astype(mix_ref.dtype)

    if chain:
        sc_prev_ref[...] = conv_in[rows - 8:]
        gc_prev_ref[...] = qkv[rows - 8:]

        @pl.when(j == pl.num_programs(1) - 1)
        def _():
            s_out_ref[0] = state_ref[...]
            scp_out_ref[...] = conv_in[rows - 8:]
            gcp_out_ref[...] = qkv[rows - 8:]
    else:
        scp_out_ref[...] = conv_in
        gcp_out_ref[...] = qkv


def _mixer_cd(x, row0, n_seq, length, s0, sconv_state, gconv_state, w_main, w_gates, sconv_w, gconv_w, a_log, dt_bias,
              norm_w):
    n_rows = n_seq * length
    chain = length >= ROW_TILE
    rows = ROW_TILE if chain else BATCH_ROW_TILE
    assert row0 % rows == 0
    first = row0 // rows
    pad8 = lambda st: jnp.pad(st, ((0, 0), (8 - st.shape[1], 0), (0, 0))).reshape(n_seq * 8, st.shape[2])
    scp, gcp = pad8(sconv_state), pad8(gconv_state)
    if chain:
        assert length % rows == 0
        tiles = length // rows
        grid = (n_seq, tiles)
        row_map = lambda b, j: (b * tiles + j, 0)
        x_map = lambda b, j: (first + b * tiles + j, 0)
        seq_map4 = lambda b, j: (b, 0, 0, 0)
        seq_map2 = lambda b, j: (b, 0)
        const2 = lambda b, j: (0, 0)
        seq_block = 1
    else:
        assert length == 8 and n_rows % rows == 0
        seq_block = rows // 8
        grid = (n_rows // rows,)
        row_map = lambda i: (i, 0)
        x_map = lambda i: (first + i, 0)
        seq_map4 = lambda i: (i, 0, 0, 0)
        seq_map2 = lambda i: (i, 0)
        const2 = lambda i: (0, 0)
    zeros4 = jnp.zeros((GD_HEADS,), jnp.float32)
    alog8 = jnp.concatenate([a_log.astype(jnp.float32), zeros4]).reshape(1, 8)
    dtb8 = jnp.concatenate([dt_bias.astype(jnp.float32), zeros4]).reshape(1, 8)
    mix, s_new, scp_new, gcp_new = pl.pallas_call(
        functools.partial(_mixer_cd_kernel, chain=chain),
        grid=grid,
        in_specs=[
            pl.BlockSpec((rows, D_MODEL), x_map),
            pl.BlockSpec((D_MODEL, CD_MAIN), const2),
            pl.BlockSpec((D_MODEL, 8), const2),
            pl.BlockSpec((SC_K, SC_WIDTH), const2),
            pl.BlockSpec((GD_CONV, GD_QKV), const2),
            pl.BlockSpec((1, 8), const2),
            pl.BlockSpec((1, 8), const2),
            pl.BlockSpec((1, GD_D), const2),
            pl.BlockSpec((seq_block, GD_HEADS, GD_D, GD_D), seq_map4),
            pl.BlockSpec((seq_block * 8, SC_WIDTH), seq_map2),
            pl.BlockSpec((seq_block * 8, GD_QKV), seq_map2),
        ],
        out_specs=[
            pl.BlockSpec((rows, MIX_OUT), row_map),
            pl.BlockSpec((seq_block, GD_HEADS, GD_D, GD_D), seq_map4),
            pl.BlockSpec((seq_block * 8, SC_WIDTH), seq_map2),
            pl.BlockSpec((seq_block * 8, GD_QKV), seq_map2),
        ],
        out_shape=[
            jax.ShapeDtypeStruct((n_rows, MIX_OUT), jnp.bfloat16),
            jax.ShapeDtypeStruct(s0.shape, jnp.float32),
            jax.ShapeDtypeStruct(scp.shape, jnp.float32),
            jax.ShapeDtypeStruct(gcp.shape, jnp.float32),
        ],
        scratch_shapes=[
            pltpu.VMEM((GD_HEADS, GD_D, GD_D), jnp.float32),
            pltpu.VMEM((8, SC_WIDTH), jnp.float32),
            pltpu.VMEM((8, GD_QKV), jnp.float32),
        ],
        compiler_params=pltpu.CompilerParams(
            dimension_semantics=("arbitrary",) * len(grid), vmem_limit_bytes=VMEM_LIMIT),
        name="mixer_cd_chain" if chain else "mixer_cd_batch",
    )(x, w_main, w_gates, sconv_w, gconv_w, alog8, dtb8, norm_w.reshape(1, -1), s0, scp, gcp)
    tail = lambda st, keep: st.reshape(n_seq, 8, -1)[:, 8 - keep:]
    return mix, s_new, tail(scp_new, SC_K - 1), tail(gcp_new, GD_CONV - 1)


def _outproj_ln_router_kernel(mix_ref, x_ref, w_ref, g_ref, b_ref, rw_ref, rb_ref,
                              x1_ref, xp_ref, tope_ref, gate_ref, rank_ref, cnt_ref):
    mix = jnp.dot(mix_ref[...], w_ref[...], preferred_element_type=jnp.float32)
    x1 = _layer_norm_rows(DEEPNORM_ALPHA * x_ref[...] + mix, g_ref[...], b_ref[...])
    x1_ref[...] = x1
    packed = _pack_bf16_pairs(x1)
    for j in range(HALF // 128):
        xp_ref[:, j * 8:(j + 1) * 8, :] = packed[:, j * 128:(j + 1) * 128].reshape(x1.shape[0] // 8, 8, 128)
    logits = _dot(_bf(x1), _bf(rw_ref[...])) + rb_ref[...]
    lane = lax.broadcasted_iota(jnp.int32, logits.shape, 1)
    vals, idxs = [], []
    for _ in range(TOP_K):
        m = jnp.max(logits, axis=-1, keepdims=True)
        idx = jnp.min(jnp.where(logits == m, lane, N_EXPERTS), axis=-1, keepdims=True)
        vals.append(m)
        idxs.append(idx)
        logits = jnp.where(lane == idx, -jnp.inf, logits)
    ex = [jnp.exp(v - vals[0]) for v in vals]
    den = ex[0] + ex[1] + ex[2] + ex[3]
    tm = logits.shape[0]
    hits = [lane == idx for idx in idxs]
    member = sum(h.astype(jnp.float32) for h in hits)
    earlier = (lax.broadcasted_iota(jnp.int32, (tm, tm), 1) < lax.broadcasted_iota(jnp.int32, (tm, tm), 0))
    before = _dot(earlier.astype(jnp.bfloat16), member.astype(jnp.bfloat16))
    cnt_ref[0] = jnp.sum(member, axis=0, keepdims=True).astype(jnp.int32)
    col = lax.broadcasted_iota(jnp.int32, tope_ref.shape, 1)
    tope = jnp.zeros(tope_ref.shape, jnp.int32)
    gate = jnp.zeros(gate_ref.shape, jnp.float32)
    rank = jnp.zeros(rank_ref.shape, jnp.int32)
    for k in range(TOP_K):
        tope = jnp.where(col == k, idxs[k], tope)
        gate = jnp.where(col == k, ex[k] / den, gate)
        rank_k = jnp.sum(jnp.where(hits[k], before, 0.0), axis=-1, keepdims=True).astype(jnp.int32)
        rank = jnp.where(col == k, rank_k, rank)
    tope_ref[...] = tope
    gate_ref[...] = gate
    rank_ref[...] = rank


def _outproj_ln_router(mix, x, w_out, ln_g, ln_b, router_w, router_b):
    n = x.shape[0]
    tm = TOKEN_TILE
    row = lambda i: (i, 0)
    full = lambda i: (0, 0)
    return pl.pallas_call(
        _outproj_ln_router_kernel,
        grid=(n // tm,),
        in_specs=[
            pl.BlockSpec((tm, D_MODEL), row),
            pl.BlockSpec((tm, D_MODEL), row),
            pl.BlockSpec((D_MODEL, D_MODEL), full),
            pl.BlockSpec((1, D_MODEL), full),
            pl.BlockSpec((1, D_MODEL), full),
            pl.BlockSpec((D_MODEL, N_EXPERTS), full),
            pl.BlockSpec((1, N_EXPERTS), full),
        ],
        out_specs=[
            pl.BlockSpec((tm, D_MODEL), row),
            pl.BlockSpec((tm // 8, HALF // 16, 128), lambda i: (i, 0, 0)),
            pl.BlockSpec((tm, TOP_K), row),
            pl.BlockSpec((tm, TOP_K), row),
            pl.BlockSpec((tm, TOP_K), row),
            pl.BlockSpec((1, 1, N_EXPERTS), lambda i: (i, 0, 0)),
        ],
        out_shape=[
            jax.ShapeDtypeStruct((n, D_MODEL), jnp.float32),
            jax.ShapeDtypeStruct((n // 8, HALF // 16, 128), jnp.uint32),
            jax.ShapeDtypeStruct((n, TOP_K), jnp.int32),
            jax.ShapeDtypeStruct((n, TOP_K), jnp.float32),
            jax.ShapeDtypeStruct((n, TOP_K), jnp.int32),
            jax.ShapeDtypeStruct((n // tm, 1, N_EXPERTS), jnp.int32),
        ],
        compiler_params=pltpu.CompilerParams(dimension_semantics=("arbitrary",), vmem_limit_bytes=VMEM_LIMIT),
        name="outproj_ln_router",
    )(mix, x, w_out, ln_g.reshape(1, -1), ln_b.reshape(1, -1), router_w, router_b.reshape(1, -1))


def _route_plan(top_e, rank, tile_counts):
    n = top_e.shape[0]
    a = n * TOP_K
    br, tm, ch = EXPERT_ROWS, TOKEN_TILE, COMBINE_CHUNK
    n_virtual = N_EXPERTS * N_GROUPS
    group_tokens = n // N_GROUPS
    n_blocks = a // br + n_virtual + 1
    n_tiles = n // tm
    group_tiles = n_tiles // N_GROUPS
    max_chunks = (tm * TOP_K) // ch + N_EXPERTS + 1
    i32 = jnp.int32

    cnt = tile_counts.reshape(N_GROUPS, group_tiles, N_EXPERTS)
    nblk = (cnt.sum(axis=1) + br - 1) // br
    blk_end = jnp.cumsum(nblk.reshape(-1))
    row_off = ((blk_end - nblk.reshape(-1)) * br).reshape(N_GROUPS, 1, N_EXPERTS)
    seg_start = (row_off + jnp.cumsum(cnt, axis=1) - cnt).reshape(n_tiles, N_EXPERTS)
    seg_len = tile_counts
    blocks = jnp.arange(n_blocks, dtype=i32)
    block_ve = jnp.minimum(jnp.sum(blk_end[None, :] <= blocks[:, None], axis=1), n_virtual - 1).astype(i32)
    block_info = jnp.stack([block_ve % N_EXPERTS, block_ve // N_EXPERTS,
                            (blocks < blk_end[-1]).astype(i32)])

    aligned = seg_start // 8 * 8
    lead = seg_start - aligned
    nch = jnp.where(seg_len > 0, (lead + seg_len + ch - 1) // ch, 0)
    ch_end = jnp.cumsum(nch, axis=1)
    ch_first = ch_end - nch
    n_chunks = ch_end[:, -1]
    cidx = jnp.arange(max_chunks, dtype=i32)[None, :, None]
    owns = jnp.logical_and(ch_first[:, None, :] <= cidx, cidx < ch_end[:, None, :])
    chunk_src = jnp.sum(jnp.where(owns, aligned[:, None, :] + (cidx - ch_first[:, None, :]) * ch, 0), axis=2)

    hit = top_e.reshape(n_tiles, tm, TOP_K, 1) == jnp.arange(N_EXPERTS, dtype=i32)
    pick = lambda table: jnp.sum(jnp.where(hit, table[:, None, None, :], 0), axis=-1)
    rank = rank.reshape(n_tiles, tm, TOP_K)
    dest = pick(seg_start) + rank
    pos = pick(ch_first * ch + lead) + rank
    tok = jnp.arange(n, dtype=i32) % group_tokens
    addr = ((tok // 8) * (8 * (HALF // 128)) + tok % 8).reshape(n_tiles, tm, 1)
    row_addr = jnp.zeros((n_blocks * br,), i32).at[dest.reshape(-1)].set(
        jnp.broadcast_to(addr, dest.shape).reshape(-1), unique_indices=True, mode="promise_in_bounds")
    pos = (pos // 8) * (8 * LANE_TILES) + pos % 8
    return dict(row_addr=row_addr.reshape(n_blocks, 1, br), block_info=block_info,
                chunk_src=(chunk_src * LANE_TILES).astype(i32), n_chunks=n_chunks.astype(i32),
                pos=pos.reshape(n_tiles, 1, tm * TOP_K).astype(i32))


def _expert_kernel(info_ref, addr_ref, addr_next_ref, xq_hbm, wgu_f32_ref, bgu_ref, wdn_f32_ref, bdn_ref, y_ref,
                   xq_vmem, stage_ref, wgu_ref, wdn_ref, load_sem):
    b = pl.program_id(0)
    grp = info_ref[1, b]
    prev_grp = info_ref[1, jnp.maximum(b - 1, 0)]

    @pl.when(jnp.logical_or(b == 0, info_ref[0, b] != info_ref[0, jnp.maximum(b - 1, 0)]))
    def _():
        rows = D_MODEL // 8

        def cast(i, carry):
            r0 = pl.multiple_of(i * rows, rows)
            wgu_ref[pl.ds(r0, rows), :] = wgu_f32_ref[0, 0, pl.ds(r0, rows), :].astype(jnp.bfloat16)
            wdn_ref[pl.ds(r0, rows), :] = wdn_f32_ref[0, 0, pl.ds(r0, rows), :].astype(jnp.bfloat16)
            return carry

        lax.fori_loop(0, 8, cast, 0)

    group_rows = xq_vmem.shape[0]
    slot = b % 2
    tiles = HALF // 128

    def gather_row(addr, dst_slot, group8, sub):
        stage_ref[dst_slot, group8, pl.ds(sub, tiles, stride=8), :] = xq_vmem[pl.ds(addr, tiles, stride=8), :]

    @pl.when(jnp.logical_or(b == 0, grp != prev_grp))
    def _():
        cp = pltpu.make_async_copy(xq_hbm.at[pl.ds(grp * group_rows, group_rows)], xq_vmem, load_sem)
        cp.start()
        cp.wait()

        def gather(i, carry):
            for u in range(8):
                gather_row(addr_ref[0, 0, i * 8 + u], slot, i, u)
            return carry

        lax.fori_loop(0, EXPERT_ROWS // 8, gather, 0)

    @pl.when(info_ref[2, b] == 1)
    def _():
        for r in range(EXPERT_ROWS):
            gather_row(addr_next_ref[0, 0, r], 1 - slot, r // 8, r % 8)
        pieces = [_unpack_bf16_pairs(stage_ref[slot, :, j * 8:(j + 1) * 8, :].reshape(EXPERT_ROWS, 128))
                  for j in range(tiles)]
        lo = jnp.concatenate([p[0] for p in pieces], axis=1)
        hi = jnp.concatenate([p[1] for p in pieces], axis=1)
        hgu = (jnp.dot(lo, wgu_ref[:HALF, :], preferred_element_type=jnp.float32)
               + jnp.dot(hi, wgu_ref[HALF:, :], preferred_element_type=jnp.float32) + bgu_ref[0, 0])
        glu = jnp.minimum(hgu[:, :D_FF], SWIGLU_LIMIT)
        lin = jnp.clip(hgu[:, D_FF:], -SWIGLU_LIMIT, SWIGLU_LIMIT)
        act = (lin + 1.0) * glu * jax.nn.sigmoid(SWIGLU_ALPHA * glu)
        y = jnp.dot(act.astype(jnp.bfloat16), wdn_ref[...], preferred_element_type=jnp.float32) + bdn_ref[0, 0]
        for c in range(LANE_TILES):
            y_ref[:, c * 8:(c + 1) * 8, :] = y[:, c * 128:(c + 1) * 128].reshape(EXPERT_ROWS // 8, 8, 128)

    @pl.when(info_ref[2, b] == 0)
    def _():
        y_ref[...] = jnp.zeros_like(y_ref)


def _expert_mlp(plan, xq, layer, w_gu, b_gu, w_dn, b_dn):
    br = EXPERT_ROWS
    n_blocks = plan["row_addr"].shape[0]
    assert br >= COMBINE_CHUNK + 8
    pad_rows = n_blocks * br
    expert = lambda b, info: (layer, info[0, b], 0, 0)
    grid_spec = pltpu.PrefetchScalarGridSpec(
        num_scalar_prefetch=1,
        grid=(n_blocks,),
        in_specs=[
            pl.BlockSpec((1, 1, br), lambda b, info: (b, 0, 0), memory_space=pltpu.SMEM),
            pl.BlockSpec((1, 1, br), lambda b, info: (jnp.minimum(b + 1, n_blocks - 1), 0, 0), memory_space=pltpu.SMEM),
            pl.BlockSpec(memory_space=pl.ANY),
            pl.BlockSpec((1, 1, D_MODEL, 2 * D_FF), expert),
            pl.BlockSpec((1, 1, 1, 2 * D_FF), expert),
            pl.BlockSpec((1, 1, D_FF, D_MODEL), expert),
            pl.BlockSpec((1, 1, 1, D_MODEL), expert),
        ],
        out_specs=pl.BlockSpec((br // 8, 8 * LANE_TILES, 128), lambda b, info: (b, 0, 0)),
        scratch_shapes=[
            pltpu.VMEM((xq.shape[0] // N_GROUPS, 128), jnp.uint32),
            pltpu.VMEM((2, br // 8, 8 * (HALF // 128), 128), jnp.uint32),
            pltpu.VMEM((D_MODEL, 2 * D_FF), jnp.bfloat16),
            pltpu.VMEM((D_FF, D_MODEL), jnp.bfloat16),
            pltpu.SemaphoreType.DMA(()),
        ],
    )
    return pl.pallas_call(
        _expert_kernel,
        grid_spec=grid_spec,
        out_shape=jax.ShapeDtypeStruct((pad_rows // 8, 8 * LANE_TILES, 128), jnp.float32),
        compiler_params=pltpu.CompilerParams(dimension_semantics=("arbitrary",), vmem_limit_bytes=VMEM_LIMIT),
        name="expert_mlp",
    )(plan["block_info"], plan["row_addr"], plan["row_addr"], xq, w_gu, b_gu.reshape(DEPTH, N_EXPERTS, 1, -1), w_dn,
      b_dn.reshape(DEPTH, N_EXPERTS, 1, -1))


def _combine_ln_kernel(src_ref, nch_ref, pos_ref, gate_ref, x1_ref, y_hbm, g_ref, b_ref, out_ref,
                       stage_ref, ffn_ref, sems):
    i = pl.program_id(0)
    n_tiles = pl.num_programs(0)
    chunk_rows = COMBINE_CHUNK * LANE_TILES

    def chunk_copy(tile, slot, c):
        src = pl.multiple_of(src_ref[tile, c], 8 * LANE_TILES)
        return pltpu.make_async_copy(y_hbm.at[pl.ds(src, chunk_rows)],
                                     stage_ref.at[slot, pl.ds(c * chunk_rows, chunk_rows)], sems.at[slot])

    def issue(tile, slot):
        lax.fori_loop(0, nch_ref[tile], lambda c, carry: (chunk_copy(tile, slot, c).start(), carry)[1], 0)

    @pl.when(i == 0)
    def _():
        issue(0, 0)

    @pl.when(i + 1 < n_tiles)
    def _():
        issue(i + 1, (i + 1) % 2)

    slot = i % 2
    lax.fori_loop(0, nch_ref[i], lambda c, carry: (chunk_copy(i, slot, c).wait(), carry)[1], 0)

    tm = x1_ref.shape[0]

    def combine(j, carry):
        for u in range(8):
            t = j * 8 + u
            acc = None
            for k in range(TOP_K):
                row = stage_ref[slot, pl.ds(pos_ref[0, 0, t * TOP_K + k], LANE_TILES, stride=8), :]
                row = row * gate_ref[0, 0, t * TOP_K + k]
                acc = row if acc is None else acc + row
            ffn_ref[j, pl.ds(u, LANE_TILES, stride=8), :] = acc
        return carry

    lax.fori_loop(0, tm // 8, combine, 0)
    ffn = jnp.concatenate([ffn_ref[:, c * 8:(c + 1) * 8, :].reshape(tm, 128) for c in range(LANE_TILES)], axis=1)
    out_ref[...] = _layer_norm_rows(DEEPNORM_ALPHA * x1_ref[...] + ffn, g_ref[...], b_ref[...])


def _combine_ln(plan, gate, x1, y_flat, ln_g, ln_b):
    n = x1.shape[0]
    tm = TOKEN_TILE
    n_tiles = n // tm
    max_chunks = plan["chunk_src"].shape[1]
    row = lambda i, src, nch: (i, 0)
    full = lambda i, src, nch: (0, 0)
    per_tile = lambda i, src, nch: (i, 0, 0)
    grid_spec = pltpu.PrefetchScalarGridSpec(
        num_scalar_prefetch=2,
        grid=(n_tiles,),
        in_specs=[
            pl.BlockSpec((1, 1, tm * TOP_K), per_tile, memory_space=pltpu.SMEM),
            pl.BlockSpec((1, 1, tm * TOP_K), per_tile, memory_space=pltpu.SMEM),
            pl.BlockSpec((tm, D_MODEL), row),
            pl.BlockSpec(memory_space=pl.ANY),
            pl.BlockSpec((1, D_MODEL), full),
            pl.BlockSpec((1, D_MODEL), full),
        ],
        out_specs=pl.BlockSpec((tm, D_MODEL), row),
        scratch_shapes=[
            pltpu.VMEM((2, max_chunks * COMBINE_CHUNK * LANE_TILES, 128), jnp.float32),
            pltpu.VMEM((tm // 8, 8 * LANE_TILES, 128), jnp.float32),
            pltpu.SemaphoreType.DMA((2,)),
        ],
    )
    return pl.pallas_call(
        _combine_ln_kernel,
        grid_spec=grid_spec,
        out_shape=jax.ShapeDtypeStruct((n, D_MODEL), jnp.float32),
        compiler_params=pltpu.CompilerParams(dimension_semantics=("arbitrary",), vmem_limit_bytes=VMEM_LIMIT),
        name="combine_ln",
    )(plan["chunk_src"], plan["n_chunks"], plan["pos"], gate.reshape(n_tiles, 1, tm * TOP_K), x1, y_flat,
      ln_g.reshape(1, -1), ln_b.reshape(1, -1))


def _post_mixer(mix, x, layer, w_out, ln1_g, ln1_b, router_w, router_b, w_gu, b_gu, w_dn, b_dn, ln2_g, ln2_b):
    x1, xp, top_e, gate, rank, tile_counts = _outproj_ln_router(mix, x, w_out, ln1_g, ln1_b, router_w, router_b)
    plan = _route_plan(top_e, rank, tile_counts.reshape(-1, N_EXPERTS))
    y_sorted = _expert_mlp(plan, xp.reshape(-1, 128), layer, w_gu, b_gu, w_dn, b_dn)
    return _combine_ln(plan, gate, x1, y_sorted.reshape(-1, 128), ln2_g, ln2_b)


def kernel(x_prompt, x_sample, state_hgrn, state_pool, state_sconv, state_gdn_conv, state_gdn, w_in_ab, hgrn_lower_bounds, hgrn_norm_w, pool_w, pool_scale, w_out_ab, w_in_cd, sconv_w, gdn_conv_w, gdn_a_log, gdn_dt_bias, gdn_norm_w, w_out_cd, ln1_g, ln1_b, ln2_g, ln2_b, router_w, router_b, w_gu, b_gu, w_dn, b_dn):
    bp, sp, _ = x_prompt.shape
    bs, ss, _ = x_sample.shape
    n_p, n_s = bp * sp, bs * ss
    bf = jnp.bfloat16
    f32 = jnp.float32
    lower_bounds = jnp.cumsum(jax.nn.softmax(hgrn_lower_bounds.astype(f32), axis=0), axis=0)
    x = jnp.concatenate([x_prompt.reshape(n_p, D_MODEL), x_sample.reshape(n_s, D_MODEL)], axis=0)
    zeros_like_prompt = lambda st: jnp.zeros((bp,) + st.shape[2:], f32)

    states = {}
    for l in range(DEPTH):
        if l % 2 == 0:
            args = (w_in_ab[0].astype(bf), lower_bounds[l], hgrn_norm_w[0], pool_w[0].astype(bf), pool_scale[0])
            mp, hp, pp = _mixer_ab(x, 0, bp, sp, 0, zeros_like_prompt(state_hgrn), zeros_like_prompt(state_pool), *args)
            ms, hs, ps = _mixer_ab(x, n_p, bs, ss, PAST_LEN, state_hgrn[0], state_pool[0], *args)
            states.update(hp=hp[None], hs=hs[None], pp=pp[None], ps=ps[None])
            w_out = w_out_ab[0]
        else:
            args = (w_in_cd[0][:, :CD_MAIN].astype(bf), w_in_cd[0][:, CD_MAIN:].astype(bf), sconv_w[0], gdn_conv_w[0],
                    gdn_a_log[0], gdn_dt_bias[0], gdn_norm_w[0])
            mp, gp, scp, gcp = _mixer_cd(x, 0, bp, sp, zeros_like_prompt(state_gdn), zeros_like_prompt(state_sconv),
                                         zeros_like_prompt(state_gdn_conv), *args)
            ms, gs, scs, gcs = _mixer_cd(x, n_p, bs, ss, state_gdn[0], state_sconv[0], state_gdn_conv[0], *args)
            states.update(scp=scp[None], scs=scs[None], gcp=gcp[None], gcs=gcs[None], gp=gp[None], gs=gs[None])
            w_out = w_out_cd[0]
        mix = jnp.concatenate([mp, ms], axis=0)
        x = _post_mixer(mix, x, l, w_out.astype(bf), ln1_g[l], ln1_b[l], router_w[l], router_b[l],
                        w_gu, b_gu, w_dn, b_dn, ln2_g[l], ln2_b[l])
    return (x[:n_p].reshape(bp, sp, D_MODEL), x[n_p:].reshape(bs, ss, D_MODEL),
            states["hp"], states["hs"], states["pp"], states["ps"], states["scp"], states["scs"],
            states["gcp"], states["gcs"], states["gp"], states["gs"])
```

```python
import functools

import jax
import jax.numpy as jnp
from jax import lax
from jax.experimental import pallas as pl
from jax.experimental.pallas import tpu as pltpu

D_MODEL = 1024
DEPTH = 2
N_EXPERTS = 32
TOP_K = 4
D_FF = 1024
SWIGLU_LIMIT = 7.0
SWIGLU_ALPHA = 1.702
LN_EPS = 1e-5
DEEPNORM_ALPHA = (2 * DEPTH) ** 0.25

HALF = D_MODEL // 2
LANE_TILES = D_MODEL // 128
TOKEN_TILE = 256
EXPERT_ROWS = 256
N_GROUPS = 2
COMBINE_CHUNK = 16
VMEM_LIMIT = 56 * 1024 * 1024
MIX_OUT = 1024
BATCH_ROW_TILE = 128
PAST_LEN = 16384


def _layer_norm_rows(v, g, b):
    mu = jnp.mean(v, axis=-1, keepdims=True)
    d = v - mu
    var = jnp.mean(d * d, axis=-1, keepdims=True)
    return d * lax.rsqrt(var + LN_EPS) * g + b


def _pack_bf16_pairs(v):
    lo = pltpu.bitcast(v[:, :HALF].astype(jnp.bfloat16).astype(jnp.float32), jnp.uint32)
    hi = pltpu.bitcast(v[:, HALF:].astype(jnp.bfloat16).astype(jnp.float32), jnp.uint32)
    return (lo >> 16) | (hi & jnp.uint32(0xFFFF0000))


def _unpack_bf16_pairs(p):
    lo = pltpu.bitcast(p << 16, jnp.float32).astype(jnp.bfloat16)
    hi = pltpu.bitcast(p & jnp.uint32(0xFFFF0000), jnp.float32).astype(jnp.bfloat16)
    return lo, hi


ROW_TILE = 256
HG_HEADS, HG_D = 4, 128
HG_WIDTH = HG_HEADS * HG_D
HG_CHUNK = 16
POOL_WINDOWS = (2, 4, 8, 16)
POOL_GC = 128
POOL_WIDTH = 512
RMS_EPS = 1e-6
_HI = lax.Precision.HIGHEST


def _silu(v):
    return v * jax.nn.sigmoid(v)


def _bf(v):
    return v.astype(jnp.bfloat16)


def _dot(a, b, precision=None):
    return jnp.dot(a, b, preferred_element_type=jnp.float32, precision=precision)


def _dot_nt(a, b, precision=None):
    return lax.dot_general(a, b, (((1,), (1,)), ((), ())), preferred_element_type=jnp.float32, precision=precision)


def _dot_tn(a, b, precision=None):
    return lax.dot_general(a, b, (((0,), (0,)), ((), ())), preferred_element_type=jnp.float32, precision=precision)


def _split3(v):
    hi = _bf(v)
    rest = v - hi.astype(jnp.float32)
    mid = _bf(rest)
    return hi, mid, _bf(rest - mid.astype(jnp.float32))


def _mask_bf16(mask):
    return _bf(mask.astype(jnp.float32))


def _exact_mask_dot(dot_fn, m, terms):
    return dot_fn(m, terms[0]) + (dot_fn(m, terms[1]) + dot_fn(m, terms[2]))


def _split2(v):
    hi = _bf(v)
    return hi, _bf(v - hi.astype(jnp.float32))


def _dot3(a2, b2):
    return _dot(a2[0], b2[0]) + (_dot(a2[0], b2[1]) + _dot(a2[1], b2[0]))


def _chunk_masks(rows, chunk):
    shift = chunk.bit_length() - 1
    t = lax.broadcasted_iota(jnp.int32, (rows, rows), 0)
    s = lax.broadcasted_iota(jnp.int32, (rows, rows), 1)
    same = (t >> shift) == (s >> shift)
    return same, jnp.logical_and(same, s <= t), jnp.logical_and(same, s < t)


def _shift_rows_chain(cur, prev, j):
    if j == 0:
        return cur
    p = prev.shape[0]
    rc = pltpu.roll(cur, j, 0)
    rp = prev if j == p else pltpu.roll(prev, j, 0)
    row = lax.broadcasted_iota(jnp.int32, (p, cur.shape[1]), 0)
    head = jnp.where(row < j, rp, rc[:p])
    return jnp.concatenate([head, rc[p:]], axis=0) if cur.shape[0] > p else head


def _shift_rows_batch(cur, prev, j):
    if j == 0:
        return cur
    if j == 8:
        return prev
    rows = cur.shape[0]
    row = lax.broadcasted_iota(jnp.int32, cur.shape, 0)
    return jnp.where((row & 7) < j, pltpu.roll(prev, rows + j - 8, 0), pltpu.roll(cur, j, 0))


def _rms_gate(o, norm_w, gate):
    ms = jnp.mean(o * o, axis=-1, keepdims=True)
    return o * lax.rsqrt(ms + RMS_EPS) * norm_w * _silu(gate)


def _mixer_ab_kernel(x_ref, w_ref, lb_ref, nw_ref, pw_ref, ps_ref, s0_ref, pa_ref, pb_ref,
                     mix_ref, s_out_ref, pa_out_ref, pb_out_ref, state_ref, prev_ref, *, chain, pos0):
    rows = x_ref.shape[0]
    chunk = HG_CHUNK if chain else 8
    n_chunks = rows // chunk
    j = pl.program_id(1) if chain else 0

    proj = _dot(_bf(x_ref[...]), w_ref[...])
    q = _silu(proj[:, 0:HG_WIDTH])
    f = lb_ref[...] + (1.0 - lb_ref[...]) * jax.nn.sigmoid(proj[:, HG_WIDTH:2 * HG_WIDTH])
    v = _silu(proj[:, 2 * HG_WIDTH:3 * HG_WIDTH])
    gate = proj[:, 3 * HG_WIDTH:4 * HG_WIDTH]
    u = proj[:, 4 * HG_WIDTH:]
    log_f = jnp.log(f)
    k = 1.0 - f

    same, incl, _ = _chunk_masks(rows, chunk)
    log_f3 = _split3(log_f)
    sums = _exact_mask_dot(_dot, jnp.concatenate([_mask_bf16(incl), _mask_bf16(same)], axis=0), log_f3)
    cum = sums[:rows]
    total = sums[rows:]
    q_dec = q * jnp.exp(cum)
    k_dec = k * jnp.exp(-cum)
    k_end = k * jnp.exp(total - cum)
    chunk_of_row = lax.broadcasted_iota(jnp.int32, (rows, 128), 0) >> (chunk.bit_length() - 1)
    onehot = _mask_bf16(chunk_of_row == lax.broadcasted_iota(jnp.int32, (rows, 128), 1))
    decay_cols = jnp.exp(_exact_mask_dot(lambda m, t: _dot_tn(t, m), onehot, log_f3))

    if chain:
        @pl.when(j == 0)
        def _():
            state_ref[...] = s0_ref[0]
            prev_ref[...] = jnp.concatenate([pa_ref[...], pb_ref[...]], axis=0)

    head_sl = [slice(h * HG_D, (h + 1) * HG_D) for h in range(HG_HEADS)]
    intra, inter, state = [], [[] for _ in head_sl], [None] * HG_HEADS
    for sl in head_sl:
        att = jnp.where(incl, _dot_nt(_bf(q_dec[:, sl]), _bf(k_dec[:, sl])), 0.0)
        intra.append(_dot(_bf(att), _bf(v[:, sl])))
    for c in range(n_chunks):
        rs = slice(c * chunk, (c + 1) * chunk)
        for h, sl in enumerate(head_sl):
            if chain:
                s = state_ref[h] if c == 0 else state[h]
            else:
                s = s0_ref[c, h]
            inter[h].append(_dot(_bf(q_dec[rs, sl]), _bf(s)))
            state[h] = decay_cols[sl, c:c + 1] * s + _dot_tn(_bf(k_end[rs, sl]), _bf(v[rs, sl]))
            if not chain:
                s_out_ref[c, h] = state[h]
    outs = []
    for h, sl in enumerate(head_sl):
        if chain:
            state_ref[h] = state[h]
        outs.append(_rms_gate(intra[h] + jnp.concatenate(inter[h], axis=0), nw_ref[...], gate[:, sl]))

    if chain:
        prev = prev_ref[...]
        t = j * rows + lax.broadcasted_iota(jnp.int32, (rows, POOL_GC), 0)
    else:
        prev_a, prev_b = pa_ref[...], pb_ref[...]
        t = lax.broadcasted_iota(jnp.int32, (rows, POOL_GC), 0) & 7
    for gi, win in enumerate(POOL_WINDOWS):
        gs = slice(gi * POOL_GC, (gi + 1) * POOL_GC)
        ug = u[:, gs]
        wsum = ug
        for d in range(1, win):
            if chain:
                wsum = wsum + _shift_rows_chain(ug, prev[:, gs], d)
            elif d <= 8:
                wsum = wsum + _shift_rows_batch(ug, prev_b[:, gs], d)
            else:
                wsum = wsum + _shift_rows_batch(prev_b[:, gs], prev_a[:, gs], d - 8)
        cnt = jnp.minimum(win, pos0 + t + 1).astype(jnp.float32)
        diff = wsum / cnt - ug
        outs.append(_dot(_bf(diff), pw_ref[gi]) * ps_ref[:, gs])
    mix_ref[...] = jnp.concatenate(outs, axis=-1).astype(mix_ref.dtype)

    if chain:
        prev_ref[...] = u[rows - 16:]

        @pl.when(j == pl.num_programs(1) - 1)
        def _():
            s_out_ref[0] = state_ref[...]
            pa_out_ref[...] = u[rows - 16:rows - 8]
            pb_out_ref[...] = u[rows - 8:]
    else:
        pa_out_ref[...] = prev_b
        pb_out_ref[...] = u


def _mixer_ab(x, row0, n_seq, length, pos0, s0, pool_state, w_in, lb, norm_w, pool_w, pool_scale):
    n_rows = n_seq * length
    chain = length >= ROW_TILE
    rows = ROW_TILE if chain else BATCH_ROW_TILE
    assert row0 % rows == 0
    first = row0 // rows
    pool16 = jnp.pad(pool_state, ((0, 0), (1, 0), (0, 0)))
    pa = pool16[:, :8].reshape(n_seq * 8, POOL_WIDTH)
    pb = pool16[:, 8:].reshape(n_seq * 8, POOL_WIDTH)
    if chain:
        assert length % rows == 0
        tiles = length // rows
        grid = (n_seq, tiles)
        row_map = lambda b, j: (b * tiles + j, 0)
        x_map = lambda b, j: (first + b * tiles + j, 0)
        seq_map4 = lambda b, j: (b, 0, 0, 0)
        seq_map2 = lambda b, j: (b, 0)
        const2 = lambda b, j: (0, 0)
        const3 = lambda b, j: (0, 0, 0)
        seq_block = 1
    else:
        assert length == 8 and n_rows % rows == 0
        seq_block = rows // 8
        grid = (n_rows // rows,)
        row_map = lambda i: (i, 0)
        x_map = lambda i: (first + i, 0)
        seq_map4 = lambda i: (i, 0, 0, 0)
        seq_map2 = lambda i: (i, 0)
        const2 = lambda i: (0, 0)
        const3 = lambda i: (0, 0, 0)
    d_in = w_in.shape[1]
    mix, s_new, pa_new, pb_new = pl.pallas_call(
        functools.partial(_mixer_ab_kernel, chain=chain, pos0=pos0),
        grid=grid,
        in_specs=[
            pl.BlockSpec((rows, D_MODEL), x_map),
            pl.BlockSpec((D_MODEL, d_in), const2),
            pl.BlockSpec((1, HG_WIDTH), const2),
            pl.BlockSpec((1, HG_D), const2),
            pl.BlockSpec((len(POOL_WINDOWS), POOL_GC, POOL_GC), const3),
            pl.BlockSpec((1, POOL_WIDTH), const2),
            pl.BlockSpec((seq_block, HG_HEADS, HG_D, HG_D), seq_map4),
            pl.BlockSpec((seq_block * 8, POOL_WIDTH), seq_map2),
            pl.BlockSpec((seq_block * 8, POOL_WIDTH), seq_map2),
        ],
        out_specs=[
            pl.BlockSpec((rows, MIX_OUT), row_map),
            pl.BlockSpec((seq_block, HG_HEADS, HG_D, HG_D), seq_map4),
            pl.BlockSpec((seq_block * 8, POOL_WIDTH), seq_map2),
            pl.BlockSpec((seq_block * 8, POOL_WIDTH), seq_map2),
        ],
        out_shape=[
            jax.ShapeDtypeStruct((n_rows, MIX_OUT), jnp.bfloat16),
            jax.ShapeDtypeStruct(s0.shape, jnp.float32),
            jax.ShapeDtypeStruct(pa.shape, jnp.float32),
            jax.ShapeDtypeStruct(pb.shape, jnp.float32),
        ],
        scratch_shapes=[
            pltpu.VMEM((HG_HEADS, HG_D, HG_D), jnp.float32),
            pltpu.VMEM((16, POOL_WIDTH), jnp.float32),
        ],
        compiler_params=pltpu.CompilerParams(
            dimension_semantics=("arbitrary",) * len(grid), vmem_limit_bytes=VMEM_LIMIT),
        name="mixer_ab_chain" if chain else "mixer_ab_batch",
    )(x, w_in, lb.reshape(1, -1), norm_w.reshape(1, -1), pool_w, pool_scale.reshape(1, -1), s0, pa, pb)
    pool_new = jnp.concatenate([pa_new.reshape(n_seq, 8, POOL_WIDTH), pb_new.reshape(n_seq, 8, POOL_WIDTH)], axis=1)
    return mix, s_new, pool_new[:, 1:]


SC_WIDTH, SC_K = 512, 3
GD_HEADS, GD_D = 4, 128
GD_CONV = 4
GD_CHUNK = 64
GD_QKV = GD_HEADS * 3 * GD_D
CD_MAIN = 3 * SC_WIDTH + GD_QKV + GD_HEADS * GD_D


def _causal_conv(cur, prev, w_ref, shift_fn):
    width = w_ref.shape[0]
    acc = cur * w_ref[width - 1:width, :]
    for j in range(width - 1):
        acc = acc + shift_fn(cur, prev, width - 1 - j) * w_ref[j:j + 1, :]
    return acc


def _softplus(v):
    return jnp.maximum(v, 0.0) + jnp.log1p(jnp.exp(-jnp.abs(v)))


def _mixer_cd_kernel(x_ref, w_ref, wg_ref, scw_ref, gcw_ref, alog_ref, dtb_ref, nw_ref, s0_ref, scp_ref, gcp_ref,
                     mix_ref, s_out_ref, scp_out_ref, gcp_out_ref, state_ref, sc_prev_ref, gc_prev_ref, *, chain):
    rows = x_ref.shape[0]
    chunk = GD_CHUNK if chain else 8
    n_chunks = rows // chunk
    j = pl.program_id(1) if chain else 0
    shift_fn = _shift_rows_chain if chain else _shift_rows_batch

    xb = _bf(x_ref[...])
    proj = _dot(xb, w_ref[...])
    gates = _dot(xb, wg_ref[...])
    b_gate = proj[:, 0:SC_WIDTH]
    conv_in = proj[:, SC_WIDTH:2 * SC_WIDTH] * proj[:, 2 * SC_WIDTH:3 * SC_WIDTH]
    qkv = proj[:, 3 * SC_WIDTH:3 * SC_WIDTH + GD_QKV]
    z = proj[:, 3 * SC_WIDTH + GD_QKV:]

    if chain:
        @pl.when(j == 0)
        def _():
            state_ref[...] = s0_ref[0]
            sc_prev_ref[...] = scp_ref[...]
            gc_prev_ref[...] = gcp_ref[...]
        sc_prev, gc_prev = sc_prev_ref[...], gc_prev_ref[...]
    else:
        sc_prev, gc_prev = scp_ref[...], gcp_ref[...]

    outs = [b_gate * _causal_conv(conv_in, sc_prev, scw_ref, shift_fn)]
    qkv_c = _silu(_causal_conv(qkv, gc_prev, gcw_ref, shift_fn))

    lane8 = lax.broadcasted_iota(jnp.int32, gates.shape, 1)
    g_dec = -jnp.exp(alog_ref[...]) * _softplus(gates + dtb_ref[...])
    cols = jnp.where(lane8 < GD_HEADS, g_dec, jax.nn.sigmoid(gates))
    same, incl, strict = _chunk_masks(rows, chunk)
    cols3 = _split3(cols)
    gcum = _exact_mask_dot(_dot, _mask_bf16(incl), cols3)
    gtot = _exact_mask_dot(_dot, _mask_bf16(same), cols3)
    eye8 = _mask_bf16(lax.broadcasted_iota(jnp.int32, (8, 8), 0) == lax.broadcasted_iota(jnp.int32, (8, 8), 1))
    gcum_rows = _exact_mask_dot(_dot_nt, eye8, _split3(gcum))

    eye = (lax.broadcasted_iota(jnp.int32, (rows, rows), 0)
           == lax.broadcasted_iota(jnp.int32, (rows, rows), 1)).astype(jnp.float32)
    hd = []
    for h in range(GD_HEADS):
        sl = lambda part: slice(part * GD_HEADS * GD_D + h * GD_D, part * GD_HEADS * GD_D + (h + 1) * GD_D)
        q, k, v = qkv_c[:, sl(0)], qkv_c[:, sl(1)], qkv_c[:, sl(2)]
        q = q * lax.rsqrt(jnp.sum(q * q, axis=-1, keepdims=True) + 1e-6) * (GD_D ** -0.5)
        k = k * lax.rsqrt(jnp.sum(k * k, axis=-1, keepdims=True) + 1e-6)
        beta = cols[:, GD_HEADS + h:GD_HEADS + h + 1]
        gc = gcum[:, h:h + 1]
        gt = gtot[:, h:h + 1]
        decay = jnp.where(incl, jnp.exp(jnp.where(incl, gc - gcum_rows[h:h + 1, :], 0.0)), 0.0)
        k_beta = k * beta
        kb = _bf(k)
        egc = jnp.exp(gc)
        power = -jnp.where(strict, _dot_nt(_bf(k_beta), kb) * decay, 0.0)
        hd.append(dict(
            inv=eye + power, power2=_split2(power),
            att=_bf(jnp.where(incl, _dot_nt(_bf(q), kb) * decay, 0.0)),
            rhs=_split2(jnp.concatenate([v * beta, k_beta * egc], axis=1)),
            q_dec=q * egc, k_end=k * jnp.exp(gt - gc),
            decay_end=jnp.exp(gt), inter=[], v_new=[], s=None))

    for _ in range(chunk.bit_length() - 2):
        for d in hd:
            d["power2"] = _split2(_dot3(d["power2"], d["power2"]))
        for d in hd:
            d["inv"] = d["inv"] + _dot3(_split2(d["inv"]), d["power2"])
    for d in hd:
        uw = _dot3(_split2(d["inv"]), d["rhs"])
        d["u"], d["w"] = uw[:, :GD_D], uw[:, GD_D:]

    for c in range(n_chunks):
        rs = slice(c * chunk, (c + 1) * chunk)
        for h, d in enumerate(hd):
            if chain:
                s = state_ref[h] if c == 0 else d["s"]
            else:
                s = s0_ref[c, h]
            sb = _bf(s)
            v_new = d["u"][rs] - _dot(_bf(d["w"][rs]), sb)
            d["inter"].append(_dot(_bf(d["q_dec"][rs]), sb))
            d["s"] = d["decay_end"][c * chunk:c * chunk + 1, :] * s + _dot_tn(_bf(d["k_end"][rs]), _bf(v_new))
            d["v_new"].append(v_new)
            if not chain:
                s_out_ref[c, h] = d["s"]
    for h, d in enumerate(hd):
        if chain:
            state_ref[h] = d["s"]
        o = jnp.concatenate(d["inter"], axis=0) + _dot(d["att"], _bf(jnp.concatenate(d["v_new"], axis=0)))
        outs.append(_rms_gate(o, nw_ref[...], z[:, h * GD_D:(h + 1) * GD_D]))
    mix_ref[...] = jnp.concatenate(outs, axis=-1).astype(mix_ref.dtype)

    if chain:
        sc_prev_ref[...] = conv_in[rows - 8:]
        gc_prev_ref[...] = qkv[rows - 8:]

        @pl.when(j == pl.num_programs(1) - 1)
        def _():
            s_out_ref[0] = state_ref[...]
            scp_out_ref[...] = conv_in[rows - 8:]
            gcp_out_ref[...] = qkv[rows - 8:]
    else:
        scp_out_ref[...] = conv_in
        gcp_out_ref[...] = qkv


def _mixer_cd(x, row0, n_seq, length, s0, sconv_state, gconv_state, w_main, w_gates, sconv_w, gconv_w, a_log, dt_bias,
              norm_w):
    n_rows = n_seq * length
    chain = length >= ROW_TILE
    rows = ROW_TILE if chain else BATCH_ROW_TILE
    assert row0 % rows == 0
    first = row0 // rows
    pad8 = lambda st: jnp.pad(st, ((0, 0), (8 - st.shape[1], 0), (0, 0))).reshape(n_seq * 8, st.shape[2])
    scp, gcp = pad8(sconv_state), pad8(gconv_state)
    if chain:
        assert length % rows == 0
        tiles = length // rows
        grid = (n_seq, tiles)
        row_map = lambda b, j: (b * tiles + j, 0)
        x_map = lambda b, j: (first + b * tiles + j, 0)
        seq_map4 = lambda b, j: (b, 0, 0, 0)
        seq_map2 = lambda b, j: (b, 0)
        const2 = lambda b, j: (0, 0)
        seq_block = 1
    else:
        assert length == 8 and n_rows % rows == 0
        seq_block = rows // 8
        grid = (n_rows // rows,)
        row_map = lambda i: (i, 0)
        x_map = lambda i: (first + i, 0)
        seq_map4 = lambda i: (i, 0, 0, 0)
        seq_map2 = lambda i: (i, 0)
        const2 = lambda i: (0, 0)
    zeros4 = jnp.zeros((GD_HEADS,), jnp.float32)
    alog8 = jnp.concatenate([a_log.astype(jnp.float32), zeros4]).reshape(1, 8)
    dtb8 = jnp.concatenate([dt_bias.astype(jnp.float32), zeros4]).reshape(1, 8)
    mix, s_new, scp_new, gcp_new = pl.pallas_call(
        functools.partial(_mixer_cd_kernel, chain=chain),
        grid=grid,
        in_specs=[
            pl.BlockSpec((rows, D_MODEL), x_map),
            pl.BlockSpec((D_MODEL, CD_MAIN), const2),
            pl.BlockSpec((D_MODEL, 8), const2),
            pl.BlockSpec((SC_K, SC_WIDTH), const2),
            pl.BlockSpec((GD_CONV, GD_QKV), const2),
            pl.BlockSpec((1, 8), const2),
            pl.BlockSpec((1, 8), const2),
            pl.BlockSpec((1, GD_D), const2),
            pl.BlockSpec((seq_block, GD_HEADS, GD_D, GD_D), seq_map4),
            pl.BlockSpec((seq_block * 8, SC_WIDTH), seq_map2),
            pl.BlockSpec((seq_block * 8, GD_QKV), seq_map2),
        ],
        out_specs=[
            pl.BlockSpec((rows, MIX_OUT), row_map),
            pl.BlockSpec((seq_block, GD_HEADS, GD_D, GD_D), seq_map4),
            pl.BlockSpec((seq_block * 8, SC_WIDTH), seq_map2),
            pl.BlockSpec((seq_block * 8, GD_QKV), seq_map2),
        ],
        out_shape=[
            jax.ShapeDtypeStruct((n_rows, MIX_OUT), jnp.bfloat16),
            jax.ShapeDtypeStruct(s0.shape, jnp.float32),
            jax.ShapeDtypeStruct(scp.shape, jnp.float32),
            jax.ShapeDtypeStruct(gcp.shape, jnp.float32),
        ],
        scratch_shapes=[
            pltpu.VMEM((GD_HEADS, GD_D, GD_D), jnp.float32),
            pltpu.VMEM((8, SC_WIDTH), jnp.float32),
            pltpu.VMEM((8, GD_QKV), jnp.float32),
        ],
        compiler_params=pltpu.CompilerParams(
            dimension_semantics=("arbitrary",) * len(grid), vmem_limit_bytes=VMEM_LIMIT),
        name="mixer_cd_chain" if chain else "mixer_cd_batch",
    )(x, w_main, w_gates, sconv_w, gconv_w, alog8, dtb8, norm_w.reshape(1, -1), s0, scp, gcp)
    tail = lambda st, keep: st.reshape(n_seq, 8, -1)[:, 8 - keep:]
    return mix, s_new, tail(scp_new, SC_K - 1), tail(gcp_new, GD_CONV - 1)


def _outproj_ln_router_kernel(mix_ref, x_ref, w_ref, g_ref, b_ref, rw_ref, rb_ref,
                              x1_ref, xp_ref, tope_ref, gate_ref, rank_ref, cnt_ref):
    mix = jnp.dot(mix_ref[...], w_ref[...], preferred_element_type=jnp.float32)
    x1 = _layer_norm_rows(DEEPNORM_ALPHA * x_ref[...] + mix, g_ref[...], b_ref[...])
    x1_ref[...] = x1
    packed = _pack_bf16_pairs(x1)
    for j in range(HALF // 128):
        xp_ref[:, j * 8:(j + 1) * 8, :] = packed[:, j * 128:(j + 1) * 128].reshape(x1.shape[0] // 8, 8, 128)
    logits = _dot(_bf(x1), _bf(rw_ref[...])) + rb_ref[...]
    lane = lax.broadcasted_iota(jnp.int32, logits.shape, 1)
    vals, idxs = [], []
    for _ in range(TOP_K):
        m = jnp.max(logits, axis=-1, keepdims=True)
        idx = jnp.min(jnp.where(logits == m, lane, N_EXPERTS), axis=-1, keepdims=True)
        vals.append(m)
        idxs.append(idx)
        logits = jnp.where(lane == idx, -jnp.inf, logits)
    ex = [jnp.exp(v - vals[0]) for v in vals]
    den = ex[0] + ex[1] + ex[2] + ex[3]
    tm = logits.shape[0]
    hits = [lane == idx for idx in idxs]
    member = sum(h.astype(jnp.float32) for h in hits)
    earlier = (lax.broadcasted_iota(jnp.int32, (tm, tm), 1) < lax.broadcasted_iota(jnp.int32, (tm, tm), 0))
    before = _dot(earlier.astype(jnp.bfloat16), member.astype(jnp.bfloat16))
    cnt_ref[0] = jnp.sum(member, axis=0, keepdims=True).astype(jnp.int32)
    col = lax.broadcasted_iota(jnp.int32, tope_ref.shape, 1)
    tope = jnp.zeros(tope_ref.shape, jnp.int32)
    gate = jnp.zeros(gate_ref.shape, jnp.float32)
    rank = jnp.zeros(rank_ref.shape, jnp.int32)
    for k in range(TOP_K):
        tope = jnp.where(col == k, idxs[k], tope)
        gate = jnp.where(col == k, ex[k] / den, gate)
        rank_k = jnp.sum(jnp.where(hits[k], before, 0.0), axis=-1, keepdims=True).astype(jnp.int32)
        rank = jnp.where(col == k, rank_k, rank)
    tope_ref[...] = tope
    gate_ref[...] = gate
    rank_ref[...] = rank


def _outproj_ln_router(mix, x, w_out, ln_g, ln_b, router_w, router_b):
    n = x.shape[0]
    tm = TOKEN_TILE
    row = lambda i: (i, 0)
    full = lambda i: (0, 0)
    return pl.pallas_call(
        _outproj_ln_router_kernel,
        grid=(n // tm,),
        in_specs=[
            pl.BlockSpec((tm, D_MODEL), row),
            pl.BlockSpec((tm, D_MODEL), row),
            pl.BlockSpec((D_MODEL, D_MODEL), full),
            pl.BlockSpec((1, D_MODEL), full),
            pl.BlockSpec((1, D_MODEL), full),
            pl.BlockSpec((D_MODEL, N_EXPERTS), full),
            pl.BlockSpec((1, N_EXPERTS), full),
        ],
        out_specs=[
            pl.BlockSpec((tm, D_MODEL), row),
            pl.BlockSpec((tm // 8, HALF // 16, 128), lambda i: (i, 0, 0)),
            pl.BlockSpec((tm, TOP_K), row),
            pl.BlockSpec((tm, TOP_K), row),
            pl.BlockSpec((tm, TOP_K), row),
            pl.BlockSpec((1, 1, N_EXPERTS), lambda i: (i, 0, 0)),
        ],
        out_shape=[
            jax.ShapeDtypeStruct((n, D_MODEL), jnp.float32),
            jax.ShapeDtypeStruct((n // 8, HALF // 16, 128), jnp.uint32),
            jax.ShapeDtypeStruct((n, TOP_K), jnp.int32),
            jax.ShapeDtypeStruct((n, TOP_K), jnp.float32),
            jax.ShapeDtypeStruct((n, TOP_K), jnp.int32),
            jax.ShapeDtypeStruct((n // tm, 1, N_EXPERTS), jnp.int32),
        ],
        compiler_params=pltpu.CompilerParams(dimension_semantics=("arbitrary",), vmem_limit_bytes=VMEM_LIMIT),
        name="outproj_ln_router",
    )(mix, x, w_out, ln_g.reshape(1, -1), ln_b.reshape(1, -1), router_w, router_b.reshape(1, -1))


def _route_plan(top_e, rank, tile_counts):
    n = top_e.shape[0]
    a = n * TOP_K
    br, tm, ch = EXPERT_ROWS, TOKEN_TILE, COMBINE_CHUNK
    n_virtual = N_EXPERTS * N_GROUPS
    group_tokens = n // N_GROUPS
    n_blocks = a // br + n_virtual + 1
    n_tiles = n // tm
    group_tiles = n_tiles // N_GROUPS
    max_chunks = (tm * TOP_K) // ch + N_EXPERTS + 1
    i32 = jnp.int32

    cnt = tile_counts.reshape(N_GROUPS, group_tiles, N_EXPERTS)
    nblk = (cnt.sum(axis=1) + br - 1) // br
    blk_end = jnp.cumsum(nblk.reshape(-1))
    row_off = ((blk_end - nblk.reshape(-1)) * br).reshape(N_GROUPS, 1, N_EXPERTS)
    seg_start = (row_off + jnp.cumsum(cnt, axis=1) - cnt).reshape(n_tiles, N_EXPERTS)
    seg_len = tile_counts
    blocks = jnp.arange(n_blocks, dtype=i32)
    block_ve = jnp.minimum(jnp.sum(blk_end[None, :] <= blocks[:, None], axis=1), n_virtual - 1).astype(i32)
    valid = blocks < blk_end[-1]
    block_e = block_ve % N_EXPERTS
    block_e = jnp.where(valid, block_e, block_e[jnp.maximum(blk_end[-1] - 1, 0)])
    starts = jnp.concatenate([jnp.ones((1,), bool), block_e[1:] != block_e[:-1]])
    later_start = lax.cummin(jnp.where(starts, blocks, n_blocks)[::-1], axis=0)[::-1]
    next_start = jnp.concatenate([later_start[1:], jnp.full((1,), n_blocks, i32)])
    next_e = jnp.where(next_start < n_blocks, block_e[jnp.minimum(next_start, n_blocks - 1)], -1)
    block_info = jnp.stack([block_e, block_ve // N_EXPERTS, valid.astype(i32), starts.astype(i32), next_e.astype(i32)])

    aligned = seg_start // 8 * 8
    lead = seg_start - aligned
    nch = jnp.where(seg_len > 0, (lead + seg_len + ch - 1) // ch, 0)
    ch_end = jnp.cumsum(nch, axis=1)
    ch_first = ch_end - nch
    n_chunks = ch_end[:, -1]
    cidx = jnp.arange(max_chunks, dtype=i32)[None, :, None]
    owns = jnp.logical_and(ch_first[:, None, :] <= cidx, cidx < ch_end[:, None, :])
    chunk_src = jnp.sum(jnp.where(owns, aligned[:, None, :] + (cidx - ch_first[:, None, :]) * ch, 0), axis=2)

    hit = top_e.reshape(n_tiles, tm, TOP_K, 1) == jnp.arange(N_EXPERTS, dtype=i32)
    pick = lambda table: jnp.sum(jnp.where(hit, table[:, None, None, :], 0), axis=-1)
    rank = rank.reshape(n_tiles, tm, TOP_K)
    dest = pick(seg_start) + rank
    pos = pick(ch_first * ch + lead) + rank
    tok = jnp.arange(n, dtype=i32) % group_tokens
    addr = ((tok // 8) * (8 * (HALF // 128)) + tok % 8).reshape(n_tiles, tm, 1)
    row_addr = jnp.zeros((n_blocks * br,), i32).at[dest.reshape(-1)].set(
        jnp.broadcast_to(addr, dest.shape).reshape(-1), unique_indices=True, mode="promise_in_bounds")
    pos = (pos // 8) * (8 * LANE_TILES) + pos % 8
    return dict(row_addr=row_addr.reshape(n_blocks, 1, br), block_info=block_info,
                chunk_src=(chunk_src * LANE_TILES).astype(i32), n_chunks=n_chunks.astype(i32),
                pos=pos.reshape(n_tiles, 1, tm * TOP_K).astype(i32))


def _expert_kernel(info_ref, addr_ref, addr_next_ref, xq_hbm, wgu_hbm, bgu_ref, wdn_hbm, bdn_ref, y_ref,
                   xq_vmem, stage_ref, wgu_f32_ref, wdn_f32_ref, wgu_ref, wdn_ref, load_sem, w_sems, *, layer):
    b = pl.program_id(0)
    grp = info_ref[1, b]
    prev_grp = info_ref[1, jnp.maximum(b - 1, 0)]

    def weight_copies(expert):
        return (pltpu.make_async_copy(wgu_hbm.at[layer, expert], wgu_f32_ref, w_sems.at[0]),
                pltpu.make_async_copy(wdn_hbm.at[layer, expert], wdn_f32_ref, w_sems.at[1]))

    @pl.when(b == 0)
    def _():
        for cp in weight_copies(info_ref[0, 0]):
            cp.start()

    @pl.when(info_ref[3, b] == 1)
    def _():
        for cp in weight_copies(info_ref[0, b]):
            cp.wait()
        rows = D_MODEL // 8

        def cast(i, carry):
            r0 = pl.multiple_of(i * rows, rows)
            wgu_ref[pl.ds(r0, rows), :] = wgu_f32_ref[pl.ds(r0, rows), :].astype(jnp.bfloat16)
            wdn_ref[pl.ds(r0, rows), :] = wdn_f32_ref[pl.ds(r0, rows), :].astype(jnp.bfloat16)
            return carry

        lax.fori_loop(0, 8, cast, 0)

        @pl.when(info_ref[4, b] >= 0)
        def _():
            for cp in weight_copies(info_ref[4, b]):
                cp.start()

    group_rows = xq_vmem.shape[0]
    slot = b % 2
    tiles = HALF // 128

    def gather_row(addr, dst_slot, group8, sub):
        stage_ref[dst_slot, group8, pl.ds(sub, tiles, stride=8), :] = xq_vmem[pl.ds(addr, tiles, stride=8), :]

    @pl.when(jnp.logical_or(b == 0, grp != prev_grp))
    def _():
        cp = pltpu.make_async_copy(xq_hbm.at[pl.ds(grp * group_rows, group_rows)], xq_vmem, load_sem)
        cp.start()
        cp.wait()

        def gather(i, carry):
            for u in range(8):
                gather_row(addr_ref[0, 0, i * 8 + u], slot, i, u)
            return carry

        lax.fori_loop(0, EXPERT_ROWS // 8, gather, 0)

    @pl.when(info_ref[2, b] == 1)
    def _():
        for r in range(EXPERT_ROWS):
            gather_row(addr_next_ref[0, 0, r], 1 - slot, r // 8, r % 8)
        pieces = [_unpack_bf16_pairs(stage_ref[slot, :, j * 8:(j + 1) * 8, :].reshape(EXPERT_ROWS, 128))
                  for j in range(tiles)]
        lo = jnp.concatenate([p[0] for p in pieces], axis=1)
        hi = jnp.concatenate([p[1] for p in pieces], axis=1)
        hgu = (jnp.dot(lo, wgu_ref[:HALF, :], preferred_element_type=jnp.float32)
               + jnp.dot(hi, wgu_ref[HALF:, :], preferred_element_type=jnp.float32) + bgu_ref[0, 0])
        glu = jnp.minimum(hgu[:, :D_FF], SWIGLU_LIMIT)
        lin = jnp.clip(hgu[:, D_FF:], -SWIGLU_LIMIT, SWIGLU_LIMIT)
        act = (lin + 1.0) * glu * jax.nn.sigmoid(SWIGLU_ALPHA * glu)
        y = jnp.dot(act.astype(jnp.bfloat16), wdn_ref[...], preferred_element_type=jnp.float32) + bdn_ref[0, 0]
        for c in range(LANE_TILES):
            y_ref[:, c * 8:(c + 1) * 8, :] = y[:, c * 128:(c + 1) * 128].reshape(EXPERT_ROWS // 8, 8, 128)

    @pl.when(info_ref[2, b] == 0)
    def _():
        y_ref[...] = jnp.zeros_like(y_ref)


def _expert_mlp(plan, xq, layer, w_gu, b_gu, w_dn, b_dn):
    br = EXPERT_ROWS
    n_blocks = plan["row_addr"].shape[0]
    assert br >= COMBINE_CHUNK + 8
    pad_rows = n_blocks * br
    expert = lambda b, info: (layer, info[0, b], 0, 0)
    grid_spec = pltpu.PrefetchScalarGridSpec(
        num_scalar_prefetch=1,
        grid=(n_blocks,),
        in_specs=[
            pl.BlockSpec((1, 1, br), lambda b, info: (b, 0, 0), memory_space=pltpu.SMEM),
            pl.BlockSpec((1, 1, br), lambda b, info: (jnp.minimum(b + 1, n_blocks - 1), 0, 0), memory_space=pltpu.SMEM),
            pl.BlockSpec(memory_space=pl.ANY),
            pl.BlockSpec(memory_space=pl.ANY),
            pl.BlockSpec((1, 1, 1, 2 * D_FF), expert),
            pl.BlockSpec(memory_space=pl.ANY),
            pl.BlockSpec((1, 1, 1, D_MODEL), expert),
        ],
        out_specs=pl.BlockSpec((br // 8, 8 * LANE_TILES, 128), lambda b, info: (b, 0, 0)),
        scratch_shapes=[
            pltpu.VMEM((xq.shape[0] // N_GROUPS, 128), jnp.uint32),
            pltpu.VMEM((2, br // 8, 8 * (HALF // 128), 128), jnp.uint32),
            pltpu.VMEM((D_MODEL, 2 * D_FF), jnp.float32),
            pltpu.VMEM((D_FF, D_MODEL), jnp.float32),
            pltpu.VMEM((D_MODEL, 2 * D_FF), jnp.bfloat16),
            pltpu.VMEM((D_FF, D_MODEL), jnp.bfloat16),
            pltpu.SemaphoreType.DMA(()),
            pltpu.SemaphoreType.DMA((2,)),
        ],
    )
    return pl.pallas_call(
        functools.partial(_expert_kernel, layer=layer),
        grid_spec=grid_spec,
        out_shape=jax.ShapeDtypeStruct((pad_rows // 8, 8 * LANE_TILES, 128), jnp.float32),
        compiler_params=pltpu.CompilerParams(dimension_semantics=("arbitrary",), vmem_limit_bytes=VMEM_LIMIT),
        name="expert_mlp",
    )(plan["block_info"], plan["row_addr"], plan["row_addr"], xq, w_gu, b_gu.reshape(DEPTH, N_EXPERTS, 1, -1), w_dn,
      b_dn.reshape(DEPTH, N_EXPERTS, 1, -1))


def _combine_ln_kernel(src_ref, nch_ref, pos_ref, gate_ref, x1_ref, y_hbm, g_ref, b_ref, out_ref,
                       stage_ref, ffn_ref, sems):
    i = pl.program_id(0)
    n_tiles = pl.num_programs(0)
    chunk_rows = COMBINE_CHUNK * LANE_TILES

    def chunk_copy(tile, slot, c):
        src = pl.multiple_of(src_ref[tile, c], 8 * LANE_TILES)
        return pltpu.make_async_copy(y_hbm.at[pl.ds(src, chunk_rows)],
                                     stage_ref.at[slot, pl.ds(c * chunk_rows, chunk_rows)], sems.at[slot])

    def issue(tile, slot):
        lax.fori_loop(0, nch_ref[tile], lambda c, carry: (chunk_copy(tile, slot, c).start(), carry)[1], 0)

    @pl.when(i == 0)
    def _():
        issue(0, 0)

    @pl.when(i + 1 < n_tiles)
    def _():
        issue(i + 1, (i + 1) % 2)

    slot = i % 2
    lax.fori_loop(0, nch_ref[i], lambda c, carry: (chunk_copy(i, slot, c).wait(), carry)[1], 0)

    tm = x1_ref.shape[0]

    def combine(j, carry):
        for u in range(8):
            t = j * 8 + u
            acc = None
            for k in range(TOP_K):
                row = stage_ref[slot, pl.ds(pos_ref[0, 0, t * TOP_K + k], LANE_TILES, stride=8), :]
                row = row * gate_ref[0, 0, t * TOP_K + k]
                acc = row if acc is None else acc + row
            ffn_ref[j, pl.ds(u, LANE_TILES, stride=8), :] = acc
        return carry

    lax.fori_loop(0, tm // 8, combine, 0)
    ffn = jnp.concatenate([ffn_ref[:, c * 8:(c + 1) * 8, :].reshape(tm, 128) for c in range(LANE_TILES)], axis=1)
    out_ref[...] = _layer_norm_rows(DEEPNORM_ALPHA * x1_ref[...] + ffn, g_ref[...], b_ref[...])


def _combine_ln(plan, gate, x1, y_flat, ln_g, ln_b):
    n = x1.shape[0]
    tm = TOKEN_TILE
    n_tiles = n // tm
    max_chunks = plan["chunk_src"].shape[1]
    row = lambda i, src, nch: (i, 0)
    full = lambda i, src, nch: (0, 0)
    per_tile = lambda i, src, nch: (i, 0, 0)
    grid_spec = pltpu.PrefetchScalarGridSpec(
        num_scalar_prefetch=2,
        grid=(n_tiles,),
        in_specs=[
            pl.BlockSpec((1, 1, tm * TOP_K), per_tile, memory_space=pltpu.SMEM),
            pl.BlockSpec((1, 1, tm * TOP_K), per_tile, memory_space=pltpu.SMEM),
            pl.BlockSpec((tm, D_MODEL), row),
            pl.BlockSpec(memory_space=pl.ANY),
            pl.BlockSpec((1, D_MODEL), full),
            pl.BlockSpec((1, D_MODEL), full),
        ],
        out_specs=pl.BlockSpec((tm, D_MODEL), row),
        scratch_shapes=[
            pltpu.VMEM((2, max_chunks * COMBINE_CHUNK * LANE_TILES, 128), jnp.float32),
            pltpu.VMEM((tm // 8, 8 * LANE_TILES, 128), jnp.float32),
            pltpu.SemaphoreType.DMA((2,)),
        ],
    )
    return pl.pallas_call(
        _combine_ln_kernel,
        grid_spec=grid_spec,
        out_shape=jax.ShapeDtypeStruct((n, D_MODEL), jnp.float32),
        compiler_params=pltpu.CompilerParams(dimension_semantics=("arbitrary",), vmem_limit_bytes=VMEM_LIMIT),
        name="combine_ln",
    )(plan["chunk_src"], plan["n_chunks"], plan["pos"], gate.reshape(n_tiles, 1, tm * TOP_K), x1, y_flat,
      ln_g.reshape(1, -1), ln_b.reshape(1, -1))


def _post_mixer(mix, x, layer, w_out, ln1_g, ln1_b, router_w, router_b, w_gu, b_gu, w_dn, b_dn, ln2_g, ln2_b):
    x1, xp, top_e, gate, rank, tile_counts = _outproj_ln_router(mix, x, w_out, ln1_g, ln1_b, router_w, router_b)
    plan = _route_plan(top_e, rank, tile_counts.reshape(-1, N_EXPERTS))
    y_sorted = _expert_mlp(plan, xp.reshape(-1, 128), layer, w_gu, b_gu, w_dn, b_dn)
    return _combine_ln(plan, gate, x1, y_sorted.reshape(-1, 128), ln2_g, ln2_b)


def kernel(x_prompt, x_sample, state_hgrn, state_pool, state_sconv, state_gdn_conv, state_gdn, w_in_ab, hgrn_lower_bounds, hgrn_norm_w, pool_w, pool_scale, w_out_ab, w_in_cd, sconv_w, gdn_conv_w, gdn_a_log, gdn_dt_bias, gdn_norm_w, w_out_cd, ln1_g, ln1_b, ln2_g, ln2_b, router_w, router_b, w_gu, b_gu, w_dn, b_dn):
    bp, sp, _ = x_prompt.shape
    bs, ss, _ = x_sample.shape
    n_p, n_s = bp * sp, bs * ss
    bf = jnp.bfloat16
    f32 = jnp.float32
    lower_bounds = jnp.cumsum(jax.nn.softmax(hgrn_lower_bounds.astype(f32), axis=0), axis=0)
    x = jnp.concatenate([x_prompt.reshape(n_p, D_MODEL), x_sample.reshape(n_s, D_MODEL)], axis=0)
    zeros_like_prompt = lambda st: jnp.zeros((bp,) + st.shape[2:], f32)

    states = {}
    for l in range(DEPTH):
        if l % 2 == 0:
            args = (w_in_ab[0].astype(bf), lower_bounds[l], hgrn_norm_w[0], pool_w[0].astype(bf), pool_scale[0])
            mp, hp, pp = _mixer_ab(x, 0, bp, sp, 0, zeros_like_prompt(state_hgrn), zeros_like_prompt(state_pool), *args)
            ms, hs, ps = _mixer_ab(x, n_p, bs, ss, PAST_LEN, state_hgrn[0], state_pool[0], *args)
            states.update(hp=hp[None], hs=hs[None], pp=pp[None], ps=ps[None])
            w_out = w_out_ab[0]
        else:
            args = (w_in_cd[0][:, :CD_MAIN].astype(bf), w_in_cd[0][:, CD_MAIN:].astype(bf), sconv_w[0], gdn_conv_w[0],
                    gdn_a_log[0], gdn_dt_bias[0], gdn_norm_w[0])
            mp, gp, scp, gcp = _mixer_cd(x, 0, bp, sp, zeros_like_prompt(state_gdn), zeros_like_prompt(state_sconv),
                                         zeros_like_prompt(state_gdn_conv), *args)
            ms, gs, scs, gcs = _mixer_cd(x, n_p, bs, ss, state_gdn[0], state_sconv[0], state_gdn_conv[0], *args)
            states.update(scp=scp[None], scs=scs[None], gcp=gcp[None], gcs=gcs[None], gp=gp[None], gs=gs[None])
            w_out = w_out_cd[0]
        mix = jnp.concatenate([mp, ms], axis=0)
        x = _post_mixer(mix, x, l, w_out.astype(bf), ln1_g[l], ln1_b[l], router_w[l], router_b[l],
                        w_gu, b_gu, w_dn, b_dn, ln2_g[l], ln2_b[l])
    return (x[:n_p].reshape(bp, sp, D_MODEL), x[n_p:].reshape(bs, ss, D_MODEL),
            states["hp"], states["hs"], states["pp"], states["ps"], states["scp"], states["scs"],
            states["gcp"], states["gcs"], states["gp"], states["gs"])
```

```python
import functools

import jax
import jax.numpy as jnp
from jax import lax
from jax.experimental import pallas as pl
from jax.experimental.pallas import tpu as pltpu

D_MODEL = 1024
DEPTH = 2
N_EXPERTS = 32
TOP_K = 4
D_FF = 1024
SWIGLU_LIMIT = 7.0
SWIGLU_ALPHA = 1.702
LN_EPS = 1e-5
DEEPNORM_ALPHA = (2 * DEPTH) ** 0.25

HALF = D_MODEL // 2
LANE_TILES = D_MODEL // 128
TOKEN_TILE = 256
EXPERT_ROWS = 256
N_GROUPS = 1
COMBINE_CHUNK = 16
VMEM_LIMIT = 60 * 1024 * 1024
MIX_OUT = 1024
BATCH_ROW_TILE = 128
PAST_LEN = 16384


def _layer_norm_rows(v, g, b):
    mu = jnp.mean(v, axis=-1, keepdims=True)
    d = v - mu
    var = jnp.mean(d * d, axis=-1, keepdims=True)
    return d * lax.rsqrt(var + LN_EPS) * g + b


def _pack_bf16_pairs(v):
    lo = pltpu.bitcast(v[:, :HALF].astype(jnp.bfloat16).astype(jnp.float32), jnp.uint32)
    hi = pltpu.bitcast(v[:, HALF:].astype(jnp.bfloat16).astype(jnp.float32), jnp.uint32)
    return (lo >> 16) | (hi & jnp.uint32(0xFFFF0000))


def _unpack_bf16_pairs(p):
    lo = pltpu.bitcast(p << 16, jnp.float32).astype(jnp.bfloat16)
    hi = pltpu.bitcast(p & jnp.uint32(0xFFFF0000), jnp.float32).astype(jnp.bfloat16)
    return lo, hi


ROW_TILE = 256
HG_HEADS, HG_D = 4, 128
HG_WIDTH = HG_HEADS * HG_D
HG_CHUNK = 16
POOL_WINDOWS = (2, 4, 8, 16)
POOL_GC = 128
POOL_WIDTH = 512
RMS_EPS = 1e-6
_HI = lax.Precision.HIGHEST


def _silu(v):
    return v * jax.nn.sigmoid(v)


def _bf(v):
    return v.astype(jnp.bfloat16)


def _dot(a, b, precision=None):
    return jnp.dot(a, b, preferred_element_type=jnp.float32, precision=precision)


def _dot_nt(a, b, precision=None):
    return lax.dot_general(a, b, (((1,), (1,)), ((), ())), preferred_element_type=jnp.float32, precision=precision)


def _dot_tn(a, b, precision=None):
    return lax.dot_general(a, b, (((0,), (0,)), ((), ())), preferred_element_type=jnp.float32, precision=precision)


def _split3(v):
    hi = _bf(v)
    rest = v - hi.astype(jnp.float32)
    mid = _bf(rest)
    return hi, mid, _bf(rest - mid.astype(jnp.float32))


def _mask_bf16(mask):
    return _bf(mask.astype(jnp.float32))


def _exact_mask_dot(dot_fn, m, terms):
    return dot_fn(m, terms[0]) + (dot_fn(m, terms[1]) + dot_fn(m, terms[2]))


def _split2(v):
    hi = _bf(v)
    return hi, _bf(v - hi.astype(jnp.float32))


def _dot3(a2, b2):
    return _dot(a2[0], b2[0]) + (_dot(a2[0], b2[1]) + _dot(a2[1], b2[0]))


def _chunk_masks(rows, chunk):
    shift = chunk.bit_length() - 1
    t = lax.broadcasted_iota(jnp.int32, (rows, rows), 0)
    s = lax.broadcasted_iota(jnp.int32, (rows, rows), 1)
    same = (t >> shift) == (s >> shift)
    return same, jnp.logical_and(same, s <= t), jnp.logical_and(same, s < t)


def _shift_rows_chain(cur, prev, j):
    if j == 0:
        return cur
    p = prev.shape[0]
    rc = pltpu.roll(cur, j, 0)
    rp = prev if j == p else pltpu.roll(prev, j, 0)
    row = lax.broadcasted_iota(jnp.int32, (p, cur.shape[1]), 0)
    head = jnp.where(row < j, rp, rc[:p])
    return jnp.concatenate([head, rc[p:]], axis=0) if cur.shape[0] > p else head


def _shift_rows_batch(cur, prev, j):
    if j == 0:
        return cur
    if j == 8:
        return prev
    rows = cur.shape[0]
    row = lax.broadcasted_iota(jnp.int32, cur.shape, 0)
    return jnp.where((row & 7) < j, pltpu.roll(prev, rows + j - 8, 0), pltpu.roll(cur, j, 0))


def _rms_gate(o, norm_w, gate):
    ms = jnp.mean(o * o, axis=-1, keepdims=True)
    return o * lax.rsqrt(ms + RMS_EPS) * norm_w * _silu(gate)


def _mixer_ab_kernel(x_ref, w_ref, lb_ref, nw_ref, pw_ref, ps_ref, s0_ref, pa_ref, pb_ref,
                     mix_ref, s_out_ref, pa_out_ref, pb_out_ref, state_ref, prev_ref, *, chain, pos0):
    rows = x_ref.shape[0]
    chunk = HG_CHUNK if chain else 8
    n_chunks = rows // chunk
    j = pl.program_id(1) if chain else 0

    proj = _dot(_bf(x_ref[...]), w_ref[...])
    q = _silu(proj[:, 0:HG_WIDTH])
    f = lb_ref[...] + (1.0 - lb_ref[...]) * jax.nn.sigmoid(proj[:, HG_WIDTH:2 * HG_WIDTH])
    v = _silu(proj[:, 2 * HG_WIDTH:3 * HG_WIDTH])
    gate = proj[:, 3 * HG_WIDTH:4 * HG_WIDTH]
    u = proj[:, 4 * HG_WIDTH:]
    log_f = jnp.log(f)
    k = 1.0 - f

    same, incl, _ = _chunk_masks(rows, chunk)
    log_f3 = _split3(log_f)
    sums = _exact_mask_dot(_dot, jnp.concatenate([_mask_bf16(incl), _mask_bf16(same)], axis=0), log_f3)
    cum = sums[:rows]
    total = sums[rows:]
    q_dec = q * jnp.exp(cum)
    k_dec = k * jnp.exp(-cum)
    k_end = k * jnp.exp(total - cum)
    chunk_of_row = lax.broadcasted_iota(jnp.int32, (rows, 128), 0) >> (chunk.bit_length() - 1)
    onehot = _mask_bf16(chunk_of_row == lax.broadcasted_iota(jnp.int32, (rows, 128), 1))
    decay_cols = jnp.exp(_exact_mask_dot(lambda m, t: _dot_tn(t, m), onehot, log_f3))

    if chain:
        @pl.when(j == 0)
        def _():
            state_ref[...] = s0_ref[0]
            prev_ref[...] = jnp.concatenate([pa_ref[...], pb_ref[...]], axis=0)

    head_sl = [slice(h * HG_D, (h + 1) * HG_D) for h in range(HG_HEADS)]
    intra, inter, state = [], [[] for _ in head_sl], [None] * HG_HEADS
    for sl in head_sl:
        att = jnp.where(incl, _dot_nt(_bf(q_dec[:, sl]), _bf(k_dec[:, sl])), 0.0)
        intra.append(_dot(_bf(att), _bf(v[:, sl])))
    for c in range(n_chunks):
        rs = slice(c * chunk, (c + 1) * chunk)
        for h, sl in enumerate(head_sl):
            if chain:
                s = state_ref[h] if c == 0 else state[h]
            else:
                s = s0_ref[c, h]
            inter[h].append(_dot(_bf(q_dec[rs, sl]), _bf(s)))
            state[h] = decay_cols[sl, c:c + 1] * s + _dot_tn(_bf(k_end[rs, sl]), _bf(v[rs, sl]))
            if not chain:
                s_out_ref[c, h] = state[h]
    outs = []
    for h, sl in enumerate(head_sl):
        if chain:
            state_ref[h] = state[h]
        outs.append(_rms_gate(intra[h] + jnp.concatenate(inter[h], axis=0), nw_ref[...], gate[:, sl]))

    if chain:
        prev = prev_ref[...]
        t = j * rows + lax.broadcasted_iota(jnp.int32, (rows, POOL_GC), 0)
    else:
        prev_a, prev_b = pa_ref[...], pb_ref[...]
        t = lax.broadcasted_iota(jnp.int32, (rows, POOL_GC), 0) & 7
    for gi, win in enumerate(POOL_WINDOWS):
        gs = slice(gi * POOL_GC, (gi + 1) * POOL_GC)
        ug = u[:, gs]
        wsum = ug
        for d in range(1, win):
            if chain:
                wsum = wsum + _shift_rows_chain(ug, prev[:, gs], d)
            elif d <= 8:
                wsum = wsum + _shift_rows_batch(ug, prev_b[:, gs], d)
            else:
                wsum = wsum + _shift_rows_batch(prev_b[:, gs], prev_a[:, gs], d - 8)
        cnt = jnp.minimum(win, pos0 + t + 1).astype(jnp.float32)
        diff = wsum / cnt - ug
        outs.append(_dot(_bf(diff), pw_ref[gi]) * ps_ref[:, gs])
    mix_ref[...] = jnp.concatenate(outs, axis=-1).astype(mix_ref.dtype)

    if chain:
        prev_ref[...] = u[rows - 16:]

        @pl.when(j == pl.num_programs(1) - 1)
        def _():
            s_out_ref[0] = state_ref[...]
            pa_out_ref[...] = u[rows - 16:rows - 8]
            pb_out_ref[...] = u[rows - 8:]
    else:
        pa_out_ref[...] = prev_b
        pb_out_ref[...] = u


def _mixer_ab(x, row0, n_seq, length, pos0, s0, pool_state, w_in, lb, norm_w, pool_w, pool_scale):
    n_rows = n_seq * length
    chain = length >= ROW_TILE
    rows = ROW_TILE if chain else BATCH_ROW_TILE
    assert row0 % rows == 0
    first = row0 // rows
    pool16 = jnp.pad(pool_state, ((0, 0), (1, 0), (0, 0)))
    pa = pool16[:, :8].reshape(n_seq * 8, POOL_WIDTH)
    pb = pool16[:, 8:].reshape(n_seq * 8, POOL_WIDTH)
    if chain:
        assert length % rows == 0
        tiles = length // rows
        grid = (n_seq, tiles)
        row_map = lambda b, j: (b * tiles + j, 0)
        x_map = lambda b, j: (first + b * tiles + j, 0)
        seq_map4 = lambda b, j: (b, 0, 0, 0)
        seq_map2 = lambda b, j: (b, 0)
        const2 = lambda b, j: (0, 0)
        const3 = lambda b, j: (0, 0, 0)
        seq_block = 1
    else:
        assert length == 8 and n_rows % rows == 0
        seq_block = rows // 8
        grid = (n_rows // rows,)
        row_map = lambda i: (i, 0)
        x_map = lambda i: (first + i, 0)
        seq_map4 = lambda i: (i, 0, 0, 0)
        seq_map2 = lambda i: (i, 0)
        const2 = lambda i: (0, 0)
        const3 = lambda i: (0, 0, 0)
    d_in = w_in.shape[1]
    mix, s_new, pa_new, pb_new = pl.pallas_call(
        functools.partial(_mixer_ab_kernel, chain=chain, pos0=pos0),
        grid=grid,
        in_specs=[
            pl.BlockSpec((rows, D_MODEL), x_map),
            pl.BlockSpec((D_MODEL, d_in), const2),
            pl.BlockSpec((1, HG_WIDTH), const2),
            pl.BlockSpec((1, HG_D), const2),
            pl.BlockSpec((len(POOL_WINDOWS), POOL_GC, POOL_GC), const3),
            pl.BlockSpec((1, POOL_WIDTH), const2),
            pl.BlockSpec((seq_block, HG_HEADS, HG_D, HG_D), seq_map4),
            pl.BlockSpec((seq_block * 8, POOL_WIDTH), seq_map2),
            pl.BlockSpec((seq_block * 8, POOL_WIDTH), seq_map2),
        ],
        out_specs=[
            pl.BlockSpec((rows, MIX_OUT), row_map),
            pl.BlockSpec((seq_block, HG_HEADS, HG_D, HG_D), seq_map4),
            pl.BlockSpec((seq_block * 8, POOL_WIDTH), seq_map2),
            pl.BlockSpec((seq_block * 8, POOL_WIDTH), seq_map2),
        ],
        out_shape=[
            jax.ShapeDtypeStruct((n_rows, MIX_OUT), jnp.bfloat16),
            jax.ShapeDtypeStruct(s0.shape, jnp.float32),
            jax.ShapeDtypeStruct(pa.shape, jnp.float32),
            jax.ShapeDtypeStruct(pb.shape, jnp.float32),
        ],
        scratch_shapes=[
            pltpu.VMEM((HG_HEADS, HG_D, HG_D), jnp.float32),
            pltpu.VMEM((16, POOL_WIDTH), jnp.float32),
        ],
        compiler_params=pltpu.CompilerParams(
            dimension_semantics=("arbitrary",) * len(grid), vmem_limit_bytes=VMEM_LIMIT),
        name="mixer_ab_chain" if chain else "mixer_ab_batch",
    )(x, w_in, lb.reshape(1, -1), norm_w.reshape(1, -1), pool_w, pool_scale.reshape(1, -1), s0, pa, pb)
    pool_new = jnp.concatenate([pa_new.reshape(n_seq, 8, POOL_WIDTH), pb_new.reshape(n_seq, 8, POOL_WIDTH)], axis=1)
    return mix, s_new, pool_new[:, 1:]


SC_WIDTH, SC_K = 512, 3
GD_HEADS, GD_D = 4, 128
GD_CONV = 4
GD_CHUNK = 64
GD_QKV = GD_HEADS * 3 * GD_D
CD_MAIN = 3 * SC_WIDTH + GD_QKV + GD_HEADS * GD_D


def _causal_conv(cur, prev, w_ref, shift_fn):
    width = w_ref.shape[0]
    acc = cur * w_ref[width - 1:width, :]
    for j in range(width - 1):
        acc = acc + shift_fn(cur, prev, width - 1 - j) * w_ref[j:j + 1, :]
    return acc


def _softplus(v):
    return jnp.maximum(v, 0.0) + jnp.log1p(jnp.exp(-jnp.abs(v)))


def _mixer_cd_kernel(x_ref, w_ref, wg_ref, scw_ref, gcw_ref, alog_ref, dtb_ref, nw_ref, s0_ref, scp_ref, gcp_ref,
                     mix_ref, s_out_ref, scp_out_ref, gcp_out_ref, state_ref, sc_prev_ref, gc_prev_ref, *, chain):
    rows = x_ref.shape[0]
    chunk = GD_CHUNK if chain else 8
    n_chunks = rows // chunk
    j = pl.program_id(1) if chain else 0
    shift_fn = _shift_rows_chain if chain else _shift_rows_batch

    xb = _bf(x_ref[...])
    proj = _dot(xb, w_ref[...])
    gates = _dot(xb, wg_ref[...])
    b_gate = proj[:, 0:SC_WIDTH]
    conv_in = proj[:, SC_WIDTH:2 * SC_WIDTH] * proj[:, 2 * SC_WIDTH:3 * SC_WIDTH]
    qkv = proj[:, 3 * SC_WIDTH:3 * SC_WIDTH + GD_QKV]
    z = proj[:, 3 * SC_WIDTH + GD_QKV:]

    if chain:
        @pl.when(j == 0)
        def _():
            state_ref[...] = s0_ref[0]
            sc_prev_ref[...] = scp_ref[...]
            gc_prev_ref[...] = gcp_ref[...]
        sc_prev, gc_prev = sc_prev_ref[...], gc_prev_ref[...]
    else:
        sc_prev, gc_prev = scp_ref[...], gcp_ref[...]

    outs = [b_gate * _causal_conv(conv_in, sc_prev, scw_ref, shift_fn)]
    qkv_c = _silu(_causal_conv(qkv, gc_prev, gcw_ref, shift_fn))

    lane8 = lax.broadcasted_iota(jnp.int32, gates.shape, 1)
    g_dec = -jnp.exp(alog_ref[...]) * _softplus(gates + dtb_ref[...])
    cols = jnp.where(lane8 < GD_HEADS, g_dec, jax.nn.sigmoid(gates))
    same, incl, strict = _chunk_masks(rows, chunk)
    cols3 = _split3(cols)
    gcum = _exact_mask_dot(_dot, _mask_bf16(incl), cols3)
    gtot = _exact_mask_dot(_dot, _mask_bf16(same), cols3)
    eye8 = _mask_bf16(lax.broadcasted_iota(jnp.int32, (8, 8), 0) == lax.broadcasted_iota(jnp.int32, (8, 8), 1))
    gcum_rows = _exact_mask_dot(_dot_nt, eye8, _split3(gcum))

    eye = (lax.broadcasted_iota(jnp.int32, (rows, rows), 0)
           == lax.broadcasted_iota(jnp.int32, (rows, rows), 1)).astype(jnp.float32)
    hd = []
    for h in range(GD_HEADS):
        sl = lambda part: slice(part * GD_HEADS * GD_D + h * GD_D, part * GD_HEADS * GD_D + (h + 1) * GD_D)
        q, k, v = qkv_c[:, sl(0)], qkv_c[:, sl(1)], qkv_c[:, sl(2)]
        q = q * lax.rsqrt(jnp.sum(q * q, axis=-1, keepdims=True) + 1e-6) * (GD_D ** -0.5)
        k = k * lax.rsqrt(jnp.sum(k * k, axis=-1, keepdims=True) + 1e-6)
        beta = cols[:, GD_HEADS + h:GD_HEADS + h + 1]
        gc = gcum[:, h:h + 1]
        gt = gtot[:, h:h + 1]
        decay = jnp.where(incl, jnp.exp(jnp.where(incl, gc - gcum_rows[h:h + 1, :], 0.0)), 0.0)
        k_beta = k * beta
        kb = _bf(k)
        egc = jnp.exp(gc)
        power = -jnp.where(strict, _dot_nt(_bf(k_beta), kb) * decay, 0.0)
        hd.append(dict(
            inv=eye + power, power2=_split2(power),
            att=_bf(jnp.where(incl, _dot_nt(_bf(q), kb) * decay, 0.0)),
            rhs=_split2(jnp.concatenate([v * beta, k_beta * egc], axis=1)),
            q_dec=q * egc, k_end=k * jnp.exp(gt - gc),
            decay_end=jnp.exp(gt), inter=[], v_new=[], s=None))

    for _ in range(chunk.bit_length() - 2):
        for d in hd:
            d["power2"] = _split2(_dot3(d["power2"], d["power2"]))
        for d in hd:
            d["inv"] = d["inv"] + _dot3(_split2(d["inv"]), d["power2"])
    for d in hd:
        uw = _dot3(_split2(d["inv"]), d["rhs"])
        d["u"], d["w"] = uw[:, :GD_D], uw[:, GD_D:]

    for c in range(n_chunks):
        rs = slice(c * chunk, (c + 1) * chunk)
        for h, d in enumerate(hd):
            if chain:
                s = state_ref[h] if c == 0 else d["s"]
            else:
                s = s0_ref[c, h]
            sb = _bf(s)
            v_new = d["u"][rs] - _dot(_bf(d["w"][rs]), sb)
            d["inter"].append(_dot(_bf(d["q_dec"][rs]), sb))
            d["s"] = d["decay_end"][c * chunk:c * chunk + 1, :] * s + _dot_tn(_bf(d["k_end"][rs]), _bf(v_new))
            d["v_new"].append(v_new)
            if not chain:
                s_out_ref[c, h] = d["s"]
    for h, d in enumerate(hd):
        if chain:
            state_ref[h] = d["s"]
        o = jnp.concatenate(d["inter"], axis=0) + _dot(d["att"], _bf(jnp.concatenate(d["v_new"], axis=0)))
        outs.append(_rms_gate(o, nw_ref[...], z[:, h * GD_D:(h + 1) * GD_D]))
    mix_ref[...] = jnp.concatenate(outs, axis=-1).astype(mix_ref.dtype)

    if chain:
        sc_prev_ref[...] = conv_in[rows - 8:]
        gc_prev_ref[...] = qkv[rows - 8:]

        @pl.when(j == pl.num_programs(1) - 1)
        def _():
            s_out_ref[0] = state_ref[...]
            scp_out_ref[...] = conv_in[rows - 8:]
            gcp_out_ref[...] = qkv[rows - 8:]
    else:
        scp_out_ref[...] = conv_in
        gcp_out_ref[...] = qkv


def _mixer_cd(x, row0, n_seq, length, s0, sconv_state, gconv_state, w_main, w_gates, sconv_w, gconv_w, a_log, dt_bias,
              norm_w):
    n_rows = n_seq * length
    chain = length >= ROW_TILE
    rows = ROW_TILE if chain else BATCH_ROW_TILE
    assert row0 % rows == 0
    first = row0 // rows
    pad8 = lambda st: jnp.pad(st, ((0, 0), (8 - st.shape[1], 0), (0, 0))).reshape(n_seq * 8, st.shape[2])
    scp, gcp = pad8(sconv_state), pad8(gconv_state)
    if chain:
        assert length % rows == 0
        tiles = length // rows
        grid = (n_seq, tiles)
        row_map = lambda b, j: (b * tiles + j, 0)
        x_map = lambda b, j: (first + b * tiles + j, 0)
        seq_map4 = lambda b, j: (b, 0, 0, 0)
        seq_map2 = lambda b, j: (b, 0)
        const2 = lambda b, j: (0, 0)
        seq_block = 1
    else:
        assert length == 8 and n_rows % rows == 0
        seq_block = rows // 8
        grid = (n_rows // rows,)
        row_map = lambda i: (i, 0)
        x_map = lambda i: (first + i, 0)
        seq_map4 = lambda i: (i, 0, 0, 0)
        seq_map2 = lambda i: (i, 0)
        const2 = lambda i: (0, 0)
    zeros4 = jnp.zeros((GD_HEADS,), jnp.float32)
    alog8 = jnp.concatenate([a_log.astype(jnp.float32), zeros4]).reshape(1, 8)
    dtb8 = jnp.concatenate([dt_bias.astype(jnp.float32), zeros4]).reshape(1, 8)
    mix, s_new, scp_new, gcp_new = pl.pallas_call(
        functools.partial(_mixer_cd_kernel, chain=chain),
        grid=grid,
        in_specs=[
            pl.BlockSpec((rows, D_MODEL), x_map),
            pl.BlockSpec((D_MODEL, CD_MAIN), const2),
            pl.BlockSpec((D_MODEL, 8), const2),
            pl.BlockSpec((SC_K, SC_WIDTH), const2),
            pl.BlockSpec((GD_CONV, GD_QKV), const2),
            pl.BlockSpec((1, 8), const2),
            pl.BlockSpec((1, 8), const2),
            pl.BlockSpec((1, GD_D), const2),
            pl.BlockSpec((seq_block, GD_HEADS, GD_D, GD_D), seq_map4),
            pl.BlockSpec((seq_block * 8, SC_WIDTH), seq_map2),
            pl.BlockSpec((seq_block * 8, GD_QKV), seq_map2),
        ],
        out_specs=[
            pl.BlockSpec((rows, MIX_OUT), row_map),
            pl.BlockSpec((seq_block, GD_HEADS, GD_D, GD_D), seq_map4),
            pl.BlockSpec((seq_block * 8, SC_WIDTH), seq_map2),
            pl.BlockSpec((seq_block * 8, GD_QKV), seq_map2),
        ],
        out_shape=[
            jax.ShapeDtypeStruct((n_rows, MIX_OUT), jnp.bfloat16),
            jax.ShapeDtypeStruct(s0.shape, jnp.float32),
            jax.ShapeDtypeStruct(scp.shape, jnp.float32),
            jax.ShapeDtypeStruct(gcp.shape, jnp.float32),
        ],
        scratch_shapes=[
            pltpu.VMEM((GD_HEADS, GD_D, GD_D), jnp.float32),
            pltpu.VMEM((8, SC_WIDTH), jnp.float32),
            pltpu.VMEM((8, GD_QKV), jnp.float32),
        ],
        compiler_params=pltpu.CompilerParams(
            dimension_semantics=("arbitrary",) * len(grid), vmem_limit_bytes=VMEM_LIMIT),
        name="mixer_cd_chain" if chain else "mixer_cd_batch",
    )(x, w_main, w_gates, sconv_w, gconv_w, alog8, dtb8, norm_w.reshape(1, -1), s0, scp, gcp)
    tail = lambda st, keep: st.reshape(n_seq, 8, -1)[:, 8 - keep:]
    return mix, s_new, tail(scp_new, SC_K - 1), tail(gcp_new, GD_CONV - 1)


def _outproj_ln_router_kernel(mix_ref, x_ref, w_ref, g_ref, b_ref, rw_ref, rb_ref,
                              x1_ref, xp_ref, tope_ref, gate_ref, rank_ref, cnt_ref):
    mix = jnp.dot(mix_ref[...], w_ref[...], preferred_element_type=jnp.float32)
    x1 = _layer_norm_rows(DEEPNORM_ALPHA * x_ref[...] + mix, g_ref[...], b_ref[...])
    x1_ref[...] = x1
    packed = _pack_bf16_pairs(x1)
    for j in range(HALF // 128):
        xp_ref[:, j * 8:(j + 1) * 8, :] = packed[:, j * 128:(j + 1) * 128].reshape(x1.shape[0] // 8, 8, 128)
    logits = _dot(_bf(x1), _bf(rw_ref[...])) + rb_ref[...]
    lane = lax.broadcasted_iota(jnp.int32, logits.shape, 1)
    vals, idxs = [], []
    for _ in range(TOP_K):
        m = jnp.max(logits, axis=-1, keepdims=True)
        idx = jnp.min(jnp.where(logits == m, lane, N_EXPERTS), axis=-1, keepdims=True)
        vals.append(m)
        idxs.append(idx)
        logits = jnp.where(lane == idx, -jnp.inf, logits)
    ex = [jnp.exp(v - vals[0]) for v in vals]
    den = ex[0] + ex[1] + ex[2] + ex[3]
    tm = logits.shape[0]
    hits = [lane == idx for idx in idxs]
    member = sum(h.astype(jnp.float32) for h in hits)
    earlier = (lax.broadcasted_iota(jnp.int32, (tm, tm), 1) < lax.broadcasted_iota(jnp.int32, (tm, tm), 0))
    before = _dot(earlier.astype(jnp.bfloat16), member.astype(jnp.bfloat16))
    cnt_ref[0] = jnp.sum(member, axis=0, keepdims=True).astype(jnp.int32)
    col = lax.broadcasted_iota(jnp.int32, tope_ref.shape, 1)
    tope = jnp.zeros(tope_ref.shape, jnp.int32)
    gate = jnp.zeros(gate_ref.shape, jnp.float32)
    rank = jnp.zeros(rank_ref.shape, jnp.int32)
    for k in range(TOP_K):
        tope = jnp.where(col == k, idxs[k], tope)
        gate = jnp.where(col == k, ex[k] / den, gate)
        rank_k = jnp.sum(jnp.where(hits[k], before, 0.0), axis=-1, keepdims=True).astype(jnp.int32)
        rank = jnp.where(col == k, rank_k, rank)
    tope_ref[...] = tope
    gate_ref[...] = gate
    rank_ref[...] = rank


def _outproj_ln_router(mix, x, w_out, ln_g, ln_b, router_w, router_b):
    n = x.shape[0]
    tm = TOKEN_TILE
    row = lambda i: (i, 0)
    full = lambda i: (0, 0)
    return pl.pallas_call(
        _outproj_ln_router_kernel,
        grid=(n // tm,),
        in_specs=[
            pl.BlockSpec((tm, D_MODEL), row),
            pl.BlockSpec((tm, D_MODEL), row),
            pl.BlockSpec((D_MODEL, D_MODEL), full),
            pl.BlockSpec((1, D_MODEL), full),
            pl.BlockSpec((1, D_MODEL), full),
            pl.BlockSpec((D_MODEL, N_EXPERTS), full),
            pl.BlockSpec((1, N_EXPERTS), full),
        ],
        out_specs=[
            pl.BlockSpec((tm, D_MODEL), row),
            pl.BlockSpec((tm // 8, HALF // 16, 128), lambda i: (i, 0, 0)),
            pl.BlockSpec((tm, TOP_K), row),
            pl.BlockSpec((tm, TOP_K), row),
            pl.BlockSpec((tm, TOP_K), row),
            pl.BlockSpec((1, 1, N_EXPERTS), lambda i: (i, 0, 0)),
        ],
        out_shape=[
            jax.ShapeDtypeStruct((n, D_MODEL), jnp.float32),
            jax.ShapeDtypeStruct((n // 8, HALF // 16, 128), jnp.uint32),
            jax.ShapeDtypeStruct((n, TOP_K), jnp.int32),
            jax.ShapeDtypeStruct((n, TOP_K), jnp.float32),
            jax.ShapeDtypeStruct((n, TOP_K), jnp.int32),
            jax.ShapeDtypeStruct((n // tm, 1, N_EXPERTS), jnp.int32),
        ],
        compiler_params=pltpu.CompilerParams(dimension_semantics=("arbitrary",), vmem_limit_bytes=VMEM_LIMIT),
        name="outproj_ln_router",
    )(mix, x, w_out, ln_g.reshape(1, -1), ln_b.reshape(1, -1), router_w, router_b.reshape(1, -1))


def _route_plan(top_e, rank, tile_counts):
    n = top_e.shape[0]
    a = n * TOP_K
    br, tm, ch = EXPERT_ROWS, TOKEN_TILE, COMBINE_CHUNK
    n_virtual = N_EXPERTS * N_GROUPS
    group_tokens = n // N_GROUPS
    n_blocks = a // br + n_virtual + 1
    n_tiles = n // tm
    group_tiles = n_tiles // N_GROUPS
    max_chunks = (tm * TOP_K) // ch + N_EXPERTS + 1
    i32 = jnp.int32

    cnt = tile_counts.reshape(N_GROUPS, group_tiles, N_EXPERTS)
    nblk = (cnt.sum(axis=1) + br - 1) // br
    blk_end = jnp.cumsum(nblk.reshape(-1))
    row_off = ((blk_end - nblk.reshape(-1)) * br).reshape(N_GROUPS, 1, N_EXPERTS)
    seg_start = (row_off + jnp.cumsum(cnt, axis=1) - cnt).reshape(n_tiles, N_EXPERTS)
    seg_len = tile_counts
    blocks = jnp.arange(n_blocks, dtype=i32)
    block_ve = jnp.minimum(jnp.sum(blk_end[None, :] <= blocks[:, None], axis=1), n_virtual - 1).astype(i32)
    valid = blocks < blk_end[-1]
    block_e = block_ve % N_EXPERTS
    block_e = jnp.where(valid, block_e, block_e[jnp.maximum(blk_end[-1] - 1, 0)])
    starts = jnp.concatenate([jnp.ones((1,), bool), block_e[1:] != block_e[:-1]])
    later_start = lax.cummin(jnp.where(starts, blocks, n_blocks)[::-1], axis=0)[::-1]
    next_start = jnp.concatenate([later_start[1:], jnp.full((1,), n_blocks, i32)])
    next_e = jnp.where(next_start < n_blocks, block_e[jnp.minimum(next_start, n_blocks - 1)], -1)
    block_info = jnp.stack([block_e, block_ve // N_EXPERTS, valid.astype(i32), starts.astype(i32), next_e.astype(i32)])

    aligned = seg_start // 8 * 8
    lead = seg_start - aligned
    nch = jnp.where(seg_len > 0, (lead + seg_len + ch - 1) // ch, 0)
    ch_end = jnp.cumsum(nch, axis=1)
    ch_first = ch_end - nch
    n_chunks = ch_end[:, -1]
    cidx = jnp.arange(max_chunks, dtype=i32)[None, :, None]
    owns = jnp.logical_and(ch_first[:, None, :] <= cidx, cidx < ch_end[:, None, :])
    chunk_src = jnp.sum(jnp.where(owns, aligned[:, None, :] + (cidx - ch_first[:, None, :]) * ch, 0), axis=2)

    hit = top_e.reshape(n_tiles, tm, TOP_K, 1) == jnp.arange(N_EXPERTS, dtype=i32)
    pick = lambda table: jnp.sum(jnp.where(hit, table[:, None, None, :], 0), axis=-1)
    rank = rank.reshape(n_tiles, tm, TOP_K)
    dest = pick(seg_start) + rank
    pos = pick(ch_first * ch + lead) + rank
    tok = jnp.arange(n, dtype=i32) % group_tokens
    addr = ((tok // 8) * (8 * (HALF // 128)) + tok % 8).reshape(n_tiles, tm, 1)
    row_addr = jnp.zeros((n_blocks * br,), i32).at[dest.reshape(-1)].set(
        jnp.broadcast_to(addr, dest.shape).reshape(-1), unique_indices=True, mode="promise_in_bounds")
    pos = (pos // 8) * (8 * LANE_TILES) + pos % 8
    return dict(row_addr=row_addr.reshape(n_blocks, 1, br), block_info=block_info,
                chunk_src=(chunk_src * LANE_TILES).astype(i32), n_chunks=n_chunks.astype(i32),
                pos=pos.reshape(n_tiles, 1, tm * TOP_K).astype(i32))


def _expert_kernel(info_ref, addr_ref, addr_next_ref, xq_hbm, wgu_hbm, bgu_ref, wdn_hbm, bdn_ref, y_ref,
                   xq_vmem, stage_ref, wgu_f32_ref, wdn_f32_ref, wgu_ref, wdn_ref, load_sem, w_sems, *, layer):
    b = pl.program_id(0)
    grp = info_ref[1, b]
    prev_grp = info_ref[1, jnp.maximum(b - 1, 0)]

    def weight_copies(expert):
        return (pltpu.make_async_copy(wgu_hbm.at[layer, expert], wgu_f32_ref, w_sems.at[0]),
                pltpu.make_async_copy(wdn_hbm.at[layer, expert], wdn_f32_ref, w_sems.at[1]))

    @pl.when(b == 0)
    def _():
        for cp in weight_copies(info_ref[0, 0]):
            cp.start()

    @pl.when(info_ref[3, b] == 1)
    def _():
        for cp in weight_copies(info_ref[0, b]):
            cp.wait()
        rows = D_MODEL // 8

        def cast(i, carry):
            r0 = pl.multiple_of(i * rows, rows)
            wgu_ref[pl.ds(r0, rows), :] = wgu_f32_ref[pl.ds(r0, rows), :].astype(jnp.bfloat16)
            wdn_ref[pl.ds(r0, rows), :] = wdn_f32_ref[pl.ds(r0, rows), :].astype(jnp.bfloat16)
            return carry

        lax.fori_loop(0, 8, cast, 0)

        @pl.when(info_ref[4, b] >= 0)
        def _():
            for cp in weight_copies(info_ref[4, b]):
                cp.start()

    group_rows = xq_vmem.shape[0]
    slot = b % 2
    tiles = HALF // 128

    def gather_row(addr, dst_slot, group8, sub):
        stage_ref[dst_slot, group8, pl.ds(sub, tiles, stride=8), :] = xq_vmem[pl.ds(addr, tiles, stride=8), :]

    @pl.when(jnp.logical_or(b == 0, grp != prev_grp))
    def _():
        cp = pltpu.make_async_copy(xq_hbm.at[pl.ds(grp * group_rows, group_rows)], xq_vmem, load_sem)
        cp.start()
        cp.wait()

        def gather(i, carry):
            for u in range(8):
                gather_row(addr_ref[0, 0, i * 8 + u], slot, i, u)
            return carry

        lax.fori_loop(0, EXPERT_ROWS // 8, gather, 0)

    @pl.when(info_ref[2, b] == 1)
    def _():
        for r in range(EXPERT_ROWS):
            gather_row(addr_next_ref[0, 0, r], 1 - slot, r // 8, r % 8)
        pieces = [_unpack_bf16_pairs(stage_ref[slot, :, j * 8:(j + 1) * 8, :].reshape(EXPERT_ROWS, 128))
                  for j in range(tiles)]
        lo = jnp.concatenate([p[0] for p in pieces], axis=1)
        hi = jnp.concatenate([p[1] for p in pieces], axis=1)
        hgu = (jnp.dot(lo, wgu_ref[:HALF, :], preferred_element_type=jnp.float32)
               + jnp.dot(hi, wgu_ref[HALF:, :], preferred_element_type=jnp.float32) + bgu_ref[0, 0])
        glu = jnp.minimum(hgu[:, :D_FF], SWIGLU_LIMIT)
        lin = jnp.clip(hgu[:, D_FF:], -SWIGLU_LIMIT, SWIGLU_LIMIT)
        act = (lin + 1.0) * glu * jax.nn.sigmoid(SWIGLU_ALPHA * glu)
        y = jnp.dot(act.astype(jnp.bfloat16), wdn_ref[...], preferred_element_type=jnp.float32) + bdn_ref[0, 0]
        for c in range(LANE_TILES):
            y_ref[:, c * 8:(c + 1) * 8, :] = y[:, c * 128:(c + 1) * 128].reshape(EXPERT_ROWS // 8, 8, 128)

    @pl.when(info_ref[2, b] == 0)
    def _():
        y_ref[...] = jnp.zeros_like(y_ref)


def _expert_mlp(plan, xq, layer, w_gu, b_gu, w_dn, b_dn):
    br = EXPERT_ROWS
    n_blocks = plan["row_addr"].shape[0]
    assert br >= COMBINE_CHUNK + 8
    pad_rows = n_blocks * br
    expert = lambda b, info: (layer, info[0, b], 0, 0)
    grid_spec = pltpu.PrefetchScalarGridSpec(
        num_scalar_prefetch=1,
        grid=(n_blocks,),
        in_specs=[
            pl.BlockSpec((1, 1, br), lambda b, info: (b, 0, 0), memory_space=pltpu.SMEM),
            pl.BlockSpec((1, 1, br), lambda b, info: (jnp.minimum(b + 1, n_blocks - 1), 0, 0), memory_space=pltpu.SMEM),
            pl.BlockSpec(memory_space=pl.ANY),
            pl.BlockSpec(memory_space=pl.ANY),
            pl.BlockSpec((1, 1, 1, 2 * D_FF), expert),
            pl.BlockSpec(memory_space=pl.ANY),
            pl.BlockSpec((1, 1, 1, D_MODEL), expert),
        ],
        out_specs=pl.BlockSpec((br // 8, 8 * LANE_TILES, 128), lambda b, info: (b, 0, 0)),
        scratch_shapes=[
            pltpu.VMEM((xq.shape[0] // N_GROUPS, 128), jnp.uint32),
            pltpu.VMEM((2, br // 8, 8 * (HALF // 128), 128), jnp.uint32),
            pltpu.VMEM((D_MODEL, 2 * D_FF), jnp.float32),
            pltpu.VMEM((D_FF, D_MODEL), jnp.float32),
            pltpu.VMEM((D_MODEL, 2 * D_FF), jnp.bfloat16),
            pltpu.VMEM((D_FF, D_MODEL), jnp.bfloat16),
            pltpu.SemaphoreType.DMA(()),
            pltpu.SemaphoreType.DMA((2,)),
        ],
    )
    return pl.pallas_call(
        functools.partial(_expert_kernel, layer=layer),
        grid_spec=grid_spec,
        out_shape=jax.ShapeDtypeStruct((pad_rows // 8, 8 * LANE_TILES, 128), jnp.float32),
        compiler_params=pltpu.CompilerParams(dimension_semantics=("arbitrary",), vmem_limit_bytes=VMEM_LIMIT),
        name="expert_mlp",
    )(plan["block_info"], plan["row_addr"], plan["row_addr"], xq, w_gu, b_gu.reshape(DEPTH, N_EXPERTS, 1, -1), w_dn,
      b_dn.reshape(DEPTH, N_EXPERTS, 1, -1))


def _combine_ln_kernel(src_ref, nch_ref, pos_ref, gate_ref, x1_ref, y_hbm, g_ref, b_ref, out_ref,
                       stage_ref, ffn_ref, sems):
    i = pl.program_id(0)
    n_tiles = pl.num_programs(0)
    chunk_rows = COMBINE_CHUNK * LANE_TILES

    def chunk_copy(tile, slot, c):
        src = pl.multiple_of(src_ref[tile, c], 8 * LANE_TILES)
        return pltpu.make_async_copy(y_hbm.at[pl.ds(src, chunk_rows)],
                                     stage_ref.at[slot, pl.ds(c * chunk_rows, chunk_rows)], sems.at[slot])

    def issue(tile, slot):
        lax.fori_loop(0, nch_ref[tile], lambda c, carry: (chunk_copy(tile, slot, c).start(), carry)[1], 0)

    @pl.when(i == 0)
    def _():
        issue(0, 0)

    @pl.when(i + 1 < n_tiles)
    def _():
        issue(i + 1, (i + 1) % 2)

    slot = i % 2
    lax.fori_loop(0, nch_ref[i], lambda c, carry: (chunk_copy(i, slot, c).wait(), carry)[1], 0)

    tm = x1_ref.shape[0]

    def combine(j, carry):
        for u in range(8):
            t = j * 8 + u
            acc = None
            for k in range(TOP_K):
                row = stage_ref[slot, pl.ds(pos_ref[0, 0, t * TOP_K + k], LANE_TILES, stride=8), :]
                row = row * gate_ref[0, 0, t * TOP_K + k]
                acc = row if acc is None else acc + row
            ffn_ref[j, pl.ds(u, LANE_TILES, stride=8), :] = acc
        return carry

    lax.fori_loop(0, tm // 8, combine, 0)
    ffn = jnp.concatenate([ffn_ref[:, c * 8:(c + 1) * 8, :].reshape(tm, 128) for c in range(LANE_TILES)], axis=1)
    out_ref[...] = _layer_norm_rows(DEEPNORM_ALPHA * x1_ref[...] + ffn, g_ref[...], b_ref[...])


def _combine_ln(plan, gate, x1, y_flat, ln_g, ln_b):
    n = x1.shape[0]
    tm = TOKEN_TILE
    n_tiles = n // tm
    max_chunks = plan["chunk_src"].shape[1]
    row = lambda i, src, nch: (i, 0)
    full = lambda i, src, nch: (0, 0)
    per_tile = lambda i, src, nch: (i, 0, 0)
    grid_spec = pltpu.PrefetchScalarGridSpec(
        num_scalar_prefetch=2,
        grid=(n_tiles,),
        in_specs=[
            pl.BlockSpec((1, 1, tm * TOP_K), per_tile, memory_space=pltpu.SMEM),
            pl.BlockSpec((1, 1, tm * TOP_K), per_tile, memory_space=pltpu.SMEM),
            pl.BlockSpec((tm, D_MODEL), row),
            pl.BlockSpec(memory_space=pl.ANY),
            pl.BlockSpec((1, D_MODEL), full),
            pl.BlockSpec((1, D_MODEL), full),
        ],
        out_specs=pl.BlockSpec((tm, D_MODEL), row),
        scratch_shapes=[
            pltpu.VMEM((2, max_chunks * COMBINE_CHUNK * LANE_TILES, 128), jnp.float32),
            pltpu.VMEM((tm // 8, 8 * LANE_TILES, 128), jnp.float32),
            pltpu.SemaphoreType.DMA((2,)),
        ],
    )
    return pl.pallas_call(
        _combine_ln_kernel,
        grid_spec=grid_spec,
        out_shape=jax.ShapeDtypeStruct((n, D_MODEL), jnp.float32),
        compiler_params=pltpu.CompilerParams(dimension_semantics=("arbitrary",), vmem_limit_bytes=VMEM_LIMIT),
        name="combine_ln",
    )(plan["chunk_src"], plan["n_chunks"], plan["pos"], gate.reshape(n_tiles, 1, tm * TOP_K), x1, y_flat,
      ln_g.reshape(1, -1), ln_b.reshape(1, -1))


def _post_mixer(mix, x, layer, w_out, ln1_g, ln1_b, router_w, router_b, w_gu, b_gu, w_dn, b_dn, ln2_g, ln2_b):
    x1, xp, top_e, gate, rank, tile_counts = _outproj_ln_router(mix, x, w_out, ln1_g, ln1_b, router_w, router_b)
    plan = _route_plan(top_e, rank, tile_counts.reshape(-1, N_EXPERTS))
    y_sorted = _expert_mlp(plan, xp.reshape(-1, 128), layer, w_gu, b_gu, w_dn, b_dn)
    return _combine_ln(plan, gate, x1, y_sorted.reshape(-1, 128), ln2_g, ln2_b)


def kernel(x_prompt, x_sample, state_hgrn, state_pool, state_sconv, state_gdn_conv, state_gdn, w_in_ab, hgrn_lower_bounds, hgrn_norm_w, pool_w, pool_scale, w_out_ab, w_in_cd, sconv_w, gdn_conv_w, gdn_a_log, gdn_dt_bias, gdn_norm_w, w_out_cd, ln1_g, ln1_b, ln2_g, ln2_b, router_w, router_b, w_gu, b_gu, w_dn, b_dn):
    bp, sp, _ = x_prompt.shape
    bs, ss, _ = x_sample.shape
    n_p, n_s = bp * sp, bs * ss
    bf = jnp.bfloat16
    f32 = jnp.float32
    lower_bounds = jnp.cumsum(jax.nn.softmax(hgrn_lower_bounds.astype(f32), axis=0), axis=0)
    x = jnp.concatenate([x_prompt.reshape(n_p, D_MODEL), x_sample.reshape(n_s, D_MODEL)], axis=0)
    zeros_like_prompt = lambda st: jnp.zeros((bp,) + st.shape[2:], f32)

    states = {}
    for l in range(DEPTH):
        if l % 2 == 0:
            args = (w_in_ab[0].astype(bf), lower_bounds[l], hgrn_norm_w[0], pool_w[0].astype(bf), pool_scale[0])
            mp, hp, pp = _mixer_ab(x, 0, bp, sp, 0, zeros_like_prompt(state_hgrn), zeros_like_prompt(state_pool), *args)
            ms, hs, ps = _mixer_ab(x, n_p, bs, ss, PAST_LEN, state_hgrn[0], state_pool[0], *args)
            states.update(hp=hp[None], hs=hs[None], pp=pp[None], ps=ps[None])
            w_out = w_out_ab[0]
        else:
            args = (w_in_cd[0][:, :CD_MAIN].astype(bf), w_in_cd[0][:, CD_MAIN:].astype(bf), sconv_w[0], gdn_conv_w[0],
                    gdn_a_log[0], gdn_dt_bias[0], gdn_norm_w[0])
            mp, gp, scp, gcp = _mixer_cd(x, 0, bp, sp, zeros_like_prompt(state_gdn), zeros_like_prompt(state_sconv),
                                         zeros_like_prompt(state_gdn_conv), *args)
            ms, gs, scs, gcs = _mixer_cd(x, n_p, bs, ss, state_gdn[0], state_sconv[0], state_gdn_conv[0], *args)
            states.update(scp=scp[None], scs=scs[None], gcp=gcp[None], gcs=gcs[None], gp=gp[None], gs=gs[None])
            w_out = w_out_cd[0]
        mix = jnp.concatenate([mp, ms], axis=0)
        x = _post_mixer(mix, x, l, w_out.astype(bf), ln1_g[l], ln1_b[l], router_w[l], router_b[l],
                        w_gu, b_gu, w_dn, b_dn, ln2_g[l], ln2_b[l])
    return (x[:n_p].reshape(bp, sp, D_MODEL), x[n_p:].reshape(bs, ss, D_MODEL),
            states["hp"], states["hs"], states["pp"], states["ps"], states["scp"], states["scs"],
            states["gcp"], states["gcs"], states["gp"], states["gs"])
```

```python
import functools

import jax
import jax.numpy as jnp
from jax import lax
from jax.experimental import pallas as pl
from jax.experimental.pallas import tpu as pltpu

D_MODEL = 1024
DEPTH = 2
N_EXPERTS = 32
TOP_K = 4
D_FF = 1024
SWIGLU_LIMIT = 7.0
SWIGLU_ALPHA = 1.702
LN_EPS = 1e-5
DEEPNORM_ALPHA = (2 * DEPTH) ** 0.25

HALF = D_MODEL // 2
LANE_TILES = D_MODEL // 128
TOKEN_TILE = 256
EXPERT_ROWS = 256
N_GROUPS = 1
COMBINE_CHUNK = 16
VMEM_LIMIT = 60 * 1024 * 1024
MIX_OUT = 1024
BATCH_ROW_TILE = 128
PAST_LEN = 16384


def _layer_norm_rows(v, g, b):
    mu = jnp.mean(v, axis=-1, keepdims=True)
    d = v - mu
    var = jnp.mean(d * d, axis=-1, keepdims=True)
    return d * lax.rsqrt(var + LN_EPS) * g + b


def _pack_bf16_pairs(v):
    lo = pltpu.bitcast(v[:, :HALF].astype(jnp.bfloat16).astype(jnp.float32), jnp.uint32)
    hi = pltpu.bitcast(v[:, HALF:].astype(jnp.bfloat16).astype(jnp.float32), jnp.uint32)
    return (lo >> 16) | (hi & jnp.uint32(0xFFFF0000))


def _unpack_bf16_pairs(p):
    lo = pltpu.bitcast(p << 16, jnp.float32).astype(jnp.bfloat16)
    hi = pltpu.bitcast(p & jnp.uint32(0xFFFF0000), jnp.float32).astype(jnp.bfloat16)
    return lo, hi


ROW_TILE = 256
HG_HEADS, HG_D = 4, 128
HG_WIDTH = HG_HEADS * HG_D
HG_CHUNK = 16
POOL_WINDOWS = (2, 4, 8, 16)
POOL_GC = 128
POOL_WIDTH = 512
RMS_EPS = 1e-6
_HI = lax.Precision.HIGHEST


def _silu(v):
    return v * jax.nn.sigmoid(v)


def _bf(v):
    return v.astype(jnp.bfloat16)


def _dot(a, b, precision=None):
    return jnp.dot(a, b, preferred_element_type=jnp.float32, precision=precision)


def _dot_nt(a, b, precision=None):
    return lax.dot_general(a, b, (((1,), (1,)), ((), ())), preferred_element_type=jnp.float32, precision=precision)


def _dot_tn(a, b, precision=None):
    return lax.dot_general(a, b, (((0,), (0,)), ((), ())), preferred_element_type=jnp.float32, precision=precision)


def _split3(v):
    hi = _bf(v)
    rest = v - hi.astype(jnp.float32)
    mid = _bf(rest)
    return hi, mid, _bf(rest - mid.astype(jnp.float32))


def _mask_bf16(mask):
    return _bf(mask.astype(jnp.float32))


def _exact_mask_dot(dot_fn, m, terms):
    return dot_fn(m, terms[0]) + (dot_fn(m, terms[1]) + dot_fn(m, terms[2]))


def _split2(v):
    hi = _bf(v)
    return hi, _bf(v - hi.astype(jnp.float32))


def _dot3(a2, b2):
    return _dot(a2[0], b2[0]) + (_dot(a2[0], b2[1]) + _dot(a2[1], b2[0]))


def _chunk_masks(rows, chunk):
    shift = chunk.bit_length() - 1
    t = lax.broadcasted_iota(jnp.int32, (rows, rows), 0)
    s = lax.broadcasted_iota(jnp.int32, (rows, rows), 1)
    same = (t >> shift) == (s >> shift)
    return same, jnp.logical_and(same, s <= t), jnp.logical_and(same, s < t)


def _shift_rows_chain(cur, prev, j):
    if j == 0:
        return cur
    p = prev.shape[0]
    rc = pltpu.roll(cur, j, 0)
    rp = prev if j == p else pltpu.roll(prev, j, 0)
    row = lax.broadcasted_iota(jnp.int32, (p, cur.shape[1]), 0)
    head = jnp.where(row < j, rp, rc[:p])
    return jnp.concatenate([head, rc[p:]], axis=0) if cur.shape[0] > p else head


def _shift_rows_batch(cur, prev, j):
    if j == 0:
        return cur
    if j == 8:
        return prev
    rows = cur.shape[0]
    row = lax.broadcasted_iota(jnp.int32, cur.shape, 0)
    return jnp.where((row & 7) < j, pltpu.roll(prev, rows + j - 8, 0), pltpu.roll(cur, j, 0))


def _rms_gate(o, norm_w, gate):
    ms = jnp.mean(o * o, axis=-1, keepdims=True)
    return o * lax.rsqrt(ms + RMS_EPS) * norm_w * _silu(gate)


def _mixer_ab_kernel(x_ref, w_ref, lb_ref, nw_ref, pw_ref, ps_ref, s0_ref, pa_ref, pb_ref,
                     mix_ref, s_out_ref, pa_out_ref, pb_out_ref, state_ref, prev_ref, *, chain, pos0):
    rows = x_ref.shape[0]
    chunk = HG_CHUNK if chain else 8
    n_chunks = rows // chunk
    j = pl.program_id(1) if chain else 0

    proj = _dot(_bf(x_ref[...]), w_ref[...])
    q = _silu(proj[:, 0:HG_WIDTH])
    f = lb_ref[...] + (1.0 - lb_ref[...]) * jax.nn.sigmoid(proj[:, HG_WIDTH:2 * HG_WIDTH])
    v = _silu(proj[:, 2 * HG_WIDTH:3 * HG_WIDTH])
    gate = proj[:, 3 * HG_WIDTH:4 * HG_WIDTH]
    u = proj[:, 4 * HG_WIDTH:]
    log_f = jnp.log(f)
    k = 1.0 - f

    same, incl, _ = _chunk_masks(rows, chunk)
    log_f3 = _split3(log_f)
    sums = _exact_mask_dot(_dot, jnp.concatenate([_mask_bf16(incl), _mask_bf16(same)], axis=0), log_f3)
    cum = sums[:rows]
    total = sums[rows:]
    q_dec = q * jnp.exp(cum)
    k_dec = k * jnp.exp(-cum)
    k_end = k * jnp.exp(total - cum)
    chunk_of_row = lax.broadcasted_iota(jnp.int32, (rows, 128), 0) >> (chunk.bit_length() - 1)
    onehot = _mask_bf16(chunk_of_row == lax.broadcasted_iota(jnp.int32, (rows, 128), 1))
    decay_cols = jnp.exp(_exact_mask_dot(lambda m, t: _dot_tn(t, m), onehot, log_f3))

    if chain:
        @pl.when(j == 0)
        def _():
            state_ref[...] = s0_ref[0]
            prev_ref[...] = jnp.concatenate([pa_ref[...], pb_ref[...]], axis=0)

    head_sl = [slice(h * HG_D, (h + 1) * HG_D) for h in range(HG_HEADS)]
    intra, inter, state = [], [[] for _ in head_sl], [None] * HG_HEADS
    for sl in head_sl:
        att = jnp.where(incl, _dot_nt(_bf(q_dec[:, sl]), _bf(k_dec[:, sl])), 0.0)
        intra.append(_dot(_bf(att), _bf(v[:, sl])))
    for c in range(n_chunks):
        rs = slice(c * chunk, (c + 1) * chunk)
        for h, sl in enumerate(head_sl):
            if chain:
                s = state_ref[h] if c == 0 else state[h]
            else:
                s = s0_ref[c, h]
            inter[h].append(_dot(_bf(q_dec[rs, sl]), _bf(s)))
            state[h] = decay_cols[sl, c:c + 1] * s + _dot_tn(_bf(k_end[rs, sl]), _bf(v[rs, sl]))
            if not chain:
                s_out_ref[c, h] = state[h]
    outs = []
    for h, sl in enumerate(head_sl):
        if chain:
            state_ref[h] = state[h]
        outs.append(_rms_gate(intra[h] + jnp.concatenate(inter[h], axis=0), nw_ref[...], gate[:, sl]))

    if chain:
        prev = prev_ref[...]
        t = j * rows + lax.broadcasted_iota(jnp.int32, (rows, POOL_GC), 0)
    else:
        prev_a, prev_b = pa_ref[...], pb_ref[...]
        t = lax.broadcasted_iota(jnp.int32, (rows, POOL_GC), 0) & 7
    for gi, win in enumerate(POOL_WINDOWS):
        gs = slice(gi * POOL_GC, (gi + 1) * POOL_GC)
        ug = u[:, gs]
        wsum = ug
        for d in range(1, win):
            if chain:
                wsum = wsum + _shift_rows_chain(ug, prev[:, gs], d)
            elif d <= 8:
                wsum = wsum + _shift_rows_batch(ug, prev_b[:, gs], d)
            else:
                wsum = wsum + _shift_rows_batch(prev_b[:, gs], prev_a[:, gs], d - 8)
        cnt = jnp.minimum(win, pos0 + t + 1).astype(jnp.float32)
        diff = wsum / cnt - ug
        outs.append(_dot(_bf(diff), pw_ref[gi]) * ps_ref[:, gs])
    mix_ref[...] = jnp.concatenate(outs, axis=-1).astype(mix_ref.dtype)

    if chain:
        prev_ref[...] = u[rows - 16:]

        @pl.when(j == pl.num_programs(1) - 1)
        def _():
            s_out_ref[0] = state_ref[...]
            pa_out_ref[...] = u[rows - 16:rows - 8]
            pb_out_ref[...] = u[rows - 8:]
    else:
        pa_out_ref[...] = prev_b
        pb_out_ref[...] = u


def _mixer_ab(x, row0, n_seq, length, pos0, s0, pool_state, w_in, lb, norm_w, pool_w, pool_scale):
    n_rows = n_seq * length
    chain = length >= ROW_TILE
    rows = ROW_TILE if chain else BATCH_ROW_TILE
    assert row0 % rows == 0
    first = row0 // rows
    pool16 = jnp.pad(pool_state, ((0, 0), (1, 0), (0, 0)))
    pa = pool16[:, :8].reshape(n_seq * 8, POOL_WIDTH)
    pb = pool16[:, 8:].reshape(n_seq * 8, POOL_WIDTH)
    if chain:
        assert length % rows == 0
        tiles = length // rows
        grid = (n_seq, tiles)
        row_map = lambda b, j: (b * tiles + j, 0)
        x_map = lambda b, j: (first + b * tiles + j, 0)
        seq_map4 = lambda b, j: (b, 0, 0, 0)
        seq_map2 = lambda b, j: (b, 0)
        const2 = lambda b, j: (0, 0)
        const3 = lambda b, j: (0, 0, 0)
        seq_block = 1
    else:
        assert length == 8 and n_rows % rows == 0
        seq_block = rows // 8
        grid = (n_rows // rows,)
        row_map = lambda i: (i, 0)
        x_map = lambda i: (first + i, 0)
        seq_map4 = lambda i: (i, 0, 0, 0)
        seq_map2 = lambda i: (i, 0)
        const2 = lambda i: (0, 0)
        const3 = lambda i: (0, 0, 0)
    d_in = w_in.shape[1]
    mix, s_new, pa_new, pb_new = pl.pallas_call(
        functools.partial(_mixer_ab_kernel, chain=chain, pos0=pos0),
        grid=grid,
        in_specs=[
            pl.BlockSpec((rows, D_MODEL), x_map),
            pl.BlockSpec((D_MODEL, d_in), const2),
            pl.BlockSpec((1, HG_WIDTH), const2),
            pl.BlockSpec((1, HG_D), const2),
            pl.BlockSpec((len(POOL_WINDOWS), POOL_GC, POOL_GC), const3),
            pl.BlockSpec((1, POOL_WIDTH), const2),
            pl.BlockSpec((seq_block, HG_HEADS, HG_D, HG_D), seq_map4),
            pl.BlockSpec((seq_block * 8, POOL_WIDTH), seq_map2),
            pl.BlockSpec((seq_block * 8, POOL_WIDTH), seq_map2),
        ],
        out_specs=[
            pl.BlockSpec((rows, MIX_OUT), row_map),
            pl.BlockSpec((seq_block, HG_HEADS, HG_D, HG_D), seq_map4),
            pl.BlockSpec((seq_block * 8, POOL_WIDTH), seq_map2),
            pl.BlockSpec((seq_block * 8, POOL_WIDTH), seq_map2),
        ],
        out_shape=[
            jax.ShapeDtypeStruct((n_rows, MIX_OUT), jnp.bfloat16),
            jax.ShapeDtypeStruct(s0.shape, jnp.float32),
            jax.ShapeDtypeStruct(pa.shape, jnp.float32),
            jax.ShapeDtypeStruct(pb.shape, jnp.float32),
        ],
        scratch_shapes=[
            pltpu.VMEM((HG_HEADS, HG_D, HG_D), jnp.float32),
            pltpu.VMEM((16, POOL_WIDTH), jnp.float32),
        ],
        compiler_params=pltpu.CompilerParams(
            dimension_semantics=("arbitrary",) * len(grid), vmem_limit_bytes=VMEM_LIMIT),
        name="mixer_ab_chain" if chain else "mixer_ab_batch",
    )(x, w_in, lb.reshape(1, -1), norm_w.reshape(1, -1), pool_w, pool_scale.reshape(1, -1), s0, pa, pb)
    pool_new = jnp.concatenate([pa_new.reshape(n_seq, 8, POOL_WIDTH), pb_new.reshape(n_seq, 8, POOL_WIDTH)], axis=1)
    return mix, s_new, pool_new[:, 1:]


SC_WIDTH, SC_K = 512, 3
GD_HEADS, GD_D = 4, 128
GD_CONV = 4
GD_CHUNK = 64
GD_QKV = GD_HEADS * 3 * GD_D
CD_MAIN = 3 * SC_WIDTH + GD_QKV + GD_HEADS * GD_D


def _causal_conv(cur, prev, w_ref, shift_fn):
    width = w_ref.shape[0]
    acc = cur * w_ref[width - 1:width, :]
    for j in range(width - 1):
        acc = acc + shift_fn(cur, prev, width - 1 - j) * w_ref[j:j + 1, :]
    return acc


def _softplus(v):
    return jnp.maximum(v, 0.0) + jnp.log1p(jnp.exp(-jnp.abs(v)))


def _mixer_cd_kernel(x_ref, w_ref, wg_ref, scw_ref, gcw_ref, alog_ref, dtb_ref, nw_ref, s0_ref, scp_ref, gcp_ref,
                     mix_ref, s_out_ref, scp_out_ref, gcp_out_ref, state_ref, sc_prev_ref, gc_prev_ref, *, chain):
    rows = x_ref.shape[0]
    chunk = GD_CHUNK if chain else 8
    n_chunks = rows // chunk
    j = pl.program_id(1) if chain else 0
    shift_fn = _shift_rows_chain if chain else _shift_rows_batch

    xb = _bf(x_ref[...])
    proj = _dot(xb, w_ref[...])
    gates = _dot(xb, wg_ref[...])
    b_gate = proj[:, 0:SC_WIDTH]
    conv_in = proj[:, SC_WIDTH:2 * SC_WIDTH] * proj[:, 2 * SC_WIDTH:3 * SC_WIDTH]
    qkv = proj[:, 3 * SC_WIDTH:3 * SC_WIDTH + GD_QKV]
    z = proj[:, 3 * SC_WIDTH + GD_QKV:]

    if chain:
        @pl.when(j == 0)
        def _():
            state_ref[...] = s0_ref[0]
            sc_prev_ref[...] = scp_ref[...]
            gc_prev_ref[...] = gcp_ref[...]
        sc_prev, gc_prev = sc_prev_ref[...], gc_prev_ref[...]
    else:
        sc_prev, gc_prev = scp_ref[...], gcp_ref[...]

    outs = [b_gate * _causal_conv(conv_in, sc_prev, scw_ref, shift_fn)]
    qkv_c = _silu(_causal_conv(qkv, gc_prev, gcw_ref, shift_fn))

    lane8 = lax.broadcasted_iota(jnp.int32, gates.shape, 1)
    g_dec = -jnp.exp(alog_ref[...]) * _softplus(gates + dtb_ref[...])
    cols = jnp.where(lane8 < GD_HEADS, g_dec, jax.nn.sigmoid(gates))
    same, incl, strict = _chunk_masks(rows, chunk)
    cols3 = _split3(cols)
    gcum = _exact_mask_dot(_dot, _mask_bf16(incl), cols3)
    gtot = _exact_mask_dot(_dot, _mask_bf16(same), cols3)
    eye8 = _mask_bf16(lax.broadcasted_iota(jnp.int32, (8, 8), 0) == lax.broadcasted_iota(jnp.int32, (8, 8), 1))
    gcum_rows = _exact_mask_dot(_dot_nt, eye8, _split3(gcum))

    eye = (lax.broadcasted_iota(jnp.int32, (rows, rows), 0)
           == lax.broadcasted_iota(jnp.int32, (rows, rows), 1)).astype(jnp.float32)
    hd = []
    for h in range(GD_HEADS):
        sl = lambda part: slice(part * GD_HEADS * GD_D + h * GD_D, part * GD_HEADS * GD_D + (h + 1) * GD_D)
        q, k, v = qkv_c[:, sl(0)], qkv_c[:, sl(1)], qkv_c[:, sl(2)]
        q = q * lax.rsqrt(jnp.sum(q * q, axis=-1, keepdims=True) + 1e-6) * (GD_D ** -0.5)
        k = k * lax.rsqrt(jnp.sum(k * k, axis=-1, keepdims=True) + 1e-6)
        beta = cols[:, GD_HEADS + h:GD_HEADS + h + 1]
        gc = gcum[:, h:h + 1]
        gt = gtot[:, h:h + 1]
        decay = jnp.where(incl, jnp.exp(jnp.where(incl, gc - gcum_rows[h:h + 1, :], 0.0)), 0.0)
        k_beta = k * beta
        kb = _bf(k)
        egc = jnp.exp(gc)
        power = -jnp.where(strict, _dot_nt(_bf(k_beta), kb) * decay, 0.0)
        hd.append(dict(
            inv=eye + power, power=_bf(power),
            att=_bf(jnp.where(incl, _dot_nt(_bf(q), kb) * decay, 0.0)),
            rhs=_split2(jnp.concatenate([v * beta, k_beta * egc], axis=1)),
            q_dec=q * egc, k_end=k * jnp.exp(gt - gc),
            decay_end=jnp.exp(gt), inter=[], v_new=[], s=None))

    for _ in range(chunk.bit_length() - 2):
        for d in hd:
            d["power"] = _bf(_dot(d["power"], d["power"]))
        for d in hd:
            d["inv"] = d["inv"] + _dot(_bf(d["inv"]), d["power"])
    for d in hd:
        uw = _dot3(_split2(d["inv"]), d["rhs"])
        d["u"], d["w"] = uw[:, :GD_D], uw[:, GD_D:]

    for c in range(n_chunks):
        rs = slice(c * chunk, (c + 1) * chunk)
        for h, d in enumerate(hd):
            if chain:
                s = state_ref[h] if c == 0 else d["s"]
            else:
                s = s0_ref[c, h]
            sb = _bf(s)
            v_new = d["u"][rs] - _dot(_bf(d["w"][rs]), sb)
            d["inter"].append(_dot(_bf(d["q_dec"][rs]), sb))
            d["s"] = d["decay_end"][c * chunk:c * chunk + 1, :] * s + _dot_tn(_bf(d["k_end"][rs]), _bf(v_new))
            d["v_new"].append(v_new)
            if not chain:
                s_out_ref[c, h] = d["s"]
    for h, d in enumerate(hd):
        if chain:
            state_ref[h] = d["s"]
        o = jnp.concatenate(d["inter"], axis=0) + _dot(d["att"], _bf(jnp.concatenate(d["v_new"], axis=0)))
        outs.append(_rms_gate(o, nw_ref[...], z[:, h * GD_D:(h + 1) * GD_D]))
    mix_ref[...] = jnp.concatenate(outs, axis=-1).astype(mix_ref.dtype)

    if chain:
        sc_prev_ref[...] = conv_in[rows - 8:]
        gc_prev_ref[...] = qkv[rows - 8:]

        @pl.when(j == pl.num_programs(1) - 1)
        def _():
            s_out_ref[0] = state_ref[...]
            scp_out_ref[...] = conv_in[rows - 8:]
            gcp_out_ref[...] = qkv[rows - 8:]
    else:
        scp_out_ref[...] = conv_in
        gcp_out_ref[...] = qkv


def _mixer_cd(x, row0, n_seq, length, s0, sconv_state, gconv_state, w_main, w_gates, sconv_w, gconv_w, a_log, dt_bias,
              norm_w):
    n_rows = n_seq * length
    chain = length >= ROW_TILE
    rows = ROW_TILE if chain else BATCH_ROW_TILE
    assert row0 % rows == 0
    first = row0 // rows
    pad8 = lambda st: jnp.pad(st, ((0, 0), (8 - st.shape[1], 0), (0, 0))).reshape(n_seq * 8, st.shape[2])
    scp, gcp = pad8(sconv_state), pad8(gconv_state)
    if chain:
        assert length % rows == 0
        tiles = length // rows
        grid = (n_seq, tiles)
        row_map = lambda b, j: (b * tiles + j, 0)
        x_map = lambda b, j: (first + b * tiles + j, 0)
        seq_map4 = lambda b, j: (b, 0, 0, 0)
        seq_map2 = lambda b, j: (b, 0)
        const2 = lambda b, j: (0, 0)
        seq_block = 1
    else:
        assert length == 8 and n_rows % rows == 0
        seq_block = rows // 8
        grid = (n_rows // rows,)
        row_map = lambda i: (i, 0)
        x_map = lambda i: (first + i, 0)
        seq_map4 = lambda i: (i, 0, 0, 0)
        seq_map2 = lambda i: (i, 0)
        const2 = lambda i: (0, 0)
    zeros4 = jnp.zeros((GD_HEADS,), jnp.float32)
    alog8 = jnp.concatenate([a_log.astype(jnp.float32), zeros4]).reshape(1, 8)
    dtb8 = jnp.concatenate([dt_bias.astype(jnp.float32), zeros4]).reshape(1, 8)
    mix, s_new, scp_new, gcp_new = pl.pallas_call(
        functools.partial(_mixer_cd_kernel, chain=chain),
        grid=grid,
        in_specs=[
            pl.BlockSpec((rows, D_MODEL), x_map),
            pl.BlockSpec((D_MODEL, CD_MAIN), const2),
            pl.BlockSpec((D_MODEL, 8), const2),
            pl.BlockSpec((SC_K, SC_WIDTH), const2),
            pl.BlockSpec((GD_CONV, GD_QKV), const2),
            pl.BlockSpec((1, 8), const2),
            pl.BlockSpec((1, 8), const2),
            pl.BlockSpec((1, GD_D), const2),
            pl.BlockSpec((seq_block, GD_HEADS, GD_D, GD_D), seq_map4),
            pl.BlockSpec((seq_block * 8, SC_WIDTH), seq_map2),
            pl.BlockSpec((seq_block * 8, GD_QKV), seq_map2),
        ],
        out_specs=[
            pl.BlockSpec((rows, MIX_OUT), row_map),
            pl.BlockSpec((seq_block, GD_HEADS, GD_D, GD_D), seq_map4),
            pl.BlockSpec((seq_block * 8, SC_WIDTH), seq_map2),
            pl.BlockSpec((seq_block * 8, GD_QKV), seq_map2),
        ],
        out_shape=[
            jax.ShapeDtypeStruct((n_rows, MIX_OUT), jnp.bfloat16),
            jax.ShapeDtypeStruct(s0.shape, jnp.float32),
            jax.ShapeDtypeStruct(scp.shape, jnp.float32),
            jax.ShapeDtypeStruct(gcp.shape, jnp.float32),
        ],
        scratch_shapes=[
            pltpu.VMEM((GD_HEADS, GD_D, GD_D), jnp.float32),
            pltpu.VMEM((8, SC_WIDTH), jnp.float32),
            pltpu.VMEM((8, GD_QKV), jnp.float32),
        ],
        compiler_params=pltpu.CompilerParams(
            dimension_semantics=("arbitrary",) * len(grid), vmem_limit_bytes=VMEM_LIMIT),
        name="mixer_cd_chain" if chain else "mixer_cd_batch",
    )(x, w_main, w_gates, sconv_w, gconv_w, alog8, dtb8, norm_w.reshape(1, -1), s0, scp, gcp)
    tail = lambda st, keep: st.reshape(n_seq, 8, -1)[:, 8 - keep:]
    return mix, s_new, tail(scp_new, SC_K - 1), tail(gcp_new, GD_CONV - 1)


def _outproj_ln_router_kernel(mix_ref, x_ref, w_ref, g_ref, b_ref, rw_ref, rb_ref,
                              x1_ref, xp_ref, tope_ref, gate_ref, rank_ref, cnt_ref):
    mix = jnp.dot(mix_ref[...], w_ref[...], preferred_element_type=jnp.float32)
    x1 = _layer_norm_rows(DEEPNORM_ALPHA * x_ref[...] + mix, g_ref[...], b_ref[...])
    x1_ref[...] = x1
    packed = _pack_bf16_pairs(x1)
    for j in range(HALF // 128):
        xp_ref[:, j * 8:(j + 1) * 8, :] = packed[:, j * 128:(j + 1) * 128].reshape(x1.shape[0] // 8, 8, 128)
    logits = _dot(_bf(x1), _bf(rw_ref[...])) + rb_ref[...]
    lane = lax.broadcasted_iota(jnp.int32, logits.shape, 1)
    vals, idxs = [], []
    for _ in range(TOP_K):
        m = jnp.max(logits, axis=-1, keepdims=True)
        idx = jnp.min(jnp.where(logits == m, lane, N_EXPERTS), axis=-1, keepdims=True)
        vals.append(m)
        idxs.append(idx)
        logits = jnp.where(lane == idx, -jnp.inf, logits)
    ex = [jnp.exp(v - vals[0]) for v in vals]
    den = ex[0] + ex[1] + ex[2] + ex[3]
    tm = logits.shape[0]
    hits = [lane == idx for idx in idxs]
    member = sum(h.astype(jnp.float32) for h in hits)
    earlier = (lax.broadcasted_iota(jnp.int32, (tm, tm), 1) < lax.broadcasted_iota(jnp.int32, (tm, tm), 0))
    before = _dot(earlier.astype(jnp.bfloat16), member.astype(jnp.bfloat16))
    cnt_ref[0] = jnp.sum(member, axis=0, keepdims=True).astype(jnp.int32)
    col = lax.broadcasted_iota(jnp.int32, tope_ref.shape, 1)
    tope = jnp.zeros(tope_ref.shape, jnp.int32)
    gate = jnp.zeros(gate_ref.shape, jnp.float32)
    rank = jnp.zeros(rank_ref.shape, jnp.int32)
    for k in range(TOP_K):
        tope = jnp.where(col == k, idxs[k], tope)
        gate = jnp.where(col == k, ex[k] / den, gate)
        rank_k = jnp.sum(jnp.where(hits[k], before, 0.0), axis=-1, keepdims=True).astype(jnp.int32)
        rank = jnp.where(col == k, rank_k, rank)
    tope_ref[...] = tope
    gate_ref[...] = gate
    rank_ref[...] = rank


def _outproj_ln_router(mix, x, w_out, ln_g, ln_b, router_w, router_b):
    n = x.shape[0]
    tm = TOKEN_TILE
    row = lambda i: (i, 0)
    full = lambda i: (0, 0)
    return pl.pallas_call(
        _outproj_ln_router_kernel,
        grid=(n // tm,),
        in_specs=[
            pl.BlockSpec((tm, D_MODEL), row),
            pl.BlockSpec((tm, D_MODEL), row),
            pl.BlockSpec((D_MODEL, D_MODEL), full),
            pl.BlockSpec((1, D_MODEL), full),
            pl.BlockSpec((1, D_MODEL), full),
            pl.BlockSpec((D_MODEL, N_EXPERTS), full),
            pl.BlockSpec((1, N_EXPERTS), full),
        ],
        out_specs=[
            pl.BlockSpec((tm, D_MODEL), row),
            pl.BlockSpec((tm // 8, HALF // 16, 128), lambda i: (i, 0, 0)),
            pl.BlockSpec((tm, TOP_K), row),
            pl.BlockSpec((tm, TOP_K), row),
            pl.BlockSpec((tm, TOP_K), row),
            pl.BlockSpec((1, 1, N_EXPERTS), lambda i: (i, 0, 0)),
        ],
        out_shape=[
            jax.ShapeDtypeStruct((n, D_MODEL), jnp.float32),
            jax.ShapeDtypeStruct((n // 8, HALF // 16, 128), jnp.uint32),
            jax.ShapeDtypeStruct((n, TOP_K), jnp.int32),
            jax.ShapeDtypeStruct((n, TOP_K), jnp.float32),
            jax.ShapeDtypeStruct((n, TOP_K), jnp.int32),
            jax.ShapeDtypeStruct((n // tm, 1, N_EXPERTS), jnp.int32),
        ],
        compiler_params=pltpu.CompilerParams(dimension_semantics=("arbitrary",), vmem_limit_bytes=VMEM_LIMIT),
        name="outproj_ln_router",
    )(mix, x, w_out, ln_g.reshape(1, -1), ln_b.reshape(1, -1), router_w, router_b.reshape(1, -1))


def _route_plan(top_e, rank, tile_counts):
    n = top_e.shape[0]
    a = n * TOP_K
    br, tm, ch = EXPERT_ROWS, TOKEN_TILE, COMBINE_CHUNK
    n_virtual = N_EXPERTS * N_GROUPS
    group_tokens = n // N_GROUPS
    n_blocks = a // br + n_virtual + 1
    n_tiles = n // tm
    group_tiles = n_tiles // N_GROUPS
    max_chunks = (tm * TOP_K) // ch + N_EXPERTS + 1
    i32 = jnp.int32

    cnt = tile_counts.reshape(N_GROUPS, group_tiles, N_EXPERTS)
    nblk = (cnt.sum(axis=1) + br - 1) // br
    blk_end = jnp.cumsum(nblk.reshape(-1))
    row_off = ((blk_end - nblk.reshape(-1)) * br).reshape(N_GROUPS, 1, N_EXPERTS)
    seg_start = (row_off + jnp.cumsum(cnt, axis=1) - cnt).reshape(n_tiles, N_EXPERTS)
    seg_len = tile_counts
    blocks = jnp.arange(n_blocks, dtype=i32)
    block_ve = jnp.minimum(jnp.sum(blk_end[None, :] <= blocks[:, None], axis=1), n_virtual - 1).astype(i32)
    valid = blocks < blk_end[-1]
    block_e = block_ve % N_EXPERTS
    block_e = jnp.where(valid, block_e, block_e[jnp.maximum(blk_end[-1] - 1, 0)])
    starts = jnp.concatenate([jnp.ones((1,), bool), block_e[1:] != block_e[:-1]])
    later_start = lax.cummin(jnp.where(starts, blocks, n_blocks)[::-1], axis=0)[::-1]
    next_start = jnp.concatenate([later_start[1:], jnp.full((1,), n_blocks, i32)])
    next_e = jnp.where(next_start < n_blocks, block_e[jnp.minimum(next_start, n_blocks - 1)], -1)
    block_info = jnp.stack([block_e, block_ve // N_EXPERTS, valid.astype(i32), starts.astype(i32), next_e.astype(i32)])

    aligned = seg_start // 8 * 8
    lead = seg_start - aligned
    nch = jnp.where(seg_len > 0, (lead + seg_len + ch - 1) // ch, 0)
    ch_end = jnp.cumsum(nch, axis=1)
    ch_first = ch_end - nch
    n_chunks = ch_end[:, -1]
    cidx = jnp.arange(max_chunks, dtype=i32)[None, :, None]
    owns = jnp.logical_and(ch_first[:, None, :] <= cidx, cidx < ch_end[:, None, :])
    chunk_src = jnp.sum(jnp.where(owns, aligned[:, None, :] + (cidx - ch_first[:, None, :]) * ch, 0), axis=2)

    hit = top_e.reshape(n_tiles, tm, TOP_K, 1) == jnp.arange(N_EXPERTS, dtype=i32)
    pick = lambda table: jnp.sum(jnp.where(hit, table[:, None, None, :], 0), axis=-1)
    rank = rank.reshape(n_tiles, tm, TOP_K)
    dest = pick(seg_start) + rank
    pos = pick(ch_first * ch + lead) + rank
    tok = jnp.arange(n, dtype=i32) % group_tokens
    addr = ((tok // 8) * (8 * (HALF // 128)) + tok % 8).reshape(n_tiles, tm, 1)
    row_addr = jnp.zeros((n_blocks * br,), i32).at[dest.reshape(-1)].set(
        jnp.broadcast_to(addr, dest.shape).reshape(-1), unique_indices=True, mode="promise_in_bounds")
    pos = (pos // 8) * (8 * LANE_TILES) + pos % 8
    return dict(row_addr=row_addr.reshape(n_blocks, 1, br), block_info=block_info,
                chunk_src=(chunk_src * LANE_TILES).astype(i32), n_chunks=n_chunks.astype(i32),
                pos=pos.reshape(n_tiles, 1, tm * TOP_K).astype(i32))


def _expert_kernel(info_ref, addr_ref, addr_next_ref, xq_hbm, wgu_hbm, bgu_ref, wdn_hbm, bdn_ref, y_ref,
                   xq_vmem, stage_ref, wgu_f32_ref, wdn_f32_ref, wgu_ref, wdn_ref, load_sem, w_sems, *, layer):
    b = pl.program_id(0)
    grp = info_ref[1, b]
    prev_grp = info_ref[1, jnp.maximum(b - 1, 0)]

    def weight_copies(expert):
        return (pltpu.make_async_copy(wgu_hbm.at[layer, expert], wgu_f32_ref, w_sems.at[0]),
                pltpu.make_async_copy(wdn_hbm.at[layer, expert], wdn_f32_ref, w_sems.at[1]))

    @pl.when(b == 0)
    def _():
        for cp in weight_copies(info_ref[0, 0]):
            cp.start()

    @pl.when(info_ref[3, b] == 1)
    def _():
        for cp in weight_copies(info_ref[0, b]):
            cp.wait()
        rows = D_MODEL // 8

        def cast(i, carry):
            r0 = pl.multiple_of(i * rows, rows)
            wgu_ref[pl.ds(r0, rows), :] = wgu_f32_ref[pl.ds(r0, rows), :].astype(jnp.bfloat16)
            wdn_ref[pl.ds(r0, rows), :] = wdn_f32_ref[pl.ds(r0, rows), :].astype(jnp.bfloat16)
            return carry

        lax.fori_loop(0, 8, cast, 0)

        @pl.when(info_ref[4, b] >= 0)
        def _():
            for cp in weight_copies(info_ref[4, b]):
                cp.start()

    group_rows = xq_vmem.shape[0]
    slot = b % 2
    tiles = HALF // 128

    def gather_row(addr, dst_slot, group8, sub):
        stage_ref[dst_slot, group8, pl.ds(sub, tiles, stride=8), :] = xq_vmem[pl.ds(addr, tiles, stride=8), :]

    @pl.when(jnp.logical_or(b == 0, grp != prev_grp))
    def _():
        cp = pltpu.make_async_copy(xq_hbm.at[pl.ds(grp * group_rows, group_rows)], xq_vmem, load_sem)
        cp.start()
        cp.wait()

        def gather(i, carry):
            for u in range(8):
                gather_row(addr_ref[0, 0, i * 8 + u], slot, i, u)
            return carry

        lax.fori_loop(0, EXPERT_ROWS // 8, gather, 0)

    @pl.when(info_ref[2, b] == 1)
    def _():
        for r in range(EXPERT_ROWS):
            gather_row(addr_next_ref[0, 0, r], 1 - slot, r // 8, r % 8)
        pieces = [_unpack_bf16_pairs(stage_ref[slot, :, j * 8:(j + 1) * 8, :].reshape(EXPERT_ROWS, 128))
                  for j in range(tiles)]
        lo = jnp.concatenate([p[0] for p in pieces], axis=1)
        hi = jnp.concatenate([p[1] for p in pieces], axis=1)
        hgu = (jnp.dot(lo, wgu_ref[:HALF, :], preferred_element_type=jnp.float32)
               + jnp.dot(hi, wgu_ref[HALF:, :], preferred_element_type=jnp.float32) + bgu_ref[0, 0])
        glu = jnp.minimum(hgu[:, :D_FF], SWIGLU_LIMIT)
        lin = jnp.clip(hgu[:, D_FF:], -SWIGLU_LIMIT, SWIGLU_LIMIT)
        act = (lin + 1.0) * glu * jax.nn.sigmoid(SWIGLU_ALPHA * glu)
        y = jnp.dot(act.astype(jnp.bfloat16), wdn_ref[...], preferred_element_type=jnp.float32) + bdn_ref[0, 0]
        for c in range(LANE_TILES):
            y_ref[:, c * 8:(c + 1) * 8, :] = y[:, c * 128:(c + 1) * 128].reshape(EXPERT_ROWS // 8, 8, 128)

    @pl.when(info_ref[2, b] == 0)
    def _():
        y_ref[...] = jnp.zeros_like(y_ref)


def _expert_mlp(plan, xq, layer, w_gu, b_gu, w_dn, b_dn):
    br = EXPERT_ROWS
    n_blocks = plan["row_addr"].shape[0]
    assert br >= COMBINE_CHUNK + 8
    pad_rows = n_blocks * br
    expert = lambda b, info: (layer, info[0, b], 0, 0)
    grid_spec = pltpu.PrefetchScalarGridSpec(
        num_scalar_prefetch=1,
        grid=(n_blocks,),
        in_specs=[
            pl.BlockSpec((1, 1, br), lambda b, info: (b, 0, 0), memory_space=pltpu.SMEM),
            pl.BlockSpec((1, 1, br), lambda b, info: (jnp.minimum(b + 1, n_blocks - 1), 0, 0), memory_space=pltpu.SMEM),
            pl.BlockSpec(memory_space=pl.ANY),
            pl.BlockSpec(memory_space=pl.ANY),
            pl.BlockSpec((1, 1, 1, 2 * D_FF), expert),
            pl.BlockSpec(memory_space=pl.ANY),
            pl.BlockSpec((1, 1, 1, D_MODEL), expert),
        ],
        out_specs=pl.BlockSpec((br // 8, 8 * LANE_TILES, 128), lambda b, info: (b, 0, 0)),
        scratch_shapes=[
            pltpu.VMEM((xq.shape[0] // N_GROUPS, 128), jnp.uint32),
            pltpu.VMEM((2, br // 8, 8 * (HALF // 128), 128), jnp.uint32),
            pltpu.VMEM((D_MODEL, 2 * D_FF), jnp.float32),
            pltpu.VMEM((D_FF, D_MODEL), jnp.float32),
            pltpu.VMEM((D_MODEL, 2 * D_FF), jnp.bfloat16),
            pltpu.VMEM((D_FF, D_MODEL), jnp.bfloat16),
            pltpu.SemaphoreType.DMA(()),
            pltpu.SemaphoreType.DMA((2,)),
        ],
    )
    return pl.pallas_call(
        functools.partial(_expert_kernel, layer=layer),
        grid_spec=grid_spec,
        out_shape=jax.ShapeDtypeStruct((pad_rows // 8, 8 * LANE_TILES, 128), jnp.float32),
        compiler_params=pltpu.CompilerParams(dimension_semantics=("arbitrary",), vmem_limit_bytes=VMEM_LIMIT),
        name="expert_mlp",
    )(plan["block_info"], plan["row_addr"], plan["row_addr"], xq, w_gu, b_gu.reshape(DEPTH, N_EXPERTS, 1, -1), w_dn,
      b_dn.reshape(DEPTH, N_EXPERTS, 1, -1))


def _combine_ln_kernel(src_ref, nch_ref, pos_ref, gate_ref, x1_ref, y_hbm, g_ref, b_ref, out_ref,
                       stage_ref, ffn_ref, sems):
    i = pl.program_id(0)
    n_tiles = pl.num_programs(0)
    chunk_rows = COMBINE_CHUNK * LANE_TILES

    def chunk_copy(tile, slot, c):
        src = pl.multiple_of(src_ref[tile, c], 8 * LANE_TILES)
        return pltpu.make_async_copy(y_hbm.at[pl.ds(src, chunk_rows)],
                                     stage_ref.at[slot, pl.ds(c * chunk_rows, chunk_rows)], sems.at[slot])

    def issue(tile, slot):
        lax.fori_loop(0, nch_ref[tile], lambda c, carry: (chunk_copy(tile, slot, c).start(), carry)[1], 0)

    @pl.when(i == 0)
    def _():
        issue(0, 0)

    @pl.when(i + 1 < n_tiles)
    def _():
        issue(i + 1, (i + 1) % 2)

    slot = i % 2
    lax.fori_loop(0, nch_ref[i], lambda c, carry: (chunk_copy(i, slot, c).wait(), carry)[1], 0)

    tm = x1_ref.shape[0]

    def combine(j, carry):
        for u in range(8):
            t = j * 8 + u
            acc = None
            for k in range(TOP_K):
                row = stage_ref[slot, pl.ds(pos_ref[0, 0, t * TOP_K + k], LANE_TILES, stride=8), :]
                row = row * gate_ref[0, 0, t * TOP_K + k]
                acc = row if acc is None else acc + row
            ffn_ref[j, pl.ds(u, LANE_TILES, stride=8), :] = acc
        return carry

    lax.fori_loop(0, tm // 8, combine, 0)
    ffn = jnp.concatenate([ffn_ref[:, c * 8:(c + 1) * 8, :].reshape(tm, 128) for c in range(LANE_TILES)], axis=1)
    out_ref[...] = _layer_norm_rows(DEEPNORM_ALPHA * x1_ref[...] + ffn, g_ref[...], b_ref[...])


def _combine_ln(plan, gate, x1, y_flat, ln_g, ln_b):
    n = x1.shape[0]
    tm = TOKEN_TILE
    n_tiles = n // tm
    max_chunks = plan["chunk_src"].shape[1]
    row = lambda i, src, nch: (i, 0)
    full = lambda i, src, nch: (0, 0)
    per_tile = lambda i, src, nch: (i, 0, 0)
    grid_spec = pltpu.PrefetchScalarGridSpec(
        num_scalar_prefetch=2,
        grid=(n_tiles,),
        in_specs=[
            pl.BlockSpec((1, 1, tm * TOP_K), per_tile, memory_space=pltpu.SMEM),
            pl.BlockSpec((1, 1, tm * TOP_K), per_tile, memory_space=pltpu.SMEM),
            pl.BlockSpec((tm, D_MODEL), row),
            pl.BlockSpec(memory_space=pl.ANY),
            pl.BlockSpec((1, D_MODEL), full),
            pl.BlockSpec((1, D_MODEL), full),
        ],
        out_specs=pl.BlockSpec((tm, D_MODEL), row),
        scratch_shapes=[
            pltpu.VMEM((2, max_chunks * COMBINE_CHUNK * LANE_TILES, 128), jnp.float32),
            pltpu.VMEM((tm // 8, 8 * LANE_TILES, 128), jnp.float32),
            pltpu.SemaphoreType.DMA((2,)),
        ],
    )
    return pl.pallas_call(
        _combine_ln_kernel,
        grid_spec=grid_spec,
        out_shape=jax.ShapeDtypeStruct((n, D_MODEL), jnp.float32),
        compiler_params=pltpu.CompilerParams(dimension_semantics=("arbitrary",), vmem_limit_bytes=VMEM_LIMIT),
        name="combine_ln",
    )(plan["chunk_src"], plan["n_chunks"], plan["pos"], gate.reshape(n_tiles, 1, tm * TOP_K), x1, y_flat,
      ln_g.reshape(1, -1), ln_b.reshape(1, -1))


def _post_mixer(mix, x, layer, w_out, ln1_g, ln1_b, router_w, router_b, w_gu, b_gu, w_dn, b_dn, ln2_g, ln2_b):
    x1, xp, top_e, gate, rank, tile_counts = _outproj_ln_router(mix, x, w_out, ln1_g, ln1_b, router_w, router_b)
    plan = _route_plan(top_e, rank, tile_counts.reshape(-1, N_EXPERTS))
    y_sorted = _expert_mlp(plan, xp.reshape(-1, 128), layer, w_gu, b_gu, w_dn, b_dn)
    return _combine_ln(plan, gate, x1, y_sorted.reshape(-1, 128), ln2_g, ln2_b)


def kernel(x_prompt, x_sample, state_hgrn, state_pool, state_sconv, state_gdn_conv, state_gdn, w_in_ab, hgrn_lower_bounds, hgrn_norm_w, pool_w, pool_scale, w_out_ab, w_in_cd, sconv_w, gdn_conv_w, gdn_a_log, gdn_dt_bias, gdn_norm_w, w_out_cd, ln1_g, ln1_b, ln2_g, ln2_b, router_w, router_b, w_gu, b_gu, w_dn, b_dn):
    bp, sp, _ = x_prompt.shape
    bs, ss, _ = x_sample.shape
    n_p, n_s = bp * sp, bs * ss
    bf = jnp.bfloat16
    f32 = jnp.float32
    lower_bounds = jnp.cumsum(jax.nn.softmax(hgrn_lower_bounds.astype(f32), axis=0), axis=0)
    x = jnp.concatenate([x_prompt.reshape(n_p, D_MODEL), x_sample.reshape(n_s, D_MODEL)], axis=0)
    zeros_like_prompt = lambda st: jnp.zeros((bp,) + st.shape[2:], f32)

    states = {}
    for l in range(DEPTH):
        if l % 2 == 0:
            args = (w_in_ab[0].astype(bf), lower_bounds[l], hgrn_norm_w[0], pool_w[0].astype(bf), pool_scale[0])
            mp, hp, pp = _mixer_ab(x, 0, bp, sp, 0, zeros_like_prompt(state_hgrn), zeros_like_prompt(state_pool), *args)
            ms, hs, ps = _mixer_ab(x, n_p, bs, ss, PAST_LEN, state_hgrn[0], state_pool[0], *args)
            states.update(hp=hp[None], hs=hs[None], pp=pp[None], ps=ps[None])
            w_out = w_out_ab[0]
        else:
            args = (w_in_cd[0][:, :CD_MAIN].astype(bf), w_in_cd[0][:, CD_MAIN:].astype(bf), sconv_w[0], gdn_conv_w[0],
                    gdn_a_log[0], gdn_dt_bias[0], gdn_norm_w[0])
            mp, gp, scp, gcp = _mixer_cd(x, 0, bp, sp, zeros_like_prompt(state_gdn), zeros_like_prompt(state_sconv),
                                         zeros_like_prompt(state_gdn_conv), *args)
            ms, gs, scs, gcs = _mixer_cd(x, n_p, bs, ss, state_gdn[0], state_sconv[0], state_gdn_conv[0], *args)
            states.update(scp=scp[None], scs=scs[None], gcp=gcp[None], gcs=gcs[None], gp=gp[None], gs=gs[None])
            w_out = w_out_cd[0]
        mix = jnp.concatenate([mp, ms], axis=0)
        x = _post_mixer(mix, x, l, w_out.astype(bf), ln1_g[l], ln1_b[l], router_w[l], router_b[l],
                        w_gu, b_gu, w_dn, b_dn, ln2_g[l], ln2_b[l])
    return (x[:n_p].reshape(bp, sp, D_MODEL), x[n_p:].reshape(bs, ss, D_MODEL),
            states["hp"], states["hs"], states["pp"], states["ps"], states["scp"], states["scs"],
            states["gcp"], states["gcs"], states["gp"], states["gs"])
```

```python
import functools

import jax
import jax.numpy as jnp
from jax import lax
from jax.experimental import pallas as pl
from jax.experimental.pallas import tpu as pltpu

D_MODEL = 1024
DEPTH = 2
N_EXPERTS = 32
TOP_K = 4
D_FF = 1024
SWIGLU_LIMIT = 7.0
SWIGLU_ALPHA = 1.702
LN_EPS = 1e-5
DEEPNORM_ALPHA = (2 * DEPTH) ** 0.25

HALF = D_MODEL // 2
LANE_TILES = D_MODEL // 128
TOKEN_TILE = 256
EXPERT_ROWS = 256
N_GROUPS = 1
COMBINE_CHUNK = 16
VMEM_LIMIT = 60 * 1024 * 1024
MIX_OUT = 1024
BATCH_ROW_TILE = 128
PAST_LEN = 16384


def _layer_norm_rows(v, g, b):
    mu = jnp.mean(v, axis=-1, keepdims=True)
    d = v - mu
    var = jnp.mean(d * d, axis=-1, keepdims=True)
    return d * lax.rsqrt(var + LN_EPS) * g + b


def _pack_bf16_pairs(v):
    lo = pltpu.bitcast(v[:, :HALF].astype(jnp.bfloat16).astype(jnp.float32), jnp.uint32)
    hi = pltpu.bitcast(v[:, HALF:].astype(jnp.bfloat16).astype(jnp.float32), jnp.uint32)
    return (lo >> 16) | (hi & jnp.uint32(0xFFFF0000))


def _unpack_bf16_pairs(p):
    lo = pltpu.bitcast(p << 16, jnp.float32).astype(jnp.bfloat16)
    hi = pltpu.bitcast(p & jnp.uint32(0xFFFF0000), jnp.float32).astype(jnp.bfloat16)
    return lo, hi


ROW_TILE = 256
HG_HEADS, HG_D = 4, 128
HG_WIDTH = HG_HEADS * HG_D
HG_CHUNK = 16
POOL_WINDOWS = (2, 4, 8, 16)
POOL_GC = 128
POOL_WIDTH = 512
RMS_EPS = 1e-6
_HI = lax.Precision.HIGHEST


def _silu(v):
    return v * jax.nn.sigmoid(v)


def _bf(v):
    return v.astype(jnp.bfloat16)


def _dot(a, b, precision=None):
    return jnp.dot(a, b, preferred_element_type=jnp.float32, precision=precision)


def _dot_nt(a, b, precision=None):
    return lax.dot_general(a, b, (((1,), (1,)), ((), ())), preferred_element_type=jnp.float32, precision=precision)


def _dot_tn(a, b, precision=None):
    return lax.dot_general(a, b, (((0,), (0,)), ((), ())), preferred_element_type=jnp.float32, precision=precision)


def _split3(v):
    hi = _bf(v)
    rest = v - hi.astype(jnp.float32)
    mid = _bf(rest)
    return hi, mid, _bf(rest - mid.astype(jnp.float32))


def _mask_bf16(mask):
    return _bf(mask.astype(jnp.float32))


def _exact_mask_dot(dot_fn, m, terms):
    return dot_fn(m, terms[0]) + (dot_fn(m, terms[1]) + dot_fn(m, terms[2]))


def _split2(v):
    hi = _bf(v)
    return hi, _bf(v - hi.astype(jnp.float32))


def _dot3(a2, b2):
    return _dot(a2[0], b2[0]) + (_dot(a2[0], b2[1]) + _dot(a2[1], b2[0]))


def _chunk_masks(rows, chunk):
    shift = chunk.bit_length() - 1
    t = lax.broadcasted_iota(jnp.int32, (rows, rows), 0)
    s = lax.broadcasted_iota(jnp.int32, (rows, rows), 1)
    same = (t >> shift) == (s >> shift)
    return same, jnp.logical_and(same, s <= t), jnp.logical_and(same, s < t)


def _shift_rows_chain(cur, prev, j):
    if j == 0:
        return cur
    p = prev.shape[0]
    rc = pltpu.roll(cur, j, 0)
    rp = prev if j == p else pltpu.roll(prev, j, 0)
    row = lax.broadcasted_iota(jnp.int32, (p, cur.shape[1]), 0)
    head = jnp.where(row < j, rp, rc[:p])
    return jnp.concatenate([head, rc[p:]], axis=0) if cur.shape[0] > p else head


def _shift_rows_batch(cur, prev, j):
    if j == 0:
        return cur
    if j == 8:
        return prev
    rows = cur.shape[0]
    row = lax.broadcasted_iota(jnp.int32, cur.shape, 0)
    return jnp.where((row & 7) < j, pltpu.roll(prev, rows + j - 8, 0), pltpu.roll(cur, j, 0))


def _rms_gate(o, norm_w, gate):
    ms = jnp.mean(o * o, axis=-1, keepdims=True)
    return o * lax.rsqrt(ms + RMS_EPS) * norm_w * _silu(gate)


def _mixer_ab_kernel(x_ref, w_ref, lb_ref, nw_ref, pw_ref, ps_ref, s0_ref, pa_ref, pb_ref,
                     mix_ref, s_out_ref, pa_out_ref, pb_out_ref, state_ref, prev_ref, *, chain, pos0):
    rows = x_ref.shape[0]
    chunk = HG_CHUNK if chain else 8
    n_chunks = rows // chunk
    j = pl.program_id(1) if chain else 0

    proj = _dot(_bf(x_ref[...]), w_ref[...])
    q = _silu(proj[:, 0:HG_WIDTH])
    f = lb_ref[...] + (1.0 - lb_ref[...]) * jax.nn.sigmoid(proj[:, HG_WIDTH:2 * HG_WIDTH])
    v = _silu(proj[:, 2 * HG_WIDTH:3 * HG_WIDTH])
    gate = proj[:, 3 * HG_WIDTH:4 * HG_WIDTH]
    u = proj[:, 4 * HG_WIDTH:]
    log_f = jnp.log(f)
    k = 1.0 - f

    same, incl, _ = _chunk_masks(rows, chunk)
    log_f3 = _split3(log_f)
    sums = _exact_mask_dot(_dot, jnp.concatenate([_mask_bf16(incl), _mask_bf16(same)], axis=0), log_f3)
    cum = sums[:rows]
    total = sums[rows:]
    q_dec = q * jnp.exp(cum)
    k_dec = k * jnp.exp(-cum)
    k_end = k * jnp.exp(total - cum)
    chunk_of_row = lax.broadcasted_iota(jnp.int32, (rows, 128), 0) >> (chunk.bit_length() - 1)
    onehot = _mask_bf16(chunk_of_row == lax.broadcasted_iota(jnp.int32, (rows, 128), 1))
    decay_cols = jnp.exp(_exact_mask_dot(lambda m, t: _dot_tn(t, m), onehot, log_f3))

    if chain:
        @pl.when(j == 0)
        def _():
            state_ref[...] = s0_ref[0]
            prev_ref[...] = jnp.concatenate([pa_ref[...], pb_ref[...]], axis=0)

    head_sl = [slice(h * HG_D, (h + 1) * HG_D) for h in range(HG_HEADS)]
    intra, inter, state = [], [[] for _ in head_sl], [None] * HG_HEADS
    for sl in head_sl:
        att = jnp.where(incl, _dot_nt(_bf(q_dec[:, sl]), _bf(k_dec[:, sl])), 0.0)
        intra.append(_dot(_bf(att), _bf(v[:, sl])))
    for c in range(n_chunks):
        rs = slice(c * chunk, (c + 1) * chunk)
        for h, sl in enumerate(head_sl):
            if chain:
                s = state_ref[h] if c == 0 else state[h]
            else:
                s = s0_ref[c, h]
            inter[h].append(_dot(_bf(q_dec[rs, sl]), _bf(s)))
            state[h] = decay_cols[sl, c:c + 1] * s + _dot_tn(_bf(k_end[rs, sl]), _bf(v[rs, sl]))
            if not chain:
                s_out_ref[c, h] = state[h]
    outs = []
    for h, sl in enumerate(head_sl):
        if chain:
            state_ref[h] = state[h]
        outs.append(_rms_gate(intra[h] + jnp.concatenate(inter[h], axis=0), nw_ref[...], gate[:, sl]))

    if chain:
        prev = prev_ref[...]
        t = j * rows + lax.broadcasted_iota(jnp.int32, (rows, POOL_GC), 0)
    else:
        prev_a, prev_b = pa_ref[...], pb_ref[...]
        t = lax.broadcasted_iota(jnp.int32, (rows, POOL_GC), 0) & 7
    for gi, win in enumerate(POOL_WINDOWS):
        gs = slice(gi * POOL_GC, (gi + 1) * POOL_GC)
        ug = u[:, gs]
        wsum = ug
        for d in range(1, win):
            if chain:
                wsum = wsum + _shift_rows_chain(ug, prev[:, gs], d)
            elif d <= 8:
                wsum = wsum + _shift_rows_batch(ug, prev_b[:, gs], d)
            else:
                wsum = wsum + _shift_rows_batch(prev_b[:, gs], prev_a[:, gs], d - 8)
        cnt = jnp.minimum(win, pos0 + t + 1).astype(jnp.float32)
        diff = wsum / cnt - ug
        outs.append(_dot(_bf(diff), pw_ref[gi]) * ps_ref[:, gs])
    mix_ref[...] = jnp.concatenate(outs, axis=-1).astype(mix_ref.dtype)

    if chain:
        prev_ref[...] = u[rows - 16:]

        @pl.when(j == pl.num_programs(1) - 1)
        def _():
            s_out_ref[0] = state_ref[...]
            pa_out_ref[...] = u[rows - 16:rows - 8]
            pb_out_ref[...] = u[rows - 8:]
    else:
        pa_out_ref[...] = prev_b
        pb_out_ref[...] = u


def _mixer_ab(x, row0, n_seq, length, pos0, s0, pool_state, w_in, lb, norm_w, pool_w, pool_scale):
    n_rows = n_seq * length
    chain = length >= ROW_TILE
    rows = ROW_TILE if chain else BATCH_ROW_TILE
    assert row0 % rows == 0
    first = row0 // rows
    pool16 = jnp.pad(pool_state, ((0, 0), (1, 0), (0, 0)))
    pa = pool16[:, :8].reshape(n_seq * 8, POOL_WIDTH)
    pb = pool16[:, 8:].reshape(n_seq * 8, POOL_WIDTH)
    if chain:
        assert length % rows == 0
        tiles = length // rows
        grid = (n_seq, tiles)
        row_map = lambda b, j: (b * tiles + j, 0)
        x_map = lambda b, j: (first + b * tiles + j, 0)
        seq_map4 = lambda b, j: (b, 0, 0, 0)
        seq_map2 = lambda b, j: (b, 0)
        const2 = lambda b, j: (0, 0)
        const3 = lambda b, j: (0, 0, 0)
        seq_block = 1
    else:
        assert length == 8 and n_rows % rows == 0
        seq_block = rows // 8
        grid = (n_rows // rows,)
        row_map = lambda i: (i, 0)
        x_map = lambda i: (first + i, 0)
        seq_map4 = lambda i: (i, 0, 0, 0)
        seq_map2 = lambda i: (i, 0)
        const2 = lambda i: (0, 0)
        const3 = lambda i: (0, 0, 0)
    d_in = w_in.shape[1]
    mix, s_new, pa_new, pb_new = pl.pallas_call(
        functools.partial(_mixer_ab_kernel, chain=chain, pos0=pos0),
        grid=grid,
        in_specs=[
            pl.BlockSpec((rows, D_MODEL), x_map),
            pl.BlockSpec((D_MODEL, d_in), const2),
            pl.BlockSpec((1, HG_WIDTH), const2),
            pl.BlockSpec((1, HG_D), const2),
            pl.BlockSpec((len(POOL_WINDOWS), POOL_GC, POOL_GC), const3),
            pl.BlockSpec((1, POOL_WIDTH), const2),
            pl.BlockSpec((seq_block, HG_HEADS, HG_D, HG_D), seq_map4),
            pl.BlockSpec((seq_block * 8, POOL_WIDTH), seq_map2),
            pl.BlockSpec((seq_block * 8, POOL_WIDTH), seq_map2),
        ],
        out_specs=[
            pl.BlockSpec((rows, MIX_OUT), row_map),
            pl.BlockSpec((seq_block, HG_HEADS, HG_D, HG_D), seq_map4),
            pl.BlockSpec((seq_block * 8, POOL_WIDTH), seq_map2),
            pl.BlockSpec((seq_block * 8, POOL_WIDTH), seq_map2),
        ],
        out_shape=[
            jax.ShapeDtypeStruct((n_rows, MIX_OUT), jnp.bfloat16),
            jax.ShapeDtypeStruct(s0.shape, jnp.float32),
            jax.ShapeDtypeStruct(pa.shape, jnp.float32),
            jax.ShapeDtypeStruct(pb.shape, jnp.float32),
        ],
        scratch_shapes=[
            pltpu.VMEM((HG_HEADS, HG_D, HG_D), jnp.float32),
            pltpu.VMEM((16, POOL_WIDTH), jnp.float32),
        ],
        compiler_params=pltpu.CompilerParams(
            dimension_semantics=("arbitrary",) * len(grid), vmem_limit_bytes=VMEM_LIMIT),
        name="mixer_ab_chain" if chain else "mixer_ab_batch",
    )(x, w_in, lb.reshape(1, -1), norm_w.reshape(1, -1), pool_w, pool_scale.reshape(1, -1), s0, pa, pb)
    pool_new = jnp.concatenate([pa_new.reshape(n_seq, 8, POOL_WIDTH), pb_new.reshape(n_seq, 8, POOL_WIDTH)], axis=1)
    return mix, s_new, pool_new[:, 1:]


SC_WIDTH, SC_K = 512, 3
GD_HEADS, GD_D = 4, 128
GD_CONV = 4
GD_CHUNK = 64
GD_QKV = GD_HEADS * 3 * GD_D
CD_MAIN = 3 * SC_WIDTH + GD_QKV + GD_HEADS * GD_D


def _causal_conv(cur, prev, w_ref, shift_fn):
    width = w_ref.shape[0]
    acc = cur * w_ref[width - 1:width, :]
    for j in range(width - 1):
        acc = acc + shift_fn(cur, prev, width - 1 - j) * w_ref[j:j + 1, :]
    return acc


def _softplus(v):
    return jnp.maximum(v, 0.0) + jnp.log1p(jnp.exp(-jnp.abs(v)))


def _mixer_cd_kernel(x_ref, w_ref, wg_ref, scw_ref, gcw_ref, alog_ref, dtb_ref, nw_ref, s0_ref, scp_ref, gcp_ref,
                     mix_ref, s_out_ref, scp_out_ref, gcp_out_ref, state_ref, sc_prev_ref, gc_prev_ref, *, chain):
    rows = x_ref.shape[0]
    chunk = GD_CHUNK if chain else 8
    n_chunks = rows // chunk
    j = pl.program_id(1) if chain else 0
    shift_fn = _shift_rows_chain if chain else _shift_rows_batch

    xb = _bf(x_ref[...])
    proj = _dot(xb, w_ref[...])
    gates = _dot(xb, wg_ref[...])
    b_gate = proj[:, 0:SC_WIDTH]
    conv_in = proj[:, SC_WIDTH:2 * SC_WIDTH] * proj[:, 2 * SC_WIDTH:3 * SC_WIDTH]
    qkv = proj[:, 3 * SC_WIDTH:3 * SC_WIDTH + GD_QKV]
    z = proj[:, 3 * SC_WIDTH + GD_QKV:]

    if chain:
        @pl.when(j == 0)
        def _():
            state_ref[...] = s0_ref[0]
            sc_prev_ref[...] = scp_ref[...]
            gc_prev_ref[...] = gcp_ref[...]
        sc_prev, gc_prev = sc_prev_ref[...], gc_prev_ref[...]
    else:
        sc_prev, gc_prev = scp_ref[...], gcp_ref[...]

    outs = [b_gate * _causal_conv(conv_in, sc_prev, scw_ref, shift_fn)]
    qkv_c = _silu(_causal_conv(qkv, gc_prev, gcw_ref, shift_fn))

    lane8 = lax.broadcasted_iota(jnp.int32, gates.shape, 1)
    g_dec = -jnp.exp(alog_ref[...]) * _softplus(gates + dtb_ref[...])
    cols = jnp.where(lane8 < GD_HEADS, g_dec, jax.nn.sigmoid(gates))
    same, incl, strict = _chunk_masks(rows, chunk)
    cols3 = _split3(cols)
    gcum = _exact_mask_dot(_dot, _mask_bf16(incl), cols3)
    gtot = _exact_mask_dot(_dot, _mask_bf16(same), cols3)
    eye8 = _mask_bf16(lax.broadcasted_iota(jnp.int32, (8, 8), 0) == lax.broadcasted_iota(jnp.int32, (8, 8), 1))
    gcum_rows = _exact_mask_dot(_dot_nt, eye8, _split3(gcum))

    eye = (lax.broadcasted_iota(jnp.int32, (rows, rows), 0)
           == lax.broadcasted_iota(jnp.int32, (rows, rows), 1)).astype(jnp.float32)
    hd = []
    for h in range(GD_HEADS):
        sl = lambda part: slice(part * GD_HEADS * GD_D + h * GD_D, part * GD_HEADS * GD_D + (h + 1) * GD_D)
        q, k, v = qkv_c[:, sl(0)], qkv_c[:, sl(1)], qkv_c[:, sl(2)]
        q = q * lax.rsqrt(jnp.sum(q * q, axis=-1, keepdims=True) + 1e-6) * (GD_D ** -0.5)
        k = k * lax.rsqrt(jnp.sum(k * k, axis=-1, keepdims=True) + 1e-6)
        beta = cols[:, GD_HEADS + h:GD_HEADS + h + 1]
        gc = gcum[:, h:h + 1]
        gt = gtot[:, h:h + 1]
        decay = jnp.where(incl, jnp.exp(jnp.where(incl, gc - gcum_rows[h:h + 1, :], 0.0)), 0.0)
        k_beta = k * beta
        kb = _bf(k)
        egc = jnp.exp(gc)
        power = -jnp.where(strict, _dot_nt(_bf(k_beta), kb) * decay, 0.0)
        hd.append(dict(
            inv=eye + power, power=_bf(power),
            att=_bf(jnp.where(incl, _dot_nt(_bf(q), kb) * decay, 0.0)),
            rhs=_split2(jnp.concatenate([v * beta, k_beta * egc], axis=1)),
            q_dec=q * egc, k_end=k * jnp.exp(gt - gc),
            decay_end=jnp.exp(gt), inter=[], v_new=[], s=None))

    for _ in range(chunk.bit_length() - 2):
        for d in hd:
            d["power"] = _bf(_dot(d["power"], d["power"]))
        for d in hd:
            d["inv"] = d["inv"] + _dot(_bf(d["inv"]), d["power"])
    for d in hd:
        uw = _dot3(_split2(d["inv"]), d["rhs"])
        d["u"], d["w"] = uw[:, :GD_D], uw[:, GD_D:]

    for c in range(n_chunks):
        rs = slice(c * chunk, (c + 1) * chunk)
        for h, d in enumerate(hd):
            if chain:
                s = state_ref[h] if c == 0 else d["s"]
            else:
                s = s0_ref[c, h]
            sb = _bf(s)
            v_new = d["u"][rs] - _dot(_bf(d["w"][rs]), sb)
            d["inter"].append(_dot(_bf(d["q_dec"][rs]), sb))
            d["s"] = d["decay_end"][c * chunk:c * chunk + 1, :] * s + _dot_tn(_bf(d["k_end"][rs]), _bf(v_new))
            d["v_new"].append(v_new)
            if not chain:
                s_out_ref[c, h] = d["s"]
    for h, d in enumerate(hd):
        if chain:
            state_ref[h] = d["s"]
        o = jnp.concatenate(d["inter"], axis=0) + _dot(d["att"], _bf(jnp.concatenate(d["v_new"], axis=0)))
        outs.append(_rms_gate(o, nw_ref[...], z[:, h * GD_D:(h + 1) * GD_D]))
    mix_ref[...] = jnp.concatenate(outs, axis=-1).astype(mix_ref.dtype)

    if chain:
        sc_prev_ref[...] = conv_in[rows - 8:]
        gc_prev_ref[...] = qkv[rows - 8:]

        @pl.when(j == pl.num_programs(1) - 1)
        def _():
            s_out_ref[0] = state_ref[...]
            scp_out_ref[...] = conv_in[rows - 8:]
            gcp_out_ref[...] = qkv[rows - 8:]
    else:
        scp_out_ref[...] = conv_in
        gcp_out_ref[...] = qkv


def _mixer_cd(x, row0, n_seq, length, s0, sconv_state, gconv_state, w_main, w_gates, sconv_w, gconv_w, a_log, dt_bias,
              norm_w):
    n_rows = n_seq * length
    chain = length >= ROW_TILE
    rows = ROW_TILE if chain else BATCH_ROW_TILE
    assert row0 % rows == 0
    first = row0 // rows
    pad8 = lambda st: jnp.pad(st, ((0, 0), (8 - st.shape[1], 0), (0, 0))).reshape(n_seq * 8, st.shape[2])
    scp, gcp = pad8(sconv_state), pad8(gconv_state)
    if chain:
        assert length % rows == 0
        tiles = length // rows
        grid = (n_seq, tiles)
        row_map = lambda b, j: (b * tiles + j, 0)
        x_map = lambda b, j: (first + b * tiles + j, 0)
        seq_map4 = lambda b, j: (b, 0, 0, 0)
        seq_map2 = lambda b, j: (b, 0)
        const2 = lambda b, j: (0, 0)
        seq_block = 1
    else:
        assert length == 8 and n_rows % rows == 0
        seq_block = rows // 8
        grid = (n_rows // rows,)
        row_map = lambda i: (i, 0)
        x_map = lambda i: (first + i, 0)
        seq_map4 = lambda i: (i, 0, 0, 0)
        seq_map2 = lambda i: (i, 0)
        const2 = lambda i: (0, 0)
    zeros4 = jnp.zeros((GD_HEADS,), jnp.float32)
    alog8 = jnp.concatenate([a_log.astype(jnp.float32), zeros4]).reshape(1, 8)
    dtb8 = jnp.concatenate([dt_bias.astype(jnp.float32), zeros4]).reshape(1, 8)
    mix, s_new, scp_new, gcp_new = pl.pallas_call(
        functools.partial(_mixer_cd_kernel, chain=chain),
        grid=grid,
        in_specs=[
            pl.BlockSpec((rows, D_MODEL), x_map),
            pl.BlockSpec((D_MODEL, CD_MAIN), const2),
            pl.BlockSpec((D_MODEL, 8), const2),
            pl.BlockSpec((SC_K, SC_WIDTH), const2),
            pl.BlockSpec((GD_CONV, GD_QKV), const2),
            pl.BlockSpec((1, 8), const2),
            pl.BlockSpec((1, 8), const2),
            pl.BlockSpec((1, GD_D), const2),
            pl.BlockSpec((seq_block, GD_HEADS, GD_D, GD_D), seq_map4),
            pl.BlockSpec((seq_block * 8, SC_WIDTH), seq_map2),
            pl.BlockSpec((seq_block * 8, GD_QKV), seq_map2),
        ],
        out_specs=[
            pl.BlockSpec((rows, MIX_OUT), row_map),
            pl.BlockSpec((seq_block, GD_HEADS, GD_D, GD_D), seq_map4),
            pl.BlockSpec((seq_block * 8, SC_WIDTH), seq_map2),
            pl.BlockSpec((seq_block * 8, GD_QKV), seq_map2),
        ],
        out_shape=[
            jax.ShapeDtypeStruct((n_rows, MIX_OUT), jnp.bfloat16),
            jax.ShapeDtypeStruct(s0.shape, jnp.float32),
            jax.ShapeDtypeStruct(scp.shape, jnp.float32),
            jax.ShapeDtypeStruct(gcp.shape, jnp.float32),
        ],
        scratch_shapes=[
            pltpu.VMEM((GD_HEADS, GD_D, GD_D), jnp.float32),
            pltpu.VMEM((8, SC_WIDTH), jnp.float32),
            pltpu.VMEM((8, GD_QKV), jnp.float32),
        ],
        compiler_params=pltpu.CompilerParams(
            dimension_semantics=("arbitrary",) * len(grid), vmem_limit_bytes=VMEM_LIMIT),
        name="mixer_cd_chain" if chain else "mixer_cd_batch",
    )(x, w_main, w_gates, sconv_w, gconv_w, alog8, dtb8, norm_w.reshape(1, -1), s0, scp, gcp)
    tail = lambda st, keep: st.reshape(n_seq, 8, -1)[:, 8 - keep:]
    return mix, s_new, tail(scp_new, SC_K - 1), tail(gcp_new, GD_CONV - 1)


def _outproj_ln_router_kernel(mix_ref, x_ref, w_ref, g_ref, b_ref, rwt_ref, rb_ref,
                              x1_ref, xp_ref, tope_ref, gate_ref, rank_ref, cnt_ref):
    mix = jnp.dot(mix_ref[...], w_ref[...], preferred_element_type=jnp.float32)
    x1 = _layer_norm_rows(DEEPNORM_ALPHA * x_ref[...] + mix, g_ref[...], b_ref[...])
    x1_ref[...] = x1
    packed = _pack_bf16_pairs(x1)
    for j in range(HALF // 128):
        xp_ref[:, j * 8:(j + 1) * 8, :] = packed[:, j * 128:(j + 1) * 128].reshape(x1.shape[0] // 8, 8, 128)
    logits = _dot_nt(_bf(rwt_ref[...]), _bf(x1)) + rb_ref[...]
    tm = logits.shape[1]
    expert = lax.broadcasted_iota(jnp.int32, logits.shape, 0)
    vals, idxs = [], []
    for _ in range(TOP_K):
        m = jnp.max(logits, axis=0, keepdims=True)
        idx = jnp.min(jnp.where(logits == m, expert, N_EXPERTS), axis=0, keepdims=True)
        vals.append(m)
        idxs.append(idx)
        logits = jnp.where(expert == idx, -jnp.inf, logits)
    ex = [jnp.exp(v - vals[0]) for v in vals]
    den = ex[0] + ex[1] + ex[2] + ex[3]
    hits = [expert == idx for idx in idxs]
    member = sum(h.astype(jnp.float32) for h in hits)
    earlier = (lax.broadcasted_iota(jnp.int32, (tm, tm), 0) < lax.broadcasted_iota(jnp.int32, (tm, tm), 1))
    before = _dot(member.astype(jnp.bfloat16), earlier.astype(jnp.bfloat16))
    cnt_ref[0] = jnp.sum(member, axis=1, keepdims=True).astype(jnp.int32)
    slot = lax.broadcasted_iota(jnp.int32, tope_ref.shape[1:], 0)
    tope = jnp.zeros(slot.shape, jnp.int32)
    gate = jnp.zeros(slot.shape, jnp.float32)
    rank = jnp.zeros(slot.shape, jnp.int32)
    for k in range(TOP_K):
        tope = jnp.where(slot == k, idxs[k], tope)
        gate = jnp.where(slot == k, ex[k] / den, gate)
        rank_k = jnp.sum(jnp.where(hits[k], before, 0.0), axis=0, keepdims=True).astype(jnp.int32)
        rank = jnp.where(slot == k, rank_k, rank)
    tope_ref[0] = tope
    gate_ref[0] = gate
    rank_ref[0] = rank


def _outproj_ln_router(mix, x, w_out, ln_g, ln_b, router_w, router_b):
    n = x.shape[0]
    tm = TOKEN_TILE
    row = lambda i: (i, 0)
    full = lambda i: (0, 0)
    return pl.pallas_call(
        _outproj_ln_router_kernel,
        grid=(n // tm,),
        in_specs=[
            pl.BlockSpec((tm, D_MODEL), row),
            pl.BlockSpec((tm, D_MODEL), row),
            pl.BlockSpec((D_MODEL, D_MODEL), full),
            pl.BlockSpec((1, D_MODEL), full),
            pl.BlockSpec((1, D_MODEL), full),
            pl.BlockSpec((N_EXPERTS, D_MODEL), full),
            pl.BlockSpec((N_EXPERTS, 1), full),
        ],
        out_specs=[
            pl.BlockSpec((tm, D_MODEL), row),
            pl.BlockSpec((tm // 8, HALF // 16, 128), lambda i: (i, 0, 0)),
            pl.BlockSpec((1, TOP_K, tm), lambda i: (i, 0, 0)),
            pl.BlockSpec((1, TOP_K, tm), lambda i: (i, 0, 0)),
            pl.BlockSpec((1, TOP_K, tm), lambda i: (i, 0, 0)),
            pl.BlockSpec((1, N_EXPERTS, 1), lambda i: (i, 0, 0)),
        ],
        out_shape=[
            jax.ShapeDtypeStruct((n, D_MODEL), jnp.float32),
            jax.ShapeDtypeStruct((n // 8, HALF // 16, 128), jnp.uint32),
            jax.ShapeDtypeStruct((n // tm, TOP_K, tm), jnp.int32),
            jax.ShapeDtypeStruct((n // tm, TOP_K, tm), jnp.float32),
            jax.ShapeDtypeStruct((n // tm, TOP_K, tm), jnp.int32),
            jax.ShapeDtypeStruct((n // tm, N_EXPERTS, 1), jnp.int32),
        ],
        compiler_params=pltpu.CompilerParams(dimension_semantics=("arbitrary",), vmem_limit_bytes=VMEM_LIMIT),
        name="outproj_ln_router",
    )(mix, x, w_out, ln_g.reshape(1, -1), ln_b.reshape(1, -1), router_w.T, router_b.reshape(-1, 1))


def _route_plan(top_e, rank, tile_counts):
    n = top_e.shape[0] * top_e.shape[2]
    a = n * TOP_K
    br, tm, ch = EXPERT_ROWS, TOKEN_TILE, COMBINE_CHUNK
    n_virtual = N_EXPERTS * N_GROUPS
    group_tokens = n // N_GROUPS
    n_blocks = a // br + n_virtual + 1
    n_tiles = n // tm
    group_tiles = n_tiles // N_GROUPS
    max_chunks = (tm * TOP_K) // ch + N_EXPERTS + 1
    i32 = jnp.int32

    cnt = tile_counts.reshape(N_GROUPS, group_tiles, N_EXPERTS)
    nblk = (cnt.sum(axis=1) + br - 1) // br
    blk_end = jnp.cumsum(nblk.reshape(-1))
    row_off = ((blk_end - nblk.reshape(-1)) * br).reshape(N_GROUPS, 1, N_EXPERTS)
    seg_start = (row_off + jnp.cumsum(cnt, axis=1) - cnt).reshape(n_tiles, N_EXPERTS)
    seg_len = tile_counts
    blocks = jnp.arange(n_blocks, dtype=i32)
    block_ve = jnp.minimum(jnp.sum(blk_end[None, :] <= blocks[:, None], axis=1), n_virtual - 1).astype(i32)
    valid = blocks < blk_end[-1]
    block_e = block_ve % N_EXPERTS
    block_e = jnp.where(valid, block_e, block_e[jnp.maximum(blk_end[-1] - 1, 0)])
    starts = jnp.concatenate([jnp.ones((1,), bool), block_e[1:] != block_e[:-1]])
    later_start = lax.cummin(jnp.where(starts, blocks, n_blocks)[::-1], axis=0)[::-1]
    next_start = jnp.concatenate([later_start[1:], jnp.full((1,), n_blocks, i32)])
    next_e = jnp.where(next_start < n_blocks, block_e[jnp.minimum(next_start, n_blocks - 1)], -1)
    block_info = jnp.stack([block_e, block_ve // N_EXPERTS, valid.astype(i32), starts.astype(i32), next_e.astype(i32)])

    aligned = seg_start // 8 * 8
    lead = seg_start - aligned
    nch = jnp.where(seg_len > 0, (lead + seg_len + ch - 1) // ch, 0)
    ch_end = jnp.cumsum(nch, axis=1)
    ch_first = ch_end - nch
    n_chunks = ch_end[:, -1]
    cidx = jnp.arange(max_chunks, dtype=i32)[None, :, None]
    owns = jnp.logical_and(ch_first[:, None, :] <= cidx, cidx < ch_end[:, None, :])
    chunk_src = jnp.sum(jnp.where(owns, aligned[:, None, :] + (cidx - ch_first[:, None, :]) * ch, 0), axis=2)

    hit = top_e[..., None] == jnp.arange(N_EXPERTS, dtype=i32)
    pick = lambda table: jnp.sum(jnp.where(hit, table[:, None, None, :], 0), axis=-1)
    dest = pick(seg_start) + rank
    pos = pick(ch_first * ch + lead) + rank
    tok = jnp.arange(n, dtype=i32) % group_tokens
    addr = ((tok // 8) * (8 * (HALF // 128)) + tok % 8).reshape(n_tiles, 1, tm)
    row_addr = jnp.zeros((n_blocks * br,), i32).at[dest.reshape(-1)].set(
        jnp.broadcast_to(addr, dest.shape).reshape(-1), unique_indices=True, mode="promise_in_bounds")
    pos = (pos // 8) * (8 * LANE_TILES) + pos % 8
    return dict(row_addr=row_addr.reshape(n_blocks, 1, br), block_info=block_info,
                chunk_src=(chunk_src * LANE_TILES).astype(i32), n_chunks=n_chunks.astype(i32),
                pos=pos.reshape(n_tiles, 1, tm * TOP_K).astype(i32))


def _expert_kernel(info_ref, addr_ref, addr_next_ref, xq_hbm, wgu_hbm, bgu_ref, wdn_hbm, bdn_ref, y_ref,
                   xq_vmem, stage_ref, wgu_f32_ref, wdn_f32_ref, wgu_ref, wdn_ref, load_sem, w_sems, *, layer):
    b = pl.program_id(0)
    grp = info_ref[1, b]
    prev_grp = info_ref[1, jnp.maximum(b - 1, 0)]

    def weight_copies(expert):
        return (pltpu.make_async_copy(wgu_hbm.at[layer, expert], wgu_f32_ref, w_sems.at[0]),
                pltpu.make_async_copy(wdn_hbm.at[layer, expert], wdn_f32_ref, w_sems.at[1]))

    @pl.when(b == 0)
    def _():
        for cp in weight_copies(info_ref[0, 0]):
            cp.start()

    @pl.when(info_ref[3, b] == 1)
    def _():
        for cp in weight_copies(info_ref[0, b]):
            cp.wait()
        rows = D_MODEL // 8

        def cast(i, carry):
            r0 = pl.multiple_of(i * rows, rows)
            wgu_ref[pl.ds(r0, rows), :] = wgu_f32_ref[pl.ds(r0, rows), :].astype(jnp.bfloat16)
            wdn_ref[pl.ds(r0, rows), :] = wdn_f32_ref[pl.ds(r0, rows), :].astype(jnp.bfloat16)
            return carry

        lax.fori_loop(0, 8, cast, 0)

        @pl.when(info_ref[4, b] >= 0)
        def _():
            for cp in weight_copies(info_ref[4, b]):
                cp.start()

    group_rows = xq_vmem.shape[0]
    slot = b % 2
    tiles = HALF // 128

    def gather_row(addr, dst_slot, group8, sub):
        stage_ref[dst_slot, group8, pl.ds(sub, tiles, stride=8), :] = xq_vmem[pl.ds(addr, tiles, stride=8), :]

    @pl.when(jnp.logical_or(b == 0, grp != prev_grp))
    def _():
        cp = pltpu.make_async_copy(xq_hbm.at[pl.ds(grp * group_rows, group_rows)], xq_vmem, load_sem)
        cp.start()
        cp.wait()

        def gather(i, carry):
            for u in range(8):
                gather_row(addr_ref[0, 0, i * 8 + u], slot, i, u)
            return carry

        lax.fori_loop(0, EXPERT_ROWS // 8, gather, 0)

    @pl.when(info_ref[2, b] == 1)
    def _():
        for r in range(EXPERT_ROWS):
            gather_row(addr_next_ref[0, 0, r], 1 - slot, r // 8, r % 8)
        pieces = [_unpack_bf16_pairs(stage_ref[slot, :, j * 8:(j + 1) * 8, :].reshape(EXPERT_ROWS, 128))
                  for j in range(tiles)]
        lo = jnp.concatenate([p[0] for p in pieces], axis=1)
        hi = jnp.concatenate([p[1] for p in pieces], axis=1)
        hgu = (jnp.dot(lo, wgu_ref[:HALF, :], preferred_element_type=jnp.float32)
               + jnp.dot(hi, wgu_ref[HALF:, :], preferred_element_type=jnp.float32) + bgu_ref[0, 0])
        glu = jnp.minimum(hgu[:, :D_FF], SWIGLU_LIMIT)
        lin = jnp.clip(hgu[:, D_FF:], -SWIGLU_LIMIT, SWIGLU_LIMIT)
        act = (lin + 1.0) * glu * jax.nn.sigmoid(SWIGLU_ALPHA * glu)
        y = jnp.dot(act.astype(jnp.bfloat16), wdn_ref[...], preferred_element_type=jnp.float32) + bdn_ref[0, 0]
        for c in range(LANE_TILES):
            y_ref[:, c * 8:(c + 1) * 8, :] = y[:, c * 128:(c + 1) * 128].reshape(EXPERT_ROWS // 8, 8, 128)

    @pl.when(info_ref[2, b] == 0)
    def _():
        y_ref[...] = jnp.zeros_like(y_ref)


def _expert_mlp(plan, xq, layer, w_gu, b_gu, w_dn, b_dn):
    br = EXPERT_ROWS
    n_blocks = plan["row_addr"].shape[0]
    assert br >= COMBINE_CHUNK + 8
    pad_rows = n_blocks * br
    expert = lambda b, info: (layer, info[0, b], 0, 0)
    grid_spec = pltpu.PrefetchScalarGridSpec(
        num_scalar_prefetch=1,
        grid=(n_blocks,),
        in_specs=[
            pl.BlockSpec((1, 1, br), lambda b, info: (b, 0, 0), memory_space=pltpu.SMEM),
            pl.BlockSpec((1, 1, br), lambda b, info: (jnp.minimum(b + 1, n_blocks - 1), 0, 0), memory_space=pltpu.SMEM),
            pl.BlockSpec(memory_space=pl.ANY),
            pl.BlockSpec(memory_space=pl.ANY),
            pl.BlockSpec((1, 1, 1, 2 * D_FF), expert),
            pl.BlockSpec(memory_space=pl.ANY),
            pl.BlockSpec((1, 1, 1, D_MODEL), expert),
        ],
        out_specs=pl.BlockSpec((br // 8, 8 * LANE_TILES, 128), lambda b, info: (b, 0, 0)),
        scratch_shapes=[
            pltpu.VMEM((xq.shape[0] // N_GROUPS, 128), jnp.uint32),
            pltpu.VMEM((2, br // 8, 8 * (HALF // 128), 128), jnp.uint32),
            pltpu.VMEM((D_MODEL, 2 * D_FF), jnp.float32),
            pltpu.VMEM((D_FF, D_MODEL), jnp.float32),
            pltpu.VMEM((D_MODEL, 2 * D_FF), jnp.bfloat16),
            pltpu.VMEM((D_FF, D_MODEL), jnp.bfloat16),
            pltpu.SemaphoreType.DMA(()),
            pltpu.SemaphoreType.DMA((2,)),
        ],
    )
    return pl.pallas_call(
        functools.partial(_expert_kernel, layer=layer),
        grid_spec=grid_spec,
        out_shape=jax.ShapeDtypeStruct((pad_rows // 8, 8 * LANE_TILES, 128), jnp.float32),
        compiler_params=pltpu.CompilerParams(dimension_semantics=("arbitrary",), vmem_limit_bytes=VMEM_LIMIT),
        name="expert_mlp",
    )(plan["block_info"], plan["row_addr"], plan["row_addr"], xq, w_gu, b_gu.reshape(DEPTH, N_EXPERTS, 1, -1), w_dn,
      b_dn.reshape(DEPTH, N_EXPERTS, 1, -1))


def _combine_ln_kernel(src_ref, nch_ref, pos_ref, gate_ref, x1_ref, y_hbm, g_ref, b_ref, out_ref,
                       stage_ref, ffn_ref, sems):
    i = pl.program_id(0)
    n_tiles = pl.num_programs(0)
    chunk_rows = COMBINE_CHUNK * LANE_TILES

    def chunk_copy(tile, slot, c):
        src = pl.multiple_of(src_ref[tile, c], 8 * LANE_TILES)
        return pltpu.make_async_copy(y_hbm.at[pl.ds(src, chunk_rows)],
                                     stage_ref.at[slot, pl.ds(c * chunk_rows, chunk_rows)], sems.at[slot])

    def issue(tile, slot):
        lax.fori_loop(0, nch_ref[tile], lambda c, carry: (chunk_copy(tile, slot, c).start(), carry)[1], 0)

    @pl.when(i == 0)
    def _():
        issue(0, 0)

    @pl.when(i + 1 < n_tiles)
    def _():
        issue(i + 1, (i + 1) % 2)

    slot = i % 2
    lax.fori_loop(0, nch_ref[i], lambda c, carry: (chunk_copy(i, slot, c).wait(), carry)[1], 0)

    tm = x1_ref.shape[0]

    def combine(j, carry):
        for u in range(8):
            t = j * 8 + u
            acc = None
            for k in range(TOP_K):
                row = stage_ref[slot, pl.ds(pos_ref[0, 0, k * tm + t], LANE_TILES, stride=8), :]
                row = row * gate_ref[0, 0, k * tm + t]
                acc = row if acc is None else acc + row
            ffn_ref[j, pl.ds(u, LANE_TILES, stride=8), :] = acc
        return carry

    lax.fori_loop(0, tm // 8, combine, 0)
    ffn = jnp.concatenate([ffn_ref[:, c * 8:(c + 1) * 8, :].reshape(tm, 128) for c in range(LANE_TILES)], axis=1)
    out_ref[...] = _layer_norm_rows(DEEPNORM_ALPHA * x1_ref[...] + ffn, g_ref[...], b_ref[...])


def _combine_ln(plan, gate, x1, y_flat, ln_g, ln_b):
    n = x1.shape[0]
    tm = TOKEN_TILE
    n_tiles = n // tm
    max_chunks = plan["chunk_src"].shape[1]
    row = lambda i, src, nch: (i, 0)
    full = lambda i, src, nch: (0, 0)
    per_tile = lambda i, src, nch: (i, 0, 0)
    grid_spec = pltpu.PrefetchScalarGridSpec(
        num_scalar_prefetch=2,
        grid=(n_tiles,),
        in_specs=[
            pl.BlockSpec((1, 1, tm * TOP_K), per_tile, memory_space=pltpu.SMEM),
            pl.BlockSpec((1, 1, tm * TOP_K), per_tile, memory_space=pltpu.SMEM),
            pl.BlockSpec((tm, D_MODEL), row),
            pl.BlockSpec(memory_space=pl.ANY),
            pl.BlockSpec((1, D_MODEL), full),
            pl.BlockSpec((1, D_MODEL), full),
        ],
        out_specs=pl.BlockSpec((tm, D_MODEL), row),
        scratch_shapes=[
            pltpu.VMEM((2, max_chunks * COMBINE_CHUNK * LANE_TILES, 128), jnp.float32),
            pltpu.VMEM((tm // 8, 8 * LANE_TILES, 128), jnp.float32),
            pltpu.SemaphoreType.DMA((2,)),
        ],
    )
    return pl.pallas_call(
        _combine_ln_kernel,
        grid_spec=grid_spec,
        out_shape=jax.ShapeDtypeStruct((n, D_MODEL), jnp.float32),
        compiler_params=pltpu.CompilerParams(dimension_semantics=("arbitrary",), vmem_limit_bytes=VMEM_LIMIT),
        name="combine_ln",
    )(plan["chunk_src"], plan["n_chunks"], plan["pos"], gate.reshape(n_tiles, 1, tm * TOP_K), x1, y_flat,
      ln_g.reshape(1, -1), ln_b.reshape(1, -1))


def _post_mixer(mix, x, layer, w_out, ln1_g, ln1_b, router_w, router_b, w_gu, b_gu, w_dn, b_dn, ln2_g, ln2_b):
    x1, xp, top_e, gate, rank, tile_counts = _outproj_ln_router(mix, x, w_out, ln1_g, ln1_b, router_w, router_b)
    plan = _route_plan(top_e, rank, tile_counts.reshape(-1, N_EXPERTS))
    y_sorted = _expert_mlp(plan, xp.reshape(-1, 128), layer, w_gu, b_gu, w_dn, b_dn)
    return _combine_ln(plan, gate, x1, y_sorted.reshape(-1, 128), ln2_g, ln2_b)


def kernel(x_prompt, x_sample, state_hgrn, state_pool, state_sconv, state_gdn_conv, state_gdn, w_in_ab, hgrn_lower_bounds, hgrn_norm_w, pool_w, pool_scale, w_out_ab, w_in_cd, sconv_w, gdn_conv_w, gdn_a_log, gdn_dt_bias, gdn_norm_w, w_out_cd, ln1_g, ln1_b, ln2_g, ln2_b, router_w, router_b, w_gu, b_gu, w_dn, b_dn):
    bp, sp, _ = x_prompt.shape
    bs, ss, _ = x_sample.shape
    n_p, n_s = bp * sp, bs * ss
    bf = jnp.bfloat16
    f32 = jnp.float32
    lower_bounds = jnp.cumsum(jax.nn.softmax(hgrn_lower_bounds.astype(f32), axis=0), axis=0)
    x = jnp.concatenate([x_prompt.reshape(n_p, D_MODEL), x_sample.reshape(n_s, D_MODEL)], axis=0)
    zeros_like_prompt = lambda st: jnp.zeros((bp,) + st.shape[2:], f32)

    states = {}
    for l in range(DEPTH):
        if l % 2 == 0:
            args = (w_in_ab[0].astype(bf), lower_bounds[l], hgrn_norm_w[0], pool_w[0].astype(bf), pool_scale[0])
            mp, hp, pp = _mixer_ab(x, 0, bp, sp, 0, zeros_like_prompt(state_hgrn), zeros_like_prompt(state_pool), *args)
            ms, hs, ps = _mixer_ab(x, n_p, bs, ss, PAST_LEN, state_hgrn[0], state_pool[0], *args)
            states.update(hp=hp[None], hs=hs[None], pp=pp[None], ps=ps[None])
            w_out = w_out_ab[0]
        else:
            args = (w_in_cd[0][:, :CD_MAIN].astype(bf), w_in_cd[0][:, CD_MAIN:].astype(bf), sconv_w[0], gdn_conv_w[0],
                    gdn_a_log[0], gdn_dt_bias[0], gdn_norm_w[0])
            mp, gp, scp, gcp = _mixer_cd(x, 0, bp, sp, zeros_like_prompt(state_gdn), zeros_like_prompt(state_sconv),
                                         zeros_like_prompt(state_gdn_conv), *args)
            ms, gs, scs, gcs = _mixer_cd(x, n_p, bs, ss, state_gdn[0], state_sconv[0], state_gdn_conv[0], *args)
            states.update(scp=scp[None], scs=scs[None], gcp=gcp[None], gcs=gcs[None], gp=gp[None], gs=gs[None])
            w_out = w_out_cd[0]
        mix = jnp.concatenate([mp, ms], axis=0)
        x = _post_mixer(mix, x, l, w_out.astype(bf), ln1_g[l], ln1_b[l], router_w[l], router_b[l],
                        w_gu, b_gu, w_dn, b_dn, ln2_g[l], ln2_b[l])
    return (x[:n_p].reshape(bp, sp, D_MODEL), x[n_p:].reshape(bs, ss, D_MODEL),
            states["hp"], states["hs"], states["pp"], states["ps"], states["scp"], states["scs"],
            states["gcp"], states["gcs"], states["gp"], states["gs"])
```

```python
import functools

import jax
import jax.numpy as jnp
from jax import lax
from jax.experimental import pallas as pl
from jax.experimental.pallas import tpu as pltpu

D_MODEL = 1024
DEPTH = 2
N_EXPERTS = 32
TOP_K = 4
D_FF = 1024
SWIGLU_LIMIT = 7.0
SWIGLU_ALPHA = 1.702
LN_EPS = 1e-5
DEEPNORM_ALPHA = (2 * DEPTH) ** 0.25

HALF = D_MODEL // 2
LANE_TILES = D_MODEL // 128
TOKEN_TILE = 256
EXPERT_ROWS = 256
N_GROUPS = 1
COMBINE_CHUNK = 16
VMEM_LIMIT = 60 * 1024 * 1024
MIX_OUT = 1024
BATCH_ROW_TILE = 128
PAST_LEN = 16384


def _layer_norm_rows(v, g, b):
    mu = jnp.mean(v, axis=-1, keepdims=True)
    d = v - mu
    var = jnp.mean(d * d, axis=-1, keepdims=True)
    return d * lax.rsqrt(var + LN_EPS) * g + b


def _pack_bf16_pairs(v):
    lo = pltpu.bitcast(v[:, :HALF].astype(jnp.bfloat16).astype(jnp.float32), jnp.uint32)
    hi = pltpu.bitcast(v[:, HALF:].astype(jnp.bfloat16).astype(jnp.float32), jnp.uint32)
    return (lo >> 16) | (hi & jnp.uint32(0xFFFF0000))


def _unpack_bf16_pairs(p):
    lo = pltpu.bitcast(p << 16, jnp.float32).astype(jnp.bfloat16)
    hi = pltpu.bitcast(p & jnp.uint32(0xFFFF0000), jnp.float32).astype(jnp.bfloat16)
    return lo, hi


ROW_TILE = 256
HG_HEADS, HG_D = 4, 128
HG_WIDTH = HG_HEADS * HG_D
HG_CHUNK = 16
POOL_WINDOWS = (2, 4, 8, 16)
POOL_GC = 128
POOL_WIDTH = 512
RMS_EPS = 1e-6
_HI = lax.Precision.HIGHEST


def _silu(v):
    return v * jax.nn.sigmoid(v)


def _bf(v):
    return v.astype(jnp.bfloat16)


def _dot(a, b, precision=None):
    return jnp.dot(a, b, preferred_element_type=jnp.float32, precision=precision)


def _dot_nt(a, b, precision=None):
    return lax.dot_general(a, b, (((1,), (1,)), ((), ())), preferred_element_type=jnp.float32, precision=precision)


def _dot_tn(a, b, precision=None):
    return lax.dot_general(a, b, (((0,), (0,)), ((), ())), preferred_element_type=jnp.float32, precision=precision)


def _split3(v):
    hi = _bf(v)
    rest = v - hi.astype(jnp.float32)
    mid = _bf(rest)
    return hi, mid, _bf(rest - mid.astype(jnp.float32))


def _mask_bf16(mask):
    return _bf(mask.astype(jnp.float32))


def _exact_mask_dot(dot_fn, m, terms):
    return dot_fn(m, terms[0]) + (dot_fn(m, terms[1]) + dot_fn(m, terms[2]))


def _split2(v):
    hi = _bf(v)
    return hi, _bf(v - hi.astype(jnp.float32))


def _dot3(a2, b2):
    return _dot(a2[0], b2[0]) + (_dot(a2[0], b2[1]) + _dot(a2[1], b2[0]))


def _chunk_masks(rows, chunk):
    shift = chunk.bit_length() - 1
    t = lax.broadcasted_iota(jnp.int32, (rows, rows), 0)
    s = lax.broadcasted_iota(jnp.int32, (rows, rows), 1)
    same = (t >> shift) == (s >> shift)
    return same, jnp.logical_and(same, s <= t), jnp.logical_and(same, s < t)


def _shift_rows_chain(cur, prev, j):
    if j == 0:
        return cur
    p = prev.shape[0]
    rc = pltpu.roll(cur, j, 0)
    rp = prev if j == p else pltpu.roll(prev, j, 0)
    row = lax.broadcasted_iota(jnp.int32, (p, cur.shape[1]), 0)
    head = jnp.where(row < j, rp, rc[:p])
    return jnp.concatenate([head, rc[p:]], axis=0) if cur.shape[0] > p else head


def _shift_rows_batch(cur, prev, j):
    if j == 0:
        return cur
    if j == 8:
        return prev
    rows = cur.shape[0]
    row = lax.broadcasted_iota(jnp.int32, cur.shape, 0)
    return jnp.where((row & 7) < j, pltpu.roll(prev, rows + j - 8, 0), pltpu.roll(cur, j, 0))


def _rms_gate(o, norm_w, gate):
    ms = jnp.mean(o * o, axis=-1, keepdims=True)
    return o * lax.rsqrt(ms + RMS_EPS) * norm_w * _silu(gate)


def _mixer_ab_kernel(x_ref, w_ref, lb_ref, nw_ref, pw_ref, ps_ref, s0_ref, pa_ref, pb_ref,
                     mix_ref, s_out_ref, pa_out_ref, pb_out_ref, state_ref, prev_ref, *, chain, pos0):
    rows = x_ref.shape[0]
    chunk = HG_CHUNK if chain else 8
    n_chunks = rows // chunk
    j = pl.program_id(1) if chain else 0

    proj = _dot(_bf(x_ref[...]), w_ref[...])
    q = _silu(proj[:, 0:HG_WIDTH])
    f = lb_ref[...] + (1.0 - lb_ref[...]) * jax.nn.sigmoid(proj[:, HG_WIDTH:2 * HG_WIDTH])
    v = _silu(proj[:, 2 * HG_WIDTH:3 * HG_WIDTH])
    gate = proj[:, 3 * HG_WIDTH:4 * HG_WIDTH]
    u = proj[:, 4 * HG_WIDTH:]
    log_f = jnp.log(f)
    k = 1.0 - f

    same, incl, _ = _chunk_masks(rows, chunk)
    log_f3 = _split3(log_f)
    sums = _exact_mask_dot(_dot, jnp.concatenate([_mask_bf16(incl), _mask_bf16(same)], axis=0), log_f3)
    cum = sums[:rows]
    total = sums[rows:]
    q_dec = q * jnp.exp(cum)
    k_dec = k * jnp.exp(-cum)
    k_end = k * jnp.exp(total - cum)
    chunk_of_row = lax.broadcasted_iota(jnp.int32, (rows, 128), 0) >> (chunk.bit_length() - 1)
    onehot = _mask_bf16(chunk_of_row == lax.broadcasted_iota(jnp.int32, (rows, 128), 1))
    decay_cols = jnp.exp(_exact_mask_dot(lambda m, t: _dot_tn(t, m), onehot, log_f3))

    if chain:
        @pl.when(j == 0)
        def _():
            state_ref[...] = s0_ref[0]
            prev_ref[...] = jnp.concatenate([pa_ref[...], pb_ref[...]], axis=0)

    head_sl = [slice(h * HG_D, (h + 1) * HG_D) for h in range(HG_HEADS)]
    intra, inter, state = [], [[] for _ in head_sl], [None] * HG_HEADS
    for sl in head_sl:
        att = jnp.where(incl, _dot_nt(_bf(q_dec[:, sl]), _bf(k_dec[:, sl])), 0.0)
        intra.append(_dot(_bf(att), _bf(v[:, sl])))
    for c in range(n_chunks):
        rs = slice(c * chunk, (c + 1) * chunk)
        for h, sl in enumerate(head_sl):
            if chain:
                s = state_ref[h] if c == 0 else state[h]
            else:
                s = s0_ref[c, h]
            inter[h].append(_dot(_bf(q_dec[rs, sl]), _bf(s)))
            state[h] = decay_cols[sl, c:c + 1] * s + _dot_tn(_bf(k_end[rs, sl]), _bf(v[rs, sl]))
            if not chain:
                s_out_ref[c, h] = state[h]
    outs = []
    for h, sl in enumerate(head_sl):
        if chain:
            state_ref[h] = state[h]
        outs.append(_rms_gate(intra[h] + jnp.concatenate(inter[h], axis=0), nw_ref[...], gate[:, sl]))

    if chain:
        prev = prev_ref[...]
        t = j * rows + lax.broadcasted_iota(jnp.int32, (rows, POOL_GC), 0)
    else:
        prev_a, prev_b = pa_ref[...], pb_ref[...]
        t = lax.broadcasted_iota(jnp.int32, (rows, POOL_GC), 0) & 7
    for gi, win in enumerate(POOL_WINDOWS):
        gs = slice(gi * POOL_GC, (gi + 1) * POOL_GC)
        ug = u[:, gs]
        wsum = ug
        for d in range(1, win):
            if chain:
                wsum = wsum + _shift_rows_chain(ug, prev[:, gs], d)
            elif d <= 8:
                wsum = wsum + _shift_rows_batch(ug, prev_b[:, gs], d)
            else:
                wsum = wsum + _shift_rows_batch(prev_b[:, gs], prev_a[:, gs], d - 8)
        cnt = jnp.minimum(win, pos0 + t + 1).astype(jnp.float32)
        diff = wsum / cnt - ug
        outs.append(_dot(_bf(diff), pw_ref[gi]) * ps_ref[:, gs])
    mix_ref[...] = jnp.concatenate(outs, axis=-1).astype(mix_ref.dtype)

    if chain:
        prev_ref[...] = u[rows - 16:]

        @pl.when(j == pl.num_programs(1) - 1)
        def _():
            s_out_ref[0] = state_ref[...]
            pa_out_ref[...] = u[rows - 16:rows - 8]
            pb_out_ref[...] = u[rows - 8:]
    else:
        pa_out_ref[...] = prev_b
        pb_out_ref[...] = u


def _mixer_ab(x, row0, n_seq, length, pos0, s0, pool_state, w_in, lb, norm_w, pool_w, pool_scale):
    n_rows = n_seq * length
    chain = length >= ROW_TILE
    rows = ROW_TILE if chain else BATCH_ROW_TILE
    assert row0 % rows == 0
    first = row0 // rows
    pool16 = jnp.pad(pool_state, ((0, 0), (1, 0), (0, 0)))
    pa = pool16[:, :8].reshape(n_seq * 8, POOL_WIDTH)
    pb = pool16[:, 8:].reshape(n_seq * 8, POOL_WIDTH)
    if chain:
        assert length % rows == 0
        tiles = length // rows
        grid = (n_seq, tiles)
        row_map = lambda b, j: (b * tiles + j, 0)
        x_map = lambda b, j: (first + b * tiles + j, 0)
        seq_map4 = lambda b, j: (b, 0, 0, 0)
        seq_map2 = lambda b, j: (b, 0)
        const2 = lambda b, j: (0, 0)
        const3 = lambda b, j: (0, 0, 0)
        seq_block = 1
    else:
        assert length == 8 and n_rows % rows == 0
        seq_block = rows // 8
        grid = (n_rows // rows,)
        row_map = lambda i: (i, 0)
        x_map = lambda i: (first + i, 0)
        seq_map4 = lambda i: (i, 0, 0, 0)
        seq_map2 = lambda i: (i, 0)
        const2 = lambda i: (0, 0)
        const3 = lambda i: (0, 0, 0)
    d_in = w_in.shape[1]
    mix, s_new, pa_new, pb_new = pl.pallas_call(
        functools.partial(_mixer_ab_kernel, chain=chain, pos0=pos0),
        grid=grid,
        in_specs=[
            pl.BlockSpec((rows, D_MODEL), x_map),
            pl.BlockSpec((D_MODEL, d_in), const2),
            pl.BlockSpec((1, HG_WIDTH), const2),
            pl.BlockSpec((1, HG_D), const2),
            pl.BlockSpec((len(POOL_WINDOWS), POOL_GC, POOL_GC), const3),
            pl.BlockSpec((1, POOL_WIDTH), const2),
            pl.BlockSpec((seq_block, HG_HEADS, HG_D, HG_D), seq_map4),
            pl.BlockSpec((seq_block * 8, POOL_WIDTH), seq_map2),
            pl.BlockSpec((seq_block * 8, POOL_WIDTH), seq_map2),
        ],
        out_specs=[
            pl.BlockSpec((rows, MIX_OUT), row_map),
            pl.BlockSpec((seq_block, HG_HEADS, HG_D, HG_D), seq_map4),
            pl.BlockSpec((seq_block * 8, POOL_WIDTH), seq_map2),
            pl.BlockSpec((seq_block * 8, POOL_WIDTH), seq_map2),
        ],
        out_shape=[
            jax.ShapeDtypeStruct((n_rows, MIX_OUT), jnp.bfloat16),
            jax.ShapeDtypeStruct(s0.shape, jnp.float32),
            jax.ShapeDtypeStruct(pa.shape, jnp.float32),
            jax.ShapeDtypeStruct(pb.shape, jnp.float32),
        ],
        scratch_shapes=[
            pltpu.VMEM((HG_HEADS, HG_D, HG_D), jnp.float32),
            pltpu.VMEM((16, POOL_WIDTH), jnp.float32),
        ],
        compiler_params=pltpu.CompilerParams(
            dimension_semantics=("arbitrary",) * len(grid), vmem_limit_bytes=VMEM_LIMIT),
        name="mixer_ab_chain" if chain else "mixer_ab_batch",
    )(x, w_in, lb.reshape(1, -1), norm_w.reshape(1, -1), pool_w, pool_scale.reshape(1, -1), s0, pa, pb)
    pool_new = jnp.concatenate([pa_new.reshape(n_seq, 8, POOL_WIDTH), pb_new.reshape(n_seq, 8, POOL_WIDTH)], axis=1)
    return mix, s_new, pool_new[:, 1:]


SC_WIDTH, SC_K = 512, 3
GD_HEADS, GD_D = 4, 128
GD_CONV = 4
GD_CHUNK = 64
GD_QKV = GD_HEADS * 3 * GD_D
CD_MAIN = 3 * SC_WIDTH + GD_QKV + GD_HEADS * GD_D


def _causal_conv(cur, prev, w_ref, shift_fn):
    width = w_ref.shape[0]
    acc = cur * w_ref[width - 1:width, :]
    for j in range(width - 1):
        acc = acc + shift_fn(cur, prev, width - 1 - j) * w_ref[j:j + 1, :]
    return acc


def _softplus(v):
    return jnp.maximum(v, 0.0) + jnp.log1p(jnp.exp(-jnp.abs(v)))


def _mixer_cd_kernel(x_ref, w_ref, wg_ref, scw_ref, gcw_ref, alog_ref, dtb_ref, nw_ref, s0_ref, scp_ref, gcp_ref,
                     mix_ref, s_out_ref, scp_out_ref, gcp_out_ref, state_ref, sc_prev_ref, gc_prev_ref, *, chain):
    rows = x_ref.shape[0]
    chunk = GD_CHUNK if chain else 8
    n_chunks = rows // chunk
    j = pl.program_id(1) if chain else 0
    shift_fn = _shift_rows_chain if chain else _shift_rows_batch

    xb = _bf(x_ref[...])
    proj = _dot(xb, w_ref[...])
    gates = _dot(xb, wg_ref[...])
    b_gate = proj[:, 0:SC_WIDTH]
    conv_in = proj[:, SC_WIDTH:2 * SC_WIDTH] * proj[:, 2 * SC_WIDTH:3 * SC_WIDTH]
    qkv = proj[:, 3 * SC_WIDTH:3 * SC_WIDTH + GD_QKV]
    z = proj[:, 3 * SC_WIDTH + GD_QKV:]

    if chain:
        @pl.when(j == 0)
        def _():
            state_ref[...] = s0_ref[0]
            sc_prev_ref[...] = scp_ref[...]
            gc_prev_ref[...] = gcp_ref[...]
        sc_prev, gc_prev = sc_prev_ref[...], gc_prev_ref[...]
    else:
        sc_prev, gc_prev = scp_ref[...], gcp_ref[...]

    outs = [b_gate * _causal_conv(conv_in, sc_prev, scw_ref, shift_fn)]
    qkv_c = _silu(_causal_conv(qkv, gc_prev, gcw_ref, shift_fn))

    lane8 = lax.broadcasted_iota(jnp.int32, gates.shape, 1)
    g_dec = -jnp.exp(alog_ref[...]) * _softplus(gates + dtb_ref[...])
    cols = jnp.where(lane8 < GD_HEADS, g_dec, jax.nn.sigmoid(gates))
    same, incl, strict = _chunk_masks(rows, chunk)
    cols3 = _split3(cols)
    gcum = _exact_mask_dot(_dot, _mask_bf16(incl), cols3)
    gtot = _exact_mask_dot(_dot, _mask_bf16(same), cols3)
    eye8 = _mask_bf16(lax.broadcasted_iota(jnp.int32, (8, 8), 0) == lax.broadcasted_iota(jnp.int32, (8, 8), 1))
    gcum_rows = _exact_mask_dot(_dot_nt, eye8, _split3(gcum))

    eye = (lax.broadcasted_iota(jnp.int32, (rows, rows), 0)
           == lax.broadcasted_iota(jnp.int32, (rows, rows), 1)).astype(jnp.float32)
    hd = []
    for h in range(GD_HEADS):
        sl = lambda part: slice(part * GD_HEADS * GD_D + h * GD_D, part * GD_HEADS * GD_D + (h + 1) * GD_D)
        q, k, v = qkv_c[:, sl(0)], qkv_c[:, sl(1)], qkv_c[:, sl(2)]
        q = q * lax.rsqrt(jnp.sum(q * q, axis=-1, keepdims=True) + 1e-6) * (GD_D ** -0.5)
        k = k * lax.rsqrt(jnp.sum(k * k, axis=-1, keepdims=True) + 1e-6)
        beta = cols[:, GD_HEADS + h:GD_HEADS + h + 1]
        gc = gcum[:, h:h + 1]
        gt = gtot[:, h:h + 1]
        decay = jnp.where(incl, jnp.exp(jnp.where(incl, gc - gcum_rows[h:h + 1, :], 0.0)), 0.0)
        k_beta = k * beta
        kb = _bf(k)
        egc = jnp.exp(gc)
        power = -jnp.where(strict, _dot_nt(_bf(k_beta), kb) * decay, 0.0)
        hd.append(dict(
            inv=eye + power, power=_bf(power),
            att=_bf(jnp.where(incl, _dot_nt(_bf(q), kb) * decay, 0.0)),
            rhs=_split2(jnp.concatenate([v * beta, k_beta * egc], axis=1)),
            q_dec=q * egc, k_end=k * jnp.exp(gt - gc),
            decay_end=jnp.exp(gt), inter=[], v_new=[], s=None))

    for _ in range(chunk.bit_length() - 2):
        for d in hd:
            d["power"] = _bf(_dot(d["power"], d["power"]))
        for d in hd:
            d["inv"] = d["inv"] + _dot(_bf(d["inv"]), d["power"])
    for d in hd:
        uw = _dot3(_split2(d["inv"]), d["rhs"])
        d["u"], d["w"] = uw[:, :GD_D], uw[:, GD_D:]

    for c in range(n_chunks):
        rs = slice(c * chunk, (c + 1) * chunk)
        for h, d in enumerate(hd):
            if chain:
                s = state_ref[h] if c == 0 else d["s"]
            else:
                s = s0_ref[c, h]
            sb = _bf(s)
            v_new = d["u"][rs] - _dot(_bf(d["w"][rs]), sb)
            d["inter"].append(_dot(_bf(d["q_dec"][rs]), sb))
            d["s"] = d["decay_end"][c * chunk:c * chunk + 1, :] * s + _dot_tn(_bf(d["k_end"][rs]), _bf(v_new))
            d["v_new"].append(v_new)
            if not chain:
                s_out_ref[c, h] = d["s"]
    for h, d in enumerate(hd):
        if chain:
            state_ref[h] = d["s"]
        o = jnp.concatenate(d["inter"], axis=0) + _dot(d["att"], _bf(jnp.concatenate(d["v_new"], axis=0)))
        outs.append(_rms_gate(o, nw_ref[...], z[:, h * GD_D:(h + 1) * GD_D]))
    mix_ref[...] = jnp.concatenate(outs, axis=-1).astype(mix_ref.dtype)

    if chain:
        sc_prev_ref[...] = conv_in[rows - 8:]
        gc_prev_ref[...] = qkv[rows - 8:]

        @pl.when(j == pl.num_programs(1) - 1)
        def _():
            s_out_ref[0] = state_ref[...]
            scp_out_ref[...] = conv_in[rows - 8:]
            gcp_out_ref[...] = qkv[rows - 8:]
    else:
        scp_out_ref[...] = conv_in
        gcp_out_ref[...] = qkv


def _mixer_cd(x, row0, n_seq, length, s0, sconv_state, gconv_state, w_main, w_gates, sconv_w, gconv_w, a_log, dt_bias,
              norm_w):
    n_rows = n_seq * length
    chain = length >= ROW_TILE
    rows = ROW_TILE if chain else BATCH_ROW_TILE
    assert row0 % rows == 0
    first = row0 // rows
    pad8 = lambda st: jnp.pad(st, ((0, 0), (8 - st.shape[1], 0), (0, 0))).reshape(n_seq * 8, st.shape[2])
    scp, gcp = pad8(sconv_state), pad8(gconv_state)
    if chain:
        assert length % rows == 0
        tiles = length // rows
        grid = (n_seq, tiles)
        row_map = lambda b, j: (b * tiles + j, 0)
        x_map = lambda b, j: (first + b * tiles + j, 0)
        seq_map4 = lambda b, j: (b, 0, 0, 0)
        seq_map2 = lambda b, j: (b, 0)
        const2 = lambda b, j: (0, 0)
        seq_block = 1
    else:
        assert length == 8 and n_rows % rows == 0
        seq_block = rows // 8
        grid = (n_rows // rows,)
        row_map = lambda i: (i, 0)
        x_map = lambda i: (first + i, 0)
        seq_map4 = lambda i: (i, 0, 0, 0)
        seq_map2 = lambda i: (i, 0)
        const2 = lambda i: (0, 0)
    zeros4 = jnp.zeros((GD_HEADS,), jnp.float32)
    alog8 = jnp.concatenate([a_log.astype(jnp.float32), zeros4]).reshape(1, 8)
    dtb8 = jnp.concatenate([dt_bias.astype(jnp.float32), zeros4]).reshape(1, 8)
    mix, s_new, scp_new, gcp_new = pl.pallas_call(
        functools.partial(_mixer_cd_kernel, chain=chain),
        grid=grid,
        in_specs=[
            pl.BlockSpec((rows, D_MODEL), x_map),
            pl.BlockSpec((D_MODEL, CD_MAIN), const2),
            pl.BlockSpec((D_MODEL, 8), const2),
            pl.BlockSpec((SC_K, SC_WIDTH), const2),
            pl.BlockSpec((GD_CONV, GD_QKV), const2),
            pl.BlockSpec((1, 8), const2),
            pl.BlockSpec((1, 8), const2),
            pl.BlockSpec((1, GD_D), const2),
            pl.BlockSpec((seq_block, GD_HEADS, GD_D, GD_D), seq_map4),
            pl.BlockSpec((seq_block * 8, SC_WIDTH), seq_map2),
            pl.BlockSpec((seq_block * 8, GD_QKV), seq_map2),
        ],
        out_specs=[
            pl.BlockSpec((rows, MIX_OUT), row_map),
            pl.BlockSpec((seq_block, GD_HEADS, GD_D, GD_D), seq_map4),
            pl.BlockSpec((seq_block * 8, SC_WIDTH), seq_map2),
            pl.BlockSpec((seq_block * 8, GD_QKV), seq_map2),
        ],
        out_shape=[
            jax.ShapeDtypeStruct((n_rows, MIX_OUT), jnp.bfloat16),
            jax.ShapeDtypeStruct(s0.shape, jnp.float32),
            jax.ShapeDtypeStruct(scp.shape, jnp.float32),
            jax.ShapeDtypeStruct(gcp.shape, jnp.float32),
        ],
        scratch_shapes=[
            pltpu.VMEM((GD_HEADS, GD_D, GD_D), jnp.float32),
            pltpu.VMEM((8, SC_WIDTH), jnp.float32),
            pltpu.VMEM((8, GD_QKV), jnp.float32),
        ],
        compiler_params=pltpu.CompilerParams(
            dimension_semantics=("arbitrary",) * len(grid), vmem_limit_bytes=VMEM_LIMIT),
        name="mixer_cd_chain" if chain else "mixer_cd_batch",
    )(x, w_main, w_gates, sconv_w, gconv_w, alog8, dtb8, norm_w.reshape(1, -1), s0, scp, gcp)
    tail = lambda st, keep: st.reshape(n_seq, 8, -1)[:, 8 - keep:]
    return mix, s_new, tail(scp_new, SC_K - 1), tail(gcp_new, GD_CONV - 1)


def _outproj_ln_router_kernel(mix_ref, x_ref, w_ref, g_ref, b_ref, rwt_ref, rb_ref,
                              x1_ref, xp_ref, tope_ref, gate_ref, rank_ref, cnt_ref, linv_ref, *, group_tiles):
    mix = jnp.dot(mix_ref[...], w_ref[...], preferred_element_type=jnp.float32)
    x1 = _layer_norm_rows(DEEPNORM_ALPHA * x_ref[...] + mix, g_ref[...], b_ref[...])
    x1_ref[...] = x1
    packed = _pack_bf16_pairs(x1)
    for j in range(HALF // 128):
        xp_ref[:, j * 8:(j + 1) * 8, :] = packed[:, j * 128:(j + 1) * 128].reshape(x1.shape[0] // 8, 8, 128)
    logits = _dot_nt(_bf(rwt_ref[...]), _bf(x1)) + rb_ref[...]
    tm = logits.shape[1]
    expert = lax.broadcasted_iota(jnp.int32, logits.shape, 0)
    vals, idxs = [], []
    for _ in range(TOP_K):
        m = jnp.max(logits, axis=0, keepdims=True)
        idx = jnp.min(jnp.where(logits == m, expert, N_EXPERTS), axis=0, keepdims=True)
        vals.append(m)
        idxs.append(idx)
        logits = jnp.where(expert == idx, -jnp.inf, logits)
    ex = [jnp.exp(v - vals[0]) for v in vals]
    den = ex[0] + ex[1] + ex[2] + ex[3]
    hits = [expert == idx for idx in idxs]
    member = sum(h.astype(jnp.float32) for h in hits)
    earlier = (lax.broadcasted_iota(jnp.int32, (tm, tm), 0) < lax.broadcasted_iota(jnp.int32, (tm, tm), 1))
    before = _dot(member.astype(jnp.bfloat16), earlier.astype(jnp.bfloat16))
    cnt_ref[0] = jnp.sum(member, axis=1, keepdims=True).astype(jnp.int32)
    slot = lax.broadcasted_iota(jnp.int32, tope_ref.shape[1:], 0)
    tope = jnp.zeros(slot.shape, jnp.int32)
    gate = jnp.zeros(slot.shape, jnp.float32)
    rank = jnp.zeros(slot.shape, jnp.int32)
    for k in range(TOP_K):
        tope = jnp.where(slot == k, idxs[k], tope)
        gate = jnp.where(slot == k, ex[k] / den, gate)
        rank_k = jnp.sum(jnp.where(hits[k], before, 0.0), axis=0, keepdims=True).astype(jnp.int32)
        rank = jnp.where(slot == k, rank_k, rank)
    tope_ref[0] = tope
    gate_ref[0] = gate
    rank_ref[0] = rank

    lower = (lax.broadcasted_iota(jnp.int32, (tm, tm), 1) < lax.broadcasted_iota(jnp.int32, (tm, tm), 0))
    member_bf = member.astype(jnp.bfloat16)
    before_cols = _dot_nt(lower.astype(jnp.bfloat16), member_bf)
    eye = (lax.broadcasted_iota(jnp.int32, (tm, tm), 0) == lax.broadcasted_iota(jnp.int32, (tm, tm), 1))
    member_cols = _dot_nt(eye.astype(jnp.bfloat16), member_bf)
    key = jnp.where(member_cols > 0.5, before_cols, -1.0)
    order = lax.broadcasted_iota(jnp.int32, (tm, tm), 1).astype(jnp.float32)
    token = lax.broadcasted_iota(jnp.int32, (8, tm), 1).astype(jnp.bfloat16)
    expert8 = lax.broadcasted_iota(jnp.int32, (N_EXPERTS, tm), 0)
    listed = jnp.zeros((N_EXPERTS, tm), jnp.float32)
    for e in range(N_EXPERTS):
        onehot = (key[:, e:e + 1] == order).astype(jnp.bfloat16)
        listed = jnp.where(expert8 == e, _dot(token, onehot)[0:1, :], listed)
    local = listed.astype(jnp.int32)
    first = lax.rem(pl.program_id(0), group_tiles) * tm
    linv_ref[0] = ((first + local) >> 3) * (8 * (HALF // 128)) + (local & 7)


def _outproj_ln_router(mix, x, w_out, ln_g, ln_b, router_w, router_b):
    n = x.shape[0]
    tm = TOKEN_TILE
    row = lambda i: (i, 0)
    full = lambda i: (0, 0)
    return pl.pallas_call(
        functools.partial(_outproj_ln_router_kernel, group_tiles=n // tm // N_GROUPS),
        grid=(n // tm,),
        in_specs=[
            pl.BlockSpec((tm, D_MODEL), row),
            pl.BlockSpec((tm, D_MODEL), row),
            pl.BlockSpec((D_MODEL, D_MODEL), full),
            pl.BlockSpec((1, D_MODEL), full),
            pl.BlockSpec((1, D_MODEL), full),
            pl.BlockSpec((N_EXPERTS, D_MODEL), full),
            pl.BlockSpec((N_EXPERTS, 1), full),
        ],
        out_specs=[
            pl.BlockSpec((tm, D_MODEL), row),
            pl.BlockSpec((tm // 8, HALF // 16, 128), lambda i: (i, 0, 0)),
            pl.BlockSpec((1, TOP_K, tm), lambda i: (i, 0, 0)),
            pl.BlockSpec((1, TOP_K, tm), lambda i: (i, 0, 0)),
            pl.BlockSpec((1, TOP_K, tm), lambda i: (i, 0, 0)),
            pl.BlockSpec((1, N_EXPERTS, 1), lambda i: (i, 0, 0)),
            pl.BlockSpec((1, N_EXPERTS, tm), lambda i: (i, 0, 0)),
        ],
        out_shape=[
            jax.ShapeDtypeStruct((n, D_MODEL), jnp.float32),
            jax.ShapeDtypeStruct((n // 8, HALF // 16, 128), jnp.uint32),
            jax.ShapeDtypeStruct((n // tm, TOP_K, tm), jnp.int32),
            jax.ShapeDtypeStruct((n // tm, TOP_K, tm), jnp.float32),
            jax.ShapeDtypeStruct((n // tm, TOP_K, tm), jnp.int32),
            jax.ShapeDtypeStruct((n // tm, N_EXPERTS, 1), jnp.int32),
            jax.ShapeDtypeStruct((n // tm, N_EXPERTS, tm), jnp.int32),
        ],
        compiler_params=pltpu.CompilerParams(dimension_semantics=("arbitrary",), vmem_limit_bytes=VMEM_LIMIT),
        name="outproj_ln_router",
    )(mix, x, w_out, ln_g.reshape(1, -1), ln_b.reshape(1, -1), router_w.T, router_b.reshape(-1, 1))


def _route_plan(top_e, rank, tile_counts):
    n = top_e.shape[0] * top_e.shape[2]
    a = n * TOP_K
    br, tm, ch = EXPERT_ROWS, TOKEN_TILE, COMBINE_CHUNK
    n_virtual = N_EXPERTS * N_GROUPS
    group_tokens = n // N_GROUPS
    n_blocks = a // br + n_virtual + 1
    n_tiles = n // tm
    group_tiles = n_tiles // N_GROUPS
    max_chunks = (tm * TOP_K) // ch + N_EXPERTS + 1
    i32 = jnp.int32

    cnt = tile_counts.reshape(N_GROUPS, group_tiles, N_EXPERTS)
    nblk = (cnt.sum(axis=1) + br - 1) // br
    blk_end = jnp.cumsum(nblk.reshape(-1))
    row_off = ((blk_end - nblk.reshape(-1)) * br).reshape(N_GROUPS, 1, N_EXPERTS)
    seg_start = (row_off + jnp.cumsum(cnt, axis=1) - cnt).reshape(n_tiles, N_EXPERTS)
    seg_len = tile_counts
    blocks = jnp.arange(n_blocks, dtype=i32)
    block_ve = jnp.minimum(jnp.sum(blk_end[None, :] <= blocks[:, None], axis=1), n_virtual - 1).astype(i32)
    valid = blocks < blk_end[-1]
    run_ve = jnp.where(valid, block_ve, block_ve[jnp.maximum(blk_end[-1] - 1, 0)])
    block_e = run_ve % N_EXPERTS
    starts = jnp.concatenate([jnp.ones((1,), bool), block_e[1:] != block_e[:-1]])
    later_start = lax.cummin(jnp.where(starts, blocks, n_blocks)[::-1], axis=0)[::-1]
    next_start = jnp.concatenate([later_start[1:], jnp.full((1,), n_blocks, i32)])
    next_e = jnp.where(next_start < n_blocks, block_e[jnp.minimum(next_start, n_blocks - 1)], -1)
    block_info = jnp.stack([block_e, run_ve // N_EXPERTS, valid.astype(i32), starts.astype(i32), next_e.astype(i32)])

    aligned = seg_start // 8 * 8
    lead = seg_start - aligned
    nch = jnp.where(seg_len > 0, (lead + seg_len + ch - 1) // ch, 0)
    ch_end = jnp.cumsum(nch, axis=1)
    ch_first = ch_end - nch
    n_chunks = ch_end[:, -1]
    cidx = jnp.arange(max_chunks, dtype=i32)[None, :, None]
    owns = jnp.logical_and(ch_first[:, None, :] <= cidx, cidx < ch_end[:, None, :])
    chunk_src = jnp.sum(jnp.where(owns, aligned[:, None, :] + (cidx - ch_first[:, None, :]) * ch, 0), axis=2)

    hit = top_e[..., None] == jnp.arange(N_EXPERTS, dtype=i32)
    pick = lambda table: jnp.sum(jnp.where(hit, table[:, None, None, :], 0), axis=-1)
    pos = pick(ch_first * ch + lead) + rank
    cnt_ve = cnt.transpose(0, 2, 1).reshape(n_virtual, group_tiles)
    passed_rows = jnp.cumsum(cnt_ve, axis=1)[run_ve]
    offset = (blocks * br - row_off.reshape(-1)[run_ve])[:, None] + jnp.arange(br, dtype=i32)[None, :]
    passed = passed_rows[:, None, :] <= offset[:, :, None]
    tile_idx = jnp.minimum(jnp.sum(passed, axis=2), group_tiles - 1)
    place = jnp.clip(offset - jnp.sum(jnp.where(passed, cnt_ve[run_ve][:, None, :], 0), axis=2), 0, tm - 1)
    row_place = (tile_idx * tm + place).astype(i32)
    pos = (pos // 8) * (8 * LANE_TILES) + pos % 8
    return dict(row_place=row_place.reshape(n_blocks, 1, br), block_info=block_info,
                chunk_src=(chunk_src * LANE_TILES).astype(i32), n_chunks=n_chunks.astype(i32),
                pos=pos.reshape(n_tiles, 1, tm * TOP_K).astype(i32))


def _expert_kernel(info_ref, place_ref, place_next_ref, list_ref, list_next_ref, xq_hbm, wgu_hbm, bgu_ref, wdn_hbm,
                   bdn_ref, y_ref, xq_vmem, stage_ref, wgu_f32_ref, wdn_f32_ref, wgu_ref, wdn_ref, load_sem, w_sems, *,
                   layer):
    b = pl.program_id(0)
    grp = info_ref[1, b]
    prev_grp = info_ref[1, jnp.maximum(b - 1, 0)]

    def weight_copies(expert):
        return (pltpu.make_async_copy(wgu_hbm.at[layer, expert], wgu_f32_ref, w_sems.at[0]),
                pltpu.make_async_copy(wdn_hbm.at[layer, expert], wdn_f32_ref, w_sems.at[1]))

    @pl.when(b == 0)
    def _():
        for cp in weight_copies(info_ref[0, 0]):
            cp.start()

    @pl.when(info_ref[3, b] == 1)
    def _():
        for cp in weight_copies(info_ref[0, b]):
            cp.wait()
        rows = D_MODEL // 8

        def cast(i, carry):
            r0 = pl.multiple_of(i * rows, rows)
            wgu_ref[pl.ds(r0, rows), :] = wgu_f32_ref[pl.ds(r0, rows), :].astype(jnp.bfloat16)
            wdn_ref[pl.ds(r0, rows), :] = wdn_f32_ref[pl.ds(r0, rows), :].astype(jnp.bfloat16)
            return carry

        lax.fori_loop(0, 8, cast, 0)

        @pl.when(info_ref[4, b] >= 0)
        def _():
            for cp in weight_copies(info_ref[4, b]):
                cp.start()

    group_rows = xq_vmem.shape[0]
    slot = b % 2
    tiles = HALF // 128

    def gather_row(addr, dst_slot, group8, sub):
        stage_ref[dst_slot, group8, pl.ds(sub, tiles, stride=8), :] = xq_vmem[pl.ds(addr, tiles, stride=8), :]

    @pl.when(jnp.logical_or(b == 0, grp != prev_grp))
    def _():
        cp = pltpu.make_async_copy(xq_hbm.at[pl.ds(grp * group_rows, group_rows)], xq_vmem, load_sem)
        cp.start()
        cp.wait()

        def gather(i, carry):
            for u in range(8):
                gather_row(list_ref[0, 0, 0, place_ref[0, 0, i * 8 + u]], slot, i, u)
            return carry

        lax.fori_loop(0, EXPERT_ROWS // 8, gather, 0)

    @pl.when(info_ref[2, b] == 1)
    def _():
        for r in range(EXPERT_ROWS):
            gather_row(list_next_ref[0, 0, 0, place_next_ref[0, 0, r]], 1 - slot, r // 8, r % 8)
        pieces = [_unpack_bf16_pairs(stage_ref[slot, :, j * 8:(j + 1) * 8, :].reshape(EXPERT_ROWS, 128))
                  for j in range(tiles)]
        lo = jnp.concatenate([p[0] for p in pieces], axis=1)
        hi = jnp.concatenate([p[1] for p in pieces], axis=1)
        hgu = (jnp.dot(lo, wgu_ref[:HALF, :], preferred_element_type=jnp.float32)
               + jnp.dot(hi, wgu_ref[HALF:, :], preferred_element_type=jnp.float32) + bgu_ref[0, 0])
        glu = jnp.minimum(hgu[:, :D_FF], SWIGLU_LIMIT)
        lin = jnp.clip(hgu[:, D_FF:], -SWIGLU_LIMIT, SWIGLU_LIMIT)
        act = (lin + 1.0) * glu * jax.nn.sigmoid(SWIGLU_ALPHA * glu)
        y = jnp.dot(act.astype(jnp.bfloat16), wdn_ref[...], preferred_element_type=jnp.float32) + bdn_ref[0, 0]
        for c in range(LANE_TILES):
            y_ref[:, c * 8:(c + 1) * 8, :] = y[:, c * 128:(c + 1) * 128].reshape(EXPERT_ROWS // 8, 8, 128)

    @pl.when(info_ref[2, b] == 0)
    def _():
        y_ref[...] = jnp.zeros_like(y_ref)


def _expert_mlp(plan, xq, token_lists, layer, w_gu, b_gu, w_dn, b_dn):
    br = EXPERT_ROWS
    n_blocks = plan["row_place"].shape[0]
    assert br >= COMBINE_CHUNK + 8
    pad_rows = n_blocks * br
    expert = lambda b, info: (layer, info[0, b], 0, 0)
    following = lambda b: jnp.minimum(b + 1, n_blocks - 1)
    list_block = (1, 1, 1, token_lists.shape[3])
    grid_spec = pltpu.PrefetchScalarGridSpec(
        num_scalar_prefetch=1,
        grid=(n_blocks,),
        in_specs=[
            pl.BlockSpec((1, 1, br), lambda b, info: (b, 0, 0), memory_space=pltpu.SMEM),
            pl.BlockSpec((1, 1, br), lambda b, info: (following(b), 0, 0), memory_space=pltpu.SMEM),
            pl.BlockSpec(list_block, lambda b, info: (info[0, b], info[1, b], 0, 0), memory_space=pltpu.SMEM),
            pl.BlockSpec(list_block, lambda b, info: (info[0, following(b)], info[1, following(b)], 0, 0),
                         memory_space=pltpu.SMEM),
            pl.BlockSpec(memory_space=pl.ANY),
            pl.BlockSpec(memory_space=pl.ANY),
            pl.BlockSpec((1, 1, 1, 2 * D_FF), expert),
            pl.BlockSpec(memory_space=pl.ANY),
            pl.BlockSpec((1, 1, 1, D_MODEL), expert),
        ],
        out_specs=pl.BlockSpec((br // 8, 8 * LANE_TILES, 128), lambda b, info: (b, 0, 0)),
        scratch_shapes=[
            pltpu.VMEM((xq.shape[0] // N_GROUPS, 128), jnp.uint32),
            pltpu.VMEM((2, br // 8, 8 * (HALF // 128), 128), jnp.uint32),
            pltpu.VMEM((D_MODEL, 2 * D_FF), jnp.float32),
            pltpu.VMEM((D_FF, D_MODEL), jnp.float32),
            pltpu.VMEM((D_MODEL, 2 * D_FF), jnp.bfloat16),
            pltpu.VMEM((D_FF, D_MODEL), jnp.bfloat16),
            pltpu.SemaphoreType.DMA(()),
            pltpu.SemaphoreType.DMA((2,)),
        ],
    )
    return pl.pallas_call(
        functools.partial(_expert_kernel, layer=layer),
        grid_spec=grid_spec,
        out_shape=jax.ShapeDtypeStruct((pad_rows // 8, 8 * LANE_TILES, 128), jnp.float32),
        compiler_params=pltpu.CompilerParams(dimension_semantics=("arbitrary",), vmem_limit_bytes=VMEM_LIMIT),
        name="expert_mlp",
    )(plan["block_info"], plan["row_place"], plan["row_place"], token_lists, token_lists, xq, w_gu,
      b_gu.reshape(DEPTH, N_EXPERTS, 1, -1), w_dn, b_dn.reshape(DEPTH, N_EXPERTS, 1, -1))


def _combine_ln_kernel(src_ref, nch_ref, pos_ref, gate_ref, x1_ref, y_hbm, g_ref, b_ref, out_ref,
                       stage_ref, ffn_ref, sems):
    i = pl.program_id(0)
    n_tiles = pl.num_programs(0)
    chunk_rows = COMBINE_CHUNK * LANE_TILES

    def chunk_copy(tile, slot, c):
        src = pl.multiple_of(src_ref[tile, c], 8 * LANE_TILES)
        return pltpu.make_async_copy(y_hbm.at[pl.ds(src, chunk_rows)],
                                     stage_ref.at[slot, pl.ds(c * chunk_rows, chunk_rows)], sems.at[slot])

    def issue(tile, slot):
        lax.fori_loop(0, nch_ref[tile], lambda c, carry: (chunk_copy(tile, slot, c).start(), carry)[1], 0)

    @pl.when(i == 0)
    def _():
        issue(0, 0)

    @pl.when(i + 1 < n_tiles)
    def _():
        issue(i + 1, (i + 1) % 2)

    slot = i % 2
    lax.fori_loop(0, nch_ref[i], lambda c, carry: (chunk_copy(i, slot, c).wait(), carry)[1], 0)

    tm = x1_ref.shape[0]

    def combine(j, carry):
        for u in range(8):
            t = j * 8 + u
            acc = None
            for k in range(TOP_K):
                row = stage_ref[slot, pl.ds(pos_ref[0, 0, k * tm + t], LANE_TILES, stride=8), :]
                row = row * gate_ref[0, 0, k * tm + t]
                acc = row if acc is None else acc + row
            ffn_ref[j, pl.ds(u, LANE_TILES, stride=8), :] = acc
        return carry

    lax.fori_loop(0, tm // 8, combine, 0)
    ffn = jnp.concatenate([ffn_ref[:, c * 8:(c + 1) * 8, :].reshape(tm, 128) for c in range(LANE_TILES)], axis=1)
    out_ref[...] = _layer_norm_rows(DEEPNORM_ALPHA * x1_ref[...] + ffn, g_ref[...], b_ref[...])


def _combine_ln(plan, gate, x1, y_flat, ln_g, ln_b):
    n = x1.shape[0]
    tm = TOKEN_TILE
    n_tiles = n // tm
    max_chunks = plan["chunk_src"].shape[1]
    row = lambda i, src, nch: (i, 0)
    full = lambda i, src, nch: (0, 0)
    per_tile = lambda i, src, nch: (i, 0, 0)
    grid_spec = pltpu.PrefetchScalarGridSpec(
        num_scalar_prefetch=2,
        grid=(n_tiles,),
        in_specs=[
            pl.BlockSpec((1, 1, tm * TOP_K), per_tile, memory_space=pltpu.SMEM),
            pl.BlockSpec((1, 1, tm * TOP_K), per_tile, memory_space=pltpu.SMEM),
            pl.BlockSpec((tm, D_MODEL), row),
            pl.BlockSpec(memory_space=pl.ANY),
            pl.BlockSpec((1, D_MODEL), full),
            pl.BlockSpec((1, D_MODEL), full),
        ],
        out_specs=pl.BlockSpec((tm, D_MODEL), row),
        scratch_shapes=[
            pltpu.VMEM((2, max_chunks * COMBINE_CHUNK * LANE_TILES, 128), jnp.float32),
            pltpu.VMEM((tm // 8, 8 * LANE_TILES, 128), jnp.float32),
            pltpu.SemaphoreType.DMA((2,)),
        ],
    )
    return pl.pallas_call(
        _combine_ln_kernel,
        grid_spec=grid_spec,
        out_shape=jax.ShapeDtypeStruct((n, D_MODEL), jnp.float32),
        compiler_params=pltpu.CompilerParams(dimension_semantics=("arbitrary",), vmem_limit_bytes=VMEM_LIMIT),
        name="combine_ln",
    )(plan["chunk_src"], plan["n_chunks"], plan["pos"], gate.reshape(n_tiles, 1, tm * TOP_K), x1, y_flat,
      ln_g.reshape(1, -1), ln_b.reshape(1, -1))


def _post_mixer(mix, x, layer, w_out, ln1_g, ln1_b, router_w, router_b, w_gu, b_gu, w_dn, b_dn, ln2_g, ln2_b):
    x1, xp, top_e, gate, rank, tile_counts, lists = _outproj_ln_router(mix, x, w_out, ln1_g, ln1_b, router_w, router_b)
    plan = _route_plan(top_e, rank, tile_counts.reshape(-1, N_EXPERTS))
    token_lists = lists.transpose(1, 0, 2).reshape(N_EXPERTS, N_GROUPS, 1, -1)
    y_sorted = _expert_mlp(plan, xp.reshape(-1, 128), token_lists, layer, w_gu, b_gu, w_dn, b_dn)
    return _combine_ln(plan, gate, x1, y_sorted.reshape(-1, 128), ln2_g, ln2_b)


def kernel(x_prompt, x_sample, state_hgrn, state_pool, state_sconv, state_gdn_conv, state_gdn, w_in_ab, hgrn_lower_bounds, hgrn_norm_w, pool_w, pool_scale, w_out_ab, w_in_cd, sconv_w, gdn_conv_w, gdn_a_log, gdn_dt_bias, gdn_norm_w, w_out_cd, ln1_g, ln1_b, ln2_g, ln2_b, router_w, router_b, w_gu, b_gu, w_dn, b_dn):
    bp, sp, _ = x_prompt.shape
    bs, ss, _ = x_sample.shape
    n_p, n_s = bp * sp, bs * ss
    bf = jnp.bfloat16
    f32 = jnp.float32
    lower_bounds = jnp.cumsum(jax.nn.softmax(hgrn_lower_bounds.astype(f32), axis=0), axis=0)
    x = jnp.concatenate([x_prompt.reshape(n_p, D_MODEL), x_sample.reshape(n_s, D_MODEL)], axis=0)
    zeros_like_prompt = lambda st: jnp.zeros((bp,) + st.shape[2:], f32)

    states = {}
    for l in range(DEPTH):
        if l % 2 == 0:
            args = (w_in_ab[0].astype(bf), lower_bounds[l], hgrn_norm_w[0], pool_w[0].astype(bf), pool_scale[0])
            mp, hp, pp = _mixer_ab(x, 0, bp, sp, 0, zeros_like_prompt(state_hgrn), zeros_like_prompt(state_pool), *args)
            ms, hs, ps = _mixer_ab(x, n_p, bs, ss, PAST_LEN, state_hgrn[0], state_pool[0], *args)
            states.update(hp=hp[None], hs=hs[None], pp=pp[None], ps=ps[None])
            w_out = w_out_ab[0]
        else:
            args = (w_in_cd[0][:, :CD_MAIN].astype(bf), w_in_cd[0][:, CD_MAIN:].astype(bf), sconv_w[0], gdn_conv_w[0],
                    gdn_a_log[0], gdn_dt_bias[0], gdn_norm_w[0])
            mp, gp, scp, gcp = _mixer_cd(x, 0, bp, sp, zeros_like_prompt(state_gdn), zeros_like_prompt(state_sconv),
                                         zeros_like_prompt(state_gdn_conv), *args)
            ms, gs, scs, gcs = _mixer_cd(x, n_p, bs, ss, state_gdn[0], state_sconv[0], state_gdn_conv[0], *args)
            states.update(scp=scp[None], scs=scs[None], gcp=gcp[None], gcs=gcs[None], gp=gp[None], gs=gs[None])
            w_out = w_out_cd[0]
        mix = jnp.concatenate([mp, ms], axis=0)
        x = _post_mixer(mix, x, l, w_out.astype(bf), ln1_g[l], ln1_b[l], router_w[l], router_b[l],
                        w_gu, b_gu, w_dn, b_dn, ln2_g[l], ln2_b[l])
    return (x[:n_p].reshape(bp, sp, D_MODEL), x[n_p:].reshape(bs, ss, D_MODEL),
            states["hp"], states["hs"], states["pp"], states["ps"], states["scp"], states["scs"],
            states["gcp"], states["gcs"], states["gp"], states["gs"])
```

```python
import functools

import jax
import jax.numpy as jnp
from jax import lax
from jax.experimental import pallas as pl
from jax.experimental.pallas import tpu as pltpu

D_MODEL = 1024
DEPTH = 2
N_EXPERTS = 32
TOP_K = 4
D_FF = 1024
SWIGLU_LIMIT = 7.0
SWIGLU_ALPHA = 1.702
LN_EPS = 1e-5
DEEPNORM_ALPHA = (2 * DEPTH) ** 0.25

HALF = D_MODEL // 2
LANE_TILES = D_MODEL // 128
TOKEN_TILE = 256
EXPERT_ROWS = 256
N_GROUPS = 1
COMBINE_CHUNK = 16
VMEM_LIMIT = 60 * 1024 * 1024
MIX_OUT = 1024
BATCH_ROW_TILE = 128
PAST_LEN = 16384


def _layer_norm_rows(v, g, b):
    mu = jnp.mean(v, axis=-1, keepdims=True)
    d = v - mu
    var = jnp.mean(d * d, axis=-1, keepdims=True)
    return d * lax.rsqrt(var + LN_EPS) * g + b


def _pack_bf16_pairs(v):
    lo = pltpu.bitcast(v[:, :HALF].astype(jnp.bfloat16).astype(jnp.float32), jnp.uint32)
    hi = pltpu.bitcast(v[:, HALF:].astype(jnp.bfloat16).astype(jnp.float32), jnp.uint32)
    return (lo >> 16) | (hi & jnp.uint32(0xFFFF0000))


def _unpack_bf16_pairs(p):
    lo = pltpu.bitcast(p << 16, jnp.float32).astype(jnp.bfloat16)
    hi = pltpu.bitcast(p & jnp.uint32(0xFFFF0000), jnp.float32).astype(jnp.bfloat16)
    return lo, hi


ROW_TILE = 256
HG_HEADS, HG_D = 4, 128
HG_WIDTH = HG_HEADS * HG_D
HG_CHUNK = 16
POOL_WINDOWS = (2, 4, 8, 16)
POOL_GC = 128
POOL_WIDTH = 512
RMS_EPS = 1e-6
_HI = lax.Precision.HIGHEST


def _silu(v):
    return v * jax.nn.sigmoid(v)


def _bf(v):
    return v.astype(jnp.bfloat16)


def _dot(a, b, precision=None):
    return jnp.dot(a, b, preferred_element_type=jnp.float32, precision=precision)


def _dot_nt(a, b, precision=None):
    return lax.dot_general(a, b, (((1,), (1,)), ((), ())), preferred_element_type=jnp.float32, precision=precision)


def _dot_tn(a, b, precision=None):
    return lax.dot_general(a, b, (((0,), (0,)), ((), ())), preferred_element_type=jnp.float32, precision=precision)


def _split3(v):
    hi = _bf(v)
    rest = v - hi.astype(jnp.float32)
    mid = _bf(rest)
    return hi, mid, _bf(rest - mid.astype(jnp.float32))


def _mask_bf16(mask):
    return _bf(mask.astype(jnp.float32))


def _exact_mask_dot(dot_fn, m, terms):
    return dot_fn(m, terms[0]) + (dot_fn(m, terms[1]) + dot_fn(m, terms[2]))


def _split2(v):
    hi = _bf(v)
    return hi, _bf(v - hi.astype(jnp.float32))


def _dot3(a2, b2):
    return _dot(a2[0], b2[0]) + (_dot(a2[0], b2[1]) + _dot(a2[1], b2[0]))


def _chunk_masks(rows, chunk):
    shift = chunk.bit_length() - 1
    t = lax.broadcasted_iota(jnp.int32, (rows, rows), 0)
    s = lax.broadcasted_iota(jnp.int32, (rows, rows), 1)
    same = (t >> shift) == (s >> shift)
    return same, jnp.logical_and(same, s <= t), jnp.logical_and(same, s < t)


def _shift_rows_chain(cur, prev, j):
    if j == 0:
        return cur
    p = prev.shape[0]
    rc = pltpu.roll(cur, j, 0)
    rp = prev if j == p else pltpu.roll(prev, j, 0)
    row = lax.broadcasted_iota(jnp.int32, (p, cur.shape[1]), 0)
    head = jnp.where(row < j, rp, rc[:p])
    return jnp.concatenate([head, rc[p:]], axis=0) if cur.shape[0] > p else head


def _shift_rows_batch(cur, prev, j):
    if j == 0:
        return cur
    if j == 8:
        return prev
    rows = cur.shape[0]
    row = lax.broadcasted_iota(jnp.int32, cur.shape, 0)
    return jnp.where((row & 7) < j, pltpu.roll(prev, rows + j - 8, 0), pltpu.roll(cur, j, 0))


def _rms_gate(o, norm_w, gate):
    ms = jnp.mean(o * o, axis=-1, keepdims=True)
    return o * lax.rsqrt(ms + RMS_EPS) * norm_w * _silu(gate)


def _mixer_ab_kernel(x_ref, w_ref, lb_ref, nw_ref, pw_ref, ps_ref, s0_ref, pa_ref, pb_ref,
                     mix_ref, s_out_ref, pa_out_ref, pb_out_ref, state_ref, prev_ref, *, chain, pos0):
    rows = x_ref.shape[0]
    chunk = HG_CHUNK if chain else 8
    n_chunks = rows // chunk
    j = pl.program_id(1) if chain else 0

    proj = _dot(_bf(x_ref[...]), w_ref[...])
    q = _silu(proj[:, 0:HG_WIDTH])
    f = lb_ref[...] + (1.0 - lb_ref[...]) * jax.nn.sigmoid(proj[:, HG_WIDTH:2 * HG_WIDTH])
    v = _silu(proj[:, 2 * HG_WIDTH:3 * HG_WIDTH])
    gate = proj[:, 3 * HG_WIDTH:4 * HG_WIDTH]
    u = proj[:, 4 * HG_WIDTH:]
    log_f = jnp.log(f)
    k = 1.0 - f

    same, incl, _ = _chunk_masks(rows, chunk)
    log_f3 = _split3(log_f)
    sums = _exact_mask_dot(_dot, jnp.concatenate([_mask_bf16(incl), _mask_bf16(same)], axis=0), log_f3)
    cum = sums[:rows]
    total = sums[rows:]
    q_dec = q * jnp.exp(cum)
    k_dec = k * jnp.exp(-cum)
    k_end = k * jnp.exp(total - cum)
    chunk_of_row = lax.broadcasted_iota(jnp.int32, (rows, 128), 0) >> (chunk.bit_length() - 1)
    onehot = _mask_bf16(chunk_of_row == lax.broadcasted_iota(jnp.int32, (rows, 128), 1))
    decay_cols = jnp.exp(_exact_mask_dot(lambda m, t: _dot_tn(t, m), onehot, log_f3))

    if chain:
        @pl.when(j == 0)
        def _():
            state_ref[...] = s0_ref[0]
            prev_ref[...] = jnp.concatenate([pa_ref[...], pb_ref[...]], axis=0)

    head_sl = [slice(h * HG_D, (h + 1) * HG_D) for h in range(HG_HEADS)]
    intra, inter, state = [], [[] for _ in head_sl], [None] * HG_HEADS
    for sl in head_sl:
        att = jnp.where(incl, _dot_nt(_bf(q_dec[:, sl]), _bf(k_dec[:, sl])), 0.0)
        intra.append(_dot(_bf(att), _bf(v[:, sl])))
    for c in range(n_chunks):
        rs = slice(c * chunk, (c + 1) * chunk)
        for h, sl in enumerate(head_sl):
            if chain:
                s = state_ref[h] if c == 0 else state[h]
            else:
                s = s0_ref[c, h]
            inter[h].append(_dot(_bf(q_dec[rs, sl]), _bf(s)))
            state[h] = decay_cols[sl, c:c + 1] * s + _dot_tn(_bf(k_end[rs, sl]), _bf(v[rs, sl]))
            if not chain:
                s_out_ref[c, h] = state[h]
    outs = []
    for h, sl in enumerate(head_sl):
        if chain:
            state_ref[h] = state[h]
        outs.append(_rms_gate(intra[h] + jnp.concatenate(inter[h], axis=0), nw_ref[...], gate[:, sl]))

    if chain:
        prev = prev_ref[...]
        t = j * rows + lax.broadcasted_iota(jnp.int32, (rows, POOL_GC), 0)
    else:
        prev_a, prev_b = pa_ref[...], pb_ref[...]
        t = lax.broadcasted_iota(jnp.int32, (rows, POOL_GC), 0) & 7
    for gi, win in enumerate(POOL_WINDOWS):
        gs = slice(gi * POOL_GC, (gi + 1) * POOL_GC)
        ug = u[:, gs]
        wsum = ug
        for d in range(1, win):
            if chain:
                wsum = wsum + _shift_rows_chain(ug, prev[:, gs], d)
            elif d <= 8:
                wsum = wsum + _shift_rows_batch(ug, prev_b[:, gs], d)
            else:
                wsum = wsum + _shift_rows_batch(prev_b[:, gs], prev_a[:, gs], d - 8)
        cnt = jnp.minimum(win, pos0 + t + 1).astype(jnp.float32)
        diff = wsum / cnt - ug
        outs.append(_dot(_bf(diff), pw_ref[gi]) * ps_ref[:, gs])
    mix_ref[...] = jnp.concatenate(outs, axis=-1).astype(mix_ref.dtype)

    if chain:
        prev_ref[...] = u[rows - 16:]

        @pl.when(j == pl.num_programs(1) - 1)
        def _():
            s_out_ref[0] = state_ref[...]
            pa_out_ref[...] = u[rows - 16:rows - 8]
            pb_out_ref[...] = u[rows - 8:]
    else:
        pa_out_ref[...] = prev_b
        pb_out_ref[...] = u


def _mixer_ab(x, row0, n_seq, length, pos0, s0, pool_state, w_in, lb, norm_w, pool_w, pool_scale):
    n_rows = n_seq * length
    chain = length >= ROW_TILE
    rows = ROW_TILE if chain else BATCH_ROW_TILE
    assert row0 % rows == 0
    first = row0 // rows
    pool16 = jnp.pad(pool_state, ((0, 0), (1, 0), (0, 0)))
    pa = pool16[:, :8].reshape(n_seq * 8, POOL_WIDTH)
    pb = pool16[:, 8:].reshape(n_seq * 8, POOL_WIDTH)
    if chain:
        assert length % rows == 0
        tiles = length // rows
        grid = (n_seq, tiles)
        row_map = lambda b, j: (b * tiles + j, 0)
        x_map = lambda b, j: (first + b * tiles + j, 0)
        seq_map4 = lambda b, j: (b, 0, 0, 0)
        seq_map2 = lambda b, j: (b, 0)
        const2 = lambda b, j: (0, 0)
        const3 = lambda b, j: (0, 0, 0)
        seq_block = 1
    else:
        assert length == 8 and n_rows % rows == 0
        seq_block = rows // 8
        grid = (n_rows // rows,)
        row_map = lambda i: (i, 0)
        x_map = lambda i: (first + i, 0)
        seq_map4 = lambda i: (i, 0, 0, 0)
        seq_map2 = lambda i: (i, 0)
        const2 = lambda i: (0, 0)
        const3 = lambda i: (0, 0, 0)
    d_in = w_in.shape[1]
    mix, s_new, pa_new, pb_new = pl.pallas_call(
        functools.partial(_mixer_ab_kernel, chain=chain, pos0=pos0),
        grid=grid,
        in_specs=[
            pl.BlockSpec((rows, D_MODEL), x_map),
            pl.BlockSpec((D_MODEL, d_in), const2),
            pl.BlockSpec((1, HG_WIDTH), const2),
            pl.BlockSpec((1, HG_D), const2),
            pl.BlockSpec((len(POOL_WINDOWS), POOL_GC, POOL_GC), const3),
            pl.BlockSpec((1, POOL_WIDTH), const2),
            pl.BlockSpec((seq_block, HG_HEADS, HG_D, HG_D), seq_map4),
            pl.BlockSpec((seq_block * 8, POOL_WIDTH), seq_map2),
            pl.BlockSpec((seq_block * 8, POOL_WIDTH), seq_map2),
        ],
        out_specs=[
            pl.BlockSpec((rows, MIX_OUT), row_map),
            pl.BlockSpec((seq_block, HG_HEADS, HG_D, HG_D), seq_map4),
            pl.BlockSpec((seq_block * 8, POOL_WIDTH), seq_map2),
            pl.BlockSpec((seq_block * 8, POOL_WIDTH), seq_map2),
        ],
        out_shape=[
            jax.ShapeDtypeStruct((n_rows, MIX_OUT), jnp.bfloat16),
            jax.ShapeDtypeStruct(s0.shape, jnp.float32),
            jax.ShapeDtypeStruct(pa.shape, jnp.float32),
            jax.ShapeDtypeStruct(pb.shape, jnp.float32),
        ],
        scratch_shapes=[
            pltpu.VMEM((HG_HEADS, HG_D, HG_D), jnp.float32),
            pltpu.VMEM((16, POOL_WIDTH), jnp.float32),
        ],
        compiler_params=pltpu.CompilerParams(
            dimension_semantics=("arbitrary",) * len(grid), vmem_limit_bytes=VMEM_LIMIT),
        name="mixer_ab_chain" if chain else "mixer_ab_batch",
    )(x, w_in, lb.reshape(1, -1), norm_w.reshape(1, -1), pool_w, pool_scale.reshape(1, -1), s0, pa, pb)
    pool_new = jnp.concatenate([pa_new.reshape(n_seq, 8, POOL_WIDTH), pb_new.reshape(n_seq, 8, POOL_WIDTH)], axis=1)
    return mix, s_new, pool_new[:, 1:]


SC_WIDTH, SC_K = 512, 3
GD_HEADS, GD_D = 4, 128
GD_CONV = 4
GD_CHUNK = 64
GD_QKV = GD_HEADS * 3 * GD_D
CD_MAIN = 3 * SC_WIDTH + GD_QKV + GD_HEADS * GD_D


def _causal_conv(cur, prev, w_ref, shift_fn):
    width = w_ref.shape[0]
    acc = cur * w_ref[width - 1:width, :]
    for j in range(width - 1):
        acc = acc + shift_fn(cur, prev, width - 1 - j) * w_ref[j:j + 1, :]
    return acc


def _softplus(v):
    return jnp.maximum(v, 0.0) + jnp.log1p(jnp.exp(-jnp.abs(v)))


def _mixer_cd_kernel(x_ref, w_ref, wg_ref, scw_ref, gcw_ref, alog_ref, dtb_ref, nw_ref, s0_ref, scp_ref, gcp_ref,
                     mix_ref, s_out_ref, scp_out_ref, gcp_out_ref, state_ref, sc_prev_ref, gc_prev_ref, *, chain):
    rows = x_ref.shape[0]
    chunk = GD_CHUNK if chain else 8
    n_chunks = rows // chunk
    j = pl.program_id(1) if chain else 0
    shift_fn = _shift_rows_chain if chain else _shift_rows_batch

    xb = _bf(x_ref[...])
    proj = _dot(xb, w_ref[...])
    gates = _dot(xb, wg_ref[...])
    b_gate = proj[:, 0:SC_WIDTH]
    conv_in = proj[:, SC_WIDTH:2 * SC_WIDTH] * proj[:, 2 * SC_WIDTH:3 * SC_WIDTH]
    qkv = proj[:, 3 * SC_WIDTH:3 * SC_WIDTH + GD_QKV]
    z = proj[:, 3 * SC_WIDTH + GD_QKV:]

    if chain:
        @pl.when(j == 0)
        def _():
            state_ref[...] = s0_ref[0]
            sc_prev_ref[...] = scp_ref[...]
            gc_prev_ref[...] = gcp_ref[...]
        sc_prev, gc_prev = sc_prev_ref[...], gc_prev_ref[...]
    else:
        sc_prev, gc_prev = scp_ref[...], gcp_ref[...]

    outs = [b_gate * _causal_conv(conv_in, sc_prev, scw_ref, shift_fn)]
    qkv_c = _silu(_causal_conv(qkv, gc_prev, gcw_ref, shift_fn))

    lane8 = lax.broadcasted_iota(jnp.int32, gates.shape, 1)
    g_dec = -jnp.exp(alog_ref[...]) * _softplus(gates + dtb_ref[...])
    cols = jnp.where(lane8 < GD_HEADS, g_dec, jax.nn.sigmoid(gates))
    same, incl, strict = _chunk_masks(rows, chunk)
    cols3 = _split3(cols)
    gcum = _exact_mask_dot(_dot, _mask_bf16(incl), cols3)
    gtot = _exact_mask_dot(_dot, _mask_bf16(same), cols3)
    eye8 = _mask_bf16(lax.broadcasted_iota(jnp.int32, (8, 8), 0) == lax.broadcasted_iota(jnp.int32, (8, 8), 1))
    gcum_rows = _exact_mask_dot(_dot_nt, eye8, _split3(gcum))

    eye = (lax.broadcasted_iota(jnp.int32, (rows, rows), 0)
           == lax.broadcasted_iota(jnp.int32, (rows, rows), 1)).astype(jnp.float32)
    hd = []
    for h in range(GD_HEADS):
        sl = lambda part: slice(part * GD_HEADS * GD_D + h * GD_D, part * GD_HEADS * GD_D + (h + 1) * GD_D)
        q, k, v = qkv_c[:, sl(0)], qkv_c[:, sl(1)], qkv_c[:, sl(2)]
        q = q * lax.rsqrt(jnp.sum(q * q, axis=-1, keepdims=True) + 1e-6) * (GD_D ** -0.5)
        k = k * lax.rsqrt(jnp.sum(k * k, axis=-1, keepdims=True) + 1e-6)
        beta = cols[:, GD_HEADS + h:GD_HEADS + h + 1]
        gc = gcum[:, h:h + 1]
        gt = gtot[:, h:h + 1]
        decay = jnp.where(incl, jnp.exp(jnp.where(incl, gc - gcum_rows[h:h + 1, :], 0.0)), 0.0)
        k_beta = k * beta
        kb = _bf(k)
        egc = jnp.exp(gc)
        power = -jnp.where(strict, _dot_nt(_bf(k_beta), kb) * decay, 0.0)
        hd.append(dict(
            inv=eye + power, power=_bf(power),
            att=_bf(jnp.where(incl, _dot_nt(_bf(q), kb) * decay, 0.0)),
            rhs=_split2(jnp.concatenate([v * beta, k_beta * egc], axis=1)),
            q_dec=q * egc, k_end=k * jnp.exp(gt - gc),
            decay_end=jnp.exp(gt), inter=[], v_new=[], s=None))

    for _ in range(chunk.bit_length() - 2):
        for d in hd:
            d["power"] = _bf(_dot(d["power"], d["power"]))
        for d in hd:
            d["inv"] = d["inv"] + _dot(_bf(d["inv"]), d["power"])
    for d in hd:
        uw = _dot3(_split2(d["inv"]), d["rhs"])
        d["u"], d["w"] = uw[:, :GD_D], uw[:, GD_D:]

    for c in range(n_chunks):
        rs = slice(c * chunk, (c + 1) * chunk)
        for h, d in enumerate(hd):
            if chain:
                s = state_ref[h] if c == 0 else d["s"]
            else:
                s = s0_ref[c, h]
            sb = _bf(s)
            v_new = d["u"][rs] - _dot(_bf(d["w"][rs]), sb)
            d["inter"].append(_dot(_bf(d["q_dec"][rs]), sb))
            d["s"] = d["decay_end"][c * chunk:c * chunk + 1, :] * s + _dot_tn(_bf(d["k_end"][rs]), _bf(v_new))
            d["v_new"].append(v_new)
            if not chain:
                s_out_ref[c, h] = d["s"]
    for h, d in enumerate(hd):
        if chain:
            state_ref[h] = d["s"]
        o = jnp.concatenate(d["inter"], axis=0) + _dot(d["att"], _bf(jnp.concatenate(d["v_new"], axis=0)))
        outs.append(_rms_gate(o, nw_ref[...], z[:, h * GD_D:(h + 1) * GD_D]))
    mix_ref[...] = jnp.concatenate(outs, axis=-1).astype(mix_ref.dtype)

    if chain:
        sc_prev_ref[...] = conv_in[rows - 8:]
        gc_prev_ref[...] = qkv[rows - 8:]

        @pl.when(j == pl.num_programs(1) - 1)
        def _():
            s_out_ref[0] = state_ref[...]
            scp_out_ref[...] = conv_in[rows - 8:]
            gcp_out_ref[...] = qkv[rows - 8:]
    else:
        scp_out_ref[...] = conv_in
        gcp_out_ref[...] = qkv


def _mixer_cd(x, row0, n_seq, length, s0, sconv_state, gconv_state, w_main, w_gates, sconv_w, gconv_w, a_log, dt_bias,
              norm_w):
    n_rows = n_seq * length
    chain = length >= ROW_TILE
    rows = ROW_TILE if chain else BATCH_ROW_TILE
    assert row0 % rows == 0
    first = row0 // rows
    pad8 = lambda st: jnp.pad(st, ((0, 0), (8 - st.shape[1], 0), (0, 0))).reshape(n_seq * 8, st.shape[2])
    scp, gcp = pad8(sconv_state), pad8(gconv_state)
    if chain:
        assert length % rows == 0
        tiles = length // rows
        grid = (n_seq, tiles)
        row_map = lambda b, j: (b * tiles + j, 0)
        x_map = lambda b, j: (first + b * tiles + j, 0)
        seq_map4 = lambda b, j: (b, 0, 0, 0)
        seq_map2 = lambda b, j: (b, 0)
        const2 = lambda b, j: (0, 0)
        seq_block = 1
    else:
        assert length == 8 and n_rows % rows == 0
        seq_block = rows // 8
        grid = (n_rows // rows,)
        row_map = lambda i: (i, 0)
        x_map = lambda i: (first + i, 0)
        seq_map4 = lambda i: (i, 0, 0, 0)
        seq_map2 = lambda i: (i, 0)
        const2 = lambda i: (0, 0)
    zeros4 = jnp.zeros((GD_HEADS,), jnp.float32)
    alog8 = jnp.concatenate([a_log.astype(jnp.float32), zeros4]).reshape(1, 8)
    dtb8 = jnp.concatenate([dt_bias.astype(jnp.float32), zeros4]).reshape(1, 8)
    mix, s_new, scp_new, gcp_new = pl.pallas_call(
        functools.partial(_mixer_cd_kernel, chain=chain),
        grid=grid,
        in_specs=[
            pl.BlockSpec((rows, D_MODEL), x_map),
            pl.BlockSpec((D_MODEL, CD_MAIN), const2),
            pl.BlockSpec((D_MODEL, 8), const2),
            pl.BlockSpec((SC_K, SC_WIDTH), const2),
            pl.BlockSpec((GD_CONV, GD_QKV), const2),
            pl.BlockSpec((1, 8), const2),
            pl.BlockSpec((1, 8), const2),
            pl.BlockSpec((1, GD_D), const2),
            pl.BlockSpec((seq_block, GD_HEADS, GD_D, GD_D), seq_map4),
            pl.BlockSpec((seq_block * 8, SC_WIDTH), seq_map2),
            pl.BlockSpec((seq_block * 8, GD_QKV), seq_map2),
        ],
        out_specs=[
            pl.BlockSpec((rows, MIX_OUT), row_map),
            pl.BlockSpec((seq_block, GD_HEADS, GD_D, GD_D), seq_map4),
            pl.BlockSpec((seq_block * 8, SC_WIDTH), seq_map2),
            pl.BlockSpec((seq_block * 8, GD_QKV), seq_map2),
        ],
        out_shape=[
            jax.ShapeDtypeStruct((n_rows, MIX_OUT), jnp.bfloat16),
            jax.ShapeDtypeStruct(s0.shape, jnp.float32),
            jax.ShapeDtypeStruct(scp.shape, jnp.float32),
            jax.ShapeDtypeStruct(gcp.shape, jnp.float32),
        ],
        scratch_shapes=[
            pltpu.VMEM((GD_HEADS, GD_D, GD_D), jnp.float32),
            pltpu.VMEM((8, SC_WIDTH), jnp.float32),
            pltpu.VMEM((8, GD_QKV), jnp.float32),
        ],
        compiler_params=pltpu.CompilerParams(
            dimension_semantics=("arbitrary",) * len(grid), vmem_limit_bytes=VMEM_LIMIT),
        name="mixer_cd_chain" if chain else "mixer_cd_batch",
    )(x, w_main, w_gates, sconv_w, gconv_w, alog8, dtb8, norm_w.reshape(1, -1), s0, scp, gcp)
    tail = lambda st, keep: st.reshape(n_seq, 8, -1)[:, 8 - keep:]
    return mix, s_new, tail(scp_new, SC_K - 1), tail(gcp_new, GD_CONV - 1)


def _outproj_ln_router_kernel(mix_a_ref, mix_b_ref, x_a_ref, x_b_ref, w_ref, g_ref, b_ref, rwt_ref, rb_ref,
                              x1_ref, xp_ref, tope_ref, gate_ref, rank_ref, cnt_ref, linv_ref, *, group_tiles, split):
    from_a = pl.program_id(0) < split
    mix_in = jnp.where(from_a, mix_a_ref[...], mix_b_ref[...])
    x_in = jnp.where(from_a, x_a_ref[...], x_b_ref[...])
    mix = jnp.dot(mix_in, w_ref[...], preferred_element_type=jnp.float32)
    x1 = _layer_norm_rows(DEEPNORM_ALPHA * x_in + mix, g_ref[...], b_ref[...])
    x1_ref[...] = x1
    packed = _pack_bf16_pairs(x1)
    for j in range(HALF // 128):
        xp_ref[:, j * 8:(j + 1) * 8, :] = packed[:, j * 128:(j + 1) * 128].reshape(x1.shape[0] // 8, 8, 128)
    logits = _dot_nt(_bf(rwt_ref[...]), _bf(x1)) + rb_ref[...]
    tm = logits.shape[1]
    expert = lax.broadcasted_iota(jnp.int32, logits.shape, 0)
    vals, idxs = [], []
    for _ in range(TOP_K):
        m = jnp.max(logits, axis=0, keepdims=True)
        idx = jnp.min(jnp.where(logits == m, expert, N_EXPERTS), axis=0, keepdims=True)
        vals.append(m)
        idxs.append(idx)
        logits = jnp.where(expert == idx, -jnp.inf, logits)
    ex = [jnp.exp(v - vals[0]) for v in vals]
    den = ex[0] + ex[1] + ex[2] + ex[3]
    hits = [expert == idx for idx in idxs]
    member = sum(h.astype(jnp.float32) for h in hits)
    earlier = (lax.broadcasted_iota(jnp.int32, (tm, tm), 0) < lax.broadcasted_iota(jnp.int32, (tm, tm), 1))
    before = _dot(member.astype(jnp.bfloat16), earlier.astype(jnp.bfloat16))
    cnt_ref[0] = jnp.sum(member, axis=1, keepdims=True).astype(jnp.int32)
    slot = lax.broadcasted_iota(jnp.int32, tope_ref.shape[1:], 0)
    tope = jnp.zeros(slot.shape, jnp.int32)
    gate = jnp.zeros(slot.shape, jnp.float32)
    rank = jnp.zeros(slot.shape, jnp.int32)
    for k in range(TOP_K):
        tope = jnp.where(slot == k, idxs[k], tope)
        gate = jnp.where(slot == k, ex[k] / den, gate)
        rank_k = jnp.sum(jnp.where(hits[k], before, 0.0), axis=0, keepdims=True).astype(jnp.int32)
        rank = jnp.where(slot == k, rank_k, rank)
    tope_ref[0] = tope
    gate_ref[0] = gate
    rank_ref[0] = rank

    lower = (lax.broadcasted_iota(jnp.int32, (tm, tm), 1) < lax.broadcasted_iota(jnp.int32, (tm, tm), 0))
    member_bf = member.astype(jnp.bfloat16)
    before_cols = _dot_nt(lower.astype(jnp.bfloat16), member_bf)
    eye = (lax.broadcasted_iota(jnp.int32, (tm, tm), 0) == lax.broadcasted_iota(jnp.int32, (tm, tm), 1))
    member_cols = _dot_nt(eye.astype(jnp.bfloat16), member_bf)
    key = jnp.where(member_cols > 0.5, before_cols, -1.0)
    order = lax.broadcasted_iota(jnp.int32, (tm, tm), 1).astype(jnp.float32)
    token = lax.broadcasted_iota(jnp.int32, (8, tm), 1).astype(jnp.bfloat16)
    expert8 = lax.broadcasted_iota(jnp.int32, (N_EXPERTS, tm), 0)
    listed = jnp.zeros((N_EXPERTS, tm), jnp.float32)
    for e in range(N_EXPERTS):
        onehot = (key[:, e:e + 1] == order).astype(jnp.bfloat16)
        listed = jnp.where(expert8 == e, _dot(token, onehot)[0:1, :], listed)
    local = listed.astype(jnp.int32)
    first = lax.rem(pl.program_id(0), group_tiles) * tm
    linv_ref[0] = ((first + local) >> 3) * (8 * (HALF // 128)) + (local & 7)


def _outproj_ln_router(mix_a, mix_b, x_a, x_b, n, n_a, w_out, ln_g, ln_b, router_w, router_b):
    tm = TOKEN_TILE
    split = n_a // tm
    assert n_a % tm == 0 and x_a[1] % tm == 0 and x_b[1] % tm == 0
    row = lambda i: (i, 0)
    full = lambda i: (0, 0)
    part_a = lambda first: (lambda i: (first // tm + jnp.minimum(i, split - 1), 0))
    part_b = lambda first: (lambda i: (first // tm + jnp.maximum(i - split, 0), 0))
    return pl.pallas_call(
        functools.partial(_outproj_ln_router_kernel, group_tiles=n // tm // N_GROUPS, split=split),
        grid=(n // tm,),
        in_specs=[
            pl.BlockSpec((tm, D_MODEL), part_a(0)),
            pl.BlockSpec((tm, D_MODEL), part_b(0)),
            pl.BlockSpec((tm, D_MODEL), part_a(x_a[1])),
            pl.BlockSpec((tm, D_MODEL), part_b(x_b[1])),
            pl.BlockSpec((D_MODEL, D_MODEL), full),
            pl.BlockSpec((1, D_MODEL), full),
            pl.BlockSpec((1, D_MODEL), full),
            pl.BlockSpec((N_EXPERTS, D_MODEL), full),
            pl.BlockSpec((N_EXPERTS, 1), full),
        ],
        out_specs=[
            pl.BlockSpec((tm, D_MODEL), row),
            pl.BlockSpec((tm // 8, HALF // 16, 128), lambda i: (i, 0, 0)),
            pl.BlockSpec((1, TOP_K, tm), lambda i: (i, 0, 0)),
            pl.BlockSpec((1, TOP_K, tm), lambda i: (i, 0, 0)),
            pl.BlockSpec((1, TOP_K, tm), lambda i: (i, 0, 0)),
            pl.BlockSpec((1, N_EXPERTS, 1), lambda i: (i, 0, 0)),
            pl.BlockSpec((1, N_EXPERTS, tm), lambda i: (i, 0, 0)),
        ],
        out_shape=[
            jax.ShapeDtypeStruct((n, D_MODEL), jnp.float32),
            jax.ShapeDtypeStruct((n // 8, HALF // 16, 128), jnp.uint32),
            jax.ShapeDtypeStruct((n // tm, TOP_K, tm), jnp.int32),
            jax.ShapeDtypeStruct((n // tm, TOP_K, tm), jnp.float32),
            jax.ShapeDtypeStruct((n // tm, TOP_K, tm), jnp.int32),
            jax.ShapeDtypeStruct((n // tm, N_EXPERTS, 1), jnp.int32),
            jax.ShapeDtypeStruct((n // tm, N_EXPERTS, tm), jnp.int32),
        ],
        compiler_params=pltpu.CompilerParams(dimension_semantics=("arbitrary",), vmem_limit_bytes=VMEM_LIMIT),
        name="outproj_ln_router",
    )(mix_a, mix_b, x_a[0], x_b[0], w_out, ln_g.reshape(1, -1), ln_b.reshape(1, -1), router_w.T,
      router_b.reshape(-1, 1))


def _route_plan(top_e, rank, tile_counts):
    n = top_e.shape[0] * top_e.shape[2]
    a = n * TOP_K
    br, tm, ch = EXPERT_ROWS, TOKEN_TILE, COMBINE_CHUNK
    n_virtual = N_EXPERTS * N_GROUPS
    group_tokens = n // N_GROUPS
    n_blocks = a // br + n_virtual + 1
    n_tiles = n // tm
    group_tiles = n_tiles // N_GROUPS
    max_chunks = (tm * TOP_K) // ch + N_EXPERTS + 1
    i32 = jnp.int32

    cnt = tile_counts.reshape(N_GROUPS, group_tiles, N_EXPERTS)
    nblk = (cnt.sum(axis=1) + br - 1) // br
    blk_end = jnp.cumsum(nblk.reshape(-1))
    row_off = ((blk_end - nblk.reshape(-1)) * br).reshape(N_GROUPS, 1, N_EXPERTS)
    seg_start = (row_off + jnp.cumsum(cnt, axis=1) - cnt).reshape(n_tiles, N_EXPERTS)
    seg_len = tile_counts
    blocks = jnp.arange(n_blocks, dtype=i32)
    block_ve = jnp.minimum(jnp.sum(blk_end[None, :] <= blocks[:, None], axis=1), n_virtual - 1).astype(i32)
    valid = blocks < blk_end[-1]
    run_ve = jnp.where(valid, block_ve, block_ve[jnp.maximum(blk_end[-1] - 1, 0)])
    block_e = run_ve % N_EXPERTS
    starts = jnp.concatenate([jnp.ones((1,), bool), block_e[1:] != block_e[:-1]])
    later_start = lax.cummin(jnp.where(starts, blocks, n_blocks)[::-1], axis=0)[::-1]
    next_start = jnp.concatenate([later_start[1:], jnp.full((1,), n_blocks, i32)])
    next_e = jnp.where(next_start < n_blocks, block_e[jnp.minimum(next_start, n_blocks - 1)], -1)
    block_info = jnp.stack([block_e, run_ve // N_EXPERTS, valid.astype(i32), starts.astype(i32), next_e.astype(i32)])

    aligned = seg_start // 8 * 8
    lead = seg_start - aligned
    nch = jnp.where(seg_len > 0, (lead + seg_len + ch - 1) // ch, 0)
    ch_end = jnp.cumsum(nch, axis=1)
    ch_first = ch_end - nch
    n_chunks = ch_end[:, -1]
    cidx = jnp.arange(max_chunks, dtype=i32)[None, :, None]
    owns = jnp.logical_and(ch_first[:, None, :] <= cidx, cidx < ch_end[:, None, :])
    chunk_src = jnp.sum(jnp.where(owns, aligned[:, None, :] + (cidx - ch_first[:, None, :]) * ch, 0), axis=2)

    hit = top_e[..., None] == jnp.arange(N_EXPERTS, dtype=i32)
    pick = lambda table: jnp.sum(jnp.where(hit, table[:, None, None, :], 0), axis=-1)
    pos = pick(ch_first * ch + lead) + rank
    cnt_ve = cnt.transpose(0, 2, 1).reshape(n_virtual, group_tiles)
    passed_rows = jnp.cumsum(cnt_ve, axis=1)[run_ve]
    offset = (blocks * br - row_off.reshape(-1)[run_ve])[:, None] + jnp.arange(br, dtype=i32)[None, :]
    passed = passed_rows[:, None, :] <= offset[:, :, None]
    tile_idx = jnp.minimum(jnp.sum(passed, axis=2), group_tiles - 1)
    place = jnp.clip(offset - jnp.sum(jnp.where(passed, cnt_ve[run_ve][:, None, :], 0), axis=2), 0, tm - 1)
    row_place = (tile_idx * tm + place).astype(i32)
    pos = (pos // 8) * (8 * LANE_TILES) + pos % 8
    return dict(row_place=row_place.reshape(n_blocks, 1, br), block_info=block_info,
                chunk_src=(chunk_src * LANE_TILES).astype(i32), n_chunks=n_chunks.astype(i32),
                pos=pos.reshape(n_tiles, 1, tm * TOP_K).astype(i32))


def _expert_kernel(info_ref, place_ref, place_next_ref, list_ref, list_next_ref, xq_hbm, wgu_hbm, bgu_ref, wdn_hbm,
                   bdn_ref, y_ref, xq_vmem, stage_ref, wgu_f32_ref, wdn_f32_ref, wgu_ref, wdn_ref, load_sem, w_sems, *,
                   layer):
    b = pl.program_id(0)
    grp = info_ref[1, b]
    prev_grp = info_ref[1, jnp.maximum(b - 1, 0)]

    def weight_copies(expert):
        return (pltpu.make_async_copy(wgu_hbm.at[layer, expert], wgu_f32_ref, w_sems.at[0]),
                pltpu.make_async_copy(wdn_hbm.at[layer, expert], wdn_f32_ref, w_sems.at[1]))

    @pl.when(b == 0)
    def _():
        for cp in weight_copies(info_ref[0, 0]):
            cp.start()

    @pl.when(info_ref[3, b] == 1)
    def _():
        for cp in weight_copies(info_ref[0, b]):
            cp.wait()
        rows = D_MODEL // 8

        def cast(i, carry):
            r0 = pl.multiple_of(i * rows, rows)
            wgu_ref[pl.ds(r0, rows), :] = wgu_f32_ref[pl.ds(r0, rows), :].astype(jnp.bfloat16)
            wdn_ref[pl.ds(r0, rows), :] = wdn_f32_ref[pl.ds(r0, rows), :].astype(jnp.bfloat16)
            return carry

        lax.fori_loop(0, 8, cast, 0)

        @pl.when(info_ref[4, b] >= 0)
        def _():
            for cp in weight_copies(info_ref[4, b]):
                cp.start()

    group_rows = xq_vmem.shape[0]
    slot = b % 2
    tiles = HALF // 128

    def gather_row(addr, dst_slot, group8, sub):
        stage_ref[dst_slot, group8, pl.ds(sub, tiles, stride=8), :] = xq_vmem[pl.ds(addr, tiles, stride=8), :]

    @pl.when(jnp.logical_or(b == 0, grp != prev_grp))
    def _():
        cp = pltpu.make_async_copy(xq_hbm.at[pl.ds(grp * group_rows, group_rows)], xq_vmem, load_sem)
        cp.start()
        cp.wait()

        def gather(i, carry):
            for u in range(8):
                gather_row(list_ref[0, 0, 0, place_ref[0, 0, i * 8 + u]], slot, i, u)
            return carry

        lax.fori_loop(0, EXPERT_ROWS // 8, gather, 0)

    @pl.when(info_ref[2, b] == 1)
    def _():
        for r in range(EXPERT_ROWS):
            gather_row(list_next_ref[0, 0, 0, place_next_ref[0, 0, r]], 1 - slot, r // 8, r % 8)
        pieces = [_unpack_bf16_pairs(stage_ref[slot, :, j * 8:(j + 1) * 8, :].reshape(EXPERT_ROWS, 128))
                  for j in range(tiles)]
        lo = jnp.concatenate([p[0] for p in pieces], axis=1)
        hi = jnp.concatenate([p[1] for p in pieces], axis=1)
        hgu = (jnp.dot(lo, wgu_ref[:HALF, :], preferred_element_type=jnp.float32)
               + jnp.dot(hi, wgu_ref[HALF:, :], preferred_element_type=jnp.float32) + bgu_ref[0, 0])
        glu = jnp.minimum(hgu[:, :D_FF], SWIGLU_LIMIT)
        lin = jnp.clip(hgu[:, D_FF:], -SWIGLU_LIMIT, SWIGLU_LIMIT)
        act = (lin + 1.0) * glu * jax.nn.sigmoid(SWIGLU_ALPHA * glu)
        y = jnp.dot(act.astype(jnp.bfloat16), wdn_ref[...], preferred_element_type=jnp.float32) + bdn_ref[0, 0]
        for c in range(LANE_TILES):
            y_ref[:, c * 8:(c + 1) * 8, :] = y[:, c * 128:(c + 1) * 128].reshape(EXPERT_ROWS // 8, 8, 128)

    @pl.when(info_ref[2, b] == 0)
    def _():
        y_ref[...] = jnp.zeros_like(y_ref)


def _expert_mlp(plan, xq, token_lists, layer, w_gu, b_gu, w_dn, b_dn):
    br = EXPERT_ROWS
    n_blocks = plan["row_place"].shape[0]
    assert br >= COMBINE_CHUNK + 8
    pad_rows = n_blocks * br
    expert = lambda b, info: (layer, info[0, b], 0, 0)
    following = lambda b: jnp.minimum(b + 1, n_blocks - 1)
    list_block = (1, 1, 1, token_lists.shape[3])
    grid_spec = pltpu.PrefetchScalarGridSpec(
        num_scalar_prefetch=1,
        grid=(n_blocks,),
        in_specs=[
            pl.BlockSpec((1, 1, br), lambda b, info: (b, 0, 0), memory_space=pltpu.SMEM),
            pl.BlockSpec((1, 1, br), lambda b, info: (following(b), 0, 0), memory_space=pltpu.SMEM),
            pl.BlockSpec(list_block, lambda b, info: (info[0, b], info[1, b], 0, 0), memory_space=pltpu.SMEM),
            pl.BlockSpec(list_block, lambda b, info: (info[0, following(b)], info[1, following(b)], 0, 0),
                         memory_space=pltpu.SMEM),
            pl.BlockSpec(memory_space=pl.ANY),
            pl.BlockSpec(memory_space=pl.ANY),
            pl.BlockSpec((1, 1, 1, 2 * D_FF), expert),
            pl.BlockSpec(memory_space=pl.ANY),
            pl.BlockSpec((1, 1, 1, D_MODEL), expert),
        ],
        out_specs=pl.BlockSpec((br // 8, 8 * LANE_TILES, 128), lambda b, info: (b, 0, 0)),
        scratch_shapes=[
            pltpu.VMEM((xq.shape[0] // N_GROUPS, 128), jnp.uint32),
            pltpu.VMEM((2, br // 8, 8 * (HALF // 128), 128), jnp.uint32),
            pltpu.VMEM((D_MODEL, 2 * D_FF), jnp.float32),
            pltpu.VMEM((D_FF, D_MODEL), jnp.float32),
            pltpu.VMEM((D_MODEL, 2 * D_FF), jnp.bfloat16),
            pltpu.VMEM((D_FF, D_MODEL), jnp.bfloat16),
            pltpu.SemaphoreType.DMA(()),
            pltpu.SemaphoreType.DMA((2,)),
        ],
    )
    return pl.pallas_call(
        functools.partial(_expert_kernel, layer=layer),
        grid_spec=grid_spec,
        out_shape=jax.ShapeDtypeStruct((pad_rows // 8, 8 * LANE_TILES, 128), jnp.float32),
        compiler_params=pltpu.CompilerParams(dimension_semantics=("arbitrary",), vmem_limit_bytes=VMEM_LIMIT),
        name="expert_mlp",
    )(plan["block_info"], plan["row_place"], plan["row_place"], token_lists, token_lists, xq, w_gu,
      b_gu.reshape(DEPTH, N_EXPERTS, 1, -1), w_dn, b_dn.reshape(DEPTH, N_EXPERTS, 1, -1))


def _combine_ln_kernel(src_ref, nch_ref, pos_ref, gate_ref, x1_ref, y_hbm, g_ref, b_ref, out_ref,
                       stage_ref, ffn_ref, sems):
    i = pl.program_id(0)
    n_tiles = pl.num_programs(0)
    chunk_rows = COMBINE_CHUNK * LANE_TILES

    def chunk_copy(tile, slot, c):
        src = pl.multiple_of(src_ref[tile, c], 8 * LANE_TILES)
        return pltpu.make_async_copy(y_hbm.at[pl.ds(src, chunk_rows)],
                                     stage_ref.at[slot, pl.ds(c * chunk_rows, chunk_rows)], sems.at[slot])

    def issue(tile, slot):
        lax.fori_loop(0, nch_ref[tile], lambda c, carry: (chunk_copy(tile, slot, c).start(), carry)[1], 0)

    @pl.when(i == 0)
    def _():
        issue(0, 0)

    @pl.when(i + 1 < n_tiles)
    def _():
        issue(i + 1, (i + 1) % 2)

    slot = i % 2
    lax.fori_loop(0, nch_ref[i], lambda c, carry: (chunk_copy(i, slot, c).wait(), carry)[1], 0)

    tm = x1_ref.shape[0]

    def combine(j, carry):
        for u in range(8):
            t = j * 8 + u
            acc = None
            for k in range(TOP_K):
                row = stage_ref[slot, pl.ds(pos_ref[0, 0, k * tm + t], LANE_TILES, stride=8), :]
                row = row * gate_ref[0, 0, k * tm + t]
                acc = row if acc is None else acc + row
            ffn_ref[j, pl.ds(u, LANE_TILES, stride=8), :] = acc
        return carry

    lax.fori_loop(0, tm // 8, combine, 0)
    ffn = jnp.concatenate([ffn_ref[:, c * 8:(c + 1) * 8, :].reshape(tm, 128) for c in range(LANE_TILES)], axis=1)
    out_ref[...] = _layer_norm_rows(DEEPNORM_ALPHA * x1_ref[...] + ffn, g_ref[...], b_ref[...])


def _combine_ln(plan, gate, x1, y_flat, ln_g, ln_b):
    n = x1.shape[0]
    tm = TOKEN_TILE
    n_tiles = n // tm
    max_chunks = plan["chunk_src"].shape[1]
    row = lambda i, src, nch: (i, 0)
    full = lambda i, src, nch: (0, 0)
    per_tile = lambda i, src, nch: (i, 0, 0)
    grid_spec = pltpu.PrefetchScalarGridSpec(
        num_scalar_prefetch=2,
        grid=(n_tiles,),
        in_specs=[
            pl.BlockSpec((1, 1, tm * TOP_K), per_tile, memory_space=pltpu.SMEM),
            pl.BlockSpec((1, 1, tm * TOP_K), per_tile, memory_space=pltpu.SMEM),
            pl.BlockSpec((tm, D_MODEL), row),
            pl.BlockSpec(memory_space=pl.ANY),
            pl.BlockSpec((1, D_MODEL), full),
            pl.BlockSpec((1, D_MODEL), full),
        ],
        out_specs=pl.BlockSpec((tm, D_MODEL), row),
        scratch_shapes=[
            pltpu.VMEM((2, max_chunks * COMBINE_CHUNK * LANE_TILES, 128), jnp.float32),
            pltpu.VMEM((tm // 8, 8 * LANE_TILES, 128), jnp.float32),
            pltpu.SemaphoreType.DMA((2,)),
        ],
    )
    return pl.pallas_call(
        _combine_ln_kernel,
        grid_spec=grid_spec,
        out_shape=jax.ShapeDtypeStruct((n, D_MODEL), jnp.float32),
        compiler_params=pltpu.CompilerParams(dimension_semantics=("arbitrary",), vmem_limit_bytes=VMEM_LIMIT),
        name="combine_ln",
    )(plan["chunk_src"], plan["n_chunks"], plan["pos"], gate.reshape(n_tiles, 1, tm * TOP_K), x1, y_flat,
      ln_g.reshape(1, -1), ln_b.reshape(1, -1))


def _post_mixer(mix_a, mix_b, x_a, x_b, layer, w_out, ln1_g, ln1_b, router_w, router_b, w_gu, b_gu, w_dn, b_dn, ln2_g,
                ln2_b):
    n_a, n = mix_a.shape[0], mix_a.shape[0] + mix_b.shape[0]
    x1, xp, top_e, gate, rank, tile_counts, lists = _outproj_ln_router(mix_a, mix_b, x_a, x_b, n, n_a, w_out, ln1_g, ln1_b,
                                                                       router_w, router_b)
    plan = _route_plan(top_e, rank, tile_counts.reshape(-1, N_EXPERTS))
    token_lists = lists.transpose(1, 0, 2).reshape(N_EXPERTS, N_GROUPS, 1, -1)
    y_sorted = _expert_mlp(plan, xp.reshape(-1, 128), token_lists, layer, w_gu, b_gu, w_dn, b_dn)
    return _combine_ln(plan, gate, x1, y_sorted.reshape(-1, 128), ln2_g, ln2_b)


def kernel(x_prompt, x_sample, state_hgrn, state_pool, state_sconv, state_gdn_conv, state_gdn, w_in_ab, hgrn_lower_bounds, hgrn_norm_w, pool_w, pool_scale, w_out_ab, w_in_cd, sconv_w, gdn_conv_w, gdn_a_log, gdn_dt_bias, gdn_norm_w, w_out_cd, ln1_g, ln1_b, ln2_g, ln2_b, router_w, router_b, w_gu, b_gu, w_dn, b_dn):
    bp, sp, _ = x_prompt.shape
    bs, ss, _ = x_sample.shape
    n_p, n_s = bp * sp, bs * ss
    bf = jnp.bfloat16
    f32 = jnp.float32
    lower_bounds = jnp.cumsum(jax.nn.softmax(hgrn_lower_bounds.astype(f32), axis=0), axis=0)
    xa, xb = (x_prompt.reshape(n_p, D_MODEL), 0), (x_sample.reshape(n_s, D_MODEL), 0)
    zeros_like_prompt = lambda st: jnp.zeros((bp,) + st.shape[2:], f32)

    states = {}
    for l in range(DEPTH):
        if l % 2 == 0:
            args = (w_in_ab[0].astype(bf), lower_bounds[l], hgrn_norm_w[0], pool_w[0].astype(bf), pool_scale[0])
            mp, hp, pp = _mixer_ab(*xa, bp, sp, 0, zeros_like_prompt(state_hgrn), zeros_like_prompt(state_pool), *args)
            ms, hs, ps = _mixer_ab(*xb, bs, ss, PAST_LEN, state_hgrn[0], state_pool[0], *args)
            states.update(hp=hp[None], hs=hs[None], pp=pp[None], ps=ps[None])
            w_out = w_out_ab[0]
        else:
            args = (w_in_cd[0][:, :CD_MAIN].astype(bf), w_in_cd[0][:, CD_MAIN:].astype(bf), sconv_w[0], gdn_conv_w[0],
                    gdn_a_log[0], gdn_dt_bias[0], gdn_norm_w[0])
            mp, gp, scp, gcp = _mixer_cd(*xa, bp, sp, zeros_like_prompt(state_gdn), zeros_like_prompt(state_sconv),
                                         zeros_like_prompt(state_gdn_conv), *args)
            ms, gs, scs, gcs = _mixer_cd(*xb, bs, ss, state_gdn[0], state_sconv[0], state_gdn_conv[0], *args)
            states.update(scp=scp[None], scs=scs[None], gcp=gcp[None], gcs=gcs[None], gp=gp[None], gs=gs[None])
            w_out = w_out_cd[0]
        x = _post_mixer(mp, ms, xa, xb, l, w_out.astype(bf), ln1_g[l], ln1_b[l], router_w[l], router_b[l],
                        w_gu, b_gu, w_dn, b_dn, ln2_g[l], ln2_b[l])
        xa, xb = (x, 0), (x, n_p)
    return (x[:n_p].reshape(bp, sp, D_MODEL), x[n_p:].reshape(bs, ss, D_MODEL),
            states["hp"], states["hs"], states["pp"], states["ps"], states["scp"], states["scs"],
            states["gcp"], states["gcs"], states["gp"], states["gs"])
```

```python
import functools

import jax
import jax.numpy as jnp
from jax import lax
from jax.experimental import pallas as pl
from jax.experimental.pallas import tpu as pltpu

D_MODEL = 1024
DEPTH = 2
N_EXPERTS = 32
TOP_K = 4
D_FF = 1024
SWIGLU_LIMIT = 7.0
SWIGLU_ALPHA = 1.702
LN_EPS = 1e-5
DEEPNORM_ALPHA = (2 * DEPTH) ** 0.25

HALF = D_MODEL // 2
LANE_TILES = D_MODEL // 128
TOKEN_TILE = 256
EXPERT_ROWS = 256
N_GROUPS = 1
COMBINE_CHUNK = 16
VMEM_LIMIT = 60 * 1024 * 1024
MIX_OUT = 1024
BATCH_ROW_TILE = 128
PAST_LEN = 16384


def _layer_norm_rows(v, g, b):
    mu = jnp.mean(v, axis=-1, keepdims=True)
    d = v - mu
    var = jnp.mean(d * d, axis=-1, keepdims=True)
    return d * lax.rsqrt(var + LN_EPS) * g + b


def _pack_bf16_pairs(v):
    lo = pltpu.bitcast(v[:, :HALF].astype(jnp.bfloat16).astype(jnp.float32), jnp.uint32)
    hi = pltpu.bitcast(v[:, HALF:].astype(jnp.bfloat16).astype(jnp.float32), jnp.uint32)
    return (lo >> 16) | (hi & jnp.uint32(0xFFFF0000))


def _unpack_bf16_pairs(p):
    lo = pltpu.bitcast(p << 16, jnp.float32).astype(jnp.bfloat16)
    hi = pltpu.bitcast(p & jnp.uint32(0xFFFF0000), jnp.float32).astype(jnp.bfloat16)
    return lo, hi


ROW_TILE = 256
HG_HEADS, HG_D = 4, 128
HG_WIDTH = HG_HEADS * HG_D
HG_CHUNK = 16
POOL_WINDOWS = (2, 4, 8, 16)
POOL_GC = 128
POOL_WIDTH = 512
RMS_EPS = 1e-6


def _silu(v):
    return v * jax.nn.sigmoid(v)


def _bf(v):
    return v.astype(jnp.bfloat16)


def _dot(a, b, precision=None):
    return jnp.dot(a, b, preferred_element_type=jnp.float32, precision=precision)


def _dot_nt(a, b, precision=None):
    return lax.dot_general(a, b, (((1,), (1,)), ((), ())), preferred_element_type=jnp.float32, precision=precision)


def _dot_tn(a, b, precision=None):
    return lax.dot_general(a, b, (((0,), (0,)), ((), ())), preferred_element_type=jnp.float32, precision=precision)


def _split3(v):
    hi = _bf(v)
    rest = v - hi.astype(jnp.float32)
    mid = _bf(rest)
    return hi, mid, _bf(rest - mid.astype(jnp.float32))


def _mask_bf16(mask):
    return _bf(mask.astype(jnp.float32))


def _exact_mask_dot(dot_fn, m, terms):
    return dot_fn(m, terms[0]) + (dot_fn(m, terms[1]) + dot_fn(m, terms[2]))


def _split2(v):
    hi = _bf(v)
    return hi, _bf(v - hi.astype(jnp.float32))


def _dot3(a2, b2):
    return _dot(a2[0], b2[0]) + (_dot(a2[0], b2[1]) + _dot(a2[1], b2[0]))


def _chunk_masks(rows, chunk):
    shift = chunk.bit_length() - 1
    t = lax.broadcasted_iota(jnp.int32, (rows, rows), 0)
    s = lax.broadcasted_iota(jnp.int32, (rows, rows), 1)
    same = (t >> shift) == (s >> shift)
    return same, jnp.logical_and(same, s <= t), jnp.logical_and(same, s < t)


def _shift_rows_chain(cur, prev, j):
    if j == 0:
        return cur
    p = prev.shape[0]
    rc = pltpu.roll(cur, j, 0)
    rp = prev if j == p else pltpu.roll(prev, j, 0)
    row = lax.broadcasted_iota(jnp.int32, (p, cur.shape[1]), 0)
    head = jnp.where(row < j, rp, rc[:p])
    return jnp.concatenate([head, rc[p:]], axis=0) if cur.shape[0] > p else head


def _shift_rows_batch(cur, prev, j):
    if j == 0:
        return cur
    if j == 8:
        return prev
    rows = cur.shape[0]
    row = lax.broadcasted_iota(jnp.int32, cur.shape, 0)
    return jnp.where((row & 7) < j, pltpu.roll(prev, rows + j - 8, 0), pltpu.roll(cur, j, 0))


def _rms_gate(o, norm_w, gate):
    ms = jnp.mean(o * o, axis=-1, keepdims=True)
    return o * lax.rsqrt(ms + RMS_EPS) * norm_w * _silu(gate)


def _mixer_ab_kernel(x_ref, w_ref, lb_ref, nw_ref, pw_ref, ps_ref, s0_ref, pa_ref, pb_ref,
                     mix_ref, s_out_ref, pa_out_ref, pb_out_ref, state_ref, prev_ref, *, chain, pos0):
    rows = x_ref.shape[0]
    chunk = HG_CHUNK if chain else 8
    n_chunks = rows // chunk
    j = pl.program_id(1) if chain else 0

    proj = _dot(_bf(x_ref[...]), w_ref[...])
    q = _silu(proj[:, 0:HG_WIDTH])
    f = lb_ref[...] + (1.0 - lb_ref[...]) * jax.nn.sigmoid(proj[:, HG_WIDTH:2 * HG_WIDTH])
    v = _silu(proj[:, 2 * HG_WIDTH:3 * HG_WIDTH])
    gate = proj[:, 3 * HG_WIDTH:4 * HG_WIDTH]
    u = proj[:, 4 * HG_WIDTH:]
    log_f = jnp.log(f)
    k = 1.0 - f

    same, incl, _ = _chunk_masks(rows, chunk)
    log_f3 = _split3(log_f)
    sums = _exact_mask_dot(_dot, jnp.concatenate([_mask_bf16(incl), _mask_bf16(same)], axis=0), log_f3)
    cum = sums[:rows]
    total = sums[rows:]
    q_dec = q * jnp.exp(cum)
    k_dec = k * jnp.exp(-cum)
    k_end = k * jnp.exp(total - cum)
    chunk_of_row = lax.broadcasted_iota(jnp.int32, (rows, 128), 0) >> (chunk.bit_length() - 1)
    onehot = _mask_bf16(chunk_of_row == lax.broadcasted_iota(jnp.int32, (rows, 128), 1))
    decay_cols = jnp.exp(_exact_mask_dot(lambda m, t: _dot_tn(t, m), onehot, log_f3))

    if chain:
        @pl.when(j == 0)
        def _():
            state_ref[...] = s0_ref[0]
            prev_ref[...] = jnp.concatenate([pa_ref[...], pb_ref[...]], axis=0)

    head_sl = [slice(h * HG_D, (h + 1) * HG_D) for h in range(HG_HEADS)]
    intra, inter, state = [], [[] for _ in head_sl], [None] * HG_HEADS
    for sl in head_sl:
        att = jnp.where(incl, _dot_nt(_bf(q_dec[:, sl]), _bf(k_dec[:, sl])), 0.0)
        intra.append(_dot(_bf(att), _bf(v[:, sl])))
    for c in range(n_chunks):
        rs = slice(c * chunk, (c + 1) * chunk)
        for h, sl in enumerate(head_sl):
            if chain:
                s = state_ref[h] if c == 0 else state[h]
            else:
                s = s0_ref[c, h]
            inter[h].append(_dot(_bf(q_dec[rs, sl]), _bf(s)))
            state[h] = decay_cols[sl, c:c + 1] * s + _dot_tn(_bf(k_end[rs, sl]), _bf(v[rs, sl]))
            if not chain:
                s_out_ref[c, h] = state[h]
    outs = []
    for h, sl in enumerate(head_sl):
        if chain:
            state_ref[h] = state[h]
        outs.append(_rms_gate(intra[h] + jnp.concatenate(inter[h], axis=0), nw_ref[...], gate[:, sl]))

    if chain:
        prev = prev_ref[...]
        t = j * rows + lax.broadcasted_iota(jnp.int32, (rows, POOL_GC), 0)
    else:
        prev_a, prev_b = pa_ref[...], pb_ref[...]
        t = lax.broadcasted_iota(jnp.int32, (rows, POOL_GC), 0) & 7
    for gi, win in enumerate(POOL_WINDOWS):
        gs = slice(gi * POOL_GC, (gi + 1) * POOL_GC)
        ug = u[:, gs]
        wsum = ug
        for d in range(1, win):
            if chain:
                wsum = wsum + _shift_rows_chain(ug, prev[:, gs], d)
            elif d <= 8:
                wsum = wsum + _shift_rows_batch(ug, prev_b[:, gs], d)
            else:
                wsum = wsum + _shift_rows_batch(prev_b[:, gs], prev_a[:, gs], d - 8)
        cnt = jnp.minimum(win, pos0 + t + 1).astype(jnp.float32)
        diff = wsum / cnt - ug
        outs.append(_dot(_bf(diff), pw_ref[gi]) * ps_ref[:, gs])
    mix_ref[...] = jnp.concatenate(outs, axis=-1).astype(mix_ref.dtype)

    if chain:
        prev_ref[...] = u[rows - 16:]

        @pl.when(j == pl.num_programs(1) - 1)
        def _():
            s_out_ref[0] = state_ref[...]
            pa_out_ref[...] = u[rows - 16:rows - 8]
            pb_out_ref[...] = u[rows - 8:]
    else:
        pa_out_ref[...] = prev_b
        pb_out_ref[...] = u


def _mixer_ab(x, row0, n_seq, length, pos0, s0, pool_state, w_in, lb, norm_w, pool_w, pool_scale):
    n_rows = n_seq * length
    chain = length >= ROW_TILE
    rows = ROW_TILE if chain else BATCH_ROW_TILE
    assert row0 % rows == 0
    first = row0 // rows
    pool16 = jnp.pad(pool_state, ((0, 0), (1, 0), (0, 0)))
    pa = pool16[:, :8].reshape(n_seq * 8, POOL_WIDTH)
    pb = pool16[:, 8:].reshape(n_seq * 8, POOL_WIDTH)
    if chain:
        assert length % rows == 0
        tiles = length // rows
        grid = (n_seq, tiles)
        row_map = lambda b, j: (b * tiles + j, 0)
        x_map = lambda b, j: (first + b * tiles + j, 0)
        seq_map4 = lambda b, j: (b, 0, 0, 0)
        seq_map2 = lambda b, j: (b, 0)
        const2 = lambda b, j: (0, 0)
        const3 = lambda b, j: (0, 0, 0)
        seq_block = 1
    else:
        assert length == 8 and n_rows % rows == 0
        seq_block = rows // 8
        grid = (n_rows // rows,)
        row_map = lambda i: (i, 0)
        x_map = lambda i: (first + i, 0)
        seq_map4 = lambda i: (i, 0, 0, 0)
        seq_map2 = lambda i: (i, 0)
        const2 = lambda i: (0, 0)
        const3 = lambda i: (0, 0, 0)
    d_in = w_in.shape[1]
    mix, s_new, pa_new, pb_new = pl.pallas_call(
        functools.partial(_mixer_ab_kernel, chain=chain, pos0=pos0),
        grid=grid,
        in_specs=[
            pl.BlockSpec((rows, D_MODEL), x_map),
            pl.BlockSpec((D_MODEL, d_in), const2),
            pl.BlockSpec((1, HG_WIDTH), const2),
            pl.BlockSpec((1, HG_D), const2),
            pl.BlockSpec((len(POOL_WINDOWS), POOL_GC, POOL_GC), const3),
            pl.BlockSpec((1, POOL_WIDTH), const2),
            pl.BlockSpec((seq_block, HG_HEADS, HG_D, HG_D), seq_map4),
            pl.BlockSpec((seq_block * 8, POOL_WIDTH), seq_map2),
            pl.BlockSpec((seq_block * 8, POOL_WIDTH), seq_map2),
        ],
        out_specs=[
            pl.BlockSpec((rows, MIX_OUT), row_map),
            pl.BlockSpec((seq_block, HG_HEADS, HG_D, HG_D), seq_map4),
            pl.BlockSpec((seq_block * 8, POOL_WIDTH), seq_map2),
            pl.BlockSpec((seq_block * 8, POOL_WIDTH), seq_map2),
        ],
        out_shape=[
            jax.ShapeDtypeStruct((n_rows, MIX_OUT), jnp.bfloat16),
            jax.ShapeDtypeStruct(s0.shape, jnp.float32),
            jax.ShapeDtypeStruct(pa.shape, jnp.float32),
            jax.ShapeDtypeStruct(pb.shape, jnp.float32),
        ],
        scratch_shapes=[
            pltpu.VMEM((HG_HEADS, HG_D, HG_D), jnp.float32),
            pltpu.VMEM((16, POOL_WIDTH), jnp.float32),
        ],
        compiler_params=pltpu.CompilerParams(
            dimension_semantics=("arbitrary",) * len(grid), vmem_limit_bytes=VMEM_LIMIT),
        name="mixer_ab_chain" if chain else "mixer_ab_batch",
    )(x, w_in, lb.reshape(1, -1), norm_w.reshape(1, -1), pool_w, pool_scale.reshape(1, -1), s0, pa, pb)
    pool_new = jnp.concatenate([pa_new.reshape(n_seq, 8, POOL_WIDTH), pb_new.reshape(n_seq, 8, POOL_WIDTH)], axis=1)
    return mix, s_new, pool_new[:, 1:]


SC_WIDTH, SC_K = 512, 3
GD_HEADS, GD_D = 4, 128
GD_CONV = 4
GD_CHUNK = 64
GD_QKV = GD_HEADS * 3 * GD_D
CD_MAIN = 3 * SC_WIDTH + GD_QKV + GD_HEADS * GD_D


def _causal_conv(cur, prev, w_ref, shift_fn):
    width = w_ref.shape[0]
    acc = cur * w_ref[width - 1:width, :]
    for j in range(width - 1):
        acc = acc + shift_fn(cur, prev, width - 1 - j) * w_ref[j:j + 1, :]
    return acc


def _softplus(v):
    return jnp.maximum(v, 0.0) + jnp.log1p(jnp.exp(-jnp.abs(v)))


def _mixer_cd_kernel(x_ref, w_ref, wg_ref, scw_ref, gcw_ref, alog_ref, dtb_ref, nw_ref, s0_ref, scp_ref, gcp_ref,
                     mix_ref, s_out_ref, scp_out_ref, gcp_out_ref, state_ref, sc_prev_ref, gc_prev_ref, *, chain):
    rows = x_ref.shape[0]
    chunk = GD_CHUNK if chain else 8
    n_chunks = rows // chunk
    j = pl.program_id(1) if chain else 0
    shift_fn = _shift_rows_chain if chain else _shift_rows_batch

    xb = _bf(x_ref[...])
    proj = _dot(xb, w_ref[...])
    gates = _dot(xb, wg_ref[...])
    b_gate = proj[:, 0:SC_WIDTH]
    conv_in = proj[:, SC_WIDTH:2 * SC_WIDTH] * proj[:, 2 * SC_WIDTH:3 * SC_WIDTH]
    qkv = proj[:, 3 * SC_WIDTH:3 * SC_WIDTH + GD_QKV]
    z = proj[:, 3 * SC_WIDTH + GD_QKV:]

    if chain:
        @pl.when(j == 0)
        def _():
            state_ref[...] = s0_ref[0]
            sc_prev_ref[...] = scp_ref[...]
            gc_prev_ref[...] = gcp_ref[...]
        sc_prev, gc_prev = sc_prev_ref[...], gc_prev_ref[...]
    else:
        sc_prev, gc_prev = scp_ref[...], gcp_ref[...]

    outs = [b_gate * _causal_conv(conv_in, sc_prev, scw_ref, shift_fn)]
    qkv_c = _silu(_causal_conv(qkv, gc_prev, gcw_ref, shift_fn))

    lane8 = lax.broadcasted_iota(jnp.int32, gates.shape, 1)
    g_dec = -jnp.exp(alog_ref[...]) * _softplus(gates + dtb_ref[...])
    cols = jnp.where(lane8 < GD_HEADS, g_dec, jax.nn.sigmoid(gates))
    same, incl, strict = _chunk_masks(rows, chunk)
    cols3 = _split3(cols)
    gcum = _exact_mask_dot(_dot, _mask_bf16(incl), cols3)
    gtot = _exact_mask_dot(_dot, _mask_bf16(same), cols3)
    eye8 = _mask_bf16(lax.broadcasted_iota(jnp.int32, (8, 8), 0) == lax.broadcasted_iota(jnp.int32, (8, 8), 1))
    gcum_rows = _exact_mask_dot(_dot_nt, eye8, _split3(gcum))

    eye = (lax.broadcasted_iota(jnp.int32, (rows, rows), 0)
           == lax.broadcasted_iota(jnp.int32, (rows, rows), 1)).astype(jnp.float32)
    hd = []
    for h in range(GD_HEADS):
        sl = lambda part: slice(part * GD_HEADS * GD_D + h * GD_D, part * GD_HEADS * GD_D + (h + 1) * GD_D)
        q, k, v = qkv_c[:, sl(0)], qkv_c[:, sl(1)], qkv_c[:, sl(2)]
        q = q * lax.rsqrt(jnp.sum(q * q, axis=-1, keepdims=True) + 1e-6) * (GD_D ** -0.5)
        k = k * lax.rsqrt(jnp.sum(k * k, axis=-1, keepdims=True) + 1e-6)
        beta = cols[:, GD_HEADS + h:GD_HEADS + h + 1]
        gc = gcum[:, h:h + 1]
        gt = gtot[:, h:h + 1]
        decay = jnp.where(incl, jnp.exp(jnp.where(incl, gc - gcum_rows[h:h + 1, :], 0.0)), 0.0)
        k_beta = k * beta
        kb = _bf(k)
        egc = jnp.exp(gc)
        power = -jnp.where(strict, _dot_nt(_bf(k_beta), kb) * decay, 0.0)
        hd.append(dict(
            inv=eye + power, power=_bf(power),
            att=_bf(jnp.where(incl, _dot_nt(_bf(q), kb) * decay, 0.0)),
            rhs=_split2(jnp.concatenate([v * beta, k_beta * egc], axis=1)),
            q_dec=q * egc, k_end=k * jnp.exp(gt - gc),
            decay_end=jnp.exp(gt), inter=[], v_new=[], s=None))

    for _ in range(chunk.bit_length() - 2):
        for d in hd:
            d["power"] = _bf(_dot(d["power"], d["power"]))
        for d in hd:
            d["inv"] = d["inv"] + _dot(_bf(d["inv"]), d["power"])
    for d in hd:
        uw = _dot3(_split2(d["inv"]), d["rhs"])
        d["u"], d["w"] = uw[:, :GD_D], uw[:, GD_D:]

    for c in range(n_chunks):
        rs = slice(c * chunk, (c + 1) * chunk)
        for h, d in enumerate(hd):
            if chain:
                s = state_ref[h] if c == 0 else d["s"]
            else:
                s = s0_ref[c, h]
            sb = _bf(s)
            v_new = d["u"][rs] - _dot(_bf(d["w"][rs]), sb)
            d["inter"].append(_dot(_bf(d["q_dec"][rs]), sb))
            d["s"] = d["decay_end"][c * chunk:c * chunk + 1, :] * s + _dot_tn(_bf(d["k_end"][rs]), _bf(v_new))
            d["v_new"].append(v_new)
            if not chain:
                s_out_ref[c, h] = d["s"]
    for h, d in enumerate(hd):
        if chain:
            state_ref[h] = d["s"]
        o = jnp.concatenate(d["inter"], axis=0) + _dot(d["att"], _bf(jnp.concatenate(d["v_new"], axis=0)))
        outs.append(_rms_gate(o, nw_ref[...], z[:, h * GD_D:(h + 1) * GD_D]))
    mix_ref[...] = jnp.concatenate(outs, axis=-1).astype(mix_ref.dtype)

    if chain:
        sc_prev_ref[...] = conv_in[rows - 8:]
        gc_prev_ref[...] = qkv[rows - 8:]

        @pl.when(j == pl.num_programs(1) - 1)
        def _():
            s_out_ref[0] = state_ref[...]
            scp_out_ref[...] = conv_in[rows - 8:]
            gcp_out_ref[...] = qkv[rows - 8:]
    else:
        scp_out_ref[...] = conv_in
        gcp_out_ref[...] = qkv


def _mixer_cd(x, row0, n_seq, length, s0, sconv_state, gconv_state, w_main, w_gates, sconv_w, gconv_w, a_log, dt_bias,
              norm_w):
    n_rows = n_seq * length
    chain = length >= ROW_TILE
    rows = ROW_TILE if chain else BATCH_ROW_TILE
    assert row0 % rows == 0
    first = row0 // rows
    pad8 = lambda st: jnp.pad(st, ((0, 0), (8 - st.shape[1], 0), (0, 0))).reshape(n_seq * 8, st.shape[2])
    scp, gcp = pad8(sconv_state), pad8(gconv_state)
    if chain:
        assert length % rows == 0
        tiles = length // rows
        grid = (n_seq, tiles)
        row_map = lambda b, j: (b * tiles + j, 0)
        x_map = lambda b, j: (first + b * tiles + j, 0)
        seq_map4 = lambda b, j: (b, 0, 0, 0)
        seq_map2 = lambda b, j: (b, 0)
        const2 = lambda b, j: (0, 0)
        seq_block = 1
    else:
        assert length == 8 and n_rows % rows == 0
        seq_block = rows // 8
        grid = (n_rows // rows,)
        row_map = lambda i: (i, 0)
        x_map = lambda i: (first + i, 0)
        seq_map4 = lambda i: (i, 0, 0, 0)
        seq_map2 = lambda i: (i, 0)
        const2 = lambda i: (0, 0)
    zeros4 = jnp.zeros((GD_HEADS,), jnp.float32)
    alog8 = jnp.concatenate([a_log.astype(jnp.float32), zeros4]).reshape(1, 8)
    dtb8 = jnp.concatenate([dt_bias.astype(jnp.float32), zeros4]).reshape(1, 8)
    mix, s_new, scp_new, gcp_new = pl.pallas_call(
        functools.partial(_mixer_cd_kernel, chain=chain),
        grid=grid,
        in_specs=[
            pl.BlockSpec((rows, D_MODEL), x_map),
            pl.BlockSpec((D_MODEL, CD_MAIN), const2),
            pl.BlockSpec((D_MODEL, 8), const2),
            pl.BlockSpec((SC_K, SC_WIDTH), const2),
            pl.BlockSpec((GD_CONV, GD_QKV), const2),
            pl.BlockSpec((1, 8), const2),
            pl.BlockSpec((1, 8), const2),
            pl.BlockSpec((1, GD_D), const2),
            pl.BlockSpec((seq_block, GD_HEADS, GD_D, GD_D), seq_map4),
            pl.BlockSpec((seq_block * 8, SC_WIDTH), seq_map2),
            pl.BlockSpec((seq_block * 8, GD_QKV), seq_map2),
        ],
        out_specs=[
            pl.BlockSpec((rows, MIX_OUT), row_map),
            pl.BlockSpec((seq_block, GD_HEADS, GD_D, GD_D), seq_map4),
            pl.BlockSpec((seq_block * 8, SC_WIDTH), seq_map2),
            pl.BlockSpec((seq_block * 8, GD_QKV), seq_map2),
        ],
        out_shape=[
            jax.ShapeDtypeStruct((n_rows, MIX_OUT), jnp.bfloat16),
            jax.ShapeDtypeStruct(s0.shape, jnp.float32),
            jax.ShapeDtypeStruct(scp.shape, jnp.float32),
            jax.ShapeDtypeStruct(gcp.shape, jnp.float32),
        ],
        scratch_shapes=[
            pltpu.VMEM((GD_HEADS, GD_D, GD_D), jnp.float32),
            pltpu.VMEM((8, SC_WIDTH), jnp.float32),
            pltpu.VMEM((8, GD_QKV), jnp.float32),
        ],
        compiler_params=pltpu.CompilerParams(
            dimension_semantics=("arbitrary",) * len(grid), vmem_limit_bytes=VMEM_LIMIT),
        name="mixer_cd_chain" if chain else "mixer_cd_batch",
    )(x, w_main, w_gates, sconv_w, gconv_w, alog8, dtb8, norm_w.reshape(1, -1), s0, scp, gcp)
    tail = lambda st, keep: st.reshape(n_seq, 8, -1)[:, 8 - keep:]
    return mix, s_new, tail(scp_new, SC_K - 1), tail(gcp_new, GD_CONV - 1)


def _outproj_ln_router_kernel(mix_a_ref, mix_b_ref, x_a_ref, x_b_ref, w_ref, g_ref, b_ref, rwt_ref, rb_ref,
                              x1_ref, xp_ref, tope_ref, gate_ref, rank_ref, cnt_ref, linv_ref, *, group_tiles, split):
    from_a = pl.program_id(0) < split
    mix_in = jnp.where(from_a, mix_a_ref[...], mix_b_ref[...])
    x_in = jnp.where(from_a, x_a_ref[...], x_b_ref[...])
    mix = jnp.dot(mix_in, w_ref[...], preferred_element_type=jnp.float32)
    x1 = _layer_norm_rows(DEEPNORM_ALPHA * x_in + mix, g_ref[...], b_ref[...])
    x1_ref[...] = x1
    packed = _pack_bf16_pairs(x1)
    for j in range(HALF // 128):
        xp_ref[:, j * 8:(j + 1) * 8, :] = packed[:, j * 128:(j + 1) * 128].reshape(x1.shape[0] // 8, 8, 128)
    logits = _dot_nt(_bf(rwt_ref[...]), _bf(x1)) + rb_ref[...]
    tm = logits.shape[1]
    expert = lax.broadcasted_iota(jnp.int32, logits.shape, 0)
    vals, idxs = [], []
    for _ in range(TOP_K):
        m = jnp.max(logits, axis=0, keepdims=True)
        idx = jnp.min(jnp.where(logits == m, expert, N_EXPERTS), axis=0, keepdims=True)
        vals.append(m)
        idxs.append(idx)
        logits = jnp.where(expert == idx, -jnp.inf, logits)
    ex = [jnp.exp(v - vals[0]) for v in vals]
    den = ex[0] + ex[1] + ex[2] + ex[3]
    hits = [expert == idx for idx in idxs]
    member = sum(h.astype(jnp.float32) for h in hits)
    earlier = (lax.broadcasted_iota(jnp.int32, (tm, tm), 0) < lax.broadcasted_iota(jnp.int32, (tm, tm), 1))
    before = _dot(member.astype(jnp.bfloat16), earlier.astype(jnp.bfloat16))
    cnt_ref[0] = jnp.sum(member, axis=1, keepdims=True).astype(jnp.int32)
    slot = lax.broadcasted_iota(jnp.int32, tope_ref.shape[1:], 0)
    tope = jnp.zeros(slot.shape, jnp.int32)
    gate = jnp.zeros(slot.shape, jnp.float32)
    rank = jnp.zeros(slot.shape, jnp.int32)
    for k in range(TOP_K):
        tope = jnp.where(slot == k, idxs[k], tope)
        gate = jnp.where(slot == k, ex[k] / den, gate)
        rank_k = jnp.sum(jnp.where(hits[k], before, 0.0), axis=0, keepdims=True).astype(jnp.int32)
        rank = jnp.where(slot == k, rank_k, rank)
    tope_ref[0] = tope
    gate_ref[0] = gate
    rank_ref[0] = rank

    lower = (lax.broadcasted_iota(jnp.int32, (tm, tm), 1) < lax.broadcasted_iota(jnp.int32, (tm, tm), 0))
    member_bf = member.astype(jnp.bfloat16)
    before_cols = _dot_nt(lower.astype(jnp.bfloat16), member_bf)
    eye = (lax.broadcasted_iota(jnp.int32, (tm, tm), 0) == lax.broadcasted_iota(jnp.int32, (tm, tm), 1))
    member_cols = _dot_nt(eye.astype(jnp.bfloat16), member_bf)
    key = jnp.where(member_cols > 0.5, before_cols, -1.0)
    order = lax.broadcasted_iota(jnp.int32, (tm, tm), 1).astype(jnp.float32)
    token = lax.broadcasted_iota(jnp.int32, (8, tm), 1).astype(jnp.bfloat16)
    expert8 = lax.broadcasted_iota(jnp.int32, (N_EXPERTS, tm), 0)
    listed = jnp.zeros((N_EXPERTS, tm), jnp.float32)
    for e in range(N_EXPERTS):
        onehot = (key[:, e:e + 1] == order).astype(jnp.bfloat16)
        listed = jnp.where(expert8 == e, _dot(token, onehot)[0:1, :], listed)
    local = listed.astype(jnp.int32)
    first = lax.rem(pl.program_id(0), group_tiles) * tm
    linv_ref[0] = ((first + local) >> 3) * (8 * (HALF // 128)) + (local & 7)


def _outproj_ln_router(mix_a, mix_b, x_a, x_b, n, n_a, w_out, ln_g, ln_b, router_w, router_b):
    tm = TOKEN_TILE
    split = n_a // tm
    assert n_a % tm == 0 and x_a[1] % tm == 0 and x_b[1] % tm == 0
    row = lambda i: (i, 0)
    full = lambda i: (0, 0)
    part_a = lambda first: (lambda i: (first // tm + jnp.minimum(i, split - 1), 0))
    part_b = lambda first: (lambda i: (first // tm + jnp.maximum(i - split, 0), 0))
    return pl.pallas_call(
        functools.partial(_outproj_ln_router_kernel, group_tiles=n // tm // N_GROUPS, split=split),
        grid=(n // tm,),
        in_specs=[
            pl.BlockSpec((tm, D_MODEL), part_a(0)),
            pl.BlockSpec((tm, D_MODEL), part_b(0)),
            pl.BlockSpec((tm, D_MODEL), part_a(x_a[1])),
            pl.BlockSpec((tm, D_MODEL), part_b(x_b[1])),
            pl.BlockSpec((D_MODEL, D_MODEL), full),
            pl.BlockSpec((1, D_MODEL), full),
            pl.BlockSpec((1, D_MODEL), full),
            pl.BlockSpec((N_EXPERTS, D_MODEL), full),
            pl.BlockSpec((N_EXPERTS, 1), full),
        ],
        out_specs=[
            pl.BlockSpec((tm, D_MODEL), row),
            pl.BlockSpec((tm // 8, HALF // 16, 128), lambda i: (i, 0, 0)),
            pl.BlockSpec((1, TOP_K, tm), lambda i: (i, 0, 0)),
            pl.BlockSpec((1, TOP_K, tm), lambda i: (i, 0, 0)),
            pl.BlockSpec((1, TOP_K, tm), lambda i: (i, 0, 0)),
            pl.BlockSpec((1, N_EXPERTS, 1), lambda i: (i, 0, 0)),
            pl.BlockSpec((1, N_EXPERTS, tm), lambda i: (i, 0, 0)),
        ],
        out_shape=[
            jax.ShapeDtypeStruct((n, D_MODEL), jnp.float32),
            jax.ShapeDtypeStruct((n // 8, HALF // 16, 128), jnp.uint32),
            jax.ShapeDtypeStruct((n // tm, TOP_K, tm), jnp.int32),
            jax.ShapeDtypeStruct((n // tm, TOP_K, tm), jnp.float32),
            jax.ShapeDtypeStruct((n // tm, TOP_K, tm), jnp.int32),
            jax.ShapeDtypeStruct((n // tm, N_EXPERTS, 1), jnp.int32),
            jax.ShapeDtypeStruct((n // tm, N_EXPERTS, tm), jnp.int32),
        ],
        compiler_params=pltpu.CompilerParams(dimension_semantics=("arbitrary",), vmem_limit_bytes=VMEM_LIMIT),
        name="outproj_ln_router",
    )(mix_a, mix_b, x_a[0], x_b[0], w_out, ln_g.reshape(1, -1), ln_b.reshape(1, -1), router_w.T,
      router_b.reshape(-1, 1))


def _route_plan(top_e, rank, tile_counts):
    n = top_e.shape[0] * top_e.shape[2]
    br, tm, ch = EXPERT_ROWS, TOKEN_TILE, COMBINE_CHUNK
    n_virtual = N_EXPERTS * N_GROUPS
    n_blocks = n * TOP_K // br + n_virtual + 1
    n_tiles = n // tm
    group_tiles = n_tiles // N_GROUPS
    max_chunks = (tm * TOP_K) // ch + N_EXPERTS + 1
    i32 = jnp.int32

    cnt = tile_counts.reshape(N_GROUPS, group_tiles, N_EXPERTS)
    nblk = (cnt.sum(axis=1) + br - 1) // br
    blk_end = jnp.cumsum(nblk.reshape(-1))
    row_off = ((blk_end - nblk.reshape(-1)) * br).reshape(N_GROUPS, 1, N_EXPERTS)
    seg_start = (row_off + jnp.cumsum(cnt, axis=1) - cnt).reshape(n_tiles, N_EXPERTS)
    seg_len = tile_counts
    blocks = jnp.arange(n_blocks, dtype=i32)
    block_ve = jnp.minimum(jnp.sum(blk_end[None, :] <= blocks[:, None], axis=1), n_virtual - 1).astype(i32)
    valid = blocks < blk_end[-1]
    run_ve = jnp.where(valid, block_ve, block_ve[jnp.maximum(blk_end[-1] - 1, 0)])
    block_e = run_ve % N_EXPERTS
    starts = jnp.concatenate([jnp.ones((1,), bool), block_e[1:] != block_e[:-1]])
    later_start = lax.cummin(jnp.where(starts, blocks, n_blocks)[::-1], axis=0)[::-1]
    next_start = jnp.concatenate([later_start[1:], jnp.full((1,), n_blocks, i32)])
    next_e = jnp.where(next_start < n_blocks, block_e[jnp.minimum(next_start, n_blocks - 1)], -1)
    block_info = jnp.stack([block_e, run_ve // N_EXPERTS, valid.astype(i32), starts.astype(i32), next_e.astype(i32)])

    aligned = seg_start // 8 * 8
    lead = seg_start - aligned
    nch = jnp.where(seg_len > 0, (lead + seg_len + ch - 1) // ch, 0)
    ch_end = jnp.cumsum(nch, axis=1)
    ch_first = ch_end - nch
    n_chunks = ch_end[:, -1]
    cidx = jnp.arange(max_chunks, dtype=i32)[None, :, None]
    owns = jnp.logical_and(ch_first[:, None, :] <= cidx, cidx < ch_end[:, None, :])
    chunk_src = jnp.sum(jnp.where(owns, aligned[:, None, :] + (cidx - ch_first[:, None, :]) * ch, 0), axis=2)

    hit = top_e[..., None] == jnp.arange(N_EXPERTS, dtype=i32)
    pick = lambda table: jnp.sum(jnp.where(hit, table[:, None, None, :], 0), axis=-1)
    pos = pick(ch_first * ch + lead) + rank
    cnt_ve = cnt.transpose(0, 2, 1).reshape(n_virtual, group_tiles)
    passed_rows = jnp.cumsum(cnt_ve, axis=1)[run_ve]
    offset = (blocks * br - row_off.reshape(-1)[run_ve])[:, None] + jnp.arange(br, dtype=i32)[None, :]
    passed = passed_rows[:, None, :] <= offset[:, :, None]
    tile_idx = jnp.minimum(jnp.sum(passed, axis=2), group_tiles - 1)
    place = jnp.clip(offset - jnp.sum(jnp.where(passed, cnt_ve[run_ve][:, None, :], 0), axis=2), 0, tm - 1)
    row_place = (tile_idx * tm + place).astype(i32)
    pos = (pos // 8) * (8 * LANE_TILES) + pos % 8
    return dict(row_place=row_place.reshape(n_blocks, 1, br), block_info=block_info,
                chunk_src=(chunk_src * LANE_TILES).astype(i32), n_chunks=n_chunks.astype(i32),
                pos=pos.reshape(n_tiles, 1, tm * TOP_K).astype(i32))


def _expert_kernel(info_ref, place_ref, place_next_ref, list_ref, list_next_ref, xq_hbm, wgu_hbm, bgu_ref, wdn_hbm,
                   bdn_ref, y_ref, xq_vmem, stage_ref, wgu_f32_ref, wdn_f32_ref, wgu_ref, wdn_ref, load_sem, w_sems, *,
                   layer):
    b = pl.program_id(0)
    grp = info_ref[1, b]
    prev_grp = info_ref[1, jnp.maximum(b - 1, 0)]

    def weight_copies(expert):
        return (pltpu.make_async_copy(wgu_hbm.at[layer, expert], wgu_f32_ref, w_sems.at[0]),
                pltpu.make_async_copy(wdn_hbm.at[layer, expert], wdn_f32_ref, w_sems.at[1]))

    @pl.when(b == 0)
    def _():
        for cp in weight_copies(info_ref[0, 0]):
            cp.start()

    @pl.when(info_ref[3, b] == 1)
    def _():
        for cp in weight_copies(info_ref[0, b]):
            cp.wait()
        rows = D_MODEL // 8

        def cast(i, carry):
            r0 = pl.multiple_of(i * rows, rows)
            wgu_ref[pl.ds(r0, rows), :] = wgu_f32_ref[pl.ds(r0, rows), :].astype(jnp.bfloat16)
            wdn_ref[pl.ds(r0, rows), :] = wdn_f32_ref[pl.ds(r0, rows), :].astype(jnp.bfloat16)
            return carry

        lax.fori_loop(0, 8, cast, 0)

        @pl.when(info_ref[4, b] >= 0)
        def _():
            for cp in weight_copies(info_ref[4, b]):
                cp.start()

    group_rows = xq_vmem.shape[0]
    slot = b % 2
    tiles = HALF // 128

    def gather_row(addr, dst_slot, group8, sub):
        stage_ref[dst_slot, group8, pl.ds(sub, tiles, stride=8), :] = xq_vmem[pl.ds(addr, tiles, stride=8), :]

    @pl.when(jnp.logical_or(b == 0, grp != prev_grp))
    def _():
        cp = pltpu.make_async_copy(xq_hbm.at[pl.ds(grp * group_rows, group_rows)], xq_vmem, load_sem)
        cp.start()
        cp.wait()

        def gather(i, carry):
            for u in range(8):
                gather_row(list_ref[0, 0, 0, place_ref[0, 0, i * 8 + u]], slot, i, u)
            return carry

        lax.fori_loop(0, EXPERT_ROWS // 8, gather, 0)

    @pl.when(info_ref[2, b] == 1)
    def _():
        for r in range(EXPERT_ROWS):
            gather_row(list_next_ref[0, 0, 0, place_next_ref[0, 0, r]], 1 - slot, r // 8, r % 8)
        pieces = [_unpack_bf16_pairs(stage_ref[slot, :, j * 8:(j + 1) * 8, :].reshape(EXPERT_ROWS, 128))
                  for j in range(tiles)]
        lo = jnp.concatenate([p[0] for p in pieces], axis=1)
        hi = jnp.concatenate([p[1] for p in pieces], axis=1)
        hgu = (jnp.dot(lo, wgu_ref[:HALF, :], preferred_element_type=jnp.float32)
               + jnp.dot(hi, wgu_ref[HALF:, :], preferred_element_type=jnp.float32) + bgu_ref[0, 0])
        glu = jnp.minimum(hgu[:, :D_FF], SWIGLU_LIMIT)
        lin = jnp.clip(hgu[:, D_FF:], -SWIGLU_LIMIT, SWIGLU_LIMIT)
        act = (lin + 1.0) * glu * jax.nn.sigmoid(SWIGLU_ALPHA * glu)
        y = jnp.dot(act.astype(jnp.bfloat16), wdn_ref[...], preferred_element_type=jnp.float32) + bdn_ref[0, 0]
        for c in range(LANE_TILES):
            y_ref[:, c * 8:(c + 1) * 8, :] = y[:, c * 128:(c + 1) * 128].reshape(EXPERT_ROWS // 8, 8, 128)

    @pl.when(info_ref[2, b] == 0)
    def _():
        y_ref[...] = jnp.zeros_like(y_ref)


def _expert_mlp(plan, xq, token_lists, layer, w_gu, b_gu, w_dn, b_dn):
    br = EXPERT_ROWS
    n_blocks = plan["row_place"].shape[0]
    assert br >= COMBINE_CHUNK + 8
    pad_rows = n_blocks * br
    expert = lambda b, info: (layer, info[0, b], 0, 0)
    following = lambda b: jnp.minimum(b + 1, n_blocks - 1)
    list_block = (1, 1, 1, token_lists.shape[3])
    grid_spec = pltpu.PrefetchScalarGridSpec(
        num_scalar_prefetch=1,
        grid=(n_blocks,),
        in_specs=[
            pl.BlockSpec((1, 1, br), lambda b, info: (b, 0, 0), memory_space=pltpu.SMEM),
            pl.BlockSpec((1, 1, br), lambda b, info: (following(b), 0, 0), memory_space=pltpu.SMEM),
            pl.BlockSpec(list_block, lambda b, info: (info[0, b], info[1, b], 0, 0), memory_space=pltpu.SMEM),
            pl.BlockSpec(list_block, lambda b, info: (info[0, following(b)], info[1, following(b)], 0, 0),
                         memory_space=pltpu.SMEM),
            pl.BlockSpec(memory_space=pl.ANY),
            pl.BlockSpec(memory_space=pl.ANY),
            pl.BlockSpec((1, 1, 1, 2 * D_FF), expert),
            pl.BlockSpec(memory_space=pl.ANY),
            pl.BlockSpec((1, 1, 1, D_MODEL), expert),
        ],
        out_specs=pl.BlockSpec((br // 8, 8 * LANE_TILES, 128), lambda b, info: (b, 0, 0)),
        scratch_shapes=[
            pltpu.VMEM((xq.shape[0] // N_GROUPS, 128), jnp.uint32),
            pltpu.VMEM((2, br // 8, 8 * (HALF // 128), 128), jnp.uint32),
            pltpu.VMEM((D_MODEL, 2 * D_FF), jnp.float32),
            pltpu.VMEM((D_FF, D_MODEL), jnp.float32),
            pltpu.VMEM((D_MODEL, 2 * D_FF), jnp.bfloat16),
            pltpu.VMEM((D_FF, D_MODEL), jnp.bfloat16),
            pltpu.SemaphoreType.DMA(()),
            pltpu.SemaphoreType.DMA((2,)),
        ],
    )
    return pl.pallas_call(
        functools.partial(_expert_kernel, layer=layer),
        grid_spec=grid_spec,
        out_shape=jax.ShapeDtypeStruct((pad_rows // 8, 8 * LANE_TILES, 128), jnp.float32),
        compiler_params=pltpu.CompilerParams(dimension_semantics=("arbitrary",), vmem_limit_bytes=VMEM_LIMIT),
        name="expert_mlp",
    )(plan["block_info"], plan["row_place"], plan["row_place"], token_lists, token_lists, xq, w_gu,
      b_gu.reshape(DEPTH, N_EXPERTS, 1, -1), w_dn, b_dn.reshape(DEPTH, N_EXPERTS, 1, -1))


def _combine_ln_kernel(src_ref, nch_ref, pos_ref, gate_ref, x1_ref, y_hbm, g_ref, b_ref, out_ref,
                       stage_ref, ffn_ref, sems):
    i = pl.program_id(0)
    n_tiles = pl.num_programs(0)
    chunk_rows = COMBINE_CHUNK * LANE_TILES

    def chunk_copy(tile, slot, c):
        src = pl.multiple_of(src_ref[tile, c], 8 * LANE_TILES)
        return pltpu.make_async_copy(y_hbm.at[pl.ds(src, chunk_rows)],
                                     stage_ref.at[slot, pl.ds(c * chunk_rows, chunk_rows)], sems.at[slot])

    def issue(tile, slot):
        lax.fori_loop(0, nch_ref[tile], lambda c, carry: (chunk_copy(tile, slot, c).start(), carry)[1], 0)

    @pl.when(i == 0)
    def _():
        issue(0, 0)

    @pl.when(i + 1 < n_tiles)
    def _():
        issue(i + 1, (i + 1) % 2)

    slot = i % 2
    lax.fori_loop(0, nch_ref[i], lambda c, carry: (chunk_copy(i, slot, c).wait(), carry)[1], 0)

    tm = x1_ref.shape[0]

    def combine(j, carry):
        for u in range(8):
            t = j * 8 + u
            acc = None
            for k in range(TOP_K):
                row = stage_ref[slot, pl.ds(pos_ref[0, 0, k * tm + t], LANE_TILES, stride=8), :]
                row = row * gate_ref[0, 0, k * tm + t]
                acc = row if acc is None else acc + row
            ffn_ref[j, pl.ds(u, LANE_TILES, stride=8), :] = acc
        return carry

    lax.fori_loop(0, tm // 8, combine, 0)
    ffn = jnp.concatenate([ffn_ref[:, c * 8:(c + 1) * 8, :].reshape(tm, 128) for c in range(LANE_TILES)], axis=1)
    out_ref[...] = _layer_norm_rows(DEEPNORM_ALPHA * x1_ref[...] + ffn, g_ref[...], b_ref[...])


def _combine_ln(plan, gate, x1, y_flat, ln_g, ln_b):
    n = x1.shape[0]
    tm = TOKEN_TILE
    n_tiles = n // tm
    max_chunks = plan["chunk_src"].shape[1]
    row = lambda i, src, nch: (i, 0)
    full = lambda i, src, nch: (0, 0)
    per_tile = lambda i, src, nch: (i, 0, 0)
    grid_spec = pltpu.PrefetchScalarGridSpec(
        num_scalar_prefetch=2,
        grid=(n_tiles,),
        in_specs=[
            pl.BlockSpec((1, 1, tm * TOP_K), per_tile, memory_space=pltpu.SMEM),
            pl.BlockSpec((1, 1, tm * TOP_K), per_tile, memory_space=pltpu.SMEM),
            pl.BlockSpec((tm, D_MODEL), row),
            pl.BlockSpec(memory_space=pl.ANY),
            pl.BlockSpec((1, D_MODEL), full),
            pl.BlockSpec((1, D_MODEL), full),
        ],
        out_specs=pl.BlockSpec((tm, D_MODEL), row),
        scratch_shapes=[
            pltpu.VMEM((2, max_chunks * COMBINE_CHUNK * LANE_TILES, 128), jnp.float32),
            pltpu.VMEM((tm // 8, 8 * LANE_TILES, 128), jnp.float32),
            pltpu.SemaphoreType.DMA((2,)),
        ],
    )
    return pl.pallas_call(
        _combine_ln_kernel,
        grid_spec=grid_spec,
        out_shape=jax.ShapeDtypeStruct((n, D_MODEL), jnp.float32),
        compiler_params=pltpu.CompilerParams(dimension_semantics=("arbitrary",), vmem_limit_bytes=VMEM_LIMIT),
        name="combine_ln",
    )(plan["chunk_src"], plan["n_chunks"], plan["pos"], gate.reshape(n_tiles, 1, tm * TOP_K), x1, y_flat,
      ln_g.reshape(1, -1), ln_b.reshape(1, -1))


def _post_mixer(mix_a, mix_b, x_a, x_b, layer, w_out, ln1_g, ln1_b, router_w, router_b, w_gu, b_gu, w_dn, b_dn, ln2_g,
                ln2_b):
    n_a, n = mix_a.shape[0], mix_a.shape[0] + mix_b.shape[0]
    x1, xp, top_e, gate, rank, tile_counts, lists = _outproj_ln_router(mix_a, mix_b, x_a, x_b, n, n_a, w_out, ln1_g, ln1_b,
                                                                       router_w, router_b)
    plan = _route_plan(top_e, rank, tile_counts.reshape(-1, N_EXPERTS))
    token_lists = lists.transpose(1, 0, 2).reshape(N_EXPERTS, N_GROUPS, 1, -1)
    y_sorted = _expert_mlp(plan, xp.reshape(-1, 128), token_lists, layer, w_gu, b_gu, w_dn, b_dn)
    return _combine_ln(plan, gate, x1, y_sorted.reshape(-1, 128), ln2_g, ln2_b)


def kernel(x_prompt, x_sample, state_hgrn, state_pool, state_sconv, state_gdn_conv, state_gdn, w_in_ab, hgrn_lower_bounds, hgrn_norm_w, pool_w, pool_scale, w_out_ab, w_in_cd, sconv_w, gdn_conv_w, gdn_a_log, gdn_dt_bias, gdn_norm_w, w_out_cd, ln1_g, ln1_b, ln2_g, ln2_b, router_w, router_b, w_gu, b_gu, w_dn, b_dn):
    bp, sp, _ = x_prompt.shape
    bs, ss, _ = x_sample.shape
    n_p, n_s = bp * sp, bs * ss
    bf = jnp.bfloat16
    f32 = jnp.float32
    lower_bounds = jnp.cumsum(jax.nn.softmax(hgrn_lower_bounds.astype(f32), axis=0), axis=0)
    xa, xb = (x_prompt.reshape(n_p, D_MODEL), 0), (x_sample.reshape(n_s, D_MODEL), 0)
    zeros_like_prompt = lambda st: jnp.zeros((bp,) + st.shape[2:], f32)

    states = {}
    for l in range(DEPTH):
        if l % 2 == 0:
            args = (w_in_ab[0].astype(bf), lower_bounds[l], hgrn_norm_w[0], pool_w[0].astype(bf), pool_scale[0])
            mp, hp, pp = _mixer_ab(*xa, bp, sp, 0, zeros_like_prompt(state_hgrn), zeros_like_prompt(state_pool), *args)
            ms, hs, ps = _mixer_ab(*xb, bs, ss, PAST_LEN, state_hgrn[0], state_pool[0], *args)
            states.update(hp=hp[None], hs=hs[None], pp=pp[None], ps=ps[None])
            w_out = w_out_ab[0]
        else:
            args = (w_in_cd[0][:, :CD_MAIN].astype(bf), w_in_cd[0][:, CD_MAIN:].astype(bf), sconv_w[0], gdn_conv_w[0],
                    gdn_a_log[0], gdn_dt_bias[0], gdn_norm_w[0])
            mp, gp, scp, gcp = _mixer_cd(*xa, bp, sp, zeros_like_prompt(state_gdn), zeros_like_prompt(state_sconv),
                                         zeros_like_prompt(state_gdn_conv), *args)
            ms, gs, scs, gcs = _mixer_cd(*xb, bs, ss, state_gdn[0], state_sconv[0], state_gdn_conv[0], *args)
            states.update(scp=scp[None], scs=scs[None], gcp=gcp[None], gcs=gcs[None], gp=gp[None], gs=gs[None])
            w_out = w_out_cd[0]
        x = _post_mixer(mp, ms, xa, xb, l, w_out.astype(bf), ln1_g[l], ln1_b[l], router_w[l], router_b[l],
                        w_gu, b_gu, w_dn, b_dn, ln2_g[l], ln2_b[l])
        xa, xb = (x, 0), (x, n_p)
    return (x[:n_p].reshape(bp, sp, D_MODEL), x[n_p:].reshape(bs, ss, D_MODEL),
            states["hp"], states["hs"], states["pp"], states["ps"], states["scp"], states["scs"],
            states["gcp"], states["gcs"], states["gp"], states["gs"])
```
